```python
import jax, jax.numpy as jnp
from jax import lax
import numpy as np

D_MODEL = 1024
BATCH = 8
SEQ = 2048
DEPTH = 2

POOL_WINDOWS = (2, 4, 8, 16)
POOL_GROUPS = len(POOL_WINDOWS)
POOL_GROUP_DIM = D_MODEL // POOL_GROUPS
POOL_WIDTH = POOL_GROUPS * POOL_GROUP_DIM
ATTN_GROUPS = ((128, 1), (512, 4), (2048, 16))
HEADS_PER_GROUP = 4
N_HEADS = HEADS_PER_GROUP * len(ATTN_GROUPS)
HEAD_DIM = D_MODEL // 8
ATTN_WIDTH = N_HEADS * HEAD_DIM
ATTN_OUT = HEADS_PER_GROUP * HEAD_DIM
Q_BLOCK = 64
NEG_INF = -1e30
N_BUCKETS = 32
MAX_DISTANCE = 1024
D_FF = ((8 * D_MODEL // 3 + 255) // 256) * 256
CONV_WIDTH = 3
EPS = 1e-6
SPLITS = [int(s) for s in np.cumsum([POOL_WIDTH, ATTN_WIDTH, ATTN_WIDTH, ATTN_WIDTH, D_MODEL])]
IN_WIDTH = POOL_WIDTH + 3 * ATTN_WIDTH + 2 * D_MODEL

kernel_name = "hybrid_pool_dilated_attn_convffn_encoder"


def _rmsnorm(x, g):
    xf = x.astype(jnp.float32)
    y = xf * lax.rsqrt(jnp.mean(xf * xf, axis=-1, keepdims=True) + EPS)
    return (y * g.astype(jnp.float32)).astype(x.dtype)


def _t5_buckets(rel):
    n = -rel
    half = N_BUCKETS // 2
    ret = (n < 0).astype(np.int32) * half
    n = np.abs(n)
    max_exact = half // 2
    large = max_exact + (np.log(np.maximum(n, 1) / max_exact)
                         / np.log(MAX_DISTANCE / max_exact) * (half - max_exact)).astype(np.int32)
    large = np.minimum(large, half - 1)
    return (ret + np.where(n < max_exact, n, large)).astype(np.int32)


def _centred_mean(u, window):
    S = u.shape[1]
    c = jnp.pad(jnp.cumsum(u.astype(jnp.float32), axis=1), ((0, 0), (1, 0), (0, 0)))
    i = np.arange(S)
    lo = np.clip(i - window // 2, 0, S)
    hi = np.clip(i - window // 2 + window, 0, S)
    count = (hi - lo).astype(np.float32)
    return ((c[:, hi] - c[:, lo]) / count[None, :, None]).astype(u.dtype)


def _dilated_group(q, k, v, bias, window, dilation):
    B, S, H, Dh = q.shape
    half = window // (2 * dilation)
    offs = dilation * np.arange(-half, half + 1)
    pad = half * dilation
    k_pad = jnp.pad(k, ((0, 0), (pad, pad), (0, 0), (0, 0)))
    v_pad = jnp.pad(v, ((0, 0), (pad, pad), (0, 0), (0, 0)))
    t = np.arange(Q_BLOCK)
    scale = HEAD_DIM ** -0.5

    def block(q0):
        pos = q0 + t[:, None] + offs[None, :]
        valid = (pos >= 0) & (pos < S)
        idx = pos + pad
        kb = jnp.take(k_pad, idx, axis=1)
        vb = jnp.take(v_pad, idx, axis=1)
        qb = lax.dynamic_slice_in_dim(q, q0, Q_BLOCK, axis=1)
        logits = jnp.einsum('bqhd,bqkhd->bhqk', qb, kb,
                            preferred_element_type=jnp.float32) * scale
        logits = logits + bias.astype(jnp.float32)[None, :, None, :]
        logits = jnp.where(valid[None, None], logits, NEG_INF)
        lse = jax.nn.logsumexp(logits, axis=-1)
        p = jnp.exp(logits - lse[..., None]).astype(vb.dtype)
        o = jnp.einsum('bhqk,bqkhd->bqhd', p, vb)
        return o, lse

    starts = jnp.arange(S // Q_BLOCK, dtype=jnp.int32) * Q_BLOCK
    o, lse = lax.map(block, starts)
    o = o.transpose(1, 0, 2, 3, 4).reshape(B, S, H, Dh)
    lse = lse.transpose(1, 0, 3, 2).reshape(B, S, H)
    return o, lse


def _token_mixer(xn, w_in, w_pool, pool_scale, w_a, w_b, w_o, rel_bias):
    B, S, _ = xn.shape
    u, q, k, v, g_pool, g_attn = jnp.split(xn @ w_in, SPLITS, axis=-1)

    ug = u.reshape(B, S, POOL_GROUPS, POOL_GROUP_DIM)
    pooled = jnp.stack([_centred_mean(ug[:, :, gi], w) for gi, w in enumerate(POOL_WINDOWS)], axis=2) - ug
    pool_out = jnp.einsum('bsgc,gcd->bsgd', pooled, w_pool).reshape(B, S, POOL_WIDTH) * pool_scale

    q = q.reshape(B, S, N_HEADS, HEAD_DIM)
    k = k.reshape(B, S, N_HEADS, HEAD_DIM)
    v = v.reshape(B, S, N_HEADS, HEAD_DIM)
    outs, lses = [], []
    for gi, (window, dilation) in enumerate(ATTN_GROUPS):
        hs = slice(gi * HEADS_PER_GROUP, (gi + 1) * HEADS_PER_GROUP)
        half = window // (2 * dilation)
        buckets = _t5_buckets(dilation * np.arange(-half, half + 1))
        bias = rel_bias[buckets][:, hs].T
        o_g, lse_g = _dilated_group(q[:, :, hs], k[:, :, hs], v[:, :, hs], bias, window, dilation)
        outs.append(o_g)
        lses.append(lse_g)
    wts = jax.nn.softmax(jnp.stack(lses, axis=0), axis=0)
    attn = jnp.einsum('gbsh,gbshd->bshd', wts, jnp.stack(outs, axis=0).astype(jnp.float32))
    attn_out = attn.astype(xn.dtype).reshape(B, S, ATTN_OUT)

    merged = jax.nn.sigmoid(g_pool) * (pool_out @ w_a) + jax.nn.sigmoid(g_attn) * (attn_out @ w_b)
    return merged @ w_o


def _conv_ffn(xn, w_up, conv_w, conv_b, w_down):
    a, gval = jnp.split(xn @ w_up, 2, axis=-1)
    a = lax.conv_general_dilated(a, conv_w[:, None, :], window_strides=(1,),
                                 padding=((CONV_WIDTH // 2, CONV_WIDTH // 2),),
                                 dimension_numbers=('NWC', 'WIO', 'NWC'),
                                 feature_group_count=D_FF) + conv_b
    return (jax.nn.gelu(a) * gval) @ w_down


def setup_inputs(seed: int = 0) -> dict:
    key = jax.random.key(seed)
    ks = jax.random.split(key, 16)
    f32 = jnp.float32

    def nrm(k, shape, scale):
        return jax.random.normal(k, shape, f32) * scale

    return {
        "x": nrm(ks[0], (BATCH, SEQ, D_MODEL), 1.0),
        "w_in": nrm(ks[1], (DEPTH, D_MODEL, IN_WIDTH), D_MODEL ** -0.5),
        "w_pool": nrm(ks[2], (DEPTH, POOL_GROUPS, POOL_GROUP_DIM, POOL_GROUP_DIM), POOL_GROUP_DIM ** -0.5),
        "pool_scale": 1.0 + nrm(ks[3], (DEPTH, POOL_WIDTH), 0.02),
        "w_a": nrm(ks[4], (DEPTH, POOL_WIDTH, D_MODEL), POOL_WIDTH ** -0.5),
        "w_b": nrm(ks[5], (DEPTH, ATTN_OUT, D_MODEL), ATTN_OUT ** -0.5),
        "w_o": nrm(ks[6], (DEPTH, D_MODEL, D_MODEL), D_MODEL ** -0.5),
        "norm1": 1.0 + nrm(ks[7], (DEPTH, D_MODEL), 0.02),
        "norm2": 1.0 + nrm(ks[8], (DEPTH, D_MODEL), 0.02),
        "w_up": nrm(ks[9], (DEPTH, D_MODEL, 2 * D_FF), D_MODEL ** -0.5),
        "conv_w": nrm(ks[10], (DEPTH, CONV_WIDTH, D_FF), CONV_WIDTH ** -0.5),
        "conv_b": nrm(ks[11], (DEPTH, D_FF), 0.01),
        "w_down": nrm(ks[12], (DEPTH, D_FF, D_MODEL), D_FF ** -0.5),
        "rel_bias": nrm(ks[13], (N_BUCKETS, N_HEADS), 0.2),
        "norm_f": 1.0 + nrm(ks[14], (D_MODEL,), 0.02),
    }


def reference(x, w_in, w_pool, pool_scale, w_a, w_b, w_o, norm1, norm2,
              w_up, conv_w, conv_b, w_down, rel_bias, norm_f):
    h = x
    for layer in range(DEPTH):
        h = h + _token_mixer(_rmsnorm(h, norm1[layer]), w_in[layer], w_pool[layer], pool_scale[layer],
                             w_a[layer], w_b[layer], w_o[layer], rel_bias)
        h = h + _conv_ffn(_rmsnorm(h, norm2[layer]), w_up[layer], conv_w[layer], conv_b[layer], w_down[layer])
    return _rmsnorm(h, norm_f)
```

```python
import functools

import numpy as np
import jax
import jax.numpy as jnp
from jax import lax
from jax.experimental import pallas as pl
from jax.experimental.pallas import tpu as pltpu

D_MODEL = 1024
SEQ = 2048
POOL_WINDOWS = (2, 4, 8, 16)
POOL_GROUP_DIM = 256
POOL_WIDTH = 1024
ATTN_DILATIONS = (1, 4, 16)
HALF_WINDOW = 64
HEADS_PER_GROUP = 4
N_HEADS = 12
HEAD_DIM = 128
ATTN_WIDTH = N_HEADS * HEAD_DIM
ATTN_OUT = HEADS_PER_GROUP * HEAD_DIM
NEG_INF = -1e30
N_BUCKETS = 32
MAX_DISTANCE = 1024
D_FF = 2816
EPS = 1e-6
PROJ_WIDTH = POOL_WIDTH + 3 * ATTN_WIDTH
GATE_WIDTH = 2 * D_MODEL

Q_TILE = 128
PROJ_TN = 512
ROW_TILE = 512
HALO = 16
FF_CHUNK = 1408
VMEM_LIMIT = 56 * 1024 * 1024

_F32 = jnp.float32
_BF16 = jnp.bfloat16


def _rms(x, g):
    return x * lax.rsqrt(jnp.mean(x * x, axis=-1, keepdims=True) + EPS) * g


def _norm_kernel(x_ref, g_ref, o_ref):
    o_ref[...] = _rms(x_ref[...], g_ref[...]).astype(o_ref.dtype)


def _norm_call(x2d, g):
    rows = x2d.shape[0]
    tm = 1024
    return pl.pallas_call(
        _norm_kernel,
        grid=(rows // tm,),
        in_specs=[pl.BlockSpec((tm, D_MODEL), lambda i: (i, 0)),
                  pl.BlockSpec((1, D_MODEL), lambda i: (0, 0))],
        out_specs=pl.BlockSpec((tm, D_MODEL), lambda i: (i, 0)),
        out_shape=jax.ShapeDtypeStruct((rows, D_MODEL), _BF16),
        compiler_params=pltpu.CompilerParams(dimension_semantics=("parallel",)),
        name="rmsnorm",
    )(x2d, g.reshape(1, D_MODEL))


POOL_PAD = 16


def _window_sum(u, window):
    n = u.shape[0] + 2 * POOL_PAD
    pad = jnp.zeros((POOL_PAD, u.shape[1]), u.dtype)
    s = jnp.concatenate([pad, u, pad], axis=0)
    s = s + pltpu.roll(s, 1, axis=0)
    w = 2
    while w < window:
        s = pltpu.roll(s, w // 2, axis=0) + pltpu.roll(s, n - w // 2, axis=0)
        w *= 2
    return s[POOL_PAD:POOL_PAD + u.shape[0]]


def _proj_kernel(xn_ref, w_ref, wpool_ref, pscale_ref, o_ref, slab_ref):
    n = pl.program_id(1)
    y = jnp.dot(xn_ref[...], w_ref[...], preferred_element_type=_F32)

    def pool_tile(first_group):
        row = lax.broadcasted_iota(jnp.int32, (SEQ, 1), 0)
        for j in range(PROJ_TN // POOL_GROUP_DIM):
            gi = first_group + j
            window = POOL_WINDOWS[gi]
            u = y[:, j * POOL_GROUP_DIM:(j + 1) * POOL_GROUP_DIM]
            lo = jnp.maximum(row - window // 2, 0)
            hi = jnp.minimum(row - window // 2 + window, SEQ)
            pooled = _window_sum(u, window) / (hi - lo).astype(_F32) - u
            z = jnp.dot(pooled.astype(_BF16), wpool_ref[gi], preferred_element_type=_F32)
            z = z * pscale_ref[:, gi * POOL_GROUP_DIM:(gi + 1) * POOL_GROUP_DIM]
            o_ref[:, j * POOL_GROUP_DIM:(j + 1) * POOL_GROUP_DIM] = z.astype(o_ref.dtype)

    @pl.when(n == 0)
    def _():
        pool_tile(0)

    @pl.when(n == 1)
    def _():
        pool_tile(2)

    group = lax.rem(n + 1, 3)
    yq = y * jnp.where(n < 5, HEAD_DIM ** -0.5, 1.0).astype(_F32)

    @pl.when((n >= 2) & (group == 0))
    def _():
        o_ref[...] = yq.astype(o_ref.dtype)

    def permuted(dil):
        sub = SEQ // dil
        for h in range(HEADS_PER_GROUP):
            slab_ref[h] = yq[:, h * HEAD_DIM:(h + 1) * HEAD_DIM]
        for r in range(dil):
            for h in range(HEADS_PER_GROUP):
                o_ref[r * sub:(r + 1) * sub, h * HEAD_DIM:(h + 1) * HEAD_DIM] = (
                    slab_ref[h, pl.ds(r, sub, stride=dil), :].astype(o_ref.dtype))

    @pl.when((n >= 2) & (group == 1))
    def _():
        permuted(ATTN_DILATIONS[1])

    @pl.when((n >= 2) & (group == 2))
    def _():
        permuted(ATTN_DILATIONS[2])


def _proj_call(xn3, w_proj, w_pool, pool_scale):
    batch = xn3.shape[0]
    return pl.pallas_call(
        _proj_kernel,
        grid=(batch, PROJ_WIDTH // PROJ_TN),
        in_specs=[pl.BlockSpec((None, SEQ, D_MODEL), lambda b, n: (b, 0, 0)),
                  pl.BlockSpec((D_MODEL, PROJ_TN), lambda b, n: (0, n)),
                  pl.BlockSpec((len(POOL_WINDOWS), POOL_GROUP_DIM, POOL_GROUP_DIM), lambda b, n: (0, 0, 0)),
                  pl.BlockSpec((1, POOL_WIDTH), lambda b, n: (0, 0))],
        out_specs=pl.BlockSpec((None, SEQ, PROJ_TN), lambda b, n: (b, 0, n)),
        out_shape=jax.ShapeDtypeStruct((batch, SEQ, PROJ_WIDTH), _BF16),
        scratch_shapes=[pltpu.VMEM((HEADS_PER_GROUP, SEQ, HEAD_DIM), _F32)],
        compiler_params=pltpu.CompilerParams(dimension_semantics=("parallel", "arbitrary"),
                                             vmem_limit_bytes=VMEM_LIMIT),
        name="proj",
    )(xn3, w_proj, w_pool, pool_scale.reshape(1, POOL_WIDTH))


def _t5_buckets_np(rel):
    n = -rel
    half = N_BUCKETS // 2
    ret = (n < 0).astype(np.int32) * half
    n = np.abs(n)
    max_exact = half // 2
    large = max_exact + (np.log(np.maximum(n, 1) / max_exact)
                         / np.log(MAX_DISTANCE / max_exact) * (half - max_exact)).astype(np.int32)
    large = np.minimum(large, half - 1)
    return (ret + np.where(n < max_exact, n, large)).astype(np.int32)


def _key_span(dil):
    return min(Q_TILE + 2 * HALF_WINDOW, SEQ // dil)


def _bias_tiles(rel_bias, gi):
    dil = ATTN_DILATIONS[gi]
    span = _key_span(dil)
    buckets = _t5_buckets_np(dil * np.arange(-HALF_WINDOW, HALF_WINDOW + 1))
    bias = rel_bias[buckets][:, gi * HEADS_PER_GROUP:(gi + 1) * HEADS_PER_GROUP].T.astype(_F32)
    deltas = (0, HALF_WINDOW, 2 * HALF_WINDOW) if SEQ // dil > Q_TILE else (0,)
    tiles = []
    for delta in deltas:
        j = np.arange(span)[None, :] - np.arange(Q_TILE)[:, None] - delta
        inside = np.abs(j) <= HALF_WINDOW
        idx = np.clip(j + HALF_WINDOW, 0, 2 * HALF_WINDOW)
        tiles.append(jnp.where(inside[None], bias[:, idx], NEG_INF))
    return jnp.stack(tiles, axis=1)


def _attn_kernel(q0_ref, k0_ref, v0_ref, q1_ref, k1_ref, v1_ref, q2_ref, k2_ref, v2_ref,
                 b0_ref, b1_ref, b2_ref, o_ref, acc_ref, m_ref, l_ref):
    groups = ((q0_ref, k0_ref, v0_ref, b0_ref), (q1_ref, k1_ref, v1_ref, b1_ref), (q2_ref, k2_ref, v2_ref, b2_ref))
    for gi, (q_ref, k_ref, v_ref, b_ref) in enumerate(groups):
        dil = ATTN_DILATIONS[gi]
        sub = SEQ // dil
        span = _key_span(dil)
        blocks = sub // Q_TILE

        def unit(u, carry, dil=dil, sub=sub, span=span, blocks=blocks, gi=gi,
                 q_ref=q_ref, k_ref=k_ref, v_ref=v_ref, b_ref=b_ref):
            r = u // blocks
            m0 = (u % blocks) * Q_TILE
            start = jnp.clip(m0 - HALF_WINDOW, 0, sub - span)
            variant = (m0 - start) // HALF_WINDOW
            q = q_ref[pl.ds(pl.multiple_of(u * Q_TILE, Q_TILE), Q_TILE), :]
            k0 = pl.multiple_of(r * sub + start, HALF_WINDOW)
            k = k_ref[pl.ds(k0, span), :]
            v = v_ref[pl.ds(k0, span), :]
            s = lax.dot_general(q, k, (((1,), (1,)), ((), ())), preferred_element_type=_F32)
            s = s + b_ref[variant]
            m = jnp.max(s, axis=-1, keepdims=True)
            p = jnp.exp(s - m)
            l = jnp.sum(p, axis=-1, keepdims=True)
            o = jnp.dot(p.astype(_BF16), v, preferred_element_type=_F32)
            if dil == 1:
                rows = pl.ds(pl.multiple_of(m0, Q_TILE), Q_TILE)
            else:
                rows = pl.ds(m0 * dil + r, Q_TILE, stride=dil)
            acc_ref[gi, rows, :] = o
            m_ref[gi, rows, :] = jnp.broadcast_to(m, (Q_TILE, HEAD_DIM))
            l_ref[gi, rows, :] = jnp.broadcast_to(l, (Q_TILE, HEAD_DIM))
            return carry

        lax.fori_loop(0, SEQ // Q_TILE, unit, 0)

    def combine(t, carry):
        rows = pl.ds(pl.multiple_of(t * Q_TILE, Q_TILE), Q_TILE)
        ms = [m_ref[gi, rows, :] for gi in range(3)]
        top = jnp.maximum(jnp.maximum(ms[0], ms[1]), ms[2])
        num = jnp.zeros((Q_TILE, HEAD_DIM), _F32)
        den = jnp.zeros((Q_TILE, HEAD_DIM), _F32)
        for gi in range(3):
            w = jnp.exp(ms[gi] - top)
            num = num + w * acc_ref[gi, rows, :]
            den = den + w * l_ref[gi, rows, :]
        o_ref[rows, :] = (num / den).astype(o_ref.dtype)
        return carry

    lax.fori_loop(0, SEQ // Q_TILE, combine, 0)


def _attn_call(proj, bias_tiles):
    batch = proj.shape[0]
    col0 = POOL_WIDTH // HEAD_DIM

    def head_spec(which, gi):
        base = col0 + which * N_HEADS + gi * HEADS_PER_GROUP
        return pl.BlockSpec((None, SEQ, HEAD_DIM), lambda b, h: (b, 0, base + h))

    in_specs = [head_spec(which, gi) for gi in range(3) for which in range(3)]
    for t in bias_tiles:
        in_specs.append(pl.BlockSpec((None,) + t.shape[1:], lambda b, h: (h, 0, 0, 0)))
    scratch = [pltpu.VMEM((3, SEQ, HEAD_DIM), _F32) for _ in range(3)]
    return pl.pallas_call(
        _attn_kernel,
        grid=(batch, HEADS_PER_GROUP),
        in_specs=in_specs,
        out_specs=pl.BlockSpec((None, SEQ, HEAD_DIM), lambda b, h: (b, 0, h)),
        out_shape=jax.ShapeDtypeStruct((batch, SEQ, ATTN_OUT), _BF16),
        scratch_shapes=scratch,
        compiler_params=pltpu.CompilerParams(dimension_semantics=("parallel", "arbitrary"),
                                             vmem_limit_bytes=VMEM_LIMIT),
        name="attn",
    )(*([proj] * 9), *bias_tiles)


def _mix_kernel(xn_ref, pool_ref, attn_ref, h_ref, wg_ref, wa_ref, wb_ref, wo_ref, g_ref, h_out_ref, xn_out_ref):
    xn = xn_ref[...]
    gates = jnp.dot(xn, wg_ref[...], preferred_element_type=_F32)
    a = jnp.dot(pool_ref[...], wa_ref[...], preferred_element_type=_F32)
    b = jnp.dot(attn_ref[...], wb_ref[...], preferred_element_type=_F32)
    merged = jax.nn.sigmoid(gates[:, :D_MODEL]) * a + jax.nn.sigmoid(gates[:, D_MODEL:]) * b
    h = h_ref[...] + jnp.dot(merged.astype(_BF16), wo_ref[...], preferred_element_type=_F32)
    h_out_ref[...] = h
    xn_out_ref[...] = _rms(h, g_ref[...]).astype(xn_out_ref.dtype)


def _const_spec(shape):
    return pl.BlockSpec(shape, lambda i: (0,) * len(shape), pipeline_mode=pl.Buffered(1))


def _mix_call(xn2d, proj2d, attn2d, h2d, w_gate, w_a, w_b, w_o, g_next):
    rows = xn2d.shape[0]
    tm = ROW_TILE
    return pl.pallas_call(
        _mix_kernel,
        grid=(rows // tm,),
        in_specs=[pl.BlockSpec((tm, D_MODEL), lambda i: (i, 0)),
                  pl.BlockSpec((tm, POOL_WIDTH), lambda i: (i, 0)),
                  pl.BlockSpec((tm, ATTN_OUT), lambda i: (i, 0)),
                  pl.BlockSpec((tm, D_MODEL), lambda i: (i, 0)),
                  _const_spec((D_MODEL, GATE_WIDTH)),
                  _const_spec((POOL_WIDTH, D_MODEL)),
                  _const_spec((ATTN_OUT, D_MODEL)),
                  _const_spec((D_MODEL, D_MODEL)),
                  _const_spec((1, D_MODEL))],
        out_specs=[pl.BlockSpec((tm, D_MODEL), lambda i: (i, 0)),
                   pl.BlockSpec((tm, D_MODEL), lambda i: (i, 0))],
        out_shape=[jax.ShapeDtypeStruct((rows, D_MODEL), _F32),
                   jax.ShapeDtypeStruct((rows, D_MODEL), _BF16)],
        compiler_params=pltpu.CompilerParams(dimension_semantics=("parallel",),
                                             vmem_limit_bytes=VMEM_LIMIT),
        name="mix",
    )(xn2d, proj2d, attn2d, h2d, w_gate, w_a, w_b, w_o, g_next.reshape(1, D_MODEL))


def _gelu_tanh(x):
    return 0.5 * x * (1.0 + jnp.tanh(np.sqrt(2.0 / np.pi).astype(np.float32) * (x + 0.044715 * (x * x * x))))


def _ffn_kernel(xp_ref, x_ref, xnx_ref, h_ref, wup_ref, cw_ref, cb_ref, wdn_ref, g_ref, *out_refs, final):
    tm = x_ref.shape[0]
    tiles_per_seq = SEQ // tm
    t = pl.program_id(0) % tiles_per_seq
    x = x_ref[...]
    xe = jnp.concatenate([xp_ref[...], x, xnx_ref[...]], axis=0)
    pos = lax.broadcasted_iota(jnp.int32, (tm, 1), 0) + t * tm
    keep_prev = pos > 0
    keep_next = pos < SEQ - 1
    n_ext = tm + 2 * HALO
    acc = None
    for c in range(D_FF // FF_CHUNK):
        cols = slice(c * FF_CHUNK, (c + 1) * FF_CHUNK)
        gcols = slice(D_FF + c * FF_CHUNK, D_FF + (c + 1) * FF_CHUNK)
        a_ext = jnp.dot(xe, wup_ref[:, cols], preferred_element_type=_F32)
        gate = jnp.dot(x, wup_ref[:, gcols], preferred_element_type=_F32)
        a_mid = a_ext[HALO:HALO + tm]
        a_prev = pltpu.roll(a_ext, 1, axis=0)[HALO:HALO + tm]
        a_next = pltpu.roll(a_ext, n_ext - 1, axis=0)[HALO:HALO + tm]
        cw = cw_ref[:, cols]
        conv = (jnp.where(keep_prev, a_prev, 0.0) * cw[0:1] + a_mid * cw[1:2]
                + jnp.where(keep_next, a_next, 0.0) * cw[2:3] + cb_ref[:, cols])
        act = (_gelu_tanh(conv) * gate).astype(_BF16)
        part = jnp.dot(act, wdn_ref[cols, :], preferred_element_type=_F32)
        acc = part if acc is None else acc + part
    h = h_ref[...] + acc
    if final:
        out_refs[0][...] = _rms(h, g_ref[...])
    else:
        out_refs[0][...] = h
        out_refs[1][...] = _rms(h, g_ref[...]).astype(out_refs[1].dtype)


def _ffn_call(xn2d, h2d, w_up, conv_w, conv_b, w_down, g_next, final):
    rows = xn2d.shape[0]
    tm = ROW_TILE
    per = tm // HALO
    last = rows // HALO - 1
    row_spec = pl.BlockSpec((tm, D_MODEL), lambda i: (i, 0))
    if final:
        out_specs = [row_spec]
        out_shape = [jax.ShapeDtypeStruct((rows, D_MODEL), _F32)]
    else:
        out_specs = [row_spec, row_spec]
        out_shape = [jax.ShapeDtypeStruct((rows, D_MODEL), _F32),
                     jax.ShapeDtypeStruct((rows, D_MODEL), _BF16)]
    return pl.pallas_call(
        functools.partial(_ffn_kernel, final=final),
        grid=(rows // tm,),
        in_specs=[pl.BlockSpec((HALO, D_MODEL), lambda i: (jnp.maximum(i * per - 1, 0), 0)),
                  row_spec,
                  pl.BlockSpec((HALO, D_MODEL), lambda i: (jnp.minimum((i + 1) * per, last), 0)),
                  row_spec,
                  _const_spec((D_MODEL, 2 * D_FF)),
                  _const_spec((3, D_FF)),
                  _const_spec((1, D_FF)),
                  _const_spec((D_FF, D_MODEL)),
                  _const_spec((1, D_MODEL))],
        out_specs=out_specs,
        out_shape=out_shape,
        compiler_params=pltpu.CompilerParams(dimension_semantics=("parallel",),
                                             vmem_limit_bytes=VMEM_LIMIT),
        name="ffn",
    )(xn2d, xn2d, xn2d, h2d, w_up, conv_w, conv_b.reshape(1, D_FF), w_down, g_next.reshape(1, D_MODEL))


def kernel(x, w_in, w_pool, pool_scale, w_a, w_b, w_o, norm1, norm2, w_up, conv_w, conv_b, w_down, rel_bias, norm_f):
    batch, seq, d = x.shape
    assert (seq, d) == (SEQ, D_MODEL)
    depth = w_in.shape[0]
    rows = batch * seq
    bias_tiles = [_bias_tiles(rel_bias, gi) for gi in range(3)]
    h = x.reshape(rows, d)
    xn = _norm_call(h, norm1[0])
    for layer in range(depth):
        w_in_l = w_in[layer].astype(_BF16)
        proj = _proj_call(xn.reshape(batch, seq, d), w_in_l[:, :PROJ_WIDTH],
                          w_pool[layer].astype(_BF16), pool_scale[layer])
        attn = _attn_call(proj, bias_tiles)
        h, xn = _mix_call(xn, proj.reshape(rows, PROJ_WIDTH), attn.reshape(rows, ATTN_OUT), h,
                          w_in_l[:, PROJ_WIDTH:], w_a[layer].astype(_BF16), w_b[layer].astype(_BF16),
                          w_o[layer].astype(_BF16), norm2[layer])
        final = layer == depth - 1
        g_next = norm_f if final else norm1[layer + 1]
        outs = _ffn_call(xn, h, w_up[layer].astype(_BF16), conv_w[layer], conv_b[layer],
                         w_down[layer].astype(_BF16), g_next, final)
        if final:
            return outs[0].reshape(batch, seq, d)
        h, xn = outs
```

```python
import functools

import numpy as np
import jax
import jax.numpy as jnp
from jax import lax
from jax.experimental import pallas as pl
from jax.experimental.pallas import tpu as pltpu

D_MODEL = 1024
SEQ = 2048
POOL_WINDOWS = (2, 4, 8, 16)
POOL_GROUP_DIM = 256
POOL_WIDTH = 1024
ATTN_DILATIONS = (1, 4, 16)
HALF_WINDOW = 64
HEADS_PER_GROUP = 4
N_HEADS = 12
HEAD_DIM = 128
ATTN_WIDTH = N_HEADS * HEAD_DIM
ATTN_OUT = HEADS_PER_GROUP * HEAD_DIM
NEG_INF = -1e30
N_BUCKETS = 32
MAX_DISTANCE = 1024
D_FF = 2816
EPS = 1e-6
PROJ_WIDTH = POOL_WIDTH + 3 * ATTN_WIDTH
GATE_WIDTH = 2 * D_MODEL

Q_TILE = 128
UNITS_PER_STEP = 4
PROJ_TN = 512
ROW_TILE = 512
HALO = 16
FF_CHUNK = 1408
VMEM_LIMIT = 56 * 1024 * 1024

_F32 = jnp.float32
_BF16 = jnp.bfloat16


def _rms(x, g):
    return x * lax.rsqrt(jnp.mean(x * x, axis=-1, keepdims=True) + EPS) * g


def _norm_kernel(x_ref, g_ref, o_ref):
    o_ref[...] = _rms(x_ref[...], g_ref[...]).astype(o_ref.dtype)


def _norm_call(x2d, g):
    rows = x2d.shape[0]
    tm = 1024
    return pl.pallas_call(
        _norm_kernel,
        grid=(rows // tm,),
        in_specs=[pl.BlockSpec((tm, D_MODEL), lambda i: (i, 0)),
                  pl.BlockSpec((1, D_MODEL), lambda i: (0, 0))],
        out_specs=pl.BlockSpec((tm, D_MODEL), lambda i: (i, 0)),
        out_shape=jax.ShapeDtypeStruct((rows, D_MODEL), _BF16),
        compiler_params=pltpu.CompilerParams(dimension_semantics=("parallel",)),
        name="rmsnorm",
    )(x2d, g.reshape(1, D_MODEL))


POOL_PAD = 16


def _window_sum(u, window):
    n = u.shape[0] + 2 * POOL_PAD
    pad = jnp.zeros((POOL_PAD, u.shape[1]), u.dtype)
    s = jnp.concatenate([pad, u, pad], axis=0)
    s = s + pltpu.roll(s, 1, axis=0)
    w = 2
    while w < window:
        s = pltpu.roll(s, w // 2, axis=0) + pltpu.roll(s, n - w // 2, axis=0)
        w *= 2
    return s[POOL_PAD:POOL_PAD + u.shape[0]]


def _proj_kernel(xn_ref, w_ref, wpool_ref, pscale_ref, o_ref, slab_ref):
    n = pl.program_id(1)
    y = jnp.dot(xn_ref[...], w_ref[...], preferred_element_type=_F32)

    def pool_tile(first_group):
        row = lax.broadcasted_iota(jnp.int32, (SEQ, 1), 0)
        for j in range(PROJ_TN // POOL_GROUP_DIM):
            gi = first_group + j
            window = POOL_WINDOWS[gi]
            u = y[:, j * POOL_GROUP_DIM:(j + 1) * POOL_GROUP_DIM]
            lo = jnp.maximum(row - window // 2, 0)
            hi = jnp.minimum(row - window // 2 + window, SEQ)
            pooled = _window_sum(u, window) / (hi - lo).astype(_F32) - u
            z = jnp.dot(pooled.astype(_BF16), wpool_ref[gi], preferred_element_type=_F32)
            z = z * pscale_ref[:, gi * POOL_GROUP_DIM:(gi + 1) * POOL_GROUP_DIM]
            o_ref[:, j * POOL_GROUP_DIM:(j + 1) * POOL_GROUP_DIM] = z.astype(o_ref.dtype)

    @pl.when(n == 0)
    def _():
        pool_tile(0)

    @pl.when(n == 1)
    def _():
        pool_tile(2)

    group = lax.rem(n + 1, 3)
    yq = y * jnp.where(n < 5, HEAD_DIM ** -0.5, 1.0).astype(_F32)

    @pl.when((n >= 2) & (group == 0))
    def _():
        o_ref[...] = yq.astype(o_ref.dtype)

    def permuted(dil):
        sub = SEQ // dil
        for h in range(HEADS_PER_GROUP):
            slab_ref[h] = yq[:, h * HEAD_DIM:(h + 1) * HEAD_DIM]
        for r in range(dil):
            for h in range(HEADS_PER_GROUP):
                o_ref[r * sub:(r + 1) * sub, h * HEAD_DIM:(h + 1) * HEAD_DIM] = (
                    slab_ref[h, pl.ds(r, sub, stride=dil), :].astype(o_ref.dtype))

    @pl.when((n >= 2) & (group == 1))
    def _():
        permuted(ATTN_DILATIONS[1])

    @pl.when((n >= 2) & (group == 2))
    def _():
        permuted(ATTN_DILATIONS[2])


def _proj_call(xn3, w_proj, w_pool, pool_scale):
    batch = xn3.shape[0]
    return pl.pallas_call(
        _proj_kernel,
        grid=(batch, PROJ_WIDTH // PROJ_TN),
        in_specs=[pl.BlockSpec((None, SEQ, D_MODEL), lambda b, n: (b, 0, 0)),
                  pl.BlockSpec((D_MODEL, PROJ_TN), lambda b, n: (0, n)),
                  pl.BlockSpec((len(POOL_WINDOWS), POOL_GROUP_DIM, POOL_GROUP_DIM), lambda b, n: (0, 0, 0)),
                  pl.BlockSpec((1, POOL_WIDTH), lambda b, n: (0, 0))],
        out_specs=pl.BlockSpec((None, SEQ, PROJ_TN), lambda b, n: (b, 0, n)),
        out_shape=jax.ShapeDtypeStruct((batch, SEQ, PROJ_WIDTH), _BF16),
        scratch_shapes=[pltpu.VMEM((HEADS_PER_GROUP, SEQ, HEAD_DIM), _F32)],
        compiler_params=pltpu.CompilerParams(dimension_semantics=("parallel", "arbitrary"),
                                             vmem_limit_bytes=VMEM_LIMIT),
        name="proj",
    )(xn3, w_proj, w_pool, pool_scale.reshape(1, POOL_WIDTH))


def _t5_buckets_np(rel):
    n = -rel
    half = N_BUCKETS // 2
    ret = (n < 0).astype(np.int32) * half
    n = np.abs(n)
    max_exact = half // 2
    large = max_exact + (np.log(np.maximum(n, 1) / max_exact)
                         / np.log(MAX_DISTANCE / max_exact) * (half - max_exact)).astype(np.int32)
    large = np.minimum(large, half - 1)
    return (ret + np.where(n < max_exact, n, large)).astype(np.int32)


def _key_span(dil):
    return min(Q_TILE + 2 * HALF_WINDOW, SEQ // dil)


BIAS_LANES = 512


def _bias_rows(rel_bias):
    rows = []
    for gi, dil in enumerate(ATTN_DILATIONS):
        buckets = _t5_buckets_np(dil * np.arange(-HALF_WINDOW, HALF_WINDOW + 1))
        bias = rel_bias[buckets][:, gi * HEADS_PER_GROUP:(gi + 1) * HEADS_PER_GROUP].T.astype(_F32)
        rows.append(jnp.pad(bias, ((0, 0), (0, BIAS_LANES - bias.shape[1])), constant_values=NEG_INF))
    return jnp.stack(rows, axis=1)


def _tile_deltas(dil):
    return (0, HALF_WINDOW, 2 * HALF_WINDOW) if SEQ // dil > Q_TILE else (0,)


def _attn_kernel(q0_ref, k0_ref, v0_ref, q1_ref, k1_ref, v1_ref, q2_ref, k2_ref, v2_ref,
                 e_ref, o_ref, t0_ref, t1_ref, t2_ref, acc_ref, m_ref, l_ref):
    groups = ((q0_ref, k0_ref, v0_ref, t0_ref), (q1_ref, k1_ref, v1_ref, t1_ref), (q2_ref, k2_ref, v2_ref, t2_ref))

    @pl.when(pl.program_id(1) == 0)
    def _():
        for gi, (_, _, _, t_ref) in enumerate(groups):
            dil = ATTN_DILATIONS[gi]
            row = jnp.broadcast_to(e_ref[gi:gi + 1, :], (Q_TILE, BIAS_LANES))
            for vi, delta in enumerate(_tile_deltas(dil)):
                skew = pltpu.roll(row, (delta - HALF_WINDOW) % BIAS_LANES, axis=1, stride=1, stride_axis=0)
                t_ref[vi] = skew[:, :_key_span(dil)]

    def scores(u, gi):
        q_ref, k_ref, _, t_ref = groups[gi]
        dil = ATTN_DILATIONS[gi]
        sub = SEQ // dil
        span = _key_span(dil)
        blocks = sub // Q_TILE
        r = u // blocks
        m0 = (u % blocks) * Q_TILE
        start = jnp.clip(m0 - HALF_WINDOW, 0, sub - span)
        variant = (m0 - start) // HALF_WINDOW
        q = q_ref[pl.ds(pl.multiple_of(u * Q_TILE, Q_TILE), Q_TILE), :]
        k0 = pl.multiple_of(r * sub + start, HALF_WINDOW)
        k = k_ref[pl.ds(k0, span), :]
        s = lax.dot_general(q, k, (((1,), (1,)), ((), ())), preferred_element_type=_F32)
        if dil == 1:
            rows = pl.ds(pl.multiple_of(m0, Q_TILE), Q_TILE)
        else:
            rows = pl.ds(m0 * dil + r, Q_TILE, stride=dil)
        return s + t_ref[variant], k0, rows

    def weights(s):
        m = jnp.max(s, axis=-1, keepdims=True)
        p = jnp.exp(s - m)
        return m, jnp.sum(p, axis=-1, keepdims=True), p.astype(_BF16)

    def values(gi, k0, rows, m, l, p):
        v = groups[gi][2][pl.ds(k0, p.shape[1]), :]
        acc_ref[gi, rows, :] = jnp.dot(p, v, preferred_element_type=_F32)
        m_ref[gi, rows, :] = jnp.broadcast_to(m, (Q_TILE, HEAD_DIM))
        l_ref[gi, rows, :] = jnp.broadcast_to(l, (Q_TILE, HEAD_DIM))

    def units(it, carry):
        todo = [(it * UNITS_PER_STEP + j, gi) for j in range(UNITS_PER_STEP) for gi in range(3)]
        scored = [scores(u, gi) for u, gi in todo]
        soft = [weights(s) for s, _, _ in scored]
        for (_, gi), (_, k0, rows), (m, l, p) in zip(todo, scored, soft):
            values(gi, k0, rows, m, l, p)
        return carry

    lax.fori_loop(0, SEQ // Q_TILE // UNITS_PER_STEP, units, 0)

    def combine(t, carry):
        rows = pl.ds(pl.multiple_of(t * Q_TILE, Q_TILE), Q_TILE)
        ms = [m_ref[gi, rows, :] for gi in range(3)]
        top = jnp.maximum(jnp.maximum(ms[0], ms[1]), ms[2])
        num = jnp.zeros((Q_TILE, HEAD_DIM), _F32)
        den = jnp.zeros((Q_TILE, HEAD_DIM), _F32)
        for gi in range(3):
            w = jnp.exp(ms[gi] - top)
            num = num + w * acc_ref[gi, rows, :]
            den = den + w * l_ref[gi, rows, :]
        o_ref[rows, :] = (num / den).astype(o_ref.dtype)
        return carry

    lax.fori_loop(0, SEQ // Q_TILE, combine, 0)


def _attn_call(proj, bias_rows):
    batch = proj.shape[0]
    col0 = POOL_WIDTH // HEAD_DIM

    def head_spec(which, gi):
        base = col0 + which * N_HEADS + gi * HEADS_PER_GROUP
        return pl.BlockSpec((None, SEQ, HEAD_DIM), lambda h, b: (b, 0, base + h))

    in_specs = [head_spec(which, gi) for gi in range(3) for which in range(3)]
    in_specs.append(pl.BlockSpec((None, 3, BIAS_LANES), lambda h, b: (h, 0, 0)))
    scratch = [pltpu.VMEM((len(_tile_deltas(dil)), Q_TILE, _key_span(dil)), _F32) for dil in ATTN_DILATIONS]
    scratch += [pltpu.VMEM((3, SEQ, HEAD_DIM), _F32) for _ in range(3)]
    return pl.pallas_call(
        _attn_kernel,
        grid=(HEADS_PER_GROUP, batch),
        in_specs=in_specs,
        out_specs=pl.BlockSpec((None, SEQ, HEAD_DIM), lambda h, b: (b, 0, h)),
        out_shape=jax.ShapeDtypeStruct((batch, SEQ, ATTN_OUT), _BF16),
        scratch_shapes=scratch,
        compiler_params=pltpu.CompilerParams(dimension_semantics=("arbitrary", "arbitrary"),
                                             vmem_limit_bytes=VMEM_LIMIT),
        name="attn",
    )(*([proj] * 9), bias_rows)


def _mix_kernel(xn_ref, pool_ref, attn_ref, h_ref, wg_ref, wa_ref, wb_ref, wo_ref, g_ref, h_out_ref, xn_out_ref):
    xn = xn_ref[...]
    gates = jnp.dot(xn, wg_ref[...], preferred_element_type=_F32)
    a = jnp.dot(pool_ref[...], wa_ref[...], preferred_element_type=_F32)
    b = jnp.dot(attn_ref[...], wb_ref[...], preferred_element_type=_F32)
    merged = jax.nn.sigmoid(gates[:, :D_MODEL]) * a + jax.nn.sigmoid(gates[:, D_MODEL:]) * b
    h = h_ref[...] + jnp.dot(merged.astype(_BF16), wo_ref[...], preferred_element_type=_F32)
    h_out_ref[...] = h
    xn_out_ref[...] = _rms(h, g_ref[...]).astype(xn_out_ref.dtype)


def _const_spec(shape):
    return pl.BlockSpec(shape, lambda i: (0,) * len(shape), pipeline_mode=pl.Buffered(1))


def _mix_call(xn2d, proj2d, attn2d, h2d, w_gate, w_a, w_b, w_o, g_next):
    rows = xn2d.shape[0]
    tm = ROW_TILE
    return pl.pallas_call(
        _mix_kernel,
        grid=(rows // tm,),
        in_specs=[pl.BlockSpec((tm, D_MODEL), lambda i: (i, 0)),
                  pl.BlockSpec((tm, POOL_WIDTH), lambda i: (i, 0)),
                  pl.BlockSpec((tm, ATTN_OUT), lambda i: (i, 0)),
                  pl.BlockSpec((tm, D_MODEL), lambda i: (i, 0)),
                  _const_spec((D_MODEL, GATE_WIDTH)),
                  _const_spec((POOL_WIDTH, D_MODEL)),
                  _const_spec((ATTN_OUT, D_MODEL)),
                  _const_spec((D_MODEL, D_MODEL)),
                  _const_spec((1, D_MODEL))],
        out_specs=[pl.BlockSpec((tm, D_MODEL), lambda i: (i, 0)),
                   pl.BlockSpec((tm, D_MODEL), lambda i: (i, 0))],
        out_shape=[jax.ShapeDtypeStruct((rows, D_MODEL), _F32),
                   jax.ShapeDtypeStruct((rows, D_MODEL), _BF16)],
        compiler_params=pltpu.CompilerParams(dimension_semantics=("parallel",),
                                             vmem_limit_bytes=VMEM_LIMIT),
        name="mix",
    )(xn2d, proj2d, attn2d, h2d, w_gate, w_a, w_b, w_o, g_next.reshape(1, D_MODEL))


def _gelu_tanh(x):
    return 0.5 * x * (1.0 + jnp.tanh(np.sqrt(2.0 / np.pi).astype(np.float32) * (x + 0.044715 * (x * x * x))))


def _ffn_kernel(xp_ref, x_ref, xnx_ref, h_ref, wup_ref, cw_ref, cb_ref, wdn_ref, g_ref, *out_refs, final):
    tm = x_ref.shape[0]
    tiles_per_seq = SEQ // tm
    t = pl.program_id(0) % tiles_per_seq
    x = x_ref[...]
    xe = jnp.concatenate([xp_ref[...], x, xnx_ref[...]], axis=0)
    pos = lax.broadcasted_iota(jnp.int32, (tm, 1), 0) + t * tm
    keep_prev = pos > 0
    keep_next = pos < SEQ - 1
    n_ext = tm + 2 * HALO
    acc = None
    for c in range(D_FF // FF_CHUNK):
        cols = slice(c * FF_CHUNK, (c + 1) * FF_CHUNK)
        gcols = slice(D_FF + c * FF_CHUNK, D_FF + (c + 1) * FF_CHUNK)
        a_ext = jnp.dot(xe, wup_ref[:, cols], preferred_element_type=_F32)
        gate = jnp.dot(x, wup_ref[:, gcols], preferred_element_type=_F32)
        a_mid = a_ext[HALO:HALO + tm]
        a_prev = pltpu.roll(a_ext, 1, axis=0)[HALO:HALO + tm]
        a_next = pltpu.roll(a_ext, n_ext - 1, axis=0)[HALO:HALO + tm]
        cw = cw_ref[:, cols]
        conv = (jnp.where(keep_prev, a_prev, 0.0) * cw[0:1] + a_mid * cw[1:2]
                + jnp.where(keep_next, a_next, 0.0) * cw[2:3] + cb_ref[:, cols])
        act = (_gelu_tanh(conv) * gate).astype(_BF16)
        part = jnp.dot(act, wdn_ref[cols, :], preferred_element_type=_F32)
        acc = part if acc is None else acc + part
    h = h_ref[...] + acc
    if final:
        out_refs[0][...] = _rms(h, g_ref[...])
    else:
        out_refs[0][...] = h
        out_refs[1][...] = _rms(h, g_ref[...]).astype(out_refs[1].dtype)


def _ffn_call(xn2d, h2d, w_up, conv_w, conv_b, w_down, g_next, final):
    rows = xn2d.shape[0]
    tm = ROW_TILE
    per = tm // HALO
    last = rows // HALO - 1
    row_spec = pl.BlockSpec((tm, D_MODEL), lambda i: (i, 0))
    if final:
        out_specs = [row_spec]
        out_shape = [jax.ShapeDtypeStruct((rows, D_MODEL), _F32)]
    else:
        out_specs = [row_spec, row_spec]
        out_shape = [jax.ShapeDtypeStruct((rows, D_MODEL), _F32),
                     jax.ShapeDtypeStruct((rows, D_MODEL), _BF16)]
    return pl.pallas_call(
        functools.partial(_ffn_kernel, final=final),
        grid=(rows // tm,),
        in_specs=[pl.BlockSpec((HALO, D_MODEL), lambda i: (jnp.maximum(i * per - 1, 0), 0)),
                  row_spec,
                  pl.BlockSpec((HALO, D_MODEL), lambda i: (jnp.minimum((i + 1) * per, last), 0)),
                  row_spec,
                  _const_spec((D_MODEL, 2 * D_FF)),
                  _const_spec((3, D_FF)),
                  _const_spec((1, D_FF)),
                  _const_spec((D_FF, D_MODEL)),
                  _const_spec((1, D_MODEL))],
        out_specs=out_specs,
        out_shape=out_shape,
        compiler_params=pltpu.CompilerParams(dimension_semantics=("parallel",),
                                             vmem_limit_bytes=VMEM_LIMIT),
        name="ffn",
    )(xn2d, xn2d, xn2d, h2d, w_up, conv_w, conv_b.reshape(1, D_FF), w_down, g_next.reshape(1, D_MODEL))


def kernel(x, w_in, w_pool, pool_scale, w_a, w_b, w_o, norm1, norm2, w_up, conv_w, conv_b, w_down, rel_bias, norm_f):
    batch, seq, d = x.shape
    assert (seq, d) == (SEQ, D_MODEL)
    depth = w_in.shape[0]
    rows = batch * seq
    bias_rows = _bias_rows(rel_bias)
    h = x.reshape(rows, d)
    xn = _norm_call(h, norm1[0])
    for layer in range(depth):
        w_in_l = w_in[layer].astype(_BF16)
        proj = _proj_call(xn.reshape(batch, seq, d), w_in_l[:, :PROJ_WIDTH],
                          w_pool[layer].astype(_BF16), pool_scale[layer])
        attn = _attn_call(proj, bias_rows)
        h, xn = _mix_call(xn, proj.reshape(rows, PROJ_WIDTH), attn.reshape(rows, ATTN_OUT), h,
                          w_in_l[:, PROJ_WIDTH:], w_a[layer].astype(_BF16), w_b[layer].astype(_BF16),
                          w_o[layer].astype(_BF16), norm2[layer])
        final = layer == depth - 1
        g_next = norm_f if final else norm1[layer + 1]
        outs = _ffn_call(xn, h, w_up[layer].astype(_BF16), conv_w[layer], conv_b[layer],
                         w_down[layer].astype(_BF16), g_next, final)
        if final:
            return outs[0].reshape(batch, seq, d)
        h, xn = outs
```

```python
import functools

import numpy as np
import jax
import jax.numpy as jnp
from jax import lax
from jax.experimental import pallas as pl
from jax.experimental.pallas import tpu as pltpu

D_MODEL = 1024
SEQ = 2048
POOL_WINDOWS = (2, 4, 8, 16)
POOL_GROUP_DIM = 256
POOL_WIDTH = 1024
ATTN_DILATIONS = (1, 4, 16)
HALF_WINDOW = 64
HEADS_PER_GROUP = 4
N_HEADS = 12
HEAD_DIM = 128
ATTN_WIDTH = N_HEADS * HEAD_DIM
ATTN_OUT = HEADS_PER_GROUP * HEAD_DIM
NEG_INF = -1e30
N_BUCKETS = 32
MAX_DISTANCE = 1024
D_FF = 2816
EPS = 1e-6
PROJ_WIDTH = POOL_WIDTH + 3 * ATTN_WIDTH
GATE_WIDTH = 2 * D_MODEL
IN_WIDTH = PROJ_WIDTH + GATE_WIDTH
GATE_BLOCK = 2560

Q_TILE = 128
UNITS_PER_STEP = 4
PROJ_TN = 512
ROW_TILE = 512
HALO = 16
FF_CHUNK = 1408
VMEM_LIMIT = 56 * 1024 * 1024
LANES = 128

_F32 = jnp.float32
_BF16 = jnp.bfloat16


def _rms(x, g):
    return x * lax.rsqrt(jnp.mean(x * x, axis=-1, keepdims=True) + EPS) * g


def _norm_kernel(x_ref, g_ref, o_ref):
    o_ref[...] = _rms(x_ref[...], g_ref[...]).astype(o_ref.dtype)


def _norm_call(x2d, g):
    rows = x2d.shape[0]
    tm = 1024
    return pl.pallas_call(
        _norm_kernel,
        grid=(rows // tm,),
        in_specs=[pl.BlockSpec((tm, D_MODEL), lambda i: (i, 0)),
                  pl.BlockSpec((1, D_MODEL), lambda i: (0, 0))],
        out_specs=pl.BlockSpec((tm, D_MODEL), lambda i: (i, 0)),
        out_shape=jax.ShapeDtypeStruct((rows, D_MODEL), _BF16),
        compiler_params=pltpu.CompilerParams(dimension_semantics=("parallel",)),
        name="rmsnorm",
    )(x2d, g.reshape(1, D_MODEL))


POOL_PAD = 16
POOL_EDGE = 8
assert POOL_EDGE >= max(POOL_WINDOWS) // 2 and POOL_PAD - POOL_EDGE >= max(POOL_WINDOWS) // 4


def _pool_kernel(xn_ref, w_ref, wpool_ref, pscale_ref, o_ref, u_ref, t_ref):
    n = SEQ + 2 * POOL_PAD
    slabs = POOL_GROUP_DIM // LANES

    def shifted(view, start, rows, k_back, k_fwd):
        return view[pl.ds(start - k_back, rows, stride=1), :] + view[pl.ds(start + k_fwd, rows, stride=1), :]

    def steps(w):
        return (1, 0) if w == 1 else (w // 2, w // 2)

    zeros = jnp.zeros((POOL_PAD, LANES), _F32)
    for t in range(2):
        for s in range(slabs):
            t_ref[t, s, 0:POOL_EDGE, :] = zeros[:POOL_EDGE]
            t_ref[t, s, n - POOL_EDGE:n, :] = zeros[:POOL_EDGE]
    xn = xn_ref[...]
    for gi in range(len(POOL_WINDOWS)):
        u = jnp.dot(xn, w_ref[:, gi * POOL_GROUP_DIM:(gi + 1) * POOL_GROUP_DIM], preferred_element_type=_F32)
        for s in range(slabs):
            u_ref[gi, s, 0:POOL_PAD, :] = zeros
            u_ref[gi, s, n - POOL_PAD:n, :] = zeros
            u_ref[gi, s, POOL_PAD:POOL_PAD + SEQ, :] = u[:, s * LANES:(s + 1) * LANES]
    edge_row = lax.broadcasted_iota(jnp.int32, (POOL_PAD, POOL_GROUP_DIM), 0)
    for gi, window in enumerate(POOL_WINDOWS):
        cols = slice(gi * POOL_GROUP_DIM, (gi + 1) * POOL_GROUP_DIM)
        totals = []
        for s in range(slabs):
            src, w, slot = u_ref.at[gi, s], 1, 0
            while 2 * w < window:
                t_ref[slot, s, pl.ds(POOL_EDGE, n - 2 * POOL_EDGE), :] = shifted(src, POOL_EDGE, n - 2 * POOL_EDGE,
                                                                              *steps(w))
                src, w, slot = t_ref.at[slot, s], 2 * w, 1 - slot
            totals.append(shifted(src, POOL_PAD, SEQ, *steps(w)))
        total = jnp.concatenate(totals, axis=1)
        u = jnp.concatenate([u_ref[gi, s, POOL_PAD:POOL_PAD + SEQ, :] for s in range(slabs)], axis=1)
        wpool = wpool_ref[gi]
        scale = pscale_ref[:, cols]
        pooled = total * (1.0 / window) - u
        z = jnp.dot(pooled.astype(_BF16), wpool, preferred_element_type=_F32)
        o_ref[:, cols] = (z * scale).astype(o_ref.dtype)
        for r0 in (0, SEQ - POOL_PAD):
            pos = edge_row + r0
            size = (jnp.minimum(pos + window // 2, SEQ) - jnp.maximum(pos - window // 2, 0)).astype(_F32)
            pooled = total[r0:r0 + POOL_PAD] / size - u[r0:r0 + POOL_PAD]
            z = jnp.dot(pooled.astype(_BF16), wpool, preferred_element_type=_F32)
            o_ref[r0:r0 + POOL_PAD, cols] = (z * scale).astype(o_ref.dtype)


def _pool_call(xn3, w_in, w_pool, pool_scale):
    batch = xn3.shape[0]
    return pl.pallas_call(
        _pool_kernel,
        grid=(batch,),
        in_specs=[pl.BlockSpec((None, SEQ, D_MODEL), lambda b: (b, 0, 0)),
                  pl.BlockSpec((D_MODEL, POOL_WIDTH), lambda b: (0, 0), pipeline_mode=pl.Buffered(1)),
                  _const_spec((len(POOL_WINDOWS), POOL_GROUP_DIM, POOL_GROUP_DIM)),
                  _const_spec((1, POOL_WIDTH))],
        out_specs=pl.BlockSpec((None, SEQ, POOL_WIDTH), lambda b: (b, 0, 0)),
        out_shape=jax.ShapeDtypeStruct((batch, SEQ, POOL_WIDTH), _BF16),
        scratch_shapes=[pltpu.VMEM((len(POOL_WINDOWS), POOL_GROUP_DIM // LANES, SEQ + 2 * POOL_PAD, LANES), _F32),
                        pltpu.VMEM((2, POOL_GROUP_DIM // LANES, SEQ + 2 * POOL_PAD, LANES), _F32)],
        compiler_params=pltpu.CompilerParams(dimension_semantics=("parallel",),
                                             vmem_limit_bytes=VMEM_LIMIT),
        name="pool",
    )(xn3, w_in, w_pool, pool_scale.reshape(1, POOL_WIDTH))


def _qkv_kernel(x_ref, w_ref, o_ref, *scratch, dil):
    n = pl.program_id(1)
    if dil == 1:
        x = x_ref[...]
    else:
        xp_ref, = scratch
        sub = SEQ // dil

        @pl.when(n == 0)
        def _():
            for r in range(dil):
                xp_ref[r * sub:(r + 1) * sub, :] = x_ref[:, r * D_MODEL:(r + 1) * D_MODEL]

        x = xp_ref[...]
    y = jnp.dot(x, w_ref[...], preferred_element_type=_F32)
    y = y * jnp.where(n == 0, HEAD_DIM ** -0.5, 1.0).astype(_F32)
    o_ref[...] = y.astype(o_ref.dtype)


def _qkv_call(xn3, w_in, gi):
    batch = xn3.shape[0]
    dil = ATTN_DILATIONS[gi]
    first = POOL_WIDTH // PROJ_TN + gi
    return pl.pallas_call(
        functools.partial(_qkv_kernel, dil=dil),
        grid=(batch, 3),
        in_specs=[pl.BlockSpec((None, SEQ // dil, dil * D_MODEL), lambda b, n: (b, 0, 0)),
                  pl.BlockSpec((D_MODEL, PROJ_TN), lambda b, n: (0, first + 3 * n))],
        out_specs=pl.BlockSpec((None, SEQ, PROJ_TN), lambda b, n: (b, 0, n)),
        out_shape=jax.ShapeDtypeStruct((batch, SEQ, 3 * PROJ_TN), _BF16),
        scratch_shapes=[] if dil == 1 else [pltpu.VMEM((SEQ, D_MODEL), _BF16)],
        compiler_params=pltpu.CompilerParams(dimension_semantics=("parallel", "arbitrary"),
                                             vmem_limit_bytes=VMEM_LIMIT),
        name=f"qkv{gi}",
    )(xn3.reshape(batch, SEQ // dil, dil * D_MODEL), w_in)


def _t5_buckets_np(rel):
    n = -rel
    half = N_BUCKETS // 2
    ret = (n < 0).astype(np.int32) * half
    n = np.abs(n)
    max_exact = half // 2
    large = max_exact + (np.log(np.maximum(n, 1) / max_exact)
                         / np.log(MAX_DISTANCE / max_exact) * (half - max_exact)).astype(np.int32)
    large = np.minimum(large, half - 1)
    return (ret + np.where(n < max_exact, n, large)).astype(np.int32)


def _key_span(dil):
    return min(Q_TILE + 2 * HALF_WINDOW, SEQ // dil)


BIAS_LANES = 512


def _bias_rows(rel_bias):
    rows = []
    for gi, dil in enumerate(ATTN_DILATIONS):
        buckets = _t5_buckets_np(dil * np.arange(-HALF_WINDOW, HALF_WINDOW + 1))
        bias = rel_bias[buckets][:, gi * HEADS_PER_GROUP:(gi + 1) * HEADS_PER_GROUP].T.astype(_F32)
        rows.append(jnp.pad(bias, ((0, 0), (0, BIAS_LANES - bias.shape[1])), constant_values=NEG_INF))
    return jnp.stack(rows, axis=1)


def _tile_deltas(dil):
    return (0, HALF_WINDOW, 2 * HALF_WINDOW) if SEQ // dil > Q_TILE else (0,)


def _attn_kernel(q0_ref, k0_ref, v0_ref, q1_ref, k1_ref, v1_ref, q2_ref, k2_ref, v2_ref,
                 e_ref, o_ref, t0_ref, t1_ref, t2_ref, acc_ref, m_ref, l_ref):
    groups = ((q0_ref, k0_ref, v0_ref, t0_ref), (q1_ref, k1_ref, v1_ref, t1_ref), (q2_ref, k2_ref, v2_ref, t2_ref))

    @pl.when(pl.program_id(1) == 0)
    def _():
        for gi, (_, _, _, t_ref) in enumerate(groups):
            dil = ATTN_DILATIONS[gi]
            row = jnp.broadcast_to(e_ref[gi:gi + 1, :], (Q_TILE, BIAS_LANES))
            for vi, delta in enumerate(_tile_deltas(dil)):
                skew = pltpu.roll(row, (delta - HALF_WINDOW) % BIAS_LANES, axis=1, stride=1, stride_axis=0)
                t_ref[vi] = skew[:, :_key_span(dil)]

    def scores(u, gi):
        q_ref, k_ref, _, t_ref = groups[gi]
        dil = ATTN_DILATIONS[gi]
        sub = SEQ // dil
        span = _key_span(dil)
        blocks = sub // Q_TILE
        r = u // blocks
        m0 = (u % blocks) * Q_TILE
        start = jnp.clip(m0 - HALF_WINDOW, 0, sub - span)
        variant = (m0 - start) // HALF_WINDOW
        q = q_ref[pl.ds(pl.multiple_of(u * Q_TILE, Q_TILE), Q_TILE), :]
        k0 = pl.multiple_of(r * sub + start, HALF_WINDOW)
        k = k_ref[pl.ds(k0, span), :]
        s = lax.dot_general(q, k, (((1,), (1,)), ((), ())), preferred_element_type=_F32)
        if dil == 1:
            rows = pl.ds(pl.multiple_of(m0, Q_TILE), Q_TILE)
        else:
            rows = pl.ds(m0 * dil + r, Q_TILE, stride=dil)
        return s + t_ref[variant], k0, rows

    def weights(s):
        m = jnp.max(s, axis=-1, keepdims=True)
        p = jnp.exp(s - m)
        return m, jnp.sum(p, axis=-1, keepdims=True), p.astype(_BF16)

    def values(gi, k0, rows, m, l, p):
        v = groups[gi][2][pl.ds(k0, p.shape[1]), :]
        acc_ref[gi, rows, :] = jnp.dot(p, v, preferred_element_type=_F32)
        m_ref[gi, rows, :] = jnp.broadcast_to(m, (Q_TILE, HEAD_DIM))
        l_ref[gi, rows, :] = jnp.broadcast_to(l, (Q_TILE, HEAD_DIM))

    def units(it, carry):
        todo = [(it * UNITS_PER_STEP + j, gi) for j in range(UNITS_PER_STEP) for gi in range(3)]
        scored = [scores(u, gi) for u, gi in todo]
        soft = [weights(s) for s, _, _ in scored]
        for (_, gi), (_, k0, rows), (m, l, p) in zip(todo, scored, soft):
            values(gi, k0, rows, m, l, p)
        return carry

    lax.fori_loop(0, SEQ // Q_TILE // UNITS_PER_STEP, units, 0)

    def combine(t, carry):
        rows = pl.ds(pl.multiple_of(t * Q_TILE, Q_TILE), Q_TILE)
        ms = [m_ref[gi, rows, :] for gi in range(3)]
        top = jnp.maximum(jnp.maximum(ms[0], ms[1]), ms[2])
        num = jnp.zeros((Q_TILE, HEAD_DIM), _F32)
        den = jnp.zeros((Q_TILE, HEAD_DIM), _F32)
        for gi in range(3):
            w = jnp.exp(ms[gi] - top)
            num = num + w * acc_ref[gi, rows, :]
            den = den + w * l_ref[gi, rows, :]
        o_ref[rows, :] = (num / den).astype(o_ref.dtype)
        return carry

    lax.fori_loop(0, SEQ // Q_TILE, combine, 0)


def _attn_call(qkv, bias_rows):
    batch = qkv[0].shape[0]

    def head_spec(which):
        return pl.BlockSpec((None, SEQ, HEAD_DIM), lambda h, b: (b, 0, which * HEADS_PER_GROUP + h))

    in_specs = [head_spec(which) for gi in range(3) for which in range(3)]
    in_specs.append(pl.BlockSpec((None, 3, BIAS_LANES), lambda h, b: (h, 0, 0)))
    scratch = [pltpu.VMEM((len(_tile_deltas(dil)), Q_TILE, _key_span(dil)), _F32) for dil in ATTN_DILATIONS]
    scratch += [pltpu.VMEM((3, SEQ, HEAD_DIM), _F32) for _ in range(3)]
    return pl.pallas_call(
        _attn_kernel,
        grid=(HEADS_PER_GROUP, batch),
        in_specs=in_specs,
        out_specs=pl.BlockSpec((None, SEQ, HEAD_DIM), lambda h, b: (b, 0, h)),
        out_shape=jax.ShapeDtypeStruct((batch, SEQ, ATTN_OUT), _BF16),
        scratch_shapes=scratch,
        compiler_params=pltpu.CompilerParams(dimension_semantics=("arbitrary", "arbitrary"),
                                             vmem_limit_bytes=VMEM_LIMIT),
        name="attn",
    )(*[qkv[gi] for gi in range(3) for _ in range(3)], bias_rows)


def _mix_kernel(xn_ref, pool_ref, attn_ref, h_ref, wg_ref, wa_ref, wb_ref, wo_ref, g_ref, h_out_ref, xn_out_ref):
    xn = xn_ref[...]
    gates = jnp.dot(xn, wg_ref[:, GATE_BLOCK - GATE_WIDTH:], preferred_element_type=_F32)
    a = jnp.dot(pool_ref[...], wa_ref[...], preferred_element_type=_F32)
    b = jnp.dot(attn_ref[...], wb_ref[...], preferred_element_type=_F32)
    merged = jax.nn.sigmoid(gates[:, :D_MODEL]) * a + jax.nn.sigmoid(gates[:, D_MODEL:]) * b
    h = h_ref[...] + jnp.dot(merged.astype(_BF16), wo_ref[...], preferred_element_type=_F32)
    h_out_ref[...] = h
    xn_out_ref[...] = _rms(h, g_ref[...]).astype(xn_out_ref.dtype)


def _const_spec(shape):
    return pl.BlockSpec(shape, lambda i: (0,) * len(shape), pipeline_mode=pl.Buffered(1))


def _mix_call(xn2d, pool2d, attn2d, h2d, w_in, w_a, w_b, w_o, g_next):
    rows = xn2d.shape[0]
    tm = ROW_TILE
    assert IN_WIDTH % GATE_BLOCK == 0 and GATE_BLOCK >= GATE_WIDTH
    return pl.pallas_call(
        _mix_kernel,
        grid=(rows // tm,),
        in_specs=[pl.BlockSpec((tm, D_MODEL), lambda i: (i, 0)),
                  pl.BlockSpec((tm, POOL_WIDTH), lambda i: (i, 0)),
                  pl.BlockSpec((tm, ATTN_OUT), lambda i: (i, 0)),
                  pl.BlockSpec((tm, D_MODEL), lambda i: (i, 0)),
                  pl.BlockSpec((D_MODEL, GATE_BLOCK), lambda i: (0, IN_WIDTH // GATE_BLOCK - 1),
                               pipeline_mode=pl.Buffered(1)),
                  _const_spec((POOL_WIDTH, D_MODEL)),
                  _const_spec((ATTN_OUT, D_MODEL)),
                  _const_spec((D_MODEL, D_MODEL)),
                  _const_spec((1, D_MODEL))],
        out_specs=[pl.BlockSpec((tm, D_MODEL), lambda i: (i, 0)),
                   pl.BlockSpec((tm, D_MODEL), lambda i: (i, 0))],
        out_shape=[jax.ShapeDtypeStruct((rows, D_MODEL), _F32),
                   jax.ShapeDtypeStruct((rows, D_MODEL), _BF16)],
        compiler_params=pltpu.CompilerParams(dimension_semantics=("parallel",),
                                             vmem_limit_bytes=VMEM_LIMIT),
        name="mix",
    )(xn2d, pool2d, attn2d, h2d, w_in, w_a, w_b, w_o, g_next.reshape(1, D_MODEL))


def _gelu_tanh(x):
    return 0.5 * x * (1.0 + jnp.tanh(np.sqrt(2.0 / np.pi).astype(np.float32) * (x + 0.044715 * (x * x * x))))


def _ffn_kernel(xp_ref, x_ref, xnx_ref, h_ref, wup_ref, cw_ref, cb_ref, wdn_ref, g_ref, *out_refs, final):
    tm = x_ref.shape[0]
    tiles_per_seq = SEQ // tm
    t = pl.program_id(0) % tiles_per_seq
    x = x_ref[...]
    xe = jnp.concatenate([xp_ref[...], x, xnx_ref[...]], axis=0)
    pos = lax.broadcasted_iota(jnp.int32, (tm, 1), 0) + t * tm
    keep_prev = pos > 0
    keep_next = pos < SEQ - 1
    n_ext = tm + 2 * HALO
    acc = None
    for c in range(D_FF // FF_CHUNK):
        cols = slice(c * FF_CHUNK, (c + 1) * FF_CHUNK)
        gcols = slice(D_FF + c * FF_CHUNK, D_FF + (c + 1) * FF_CHUNK)
        a_ext = jnp.dot(xe, wup_ref[:, cols], preferred_element_type=_F32)
        gate = jnp.dot(x, wup_ref[:, gcols], preferred_element_type=_F32)
        a_mid = a_ext[HALO:HALO + tm]
        a_prev = pltpu.roll(a_ext, 1, axis=0)[HALO:HALO + tm]
        a_next = pltpu.roll(a_ext, n_ext - 1, axis=0)[HALO:HALO + tm]
        cw = cw_ref[:, cols]
        conv = (jnp.where(keep_prev, a_prev, 0.0) * cw[0:1] + a_mid * cw[1:2]
                + jnp.where(keep_next, a_next, 0.0) * cw[2:3] + cb_ref[:, cols])
        act = (_gelu_tanh(conv) * gate).astype(_BF16)
        part = jnp.dot(act, wdn_ref[cols, :], preferred_element_type=_F32)
        acc = part if acc is None else acc + part
    h = h_ref[...] + acc
    if final:
        out_refs[0][...] = _rms(h, g_ref[...])
    else:
        out_refs[0][...] = h
        out_refs[1][...] = _rms(h, g_ref[...]).astype(out_refs[1].dtype)


def _ffn_call(xn2d, h2d, w_up, conv_w, conv_b, w_down, g_next, final):
    rows = xn2d.shape[0]
    tm = ROW_TILE
    per = tm // HALO
    last = rows // HALO - 1
    row_spec = pl.BlockSpec((tm, D_MODEL), lambda i: (i, 0))
    if final:
        out_specs = [row_spec]
        out_shape = [jax.ShapeDtypeStruct((rows, D_MODEL), _F32)]
    else:
        out_specs = [row_spec, row_spec]
        out_shape = [jax.ShapeDtypeStruct((rows, D_MODEL), _F32),
                     jax.ShapeDtypeStruct((rows, D_MODEL), _BF16)]
    return pl.pallas_call(
        functools.partial(_ffn_kernel, final=final),
        grid=(rows // tm,),
        in_specs=[pl.BlockSpec((HALO, D_MODEL), lambda i: (jnp.maximum(i * per - 1, 0), 0)),
                  row_spec,
                  pl.BlockSpec((HALO, D_MODEL), lambda i: (jnp.minimum((i + 1) * per, last), 0)),
                  row_spec,
                  _const_spec((D_MODEL, 2 * D_FF)),
                  _const_spec((3, D_FF)),
                  _const_spec((1, D_FF)),
                  _const_spec((D_FF, D_MODEL)),
                  _const_spec((1, D_MODEL))],
        out_specs=out_specs,
        out_shape=out_shape,
        compiler_params=pltpu.CompilerParams(dimension_semantics=("parallel",),
                                             vmem_limit_bytes=VMEM_LIMIT),
        name="ffn",
    )(xn2d, xn2d, xn2d, h2d, w_up, conv_w, conv_b.reshape(1, D_FF), w_down, g_next.reshape(1, D_MODEL))


def kernel(x, w_in, w_pool, pool_scale, w_a, w_b, w_o, norm1, norm2, w_up, conv_w, conv_b, w_down, rel_bias, norm_f):
    batch, seq, d = x.shape
    assert (seq, d) == (SEQ, D_MODEL)
    depth = w_in.shape[0]
    rows = batch * seq
    bias_rows = _bias_rows(rel_bias)
    h = x.reshape(rows, d)
    xn = _norm_call(h, norm1[0])
    for layer in range(depth):
        w_in_l = w_in[layer].astype(_BF16)
        xn3 = xn.reshape(batch, seq, d)
        pool = _pool_call(xn3, w_in_l, w_pool[layer].astype(_BF16), pool_scale[layer])
        attn = _attn_call([_qkv_call(xn3, w_in_l, gi) for gi in range(3)], bias_rows)
        h, xn = _mix_call(xn, pool.reshape(rows, POOL_WIDTH), attn.reshape(rows, ATTN_OUT), h,
                          w_in_l, w_a[layer].astype(_BF16), w_b[layer].astype(_BF16),
                          w_o[layer].astype(_BF16), norm2[layer])
        final = layer == depth - 1
        g_next = norm_f if final else norm1[layer + 1]
        outs = _ffn_call(xn, h, w_up[layer].astype(_BF16), conv_w[layer], conv_b[layer],
                         w_down[layer].astype(_BF16), g_next, final)
        if final:
            return outs[0].reshape(batch, seq, d)
        h, xn = outs
```

```python
import functools

import numpy as np
import jax
import jax.numpy as jnp
from jax import lax
from jax.experimental import pallas as pl
from jax.experimental.pallas import tpu as pltpu

D_MODEL = 1024
SEQ = 2048
POOL_WINDOWS = (2, 4, 8, 16)
POOL_GROUP_DIM = 256
POOL_WIDTH = 1024
ATTN_DILATIONS = (1, 4, 16)
HALF_WINDOW = 64
HEADS_PER_GROUP = 4
N_HEADS = 12
HEAD_DIM = 128
ATTN_WIDTH = N_HEADS * HEAD_DIM
ATTN_OUT = HEADS_PER_GROUP * HEAD_DIM
NEG_INF = -1e30
N_BUCKETS = 32
MAX_DISTANCE = 1024
D_FF = 2816
EPS = 1e-6
PROJ_WIDTH = POOL_WIDTH + 3 * ATTN_WIDTH
GATE_WIDTH = 2 * D_MODEL
IN_WIDTH = PROJ_WIDTH + GATE_WIDTH
GATE_BLOCK = 2560

Q_TILE = 128
UNITS_PER_STEP = 4
PROJ_TN = 512
ROW_TILE = 512
HALO = 16
FF_CHUNK = 1408
VMEM_LIMIT = 56 * 1024 * 1024
LANES = 128
REGROUP_TILE = 256

_F32 = jnp.float32
_BF16 = jnp.bfloat16


def _rms(x, g):
    return x * lax.rsqrt(jnp.mean(x * x, axis=-1, keepdims=True) + EPS) * g


def _norm_kernel(x_ref, g_ref, o_ref):
    o_ref[...] = _rms(x_ref[...], g_ref[...]).astype(o_ref.dtype)


def _norm_call(x2d, g):
    rows = x2d.shape[0]
    tm = 1024
    return pl.pallas_call(
        _norm_kernel,
        grid=(rows // tm,),
        in_specs=[pl.BlockSpec((tm, D_MODEL), lambda i: (i, 0)),
                  pl.BlockSpec((1, D_MODEL), lambda i: (0, 0))],
        out_specs=pl.BlockSpec((tm, D_MODEL), lambda i: (i, 0)),
        out_shape=jax.ShapeDtypeStruct((rows, D_MODEL), _BF16),
        compiler_params=pltpu.CompilerParams(dimension_semantics=("parallel",)),
        name="rmsnorm",
    )(x2d, g.reshape(1, D_MODEL))


POOL_PAD = 16
POOL_EDGE = 8
assert POOL_EDGE >= max(POOL_WINDOWS) // 2 and POOL_PAD - POOL_EDGE >= max(POOL_WINDOWS) // 4


def _pool_kernel(xn_ref, w_ref, wpool_ref, pscale_ref, o_ref, u_ref, t_ref):
    n = SEQ + 2 * POOL_PAD
    slabs = POOL_GROUP_DIM // LANES

    def shifted(view, start, rows, k_back, k_fwd):
        return view[pl.ds(start - k_back, rows, stride=1), :] + view[pl.ds(start + k_fwd, rows, stride=1), :]

    def steps(w):
        return (1, 0) if w == 1 else (w // 2, w // 2)

    zeros = jnp.zeros((POOL_PAD, LANES), _F32)
    for t in range(2):
        for s in range(slabs):
            t_ref[t, s, 0:POOL_EDGE, :] = zeros[:POOL_EDGE]
            t_ref[t, s, n - POOL_EDGE:n, :] = zeros[:POOL_EDGE]
    xn = xn_ref[...]
    for gi in range(len(POOL_WINDOWS)):
        u = jnp.dot(xn, w_ref[:, gi * POOL_GROUP_DIM:(gi + 1) * POOL_GROUP_DIM], preferred_element_type=_F32)
        for s in range(slabs):
            u_ref[gi, s, 0:POOL_PAD, :] = zeros
            u_ref[gi, s, n - POOL_PAD:n, :] = zeros
            u_ref[gi, s, POOL_PAD:POOL_PAD + SEQ, :] = u[:, s * LANES:(s + 1) * LANES]
    edge_row = lax.broadcasted_iota(jnp.int32, (POOL_PAD, POOL_GROUP_DIM), 0)
    for gi, window in enumerate(POOL_WINDOWS):
        cols = slice(gi * POOL_GROUP_DIM, (gi + 1) * POOL_GROUP_DIM)
        totals = []
        for s in range(slabs):
            src, w, slot = u_ref.at[gi, s], 1, 0
            while 2 * w < window:
                t_ref[slot, s, pl.ds(POOL_EDGE, n - 2 * POOL_EDGE), :] = shifted(src, POOL_EDGE, n - 2 * POOL_EDGE,
                                                                              *steps(w))
                src, w, slot = t_ref.at[slot, s], 2 * w, 1 - slot
            totals.append(shifted(src, POOL_PAD, SEQ, *steps(w)))
        total = jnp.concatenate(totals, axis=1)
        u = jnp.concatenate([u_ref[gi, s, POOL_PAD:POOL_PAD + SEQ, :] for s in range(slabs)], axis=1)
        wpool = wpool_ref[gi]
        scale = pscale_ref[:, cols]
        pooled = total * (1.0 / window) - u
        z = jnp.dot(pooled.astype(_BF16), wpool, preferred_element_type=_F32)
        o_ref[:, cols] = (z * scale).astype(o_ref.dtype)
        for r0 in (0, SEQ - POOL_PAD):
            pos = edge_row + r0
            size = (jnp.minimum(pos + window // 2, SEQ) - jnp.maximum(pos - window // 2, 0)).astype(_F32)
            pooled = total[r0:r0 + POOL_PAD] / size - u[r0:r0 + POOL_PAD]
            z = jnp.dot(pooled.astype(_BF16), wpool, preferred_element_type=_F32)
            o_ref[r0:r0 + POOL_PAD, cols] = (z * scale).astype(o_ref.dtype)


def _pool_call(xn3, w_in, w_pool, pool_scale):
    batch = xn3.shape[0]
    return pl.pallas_call(
        _pool_kernel,
        grid=(batch,),
        in_specs=[pl.BlockSpec((None, SEQ, D_MODEL), lambda b: (b, 0, 0)),
                  pl.BlockSpec((D_MODEL, POOL_WIDTH), lambda b: (0, 0), pipeline_mode=pl.Buffered(1)),
                  _const_spec((len(POOL_WINDOWS), POOL_GROUP_DIM, POOL_GROUP_DIM)),
                  _const_spec((1, POOL_WIDTH))],
        out_specs=pl.BlockSpec((None, SEQ, POOL_WIDTH), lambda b: (b, 0, 0)),
        out_shape=jax.ShapeDtypeStruct((batch, SEQ, POOL_WIDTH), _BF16),
        scratch_shapes=[pltpu.VMEM((len(POOL_WINDOWS), POOL_GROUP_DIM // LANES, SEQ + 2 * POOL_PAD, LANES), _F32),
                        pltpu.VMEM((2, POOL_GROUP_DIM // LANES, SEQ + 2 * POOL_PAD, LANES), _F32)],
        compiler_params=pltpu.CompilerParams(dimension_semantics=("parallel",),
                                             vmem_limit_bytes=VMEM_LIMIT),
        name="pool",
    )(xn3, w_in, w_pool, pool_scale.reshape(1, POOL_WIDTH))


def _qkv_kernel(x_ref, w_ref, o_ref, *scratch, dil):
    n = pl.program_id(1)
    if dil == 1:
        x = x_ref[...]
    else:
        xp_ref, = scratch
        sub = SEQ // dil
        sub_t = REGROUP_TILE // dil

        @pl.when(n == 0)
        def _():
            i = lax.broadcasted_iota(jnp.int32, (REGROUP_TILE, REGROUP_TILE), 0)
            j = lax.broadcasted_iota(jnp.int32, (REGROUP_TILE, REGROUP_TILE), 1)
            pick = (j == (i % sub_t) * dil + i // sub_t).astype(_BF16)
            for t in range(SEQ // REGROUP_TILE):
                tile = x_ref[t * REGROUP_TILE:(t + 1) * REGROUP_TILE, :]
                srt = jnp.dot(pick, tile, preferred_element_type=_F32).astype(_BF16)
                for r in range(dil):
                    xp_ref[r * sub + t * sub_t:r * sub + (t + 1) * sub_t, :] = srt[r * sub_t:(r + 1) * sub_t]

        x = xp_ref[...]
    y = jnp.dot(x, w_ref[...], preferred_element_type=_F32)
    y = y * jnp.where(n == 0, HEAD_DIM ** -0.5, 1.0).astype(_F32)
    o_ref[...] = y.astype(o_ref.dtype)


def _qkv_call(xn3, w_in, gi):
    batch = xn3.shape[0]
    dil = ATTN_DILATIONS[gi]
    first = POOL_WIDTH // PROJ_TN + gi
    return pl.pallas_call(
        functools.partial(_qkv_kernel, dil=dil),
        grid=(batch, 3),
        in_specs=[pl.BlockSpec((None, SEQ, D_MODEL), lambda b, n: (b, 0, 0)),
                  pl.BlockSpec((D_MODEL, PROJ_TN), lambda b, n: (0, first + 3 * n))],
        out_specs=pl.BlockSpec((None, SEQ, PROJ_TN), lambda b, n: (b, 0, n)),
        out_shape=jax.ShapeDtypeStruct((batch, SEQ, 3 * PROJ_TN), _BF16),
        scratch_shapes=[] if dil == 1 else [pltpu.VMEM((SEQ, D_MODEL), _BF16)],
        compiler_params=pltpu.CompilerParams(dimension_semantics=("parallel", "arbitrary"),
                                             vmem_limit_bytes=VMEM_LIMIT),
        name=f"qkv{gi}",
    )(xn3, w_in)


def _t5_buckets_np(rel):
    n = -rel
    half = N_BUCKETS // 2
    ret = (n < 0).astype(np.int32) * half
    n = np.abs(n)
    max_exact = half // 2
    large = max_exact + (np.log(np.maximum(n, 1) / max_exact)
                         / np.log(MAX_DISTANCE / max_exact) * (half - max_exact)).astype(np.int32)
    large = np.minimum(large, half - 1)
    return (ret + np.where(n < max_exact, n, large)).astype(np.int32)


def _key_span(dil):
    return min(Q_TILE + 2 * HALF_WINDOW, SEQ // dil)


BIAS_LANES = 512


def _bias_rows(rel_bias):
    rows = []
    for gi, dil in enumerate(ATTN_DILATIONS):
        buckets = _t5_buckets_np(dil * np.arange(-HALF_WINDOW, HALF_WINDOW + 1))
        bias = rel_bias[buckets][:, gi * HEADS_PER_GROUP:(gi + 1) * HEADS_PER_GROUP].T.astype(_F32)
        rows.append(jnp.pad(bias, ((0, 0), (0, BIAS_LANES - bias.shape[1])), constant_values=NEG_INF))
    return jnp.stack(rows, axis=1)


def _tile_deltas(dil):
    return (0, HALF_WINDOW, 2 * HALF_WINDOW) if SEQ // dil > Q_TILE else (0,)


def _attn_kernel(q0_ref, k0_ref, v0_ref, q1_ref, k1_ref, v1_ref, q2_ref, k2_ref, v2_ref,
                 e_ref, o_ref, t0_ref, t1_ref, t2_ref, acc_ref, m_ref, l_ref):
    groups = ((q0_ref, k0_ref, v0_ref, t0_ref), (q1_ref, k1_ref, v1_ref, t1_ref), (q2_ref, k2_ref, v2_ref, t2_ref))

    @pl.when(pl.program_id(1) == 0)
    def _():
        for gi, (_, _, _, t_ref) in enumerate(groups):
            dil = ATTN_DILATIONS[gi]
            row = jnp.broadcast_to(e_ref[gi:gi + 1, :], (Q_TILE, BIAS_LANES))
            for vi, delta in enumerate(_tile_deltas(dil)):
                skew = pltpu.roll(row, (delta - HALF_WINDOW) % BIAS_LANES, axis=1, stride=1, stride_axis=0)
                t_ref[vi] = skew[:, :_key_span(dil)]

    def scores(u, gi):
        q_ref, k_ref, _, t_ref = groups[gi]
        dil = ATTN_DILATIONS[gi]
        sub = SEQ // dil
        span = _key_span(dil)
        blocks = sub // Q_TILE
        r = u // blocks
        m0 = (u % blocks) * Q_TILE
        start = jnp.clip(m0 - HALF_WINDOW, 0, sub - span)
        variant = (m0 - start) // HALF_WINDOW
        q = q_ref[pl.ds(pl.multiple_of(u * Q_TILE, Q_TILE), Q_TILE), :]
        k0 = pl.multiple_of(r * sub + start, HALF_WINDOW)
        k = k_ref[pl.ds(k0, span), :]
        s = lax.dot_general(q, k, (((1,), (1,)), ((), ())), preferred_element_type=_F32)
        if dil == 1:
            rows = pl.ds(pl.multiple_of(m0, Q_TILE), Q_TILE)
        else:
            rows = pl.ds(m0 * dil + r, Q_TILE, stride=dil)
        return s + t_ref[variant], k0, rows

    def weights(s):
        m = jnp.max(s, axis=-1, keepdims=True)
        p = jnp.exp(s - m)
        return m, jnp.sum(p, axis=-1, keepdims=True), p.astype(_BF16)

    def values(gi, k0, rows, m, l, p):
        v = groups[gi][2][pl.ds(k0, p.shape[1]), :]
        acc_ref[gi, rows, :] = jnp.dot(p, v, preferred_element_type=_F32)
        m_ref[gi, rows, :] = jnp.broadcast_to(m, (Q_TILE, HEAD_DIM))
        l_ref[gi, rows, :] = jnp.broadcast_to(l, (Q_TILE, HEAD_DIM))

    def units(it, carry):
        todo = [(it * UNITS_PER_STEP + j, gi) for j in range(UNITS_PER_STEP) for gi in range(3)]
        scored = [scores(u, gi) for u, gi in todo]
        soft = [weights(s) for s, _, _ in scored]
        for (_, gi), (_, k0, rows), (m, l, p) in zip(todo, scored, soft):
            values(gi, k0, rows, m, l, p)
        return carry

    lax.fori_loop(0, SEQ // Q_TILE // UNITS_PER_STEP, units, 0)

    def combine(t, carry):
        rows = pl.ds(pl.multiple_of(t * Q_TILE, Q_TILE), Q_TILE)
        ms = [m_ref[gi, rows, :] for gi in range(3)]
        top = jnp.maximum(jnp.maximum(ms[0], ms[1]), ms[2])
        num = jnp.zeros((Q_TILE, HEAD_DIM), _F32)
        den = jnp.zeros((Q_TILE, HEAD_DIM), _F32)
        for gi in range(3):
            w = jnp.exp(ms[gi] - top)
            num = num + w * acc_ref[gi, rows, :]
            den = den + w * l_ref[gi, rows, :]
        o_ref[rows, :] = (num / den).astype(o_ref.dtype)
        return carry

    lax.fori_loop(0, SEQ // Q_TILE, combine, 0)


def _attn_call(qkv, bias_rows):
    batch = qkv[0].shape[0]

    def head_spec(which):
        return pl.BlockSpec((None, SEQ, HEAD_DIM), lambda h, b: (b, 0, which * HEADS_PER_GROUP + h))

    in_specs = [head_spec(which) for gi in range(3) for which in range(3)]
    in_specs.append(pl.BlockSpec((None, 3, BIAS_LANES), lambda h, b: (h, 0, 0)))
    scratch = [pltpu.VMEM((len(_tile_deltas(dil)), Q_TILE, _key_span(dil)), _F32) for dil in ATTN_DILATIONS]
    scratch += [pltpu.VMEM((3, SEQ, HEAD_DIM), _F32) for _ in range(3)]
    return pl.pallas_call(
        _attn_kernel,
        grid=(HEADS_PER_GROUP, batch),
        in_specs=in_specs,
        out_specs=pl.BlockSpec((None, SEQ, HEAD_DIM), lambda h, b: (b, 0, h)),
        out_shape=jax.ShapeDtypeStruct((batch, SEQ, ATTN_OUT), _BF16),
        scratch_shapes=scratch,
        compiler_params=pltpu.CompilerParams(dimension_semantics=("arbitrary", "arbitrary"),
                                             vmem_limit_bytes=VMEM_LIMIT),
        name="attn",
    )(*[qkv[gi] for gi in range(3) for _ in range(3)], bias_rows)


def _mix_kernel(xn_ref, pool_ref, attn_ref, h_ref, wg_ref, wa_ref, wb_ref, wo_ref, g_ref, h_out_ref, xn_out_ref):
    xn = xn_ref[...]
    gates = jnp.dot(xn, wg_ref[:, GATE_BLOCK - GATE_WIDTH:], preferred_element_type=_F32)
    a = jnp.dot(pool_ref[...], wa_ref[...], preferred_element_type=_F32)
    b = jnp.dot(attn_ref[...], wb_ref[...], preferred_element_type=_F32)
    merged = jax.nn.sigmoid(gates[:, :D_MODEL]) * a + jax.nn.sigmoid(gates[:, D_MODEL:]) * b
    h = h_ref[...] + jnp.dot(merged.astype(_BF16), wo_ref[...], preferred_element_type=_F32)
    h_out_ref[...] = h
    xn_out_ref[...] = _rms(h, g_ref[...]).astype(xn_out_ref.dtype)


def _const_spec(shape):
    return pl.BlockSpec(shape, lambda i: (0,) * len(shape), pipeline_mode=pl.Buffered(1))


def _mix_call(xn2d, pool2d, attn2d, h2d, w_in, w_a, w_b, w_o, g_next):
    rows = xn2d.shape[0]
    tm = ROW_TILE
    assert IN_WIDTH % GATE_BLOCK == 0 and GATE_BLOCK >= GATE_WIDTH
    return pl.pallas_call(
        _mix_kernel,
        grid=(rows // tm,),
        in_specs=[pl.BlockSpec((tm, D_MODEL), lambda i: (i, 0)),
                  pl.BlockSpec((tm, POOL_WIDTH), lambda i: (i, 0)),
                  pl.BlockSpec((tm, ATTN_OUT), lambda i: (i, 0)),
                  pl.BlockSpec((tm, D_MODEL), lambda i: (i, 0)),
                  pl.BlockSpec((D_MODEL, GATE_BLOCK), lambda i: (0, IN_WIDTH // GATE_BLOCK - 1),
                               pipeline_mode=pl.Buffered(1)),
                  _const_spec((POOL_WIDTH, D_MODEL)),
                  _const_spec((ATTN_OUT, D_MODEL)),
                  _const_spec((D_MODEL, D_MODEL)),
                  _const_spec((1, D_MODEL))],
        out_specs=[pl.BlockSpec((tm, D_MODEL), lambda i: (i, 0)),
                   pl.BlockSpec((tm, D_MODEL), lambda i: (i, 0))],
        out_shape=[jax.ShapeDtypeStruct((rows, D_MODEL), _F32),
                   jax.ShapeDtypeStruct((rows, D_MODEL), _BF16)],
        compiler_params=pltpu.CompilerParams(dimension_semantics=("parallel",),
                                             vmem_limit_bytes=VMEM_LIMIT),
        name="mix",
    )(xn2d, pool2d, attn2d, h2d, w_in, w_a, w_b, w_o, g_next.reshape(1, D_MODEL))


def _gelu_tanh(x):
    return 0.5 * x * (1.0 + jnp.tanh(np.sqrt(2.0 / np.pi).astype(np.float32) * (x + 0.044715 * (x * x * x))))


def _ffn_kernel(xp_ref, x_ref, xnx_ref, h_ref, wup_ref, cw_ref, cb_ref, wdn_ref, g_ref, *out_refs, final):
    tm = x_ref.shape[0]
    tiles_per_seq = SEQ // tm
    t = pl.program_id(0) % tiles_per_seq
    x = x_ref[...]
    xe = jnp.concatenate([xp_ref[...], x, xnx_ref[...]], axis=0)
    pos = lax.broadcasted_iota(jnp.int32, (tm, 1), 0) + t * tm
    keep_prev = pos > 0
    keep_next = pos < SEQ - 1
    n_ext = tm + 2 * HALO
    acc = None
    for c in range(D_FF // FF_CHUNK):
        cols = slice(c * FF_CHUNK, (c + 1) * FF_CHUNK)
        gcols = slice(D_FF + c * FF_CHUNK, D_FF + (c + 1) * FF_CHUNK)
        a_ext = jnp.dot(xe, wup_ref[:, cols], preferred_element_type=_F32)
        gate = jnp.dot(x, wup_ref[:, gcols], preferred_element_type=_F32)
        a_mid = a_ext[HALO:HALO + tm]
        a_prev = pltpu.roll(a_ext, 1, axis=0)[HALO:HALO + tm]
        a_next = pltpu.roll(a_ext, n_ext - 1, axis=0)[HALO:HALO + tm]
        cw = cw_ref[:, cols]
        conv = (jnp.where(keep_prev, a_prev, 0.0) * cw[0:1] + a_mid * cw[1:2]
                + jnp.where(keep_next, a_next, 0.0) * cw[2:3] + cb_ref[:, cols])
        act = (_gelu_tanh(conv) * gate).astype(_BF16)
        part = jnp.dot(act, wdn_ref[cols, :], preferred_element_type=_F32)
        acc = part if acc is None else acc + part
    h = h_ref[...] + acc
    if final:
        out_refs[0][...] = _rms(h, g_ref[...])
    else:
        out_refs[0][...] = h
        out_refs[1][...] = _rms(h, g_ref[...]).astype(out_refs[1].dtype)


def _ffn_call(xn2d, h2d, w_up, conv_w, conv_b, w_down, g_next, final):
    rows = xn2d.shape[0]
    tm = ROW_TILE
    per = tm // HALO
    last = rows // HALO - 1
    row_spec = pl.BlockSpec((tm, D_MODEL), lambda i: (i, 0))
    if final:
        out_specs = [row_spec]
        out_shape = [jax.ShapeDtypeStruct((rows, D_MODEL), _F32)]
    else:
        out_specs = [row_spec, row_spec]
        out_shape = [jax.ShapeDtypeStruct((rows, D_MODEL), _F32),
                     jax.ShapeDtypeStruct((rows, D_MODEL), _BF16)]
    return pl.pallas_call(
        functools.partial(_ffn_kernel, final=final),
        grid=(rows // tm,),
        in_specs=[pl.BlockSpec((HALO, D_MODEL), lambda i: (jnp.maximum(i * per - 1, 0), 0)),
                  row_spec,
                  pl.BlockSpec((HALO, D_MODEL), lambda i: (jnp.minimum((i + 1) * per, last), 0)),
                  row_spec,
                  _const_spec((D_MODEL, 2 * D_FF)),
                  _const_spec((3, D_FF)),
                  _const_spec((1, D_FF)),
                  _const_spec((D_FF, D_MODEL)),
                  _const_spec((1, D_MODEL))],
        out_specs=out_specs,
        out_shape=out_shape,
        compiler_params=pltpu.CompilerParams(dimension_semantics=("parallel",),
                                             vmem_limit_bytes=VMEM_LIMIT),
        name="ffn",
    )(xn2d, xn2d, xn2d, h2d, w_up, conv_w, conv_b.reshape(1, D_FF), w_down, g_next.reshape(1, D_MODEL))


def kernel(x, w_in, w_pool, pool_scale, w_a, w_b, w_o, norm1, norm2, w_up, conv_w, conv_b, w_down, rel_bias, norm_f):
    batch, seq, d = x.shape
    assert (seq, d) == (SEQ, D_MODEL)
    depth = w_in.shape[0]
    rows = batch * seq
    bias_rows = _bias_rows(rel_bias)
    h = x.reshape(rows, d)
    xn = _norm_call(h, norm1[0])
    for layer in range(depth):
        w_in_l = w_in[layer].astype(_BF16)
        xn3 = xn.reshape(batch, seq, d)
        pool = _pool_call(xn3, w_in_l, w_pool[layer].astype(_BF16), pool_scale[layer])
        attn = _attn_call([_qkv_call(xn3, w_in_l, gi) for gi in range(3)], bias_rows)
        h, xn = _mix_call(xn, pool.reshape(rows, POOL_WIDTH), attn.reshape(rows, ATTN_OUT), h,
                          w_in_l, w_a[layer].astype(_BF16), w_b[layer].astype(_BF16),
                          w_o[layer].astype(_BF16), norm2[layer])
        final = layer == depth - 1
        g_next = norm_f if final else norm1[layer + 1]
        outs = _ffn_call(xn, h, w_up[layer].astype(_BF16), conv_w[layer], conv_b[layer],
                         w_down[layer].astype(_BF16), g_next, final)
        if final:
            return outs[0].reshape(batch, seq, d)
        h, xn = outs
```

```python
import functools

import numpy as np
import jax
import jax.numpy as jnp
from jax import lax
from jax.experimental import pallas as pl
from jax.experimental.pallas import tpu as pltpu

D_MODEL = 1024
SEQ = 2048
POOL_WINDOWS = (2, 4, 8, 16)
POOL_GROUP_DIM = 256
POOL_WIDTH = 1024
ATTN_DILATIONS = (1, 4, 16)
HALF_WINDOW = 64
HEADS_PER_GROUP = 4
N_HEADS = 12
HEAD_DIM = 128
ATTN_WIDTH = N_HEADS * HEAD_DIM
ATTN_OUT = HEADS_PER_GROUP * HEAD_DIM
NEG_INF = -1e30
N_BUCKETS = 32
MAX_DISTANCE = 1024
D_FF = 2816
EPS = 1e-6
PROJ_WIDTH = POOL_WIDTH + 3 * ATTN_WIDTH
GATE_WIDTH = 2 * D_MODEL
IN_WIDTH = PROJ_WIDTH + GATE_WIDTH
GATE_BLOCK = 2560

Q_TILE = 128
UNITS_PER_STEP = 4
RESIDUE_PITCH = Q_TILE + 4
PROJ_TN = 512
ROW_TILE = 512
HALO = 16
FF_CHUNK = 1408
VMEM_LIMIT = 56 * 1024 * 1024
LANES = 128
REGROUP_TILE = 256

_F32 = jnp.float32
_BF16 = jnp.bfloat16


def _rms(x, g):
    return x * lax.rsqrt(jnp.mean(x * x, axis=-1, keepdims=True) + EPS) * g


def _norm_kernel(x_ref, g_ref, o_ref):
    o_ref[...] = _rms(x_ref[...], g_ref[...]).astype(o_ref.dtype)


def _norm_call(x2d, g):
    rows = x2d.shape[0]
    tm = 1024
    return pl.pallas_call(
        _norm_kernel,
        grid=(rows // tm,),
        in_specs=[pl.BlockSpec((tm, D_MODEL), lambda i: (i, 0)),
                  pl.BlockSpec((1, D_MODEL), lambda i: (0, 0))],
        out_specs=pl.BlockSpec((tm, D_MODEL), lambda i: (i, 0)),
        out_shape=jax.ShapeDtypeStruct((rows, D_MODEL), _BF16),
        compiler_params=pltpu.CompilerParams(dimension_semantics=("parallel",)),
        name="rmsnorm",
    )(x2d, g.reshape(1, D_MODEL))


POOL_PAD = 16
POOL_EDGE = 8
assert POOL_EDGE >= max(POOL_WINDOWS) // 2 and POOL_PAD - POOL_EDGE >= max(POOL_WINDOWS) // 4


def _pool_kernel(xn_ref, w_ref, wpool_ref, pscale_ref, o_ref, u_ref, t_ref):
    n = SEQ + 2 * POOL_PAD
    slabs = POOL_GROUP_DIM // LANES

    def shifted(view, start, rows, k_back, k_fwd):
        return view[pl.ds(start - k_back, rows, stride=1), :] + view[pl.ds(start + k_fwd, rows, stride=1), :]

    def steps(w):
        return (1, 0) if w == 1 else (w // 2, w // 2)

    zeros = jnp.zeros((POOL_PAD, LANES), _F32)
    for t in range(2):
        for s in range(slabs):
            t_ref[t, s, 0:POOL_EDGE, :] = zeros[:POOL_EDGE]
            t_ref[t, s, n - POOL_EDGE:n, :] = zeros[:POOL_EDGE]
    xn = xn_ref[...]
    for gi in range(len(POOL_WINDOWS)):
        u = jnp.dot(xn, w_ref[:, gi * POOL_GROUP_DIM:(gi + 1) * POOL_GROUP_DIM], preferred_element_type=_F32)
        for s in range(slabs):
            u_ref[gi, s, 0:POOL_PAD, :] = zeros
            u_ref[gi, s, n - POOL_PAD:n, :] = zeros
            u_ref[gi, s, POOL_PAD:POOL_PAD + SEQ, :] = u[:, s * LANES:(s + 1) * LANES]
    edge_row = lax.broadcasted_iota(jnp.int32, (POOL_PAD, POOL_GROUP_DIM), 0)
    for gi, window in enumerate(POOL_WINDOWS):
        cols = slice(gi * POOL_GROUP_DIM, (gi + 1) * POOL_GROUP_DIM)
        totals = []
        for s in range(slabs):
            src, w, slot = u_ref.at[gi, s], 1, 0
            while 2 * w < window:
                t_ref[slot, s, pl.ds(POOL_EDGE, n - 2 * POOL_EDGE), :] = shifted(src, POOL_EDGE, n - 2 * POOL_EDGE,
                                                                              *steps(w))
                src, w, slot = t_ref.at[slot, s], 2 * w, 1 - slot
            totals.append(shifted(src, POOL_PAD, SEQ, *steps(w)))
        total = jnp.concatenate(totals, axis=1)
        u = jnp.concatenate([u_ref[gi, s, POOL_PAD:POOL_PAD + SEQ, :] for s in range(slabs)], axis=1)
        wpool = wpool_ref[gi]
        scale = pscale_ref[:, cols]
        pooled = total * (1.0 / window) - u
        z = jnp.dot(pooled.astype(_BF16), wpool, preferred_element_type=_F32)
        o_ref[:, cols] = (z * scale).astype(o_ref.dtype)
        for r0 in (0, SEQ - POOL_PAD):
            pos = edge_row + r0
            size = (jnp.minimum(pos + window // 2, SEQ) - jnp.maximum(pos - window // 2, 0)).astype(_F32)
            pooled = total[r0:r0 + POOL_PAD] / size - u[r0:r0 + POOL_PAD]
            z = jnp.dot(pooled.astype(_BF16), wpool, preferred_element_type=_F32)
            o_ref[r0:r0 + POOL_PAD, cols] = (z * scale).astype(o_ref.dtype)


def _pool_call(xn3, w_in, w_pool, pool_scale):
    batch = xn3.shape[0]
    return pl.pallas_call(
        _pool_kernel,
        grid=(batch,),
        in_specs=[pl.BlockSpec((None, SEQ, D_MODEL), lambda b: (b, 0, 0)),
                  pl.BlockSpec((D_MODEL, POOL_WIDTH), lambda b: (0, 0), pipeline_mode=pl.Buffered(1)),
                  _const_spec((len(POOL_WINDOWS), POOL_GROUP_DIM, POOL_GROUP_DIM)),
                  _const_spec((1, POOL_WIDTH))],
        out_specs=pl.BlockSpec((None, SEQ, POOL_WIDTH), lambda b: (b, 0, 0)),
        out_shape=jax.ShapeDtypeStruct((batch, SEQ, POOL_WIDTH), _BF16),
        scratch_shapes=[pltpu.VMEM((len(POOL_WINDOWS), POOL_GROUP_DIM // LANES, SEQ + 2 * POOL_PAD, LANES), _F32),
                        pltpu.VMEM((2, POOL_GROUP_DIM // LANES, SEQ + 2 * POOL_PAD, LANES), _F32)],
        compiler_params=pltpu.CompilerParams(dimension_semantics=("parallel",),
                                             vmem_limit_bytes=VMEM_LIMIT),
        name="pool",
    )(xn3, w_in, w_pool, pool_scale.reshape(1, POOL_WIDTH))


def _qkv_kernel(x_ref, w_ref, o_ref, *scratch, dil):
    n = pl.program_id(1)
    if dil == 1:
        x = x_ref[...]
    else:
        xp_ref, = scratch
        sub = SEQ // dil
        sub_t = REGROUP_TILE // dil

        @pl.when(n == 0)
        def _():
            i = lax.broadcasted_iota(jnp.int32, (REGROUP_TILE, REGROUP_TILE), 0)
            j = lax.broadcasted_iota(jnp.int32, (REGROUP_TILE, REGROUP_TILE), 1)
            pick = (j == (i % sub_t) * dil + i // sub_t).astype(_BF16)
            for t in range(SEQ // REGROUP_TILE):
                tile = x_ref[t * REGROUP_TILE:(t + 1) * REGROUP_TILE, :]
                srt = jnp.dot(pick, tile, preferred_element_type=_F32).astype(_BF16)
                for r in range(dil):
                    xp_ref[r * sub + t * sub_t:r * sub + (t + 1) * sub_t, :] = srt[r * sub_t:(r + 1) * sub_t]

        x = xp_ref[...]
    y = jnp.dot(x, w_ref[...], preferred_element_type=_F32)
    y = y * jnp.where(n == 0, HEAD_DIM ** -0.5, 1.0).astype(_F32)
    o_ref[...] = y.astype(o_ref.dtype)


def _qkv_call(xn3, w_in, gi):
    batch = xn3.shape[0]
    dil = ATTN_DILATIONS[gi]
    first = POOL_WIDTH // PROJ_TN + gi
    return pl.pallas_call(
        functools.partial(_qkv_kernel, dil=dil),
        grid=(batch, 3),
        in_specs=[pl.BlockSpec((None, SEQ, D_MODEL), lambda b, n: (b, 0, 0)),
                  pl.BlockSpec((D_MODEL, PROJ_TN), lambda b, n: (0, first + 3 * n))],
        out_specs=pl.BlockSpec((None, SEQ, PROJ_TN), lambda b, n: (b, 0, n)),
        out_shape=jax.ShapeDtypeStruct((batch, SEQ, 3 * PROJ_TN), _BF16),
        scratch_shapes=[] if dil == 1 else [pltpu.VMEM((SEQ, D_MODEL), _BF16)],
        compiler_params=pltpu.CompilerParams(dimension_semantics=("parallel", "arbitrary"),
                                             vmem_limit_bytes=VMEM_LIMIT),
        name=f"qkv{gi}",
    )(xn3, w_in)


def _t5_buckets_np(rel):
    n = -rel
    half = N_BUCKETS // 2
    ret = (n < 0).astype(np.int32) * half
    n = np.abs(n)
    max_exact = half // 2
    large = max_exact + (np.log(np.maximum(n, 1) / max_exact)
                         / np.log(MAX_DISTANCE / max_exact) * (half - max_exact)).astype(np.int32)
    large = np.minimum(large, half - 1)
    return (ret + np.where(n < max_exact, n, large)).astype(np.int32)


def _key_span(dil):
    return min(Q_TILE + 2 * HALF_WINDOW, SEQ // dil)


BIAS_LANES = 512


def _bias_rows(rel_bias):
    rows = []
    for gi, dil in enumerate(ATTN_DILATIONS):
        buckets = _t5_buckets_np(dil * np.arange(-HALF_WINDOW, HALF_WINDOW + 1))
        bias = rel_bias[buckets][:, gi * HEADS_PER_GROUP:(gi + 1) * HEADS_PER_GROUP].T.astype(_F32)
        rows.append(jnp.pad(bias, ((0, 0), (0, BIAS_LANES - bias.shape[1])), constant_values=NEG_INF))
    return jnp.stack(rows, axis=1)


def _tile_deltas(dil):
    return (0, HALF_WINDOW, 2 * HALF_WINDOW) if SEQ // dil > Q_TILE else (0,)


def _attn_kernel(q0_ref, k0_ref, v0_ref, q1_ref, k1_ref, v1_ref, q2_ref, k2_ref, v2_ref,
                 e_ref, o_ref, t0_ref, t1_ref, t2_ref, o_acc_ref, lse_ref, far_o_ref, far_lse_ref):
    groups = ((q0_ref, k0_ref, v0_ref, t0_ref), (q1_ref, k1_ref, v1_ref, t1_ref), (q2_ref, k2_ref, v2_ref, t2_ref))

    @pl.when(pl.program_id(1) == 0)
    def _():
        for gi, (_, _, _, t_ref) in enumerate(groups):
            dil = ATTN_DILATIONS[gi]
            row = jnp.broadcast_to(e_ref[gi:gi + 1, :], (Q_TILE, BIAS_LANES))
            for vi, delta in enumerate(_tile_deltas(dil)):
                skew = pltpu.roll(row, (delta - HALF_WINDOW) % BIAS_LANES, axis=1, stride=1, stride_axis=0)
                t_ref[vi] = skew[:, :_key_span(dil)]

    def scores(u, gi):
        q_ref, k_ref, _, t_ref = groups[gi]
        dil = ATTN_DILATIONS[gi]
        sub = SEQ // dil
        span = _key_span(dil)
        blocks = sub // Q_TILE
        r = u // blocks
        m0 = (u % blocks) * Q_TILE
        start = jnp.clip(m0 - HALF_WINDOW, 0, sub - span)
        variant = (m0 - start) // HALF_WINDOW
        q = q_ref[pl.ds(pl.multiple_of(u * Q_TILE, Q_TILE), Q_TILE), :]
        k0 = pl.multiple_of(r * sub + start, HALF_WINDOW)
        k = k_ref[pl.ds(k0, span), :]
        s = lax.dot_general(q, k, (((1,), (1,)), ((), ())), preferred_element_type=_F32)
        if dil == 1:
            dest = (gi, pl.ds(pl.multiple_of(m0, Q_TILE), Q_TILE))
        elif blocks > 1:
            dest = (gi, pl.ds(m0 * dil + r, Q_TILE, stride=dil))
        else:
            dest = (None, pl.ds(r * RESIDUE_PITCH, Q_TILE, stride=1))
        return s + t_ref[variant], k0, dest

    def weights(s):
        m = jnp.max(s, axis=-1, keepdims=True)
        return m, jnp.exp((s - m).astype(_BF16))

    ones = jnp.ones((Q_TILE + 2 * HALF_WINDOW, HEAD_DIM), _BF16)

    def values(gi, k0, dest, m, p):
        span = p.shape[1]
        v = groups[gi][2][pl.ds(k0, span), :]
        ov = jnp.dot(p, jnp.concatenate([v, ones[:span]], axis=1), preferred_element_type=_F32)
        l = ov[:, HEAD_DIM:]
        slot, rows = dest
        if slot is None:
            far_o_ref[rows, :] = ov[:, :HEAD_DIM] / l
            far_lse_ref[rows, :] = m + jnp.log(l)
        else:
            o_acc_ref[slot, rows, :] = ov[:, :HEAD_DIM] / l
            lse_ref[slot, rows, :] = m + jnp.log(l)

    def units(it, carry):
        todo = [(it * UNITS_PER_STEP + j, gi) for j in range(UNITS_PER_STEP) for gi in range(3)]
        scored = [scores(u, gi) for u, gi in todo]
        soft = [weights(s) for s, _, _ in scored]
        for (_, gi), (_, k0, dest), (m, p) in zip(todo, scored, soft):
            values(gi, k0, dest, m, p)
        return carry

    lax.fori_loop(0, SEQ // Q_TILE // UNITS_PER_STEP, units, 0)

    far_dil = ATTN_DILATIONS[2]

    def merge(t, carry):
        rows = pl.ds(pl.multiple_of(t * Q_TILE, Q_TILE), Q_TILE)

        def far(ref):
            return jnp.concatenate([ref[pl.ds(t * (Q_TILE // far_dil) + j, far_dil, stride=RESIDUE_PITCH), :]
                                    for j in range(Q_TILE // far_dil)], axis=0)

        lses = [lse_ref[0, rows, :], lse_ref[1, rows, :], far(far_lse_ref)]
        outs = [o_acc_ref[0, rows, :], o_acc_ref[1, rows, :], far(far_o_ref)]
        top = jnp.maximum(jnp.maximum(lses[0], lses[1]), lses[2])
        ws = [jnp.exp(x - top) for x in lses]
        num = ws[0] * outs[0] + ws[1] * outs[1] + ws[2] * outs[2]
        o_ref[rows, :] = (num / (ws[0] + ws[1] + ws[2])).astype(o_ref.dtype)
        return carry

    lax.fori_loop(0, SEQ // Q_TILE, merge, 0)


def _attn_call(qkv, bias_rows):
    batch = qkv[0].shape[0]

    def head_spec(which):
        return pl.BlockSpec((None, SEQ, HEAD_DIM), lambda h, b: (b, 0, which * HEADS_PER_GROUP + h))

    in_specs = [head_spec(which) for gi in range(3) for which in range(3)]
    in_specs.append(pl.BlockSpec((None, 3, BIAS_LANES), lambda h, b: (h, 0, 0)))
    scratch = [pltpu.VMEM((len(_tile_deltas(dil)), Q_TILE, _key_span(dil)), _F32) for dil in ATTN_DILATIONS]
    assert SEQ // ATTN_DILATIONS[2] == Q_TILE and all(SEQ // d > Q_TILE for d in ATTN_DILATIONS[:2])
    scratch += [pltpu.VMEM((2, SEQ, HEAD_DIM), _F32) for _ in range(2)]
    scratch += [pltpu.VMEM((ATTN_DILATIONS[2] * RESIDUE_PITCH, HEAD_DIM), _F32) for _ in range(2)]
    return pl.pallas_call(
        _attn_kernel,
        grid=(HEADS_PER_GROUP, batch),
        in_specs=in_specs,
        out_specs=pl.BlockSpec((None, SEQ, HEAD_DIM), lambda h, b: (b, 0, h)),
        out_shape=jax.ShapeDtypeStruct((batch, SEQ, ATTN_OUT), _BF16),
        scratch_shapes=scratch,
        compiler_params=pltpu.CompilerParams(dimension_semantics=("arbitrary", "arbitrary"),
                                             vmem_limit_bytes=VMEM_LIMIT),
        name="attn",
    )(*[qkv[gi] for gi in range(3) for _ in range(3)], bias_rows)


def _mix_kernel(xn_ref, pool_ref, attn_ref, h_ref, wg_ref, wa_ref, wb_ref, wo_ref, g_ref, h_out_ref, xn_out_ref):
    xn = xn_ref[...]
    gates = jnp.dot(xn, wg_ref[:, GATE_BLOCK - GATE_WIDTH:], preferred_element_type=_F32)
    a = jnp.dot(pool_ref[...], wa_ref[...], preferred_element_type=_F32)
    b = jnp.dot(attn_ref[...], wb_ref[...], preferred_element_type=_F32)
    merged = jax.nn.sigmoid(gates[:, :D_MODEL]) * a + jax.nn.sigmoid(gates[:, D_MODEL:]) * b
    h = h_ref[...] + jnp.dot(merged.astype(_BF16), wo_ref[...], preferred_element_type=_F32)
    h_out_ref[...] = h
    xn_out_ref[...] = _rms(h, g_ref[...]).astype(xn_out_ref.dtype)


def _const_spec(shape):
    return pl.BlockSpec(shape, lambda i: (0,) * len(shape), pipeline_mode=pl.Buffered(1))


def _mix_call(xn2d, pool2d, attn2d, h2d, w_in, w_a, w_b, w_o, g_next):
    rows = xn2d.shape[0]
    tm = ROW_TILE
    assert IN_WIDTH % GATE_BLOCK == 0 and GATE_BLOCK >= GATE_WIDTH
    return pl.pallas_call(
        _mix_kernel,
        grid=(rows // tm,),
        in_specs=[pl.BlockSpec((tm, D_MODEL), lambda i: (i, 0)),
                  pl.BlockSpec((tm, POOL_WIDTH), lambda i: (i, 0)),
                  pl.BlockSpec((tm, ATTN_OUT), lambda i: (i, 0)),
                  pl.BlockSpec((tm, D_MODEL), lambda i: (i, 0)),
                  pl.BlockSpec((D_MODEL, GATE_BLOCK), lambda i: (0, IN_WIDTH // GATE_BLOCK - 1),
                               pipeline_mode=pl.Buffered(1)),
                  _const_spec((POOL_WIDTH, D_MODEL)),
                  _const_spec((ATTN_OUT, D_MODEL)),
                  _const_spec((D_MODEL, D_MODEL)),
                  _const_spec((1, D_MODEL))],
        out_specs=[pl.BlockSpec((tm, D_MODEL), lambda i: (i, 0)),
                   pl.BlockSpec((tm, D_MODEL), lambda i: (i, 0))],
        out_shape=[jax.ShapeDtypeStruct((rows, D_MODEL), _F32),
                   jax.ShapeDtypeStruct((rows, D_MODEL), _BF16)],
        compiler_params=pltpu.CompilerParams(dimension_semantics=("parallel",),
                                             vmem_limit_bytes=VMEM_LIMIT),
        name="mix",
    )(xn2d, pool2d, attn2d, h2d, w_in, w_a, w_b, w_o, g_next.reshape(1, D_MODEL))


def _gelu_tanh(x):
    return 0.5 * x * (1.0 + jnp.tanh(np.sqrt(2.0 / np.pi).astype(np.float32) * (x + 0.044715 * (x * x * x))))


def _ffn_kernel(xp_ref, x_ref, xnx_ref, h_ref, wup_ref, cw_ref, cb_ref, wdn_ref, g_ref, *out_refs, final):
    tm = x_ref.shape[0]
    tiles_per_seq = SEQ // tm
    t = pl.program_id(0) % tiles_per_seq
    x = x_ref[...]
    xe = jnp.concatenate([xp_ref[...], x, xnx_ref[...]], axis=0)
    pos = lax.broadcasted_iota(jnp.int32, (tm, 1), 0) + t * tm
    keep_prev = pos > 0
    keep_next = pos < SEQ - 1
    n_ext = tm + 2 * HALO
    acc = None
    for c in range(D_FF // FF_CHUNK):
        cols = slice(c * FF_CHUNK, (c + 1) * FF_CHUNK)
        gcols = slice(D_FF + c * FF_CHUNK, D_FF + (c + 1) * FF_CHUNK)
        a_ext = jnp.dot(xe, wup_ref[:, cols], preferred_element_type=_F32)
        gate = jnp.dot(x, wup_ref[:, gcols], preferred_element_type=_F32)
        a_mid = a_ext[HALO:HALO + tm]
        a_prev = pltpu.roll(a_ext, 1, axis=0)[HALO:HALO + tm]
        a_next = pltpu.roll(a_ext, n_ext - 1, axis=0)[HALO:HALO + tm]
        cw = cw_ref[:, cols]
        conv = (jnp.where(keep_prev, a_prev, 0.0) * cw[0:1] + a_mid * cw[1:2]
                + jnp.where(keep_next, a_next, 0.0) * cw[2:3] + cb_ref[:, cols])
        act = (_gelu_tanh(conv) * gate).astype(_BF16)
        part = jnp.dot(act, wdn_ref[cols, :], preferred_element_type=_F32)
        acc = part if acc is None else acc + part
    h = h_ref[...] + acc
    if final:
        out_refs[0][...] = _rms(h, g_ref[...])
    else:
        out_refs[0][...] = h
        out_refs[1][...] = _rms(h, g_ref[...]).astype(out_refs[1].dtype)


def _ffn_call(xn2d, h2d, w_up, conv_w, conv_b, w_down, g_next, final):
    rows = xn2d.shape[0]
    tm = ROW_TILE
    per = tm // HALO
    last = rows // HALO - 1
    row_spec = pl.BlockSpec((tm, D_MODEL), lambda i: (i, 0))
    if final:
        out_specs = [row_spec]
        out_shape = [jax.ShapeDtypeStruct((rows, D_MODEL), _F32)]
    else:
        out_specs = [row_spec, row_spec]
        out_shape = [jax.ShapeDtypeStruct((rows, D_MODEL), _F32),
                     jax.ShapeDtypeStruct((rows, D_MODEL), _BF16)]
    return pl.pallas_call(
        functools.partial(_ffn_kernel, final=final),
        grid=(rows // tm,),
        in_specs=[pl.BlockSpec((HALO, D_MODEL), lambda i: (jnp.maximum(i * per - 1, 0), 0)),
                  row_spec,
                  pl.BlockSpec((HALO, D_MODEL), lambda i: (jnp.minimum((i + 1) * per, last), 0)),
                  row_spec,
                  _const_spec((D_MODEL, 2 * D_FF)),
                  _const_spec((3, D_FF)),
                  _const_spec((1, D_FF)),
                  _const_spec((D_FF, D_MODEL)),
                  _const_spec((1, D_MODEL))],
        out_specs=out_specs,
        out_shape=out_shape,
        compiler_params=pltpu.CompilerParams(dimension_semantics=("parallel",),
                                             vmem_limit_bytes=VMEM_LIMIT),
        name="ffn",
    )(xn2d, xn2d, xn2d, h2d, w_up, conv_w, conv_b.reshape(1, D_FF), w_down, g_next.reshape(1, D_MODEL))


def kernel(x, w_in, w_pool, pool_scale, w_a, w_b, w_o, norm1, norm2, w_up, conv_w, conv_b, w_down, rel_bias, norm_f):
    batch, seq, d = x.shape
    assert (seq, d) == (SEQ, D_MODEL)
    depth = w_in.shape[0]
    rows = batch * seq
    bias_rows = _bias_rows(rel_bias)
    h = x.reshape(rows, d)
    xn = _norm_call(h, norm1[0])
    for layer in range(depth):
        w_in_l = w_in[layer].astype(_BF16)
        xn3 = xn.reshape(batch, seq, d)
        pool = _pool_call(xn3, w_in_l, w_pool[layer].astype(_BF16), pool_scale[layer])
        attn = _attn_call([_qkv_call(xn3, w_in_l, gi) for gi in range(3)], bias_rows)
        h, xn = _mix_call(xn, pool.reshape(rows, POOL_WIDTH), attn.reshape(rows, ATTN_OUT), h,
                          w_in_l, w_a[layer].astype(_BF16), w_b[layer].astype(_BF16),
                          w_o[layer].astype(_BF16), norm2[layer])
        final = layer == depth - 1
        g_next = norm_f if final else norm1[layer + 1]
        outs = _ffn_call(xn, h, w_up[layer].astype(_BF16), conv_w[layer], conv_b[layer],
                         w_down[layer].astype(_BF16), g_next, final)
        if final:
            return outs[0].reshape(batch, seq, d)
        h, xn = outs
```

```python
import functools

import numpy as np
import jax
import jax.numpy as jnp
from jax import lax
from jax.experimental import pallas as pl
from jax.experimental.pallas import tpu as pltpu

D_MODEL = 1024
SEQ = 2048
POOL_WINDOWS = (2, 4, 8, 16)
POOL_GROUP_DIM = 256
POOL_WIDTH = 1024
ATTN_DILATIONS = (1, 4, 16)
HALF_WINDOW = 64
HEADS_PER_GROUP = 4
N_HEADS = 12
HEAD_DIM = 128
ATTN_WIDTH = N_HEADS * HEAD_DIM
ATTN_OUT = HEADS_PER_GROUP * HEAD_DIM
NEG_INF = -1e30
N_BUCKETS = 32
MAX_DISTANCE = 1024
D_FF = 2816
EPS = 1e-6
PROJ_WIDTH = POOL_WIDTH + 3 * ATTN_WIDTH
GATE_WIDTH = 2 * D_MODEL
IN_WIDTH = PROJ_WIDTH + GATE_WIDTH
GATE_BLOCK = 2560

Q_TILE = 128
UNITS_PER_STEP = 4
RESIDUE_PITCH = Q_TILE + 4
PROJ_TN = 512
ROW_TILE = 512
HALO = 16
FF_CHUNK = 1408
VMEM_LIMIT = 56 * 1024 * 1024
LANES = 128
REGROUP_TILE = 256

_F32 = jnp.float32
_BF16 = jnp.bfloat16


def _rms(x, g):
    return x * lax.rsqrt(jnp.mean(x * x, axis=-1, keepdims=True) + EPS) * g


def _norm_kernel(x_ref, g_ref, o_ref):
    o_ref[...] = _rms(x_ref[...], g_ref[...]).astype(o_ref.dtype)


def _norm_call(x2d, g):
    rows = x2d.shape[0]
    tm = 1024
    return pl.pallas_call(
        _norm_kernel,
        grid=(rows // tm,),
        in_specs=[pl.BlockSpec((tm, D_MODEL), lambda i: (i, 0)),
                  pl.BlockSpec((1, D_MODEL), lambda i: (0, 0))],
        out_specs=pl.BlockSpec((tm, D_MODEL), lambda i: (i, 0)),
        out_shape=jax.ShapeDtypeStruct((rows, D_MODEL), _BF16),
        compiler_params=pltpu.CompilerParams(dimension_semantics=("parallel",)),
        name="rmsnorm",
    )(x2d, g.reshape(1, D_MODEL))


POOL_PAD = 16
POOL_EDGE = 8
assert POOL_EDGE >= max(POOL_WINDOWS) // 2 and POOL_PAD - POOL_EDGE >= max(POOL_WINDOWS) // 4


def _pool_kernel(xn_ref, w_ref, wpool_ref, pscale_ref, o_ref, u_ref, t_ref):
    n = SEQ + 2 * POOL_PAD
    slabs = POOL_GROUP_DIM // LANES

    def shifted(view, start, rows, k_back, k_fwd):
        return view[pl.ds(start - k_back, rows, stride=1), :] + view[pl.ds(start + k_fwd, rows, stride=1), :]

    def steps(w):
        return (1, 0) if w == 1 else (w // 2, w // 2)

    zeros = jnp.zeros((POOL_PAD, LANES), _F32)
    for t in range(2):
        for s in range(slabs):
            t_ref[t, s, 0:POOL_EDGE, :] = zeros[:POOL_EDGE]
            t_ref[t, s, n - POOL_EDGE:n, :] = zeros[:POOL_EDGE]
    xn = xn_ref[...]
    for gi in range(len(POOL_WINDOWS)):
        u = jnp.dot(xn, w_ref[:, gi * POOL_GROUP_DIM:(gi + 1) * POOL_GROUP_DIM], preferred_element_type=_F32)
        for s in range(slabs):
            u_ref[gi, s, 0:POOL_PAD, :] = zeros
            u_ref[gi, s, n - POOL_PAD:n, :] = zeros
            u_ref[gi, s, POOL_PAD:POOL_PAD + SEQ, :] = u[:, s * LANES:(s + 1) * LANES]
    edge_row = lax.broadcasted_iota(jnp.int32, (POOL_PAD, POOL_GROUP_DIM), 0)
    for gi, window in enumerate(POOL_WINDOWS):
        cols = slice(gi * POOL_GROUP_DIM, (gi + 1) * POOL_GROUP_DIM)
        totals = []
        for s in range(slabs):
            src, w, slot = u_ref.at[gi, s], 1, 0
            while 2 * w < window:
                t_ref[slot, s, pl.ds(POOL_EDGE, n - 2 * POOL_EDGE), :] = shifted(src, POOL_EDGE, n - 2 * POOL_EDGE,
                                                                              *steps(w))
                src, w, slot = t_ref.at[slot, s], 2 * w, 1 - slot
            totals.append(shifted(src, POOL_PAD, SEQ, *steps(w)))
        total = jnp.concatenate(totals, axis=1)
        u = jnp.concatenate([u_ref[gi, s, POOL_PAD:POOL_PAD + SEQ, :] for s in range(slabs)], axis=1)
        wpool = wpool_ref[gi]
        scale = pscale_ref[:, cols]
        pooled = total * (1.0 / window) - u
        z = jnp.dot(pooled.astype(_BF16), wpool, preferred_element_type=_F32)
        o_ref[:, cols] = (z * scale).astype(o_ref.dtype)
        for r0 in (0, SEQ - POOL_PAD):
            pos = edge_row + r0
            size = (jnp.minimum(pos + window // 2, SEQ) - jnp.maximum(pos - window // 2, 0)).astype(_F32)
            pooled = total[r0:r0 + POOL_PAD] / size - u[r0:r0 + POOL_PAD]
            z = jnp.dot(pooled.astype(_BF16), wpool, preferred_element_type=_F32)
            o_ref[r0:r0 + POOL_PAD, cols] = (z * scale).astype(o_ref.dtype)


def _layer_spec(layer, shape, index=None):
    index = tuple(index) if index is not None else (0,) * len(shape)
    return pl.BlockSpec((None,) + tuple(shape), lambda *_: (layer,) + index, pipeline_mode=pl.Buffered(1))


def _pool_call(xn3, w_in, w_pool, pool_scale, layer):
    batch = xn3.shape[0]
    return pl.pallas_call(
        _pool_kernel,
        grid=(batch,),
        in_specs=[pl.BlockSpec((None, SEQ, D_MODEL), lambda b: (b, 0, 0)),
                  _layer_spec(layer, (D_MODEL, POOL_WIDTH)),
                  _layer_spec(layer, (len(POOL_WINDOWS), POOL_GROUP_DIM, POOL_GROUP_DIM)),
                  _const_spec((1, POOL_WIDTH))],
        out_specs=pl.BlockSpec((None, SEQ, POOL_WIDTH), lambda b: (b, 0, 0)),
        out_shape=jax.ShapeDtypeStruct((batch, SEQ, POOL_WIDTH), _BF16),
        scratch_shapes=[pltpu.VMEM((len(POOL_WINDOWS), POOL_GROUP_DIM // LANES, SEQ + 2 * POOL_PAD, LANES), _F32),
                        pltpu.VMEM((2, POOL_GROUP_DIM // LANES, SEQ + 2 * POOL_PAD, LANES), _F32)],
        compiler_params=pltpu.CompilerParams(dimension_semantics=("parallel",),
                                             vmem_limit_bytes=VMEM_LIMIT),
        name="pool",
    )(xn3, w_in, w_pool, pool_scale.reshape(1, POOL_WIDTH))


def _qkv_kernel(x_ref, wq_ref, wk_ref, wv_ref, o_ref, *scratch, dil):
    if dil == 1:
        x = x_ref[...]
    else:
        xp_ref, = scratch
        sub = SEQ // dil
        sub_t = REGROUP_TILE // dil
        i = lax.broadcasted_iota(jnp.int32, (REGROUP_TILE, REGROUP_TILE), 0)
        j = lax.broadcasted_iota(jnp.int32, (REGROUP_TILE, REGROUP_TILE), 1)
        pick = (j == (i % sub_t) * dil + i // sub_t).astype(_BF16)
        for t in range(SEQ // REGROUP_TILE):
            tile = x_ref[t * REGROUP_TILE:(t + 1) * REGROUP_TILE, :]
            srt = jnp.dot(pick, tile, preferred_element_type=_F32).astype(_BF16)
            for r in range(dil):
                xp_ref[r * sub + t * sub_t:r * sub + (t + 1) * sub_t, :] = srt[r * sub_t:(r + 1) * sub_t]
        x = xp_ref[...]
    for which, w_ref in enumerate((wq_ref, wk_ref, wv_ref)):
        y = jnp.dot(x, w_ref[...], preferred_element_type=_F32)
        if which == 0:
            y = y * HEAD_DIM ** -0.5
        o_ref[:, which * PROJ_TN:(which + 1) * PROJ_TN] = y.astype(o_ref.dtype)


def _qkv_call(xn3, w_in, gi, layer):
    batch = xn3.shape[0]
    dil = ATTN_DILATIONS[gi]
    first = POOL_WIDTH // PROJ_TN + gi
    return pl.pallas_call(
        functools.partial(_qkv_kernel, dil=dil),
        grid=(batch,),
        in_specs=[pl.BlockSpec((None, SEQ, D_MODEL), lambda b: (b, 0, 0))]
        + [_layer_spec(layer, (D_MODEL, PROJ_TN), (0, first + 3 * which)) for which in range(3)],
        out_specs=pl.BlockSpec((None, SEQ, 3 * PROJ_TN), lambda b: (b, 0, 0)),
        out_shape=jax.ShapeDtypeStruct((batch, SEQ, 3 * PROJ_TN), _BF16),
        scratch_shapes=[] if dil == 1 else [pltpu.VMEM((SEQ, D_MODEL), _BF16)],
        compiler_params=pltpu.CompilerParams(dimension_semantics=("parallel",),
                                             vmem_limit_bytes=VMEM_LIMIT),
        name=f"qkv{gi}",
    )(xn3, w_in, w_in, w_in)


def _t5_buckets_np(rel):
    n = -rel
    half = N_BUCKETS // 2
    ret = (n < 0).astype(np.int32) * half
    n = np.abs(n)
    max_exact = half // 2
    large = max_exact + (np.log(np.maximum(n, 1) / max_exact)
                         / np.log(MAX_DISTANCE / max_exact) * (half - max_exact)).astype(np.int32)
    large = np.minimum(large, half - 1)
    return (ret + np.where(n < max_exact, n, large)).astype(np.int32)


def _key_span(dil):
    return min(Q_TILE + 2 * HALF_WINDOW, SEQ // dil)


BIAS_LANES = 512


def _bias_rows(rel_bias):
    rows = []
    for gi, dil in enumerate(ATTN_DILATIONS):
        buckets = _t5_buckets_np(dil * np.arange(-HALF_WINDOW, HALF_WINDOW + 1))
        bias = rel_bias[buckets][:, gi * HEADS_PER_GROUP:(gi + 1) * HEADS_PER_GROUP].T.astype(_F32)
        rows.append(jnp.pad(bias, ((0, 0), (0, BIAS_LANES - bias.shape[1])), constant_values=NEG_INF))
    return jnp.stack(rows, axis=1)


def _tile_deltas(dil):
    return (0, HALF_WINDOW, 2 * HALF_WINDOW) if SEQ // dil > Q_TILE else (0,)


def _attn_kernel(q0_ref, k0_ref, v0_ref, q1_ref, k1_ref, v1_ref, q2_ref, k2_ref, v2_ref,
                 e_ref, o_ref, t0_ref, t1_ref, t2_ref, o_acc_ref, lse_ref, far_o_ref, far_lse_ref):
    groups = ((q0_ref, k0_ref, v0_ref, t0_ref), (q1_ref, k1_ref, v1_ref, t1_ref), (q2_ref, k2_ref, v2_ref, t2_ref))

    @pl.when(pl.program_id(1) == 0)
    def _():
        for gi, (_, _, _, t_ref) in enumerate(groups):
            dil = ATTN_DILATIONS[gi]
            row = jnp.broadcast_to(e_ref[gi:gi + 1, :], (Q_TILE, BIAS_LANES))
            for vi, delta in enumerate(_tile_deltas(dil)):
                skew = pltpu.roll(row, (delta - HALF_WINDOW) % BIAS_LANES, axis=1, stride=1, stride_axis=0)
                t_ref[vi] = skew[:, :_key_span(dil)]

    def scores(u, gi):
        q_ref, k_ref, _, t_ref = groups[gi]
        dil = ATTN_DILATIONS[gi]
        sub = SEQ // dil
        span = _key_span(dil)
        blocks = sub // Q_TILE
        r = u // blocks
        m0 = (u % blocks) * Q_TILE
        start = jnp.clip(m0 - HALF_WINDOW, 0, sub - span)
        variant = (m0 - start) // HALF_WINDOW
        q = q_ref[pl.ds(pl.multiple_of(u * Q_TILE, Q_TILE), Q_TILE), :]
        k0 = pl.multiple_of(r * sub + start, HALF_WINDOW)
        k = k_ref[pl.ds(k0, span), :]
        s = lax.dot_general(q, k, (((1,), (1,)), ((), ())), preferred_element_type=_F32)
        if dil == 1:
            dest = (gi, pl.ds(pl.multiple_of(m0, Q_TILE), Q_TILE))
        elif blocks > 1:
            dest = (gi, pl.ds(m0 * dil + r, Q_TILE, stride=dil))
        else:
            dest = (None, pl.ds(r * RESIDUE_PITCH, Q_TILE, stride=1))
        return s + t_ref[variant], k0, dest

    def weights(s):
        m = jnp.max(s, axis=-1, keepdims=True)
        return m, jnp.exp((s - m).astype(_BF16))

    ones = jnp.ones((Q_TILE + 2 * HALF_WINDOW, HEAD_DIM), _BF16)

    def values(gi, k0, dest, m, p):
        span = p.shape[1]
        v = groups[gi][2][pl.ds(k0, span), :]
        ov = jnp.dot(p, jnp.concatenate([v, ones[:span]], axis=1), preferred_element_type=_F32)
        l = ov[:, HEAD_DIM:]
        slot, rows = dest
        if slot is None:
            far_o_ref[rows, :] = ov[:, :HEAD_DIM] / l
            far_lse_ref[rows, :] = m + jnp.log(l)
        else:
            o_acc_ref[slot, rows, :] = ov[:, :HEAD_DIM] / l
            lse_ref[slot, rows, :] = m + jnp.log(l)

    def units(it, carry):
        todo = [(it * UNITS_PER_STEP + j, gi) for j in range(UNITS_PER_STEP) for gi in range(3)]
        scored = [scores(u, gi) for u, gi in todo]
        soft = [weights(s) for s, _, _ in scored]
        for (_, gi), (_, k0, dest), (m, p) in zip(todo, scored, soft):
            values(gi, k0, dest, m, p)
        return carry

    lax.fori_loop(0, SEQ // Q_TILE // UNITS_PER_STEP, units, 0)

    far_dil = ATTN_DILATIONS[2]

    def merge(t, carry):
        rows = pl.ds(pl.multiple_of(t * Q_TILE, Q_TILE), Q_TILE)

        def far(ref):
            return jnp.concatenate([ref[pl.ds(t * (Q_TILE // far_dil) + j, far_dil, stride=RESIDUE_PITCH), :]
                                    for j in range(Q_TILE // far_dil)], axis=0)

        lses = [lse_ref[0, rows, :], lse_ref[1, rows, :], far(far_lse_ref)]
        outs = [o_acc_ref[0, rows, :], o_acc_ref[1, rows, :], far(far_o_ref)]
        top = jnp.maximum(jnp.maximum(lses[0], lses[1]), lses[2])
        ws = [jnp.exp(x - top) for x in lses]
        num = ws[0] * outs[0] + ws[1] * outs[1] + ws[2] * outs[2]
        o_ref[rows, :] = (num / (ws[0] + ws[1] + ws[2])).astype(o_ref.dtype)
        return carry

    lax.fori_loop(0, SEQ // Q_TILE, merge, 0)


def _attn_call(qkv, bias_rows):
    batch = qkv[0].shape[0]

    def head_spec(which):
        return pl.BlockSpec((None, SEQ, HEAD_DIM), lambda h, b: (b, 0, which * HEADS_PER_GROUP + h))

    in_specs = [head_spec(which) for gi in range(3) for which in range(3)]
    in_specs.append(pl.BlockSpec((None, 3, BIAS_LANES), lambda h, b: (h, 0, 0)))
    scratch = [pltpu.VMEM((len(_tile_deltas(dil)), Q_TILE, _key_span(dil)), _F32) for dil in ATTN_DILATIONS]
    assert SEQ // ATTN_DILATIONS[2] == Q_TILE and all(SEQ // d > Q_TILE for d in ATTN_DILATIONS[:2])
    scratch += [pltpu.VMEM((2, SEQ, HEAD_DIM), _F32) for _ in range(2)]
    scratch += [pltpu.VMEM((ATTN_DILATIONS[2] * RESIDUE_PITCH, HEAD_DIM), _F32) for _ in range(2)]
    return pl.pallas_call(
        _attn_kernel,
        grid=(HEADS_PER_GROUP, batch),
        in_specs=in_specs,
        out_specs=pl.BlockSpec((None, SEQ, HEAD_DIM), lambda h, b: (b, 0, h)),
        out_shape=jax.ShapeDtypeStruct((batch, SEQ, ATTN_OUT), _BF16),
        scratch_shapes=scratch,
        compiler_params=pltpu.CompilerParams(dimension_semantics=("arbitrary", "arbitrary"),
                                             vmem_limit_bytes=VMEM_LIMIT),
        name="attn",
    )(*[qkv[gi] for gi in range(3) for _ in range(3)], bias_rows)


def _mix_kernel(xn_ref, pool_ref, attn_ref, h_ref, wg_ref, wa_ref, wb_ref, wo_ref, g_ref, h_out_ref, xn_out_ref):
    xn = xn_ref[...]
    gates = jnp.dot(xn, wg_ref[:, GATE_BLOCK - GATE_WIDTH:], preferred_element_type=_F32)
    a = jnp.dot(pool_ref[...], wa_ref[...], preferred_element_type=_F32)
    b = jnp.dot(attn_ref[...], wb_ref[...], preferred_element_type=_F32)
    merged = jax.nn.sigmoid(gates[:, :D_MODEL]) * a + jax.nn.sigmoid(gates[:, D_MODEL:]) * b
    h = h_ref[...] + jnp.dot(merged.astype(_BF16), wo_ref[...], preferred_element_type=_F32)
    h_out_ref[...] = h
    xn_out_ref[...] = _rms(h, g_ref[...]).astype(xn_out_ref.dtype)


def _const_spec(shape):
    return pl.BlockSpec(shape, lambda i: (0,) * len(shape), pipeline_mode=pl.Buffered(1))


def _mix_call(xn2d, pool2d, attn2d, h2d, w_in, w_a, w_b, w_o, g_next, layer):
    rows = xn2d.shape[0]
    tm = ROW_TILE
    assert IN_WIDTH % GATE_BLOCK == 0 and GATE_BLOCK >= GATE_WIDTH
    return pl.pallas_call(
        _mix_kernel,
        grid=(rows // tm,),
        in_specs=[pl.BlockSpec((tm, D_MODEL), lambda i: (i, 0)),
                  pl.BlockSpec((tm, POOL_WIDTH), lambda i: (i, 0)),
                  pl.BlockSpec((tm, ATTN_OUT), lambda i: (i, 0)),
                  pl.BlockSpec((tm, D_MODEL), lambda i: (i, 0)),
                  _layer_spec(layer, (D_MODEL, GATE_BLOCK), (0, IN_WIDTH // GATE_BLOCK - 1)),
                  _layer_spec(layer, (POOL_WIDTH, D_MODEL)),
                  _layer_spec(layer, (ATTN_OUT, D_MODEL)),
                  _layer_spec(layer, (D_MODEL, D_MODEL)),
                  _const_spec((1, D_MODEL))],
        out_specs=[pl.BlockSpec((tm, D_MODEL), lambda i: (i, 0)),
                   pl.BlockSpec((tm, D_MODEL), lambda i: (i, 0))],
        out_shape=[jax.ShapeDtypeStruct((rows, D_MODEL), _F32),
                   jax.ShapeDtypeStruct((rows, D_MODEL), _BF16)],
        compiler_params=pltpu.CompilerParams(dimension_semantics=("parallel",),
                                             vmem_limit_bytes=VMEM_LIMIT),
        name="mix",
    )(xn2d, pool2d, attn2d, h2d, w_in, w_a, w_b, w_o, g_next.reshape(1, D_MODEL))


def _gelu_tanh(x):
    return 0.5 * x * (1.0 + jnp.tanh(np.sqrt(2.0 / np.pi).astype(np.float32) * (x + 0.044715 * (x * x * x))))


def _ffn_kernel(xp_ref, x_ref, xnx_ref, h_ref, wup_ref, cw_ref, cb_ref, wdn_ref, g_ref, *out_refs, final):
    tm = x_ref.shape[0]
    tiles_per_seq = SEQ // tm
    t = pl.program_id(0) % tiles_per_seq
    x = x_ref[...]
    xe = jnp.concatenate([xp_ref[...], x, xnx_ref[...]], axis=0)
    pos = lax.broadcasted_iota(jnp.int32, (tm, 1), 0) + t * tm
    keep_prev = pos > 0
    keep_next = pos < SEQ - 1
    n_ext = tm + 2 * HALO
    acc = None
    for c in range(D_FF // FF_CHUNK):
        cols = slice(c * FF_CHUNK, (c + 1) * FF_CHUNK)
        gcols = slice(D_FF + c * FF_CHUNK, D_FF + (c + 1) * FF_CHUNK)
        a_ext = jnp.dot(xe, wup_ref[:, cols], preferred_element_type=_F32)
        gate = jnp.dot(x, wup_ref[:, gcols], preferred_element_type=_F32)
        a_mid = a_ext[HALO:HALO + tm]
        a_prev = pltpu.roll(a_ext, 1, axis=0)[HALO:HALO + tm]
        a_next = pltpu.roll(a_ext, n_ext - 1, axis=0)[HALO:HALO + tm]
        cw = cw_ref[:, cols]
        conv = (jnp.where(keep_prev, a_prev, 0.0) * cw[0:1] + a_mid * cw[1:2]
                + jnp.where(keep_next, a_next, 0.0) * cw[2:3] + cb_ref[:, cols])
        act = (_gelu_tanh(conv) * gate).astype(_BF16)
        part = jnp.dot(act, wdn_ref[cols, :], preferred_element_type=_F32)
        acc = part if acc is None else acc + part
    h = h_ref[...] + acc
    if final:
        out_refs[0][...] = _rms(h, g_ref[...])
    else:
        out_refs[0][...] = h
        out_refs[1][...] = _rms(h, g_ref[...]).astype(out_refs[1].dtype)


def _ffn_call(xn2d, h2d, w_up, conv_w, conv_b, w_down, g_next, layer, final):
    rows = xn2d.shape[0]
    tm = ROW_TILE
    per = tm // HALO
    last = rows // HALO - 1
    row_spec = pl.BlockSpec((tm, D_MODEL), lambda i: (i, 0))
    if final:
        out_specs = [row_spec]
        out_shape = [jax.ShapeDtypeStruct((rows, D_MODEL), _F32)]
    else:
        out_specs = [row_spec, row_spec]
        out_shape = [jax.ShapeDtypeStruct((rows, D_MODEL), _F32),
                     jax.ShapeDtypeStruct((rows, D_MODEL), _BF16)]
    return pl.pallas_call(
        functools.partial(_ffn_kernel, final=final),
        grid=(rows // tm,),
        in_specs=[pl.BlockSpec((HALO, D_MODEL), lambda i: (jnp.maximum(i * per - 1, 0), 0)),
                  row_spec,
                  pl.BlockSpec((HALO, D_MODEL), lambda i: (jnp.minimum((i + 1) * per, last), 0)),
                  row_spec,
                  _layer_spec(layer, (D_MODEL, 2 * D_FF)),
                  _const_spec((3, D_FF)),
                  _const_spec((1, D_FF)),
                  _layer_spec(layer, (D_FF, D_MODEL)),
                  _const_spec((1, D_MODEL))],
        out_specs=out_specs,
        out_shape=out_shape,
        compiler_params=pltpu.CompilerParams(dimension_semantics=("parallel",),
                                             vmem_limit_bytes=VMEM_LIMIT),
        name="ffn",
    )(xn2d, xn2d, xn2d, h2d, w_up, conv_w, conv_b.reshape(1, D_FF), w_down, g_next.reshape(1, D_MODEL))


def kernel(x, w_in, w_pool, pool_scale, w_a, w_b, w_o, norm1, norm2, w_up, conv_w, conv_b, w_down, rel_bias, norm_f):
    batch, seq, d = x.shape
    assert (seq, d) == (SEQ, D_MODEL)
    depth = w_in.shape[0]
    rows = batch * seq
    bias_rows = _bias_rows(rel_bias)
    h = x.reshape(rows, d)
    xn = _norm_call(h, norm1[0])
    w_in, w_pool, w_a, w_b, w_o, w_up, w_down = (w.astype(_BF16) for w in (w_in, w_pool, w_a, w_b, w_o, w_up, w_down))
    for layer in range(depth):
        xn3 = xn.reshape(batch, seq, d)
        pool = _pool_call(xn3, w_in, w_pool, pool_scale[layer], layer)
        attn = _attn_call([_qkv_call(xn3, w_in, gi, layer) for gi in range(3)], bias_rows)
        h, xn = _mix_call(xn, pool.reshape(rows, POOL_WIDTH), attn.reshape(rows, ATTN_OUT), h,
                          w_in, w_a, w_b, w_o, norm2[layer], layer)
        final = layer == depth - 1
        g_next = norm_f if final else norm1[layer + 1]
        outs = _ffn_call(xn, h, w_up, conv_w[layer], conv_b[layer], w_down, g_next, layer, final)
        if final:
            return outs[0].reshape(batch, seq, d)
        h, xn = outs
```

```python
import functools

import numpy as np
import jax
import jax.numpy as jnp
from jax import lax
from jax.experimental import pallas as pl
from jax.experimental.pallas import tpu as pltpu

D_MODEL = 1024
SEQ = 2048
POOL_WINDOWS = (2, 4, 8, 16)
POOL_GROUP_DIM = 256
POOL_WIDTH = 1024
ATTN_DILATIONS = (1, 4, 16)
HALF_WINDOW = 64
HEADS_PER_GROUP = 4
N_HEADS = 12
HEAD_DIM = 128
ATTN_WIDTH = N_HEADS * HEAD_DIM
ATTN_OUT = HEADS_PER_GROUP * HEAD_DIM
NEG_INF = -1e30
N_BUCKETS = 32
MAX_DISTANCE = 1024
D_FF = 2816
EPS = 1e-6
PROJ_WIDTH = POOL_WIDTH + 3 * ATTN_WIDTH
GATE_WIDTH = 2 * D_MODEL
IN_WIDTH = PROJ_WIDTH + GATE_WIDTH
GATE_BLOCK = 2560

Q_TILE = 128
UNITS_PER_STEP = 4
RESIDUE_PITCH = Q_TILE + 4
NORM_ROWS = 256
PROJ_TN = 512
ROW_TILE = 512
HALO = 16
FF_CHUNK = 1408
VMEM_LIMIT = 56 * 1024 * 1024
LANES = 128
REGROUP_TILE = 256

_F32 = jnp.float32
_BF16 = jnp.bfloat16


def _rms(x, g):
    return x * lax.rsqrt(jnp.mean(x * x, axis=-1, keepdims=True) + EPS) * g


def _const_spec(shape):
    return pl.BlockSpec(shape, lambda i: (0,) * len(shape), pipeline_mode=pl.Buffered(1))


def _layer_spec(layer, shape, index=None):
    index = tuple(index) if index is not None else (0,) * len(shape)
    return pl.BlockSpec((None,) + tuple(shape), lambda *_: (layer,) + index, pipeline_mode=pl.Buffered(1))


POOL_PAD = 16
POOL_EDGE = 8
assert POOL_EDGE >= max(POOL_WINDOWS) // 2 and POOL_PAD - POOL_EDGE >= max(POOL_WINDOWS) // 4


def _pool_kernel(xn_ref, w_ref, wpool_ref, pscale_ref, o_ref, u_ref, t_ref):
    n = SEQ + 2 * POOL_PAD
    slabs = POOL_GROUP_DIM // LANES

    def shifted(view, start, rows, k_back, k_fwd):
        return view[pl.ds(start - k_back, rows, stride=1), :] + view[pl.ds(start + k_fwd, rows, stride=1), :]

    def steps(w):
        return (1, 0) if w == 1 else (w // 2, w // 2)

    zeros = jnp.zeros((POOL_PAD, LANES), _F32)
    for t in range(2):
        for s in range(slabs):
            t_ref[t, s, 0:POOL_EDGE, :] = zeros[:POOL_EDGE]
            t_ref[t, s, n - POOL_EDGE:n, :] = zeros[:POOL_EDGE]
    xn = xn_ref[...]
    for gi in range(len(POOL_WINDOWS)):
        u = jnp.dot(xn, w_ref[:, gi * POOL_GROUP_DIM:(gi + 1) * POOL_GROUP_DIM], preferred_element_type=_F32)
        for s in range(slabs):
            u_ref[gi, s, 0:POOL_PAD, :] = zeros
            u_ref[gi, s, n - POOL_PAD:n, :] = zeros
            u_ref[gi, s, POOL_PAD:POOL_PAD + SEQ, :] = u[:, s * LANES:(s + 1) * LANES]
    edge_row = lax.broadcasted_iota(jnp.int32, (POOL_PAD, POOL_GROUP_DIM), 0)
    for gi, window in enumerate(POOL_WINDOWS):
        cols = slice(gi * POOL_GROUP_DIM, (gi + 1) * POOL_GROUP_DIM)
        totals = []
        for s in range(slabs):
            src, w, slot = u_ref.at[gi, s], 1, 0
            while 2 * w < window:
                t_ref[slot, s, pl.ds(POOL_EDGE, n - 2 * POOL_EDGE), :] = shifted(src, POOL_EDGE, n - 2 * POOL_EDGE,
                                                                              *steps(w))
                src, w, slot = t_ref.at[slot, s], 2 * w, 1 - slot
            totals.append(shifted(src, POOL_PAD, SEQ, *steps(w)))
        total = jnp.concatenate(totals, axis=1)
        u = jnp.concatenate([u_ref[gi, s, POOL_PAD:POOL_PAD + SEQ, :] for s in range(slabs)], axis=1)
        wpool = wpool_ref[gi]
        scale = pscale_ref[:, cols]
        pooled = total * (1.0 / window) - u
        z = jnp.dot(pooled.astype(_BF16), wpool, preferred_element_type=_F32)
        o_ref[:, cols] = (z * scale).astype(o_ref.dtype)
        for r0 in (0, SEQ - POOL_PAD):
            pos = edge_row + r0
            size = (jnp.minimum(pos + window // 2, SEQ) - jnp.maximum(pos - window // 2, 0)).astype(_F32)
            pooled = total[r0:r0 + POOL_PAD] / size - u[r0:r0 + POOL_PAD]
            z = jnp.dot(pooled.astype(_BF16), wpool, preferred_element_type=_F32)
            o_ref[r0:r0 + POOL_PAD, cols] = (z * scale).astype(o_ref.dtype)


def _pool_call(xn3, w_in, w_pool, pool_scale, layer):
    batch = xn3.shape[0]
    return pl.pallas_call(
        _pool_kernel,
        grid=(batch,),
        in_specs=[pl.BlockSpec((None, SEQ, D_MODEL), lambda b: (b, 0, 0)),
                  _layer_spec(layer, (D_MODEL, POOL_WIDTH)),
                  _layer_spec(layer, (len(POOL_WINDOWS), POOL_GROUP_DIM, POOL_GROUP_DIM)),
                  _const_spec((1, POOL_WIDTH))],
        out_specs=pl.BlockSpec((None, SEQ, POOL_WIDTH), lambda b: (b, 0, 0)),
        out_shape=jax.ShapeDtypeStruct((batch, SEQ, POOL_WIDTH), _BF16),
        scratch_shapes=[pltpu.VMEM((len(POOL_WINDOWS), POOL_GROUP_DIM // LANES, SEQ + 2 * POOL_PAD, LANES), _F32),
                        pltpu.VMEM((2, POOL_GROUP_DIM // LANES, SEQ + 2 * POOL_PAD, LANES), _F32)],
        compiler_params=pltpu.CompilerParams(dimension_semantics=("parallel",),
                                             vmem_limit_bytes=VMEM_LIMIT),
        name="pool",
    )(xn3, w_in, w_pool, pool_scale.reshape(1, POOL_WIDTH))


def _qkv_kernel(x_ref, *refs, dil, norm_input):
    if norm_input:
        g_ref, wq_ref, wk_ref, wv_ref, o_ref, xn_ref, *scratch = refs
        for r0 in range(0, SEQ, NORM_ROWS):
            xn_ref[r0:r0 + NORM_ROWS, :] = _rms(x_ref[r0:r0 + NORM_ROWS, :], g_ref[...]).astype(xn_ref.dtype)
        x_ref = xn_ref
    else:
        wq_ref, wk_ref, wv_ref, o_ref, *scratch = refs
    if dil == 1:
        x = x_ref[...]
    else:
        xp_ref, = scratch
        sub = SEQ // dil
        sub_t = REGROUP_TILE // dil
        i = lax.broadcasted_iota(jnp.int32, (REGROUP_TILE, REGROUP_TILE), 0)
        j = lax.broadcasted_iota(jnp.int32, (REGROUP_TILE, REGROUP_TILE), 1)
        pick = (j == (i % sub_t) * dil + i // sub_t).astype(_BF16)
        for t in range(SEQ // REGROUP_TILE):
            tile = x_ref[t * REGROUP_TILE:(t + 1) * REGROUP_TILE, :]
            srt = jnp.dot(pick, tile, preferred_element_type=_F32).astype(_BF16)
            for r in range(dil):
                xp_ref[r * sub + t * sub_t:r * sub + (t + 1) * sub_t, :] = srt[r * sub_t:(r + 1) * sub_t]
        x = xp_ref[...]
    for which, w_ref in enumerate((wq_ref, wk_ref, wv_ref)):
        y = jnp.dot(x, w_ref[...], preferred_element_type=_F32)
        if which == 0:
            y = y * HEAD_DIM ** -0.5
        o_ref[:, which * PROJ_TN:(which + 1) * PROJ_TN] = y.astype(o_ref.dtype)


def _qkv_call(x3, w_in, gi, layer, norm_gain=None):
    batch = x3.shape[0]
    dil = ATTN_DILATIONS[gi]
    first = POOL_WIDTH // PROJ_TN + gi
    norm_input = norm_gain is not None
    x_spec = pl.BlockSpec((None, SEQ, D_MODEL), lambda b: (b, 0, 0))
    in_specs = [x_spec] + ([_const_spec((1, D_MODEL))] if norm_input else [])
    in_specs += [_layer_spec(layer, (D_MODEL, PROJ_TN), (0, first + 3 * which)) for which in range(3)]
    out_specs = [pl.BlockSpec((None, SEQ, 3 * PROJ_TN), lambda b: (b, 0, 0))]
    out_shape = [jax.ShapeDtypeStruct((batch, SEQ, 3 * PROJ_TN), _BF16)]
    if norm_input:
        out_specs.append(x_spec)
        out_shape.append(jax.ShapeDtypeStruct((batch, SEQ, D_MODEL), _BF16))
    args = (x3,) + ((norm_gain.reshape(1, D_MODEL),) if norm_input else ()) + (w_in, w_in, w_in)
    outs = pl.pallas_call(
        functools.partial(_qkv_kernel, dil=dil, norm_input=norm_input),
        grid=(batch,),
        in_specs=in_specs,
        out_specs=out_specs,
        out_shape=out_shape,
        scratch_shapes=[] if dil == 1 else [pltpu.VMEM((SEQ, D_MODEL), _BF16)],
        compiler_params=pltpu.CompilerParams(dimension_semantics=("parallel",),
                                             vmem_limit_bytes=VMEM_LIMIT),
        name=f"qkv{gi}",
    )(*args)
    return outs if norm_input else outs[0]


def _t5_buckets_np(rel):
    n = -rel
    half = N_BUCKETS // 2
    ret = (n < 0).astype(np.int32) * half
    n = np.abs(n)
    max_exact = half // 2
    large = max_exact + (np.log(np.maximum(n, 1) / max_exact)
                         / np.log(MAX_DISTANCE / max_exact) * (half - max_exact)).astype(np.int32)
    large = np.minimum(large, half - 1)
    return (ret + np.where(n < max_exact, n, large)).astype(np.int32)


def _key_span(dil):
    return min(Q_TILE + 2 * HALF_WINDOW, SEQ // dil)


BIAS_LANES = 512


def _bias_rows(rel_bias):
    rows = []
    for gi, dil in enumerate(ATTN_DILATIONS):
        buckets = _t5_buckets_np(dil * np.arange(-HALF_WINDOW, HALF_WINDOW + 1))
        bias = rel_bias[buckets][:, gi * HEADS_PER_GROUP:(gi + 1) * HEADS_PER_GROUP].T.astype(_F32)
        rows.append(jnp.pad(bias, ((0, 0), (0, BIAS_LANES - bias.shape[1])), constant_values=NEG_INF))
    return jnp.stack(rows, axis=1)


def _tile_deltas(dil):
    return (0, HALF_WINDOW, 2 * HALF_WINDOW) if SEQ // dil > Q_TILE else (0,)


def _attn_kernel(q0_ref, k0_ref, v0_ref, q1_ref, k1_ref, v1_ref, q2_ref, k2_ref, v2_ref,
                 e_ref, o_ref, t0_ref, t1_ref, t2_ref, o_acc_ref, lse_ref, far_o_ref, far_lse_ref):
    groups = ((q0_ref, k0_ref, v0_ref, t0_ref), (q1_ref, k1_ref, v1_ref, t1_ref), (q2_ref, k2_ref, v2_ref, t2_ref))

    @pl.when(pl.program_id(1) == 0)
    def _():
        for gi, (_, _, _, t_ref) in enumerate(groups):
            dil = ATTN_DILATIONS[gi]
            row = jnp.broadcast_to(e_ref[gi:gi + 1, :], (Q_TILE, BIAS_LANES))
            for vi, delta in enumerate(_tile_deltas(dil)):
                skew = pltpu.roll(row, (delta - HALF_WINDOW) % BIAS_LANES, axis=1, stride=1, stride_axis=0)
                t_ref[vi] = skew[:, :_key_span(dil)]

    def scores(u, gi):
        q_ref, k_ref, _, t_ref = groups[gi]
        dil = ATTN_DILATIONS[gi]
        sub = SEQ // dil
        span = _key_span(dil)
        blocks = sub // Q_TILE
        r = u // blocks
        m0 = (u % blocks) * Q_TILE
        start = jnp.clip(m0 - HALF_WINDOW, 0, sub - span)
        variant = (m0 - start) // HALF_WINDOW
        q = q_ref[pl.ds(pl.multiple_of(u * Q_TILE, Q_TILE), Q_TILE), :]
        k0 = pl.multiple_of(r * sub + start, HALF_WINDOW)
        k = k_ref[pl.ds(k0, span), :]
        s = lax.dot_general(q, k, (((1,), (1,)), ((), ())), preferred_element_type=_F32)
        if dil == 1:
            dest = (gi, pl.ds(pl.multiple_of(m0, Q_TILE), Q_TILE))
        elif blocks > 1:
            dest = (gi, pl.ds(m0 * dil + r, Q_TILE, stride=dil))
        else:
            dest = (None, pl.ds(r * RESIDUE_PITCH, Q_TILE, stride=1))
        return s + t_ref[variant], k0, dest

    def weights(s):
        m = jnp.max(s, axis=-1, keepdims=True)
        return m, jnp.exp((s - m).astype(_BF16))

    ones = jnp.ones((Q_TILE + 2 * HALF_WINDOW, HEAD_DIM), _BF16)

    def values(gi, k0, dest, m, p):
        span = p.shape[1]
        v = groups[gi][2][pl.ds(k0, span), :]
        ov = jnp.dot(p, jnp.concatenate([v, ones[:span]], axis=1), preferred_element_type=_F32)
        l = ov[:, HEAD_DIM:]
        slot, rows = dest
        if slot is None:
            far_o_ref[rows, :] = ov[:, :HEAD_DIM] / l
            far_lse_ref[rows, :] = m + jnp.log(l)
        else:
            o_acc_ref[slot, rows, :] = ov[:, :HEAD_DIM] / l
            lse_ref[slot, rows, :] = m + jnp.log(l)

    def units(it, carry):
        todo = [(it * UNITS_PER_STEP + j, gi) for j in range(UNITS_PER_STEP) for gi in range(3)]
        scored = [scores(u, gi) for u, gi in todo]
        soft = [weights(s) for s, _, _ in scored]
        for (_, gi), (_, k0, dest), (m, p) in zip(todo, scored, soft):
            values(gi, k0, dest, m, p)
        return carry

    lax.fori_loop(0, SEQ // Q_TILE // UNITS_PER_STEP, units, 0)

    far_dil = ATTN_DILATIONS[2]

    def merge(t, carry):
        rows = pl.ds(pl.multiple_of(t * Q_TILE, Q_TILE), Q_TILE)

        def far(ref):
            return jnp.concatenate([ref[pl.ds(t * (Q_TILE // far_dil) + j, far_dil, stride=RESIDUE_PITCH), :]
                                    for j in range(Q_TILE // far_dil)], axis=0)

        lses = [lse_ref[0, rows, :], lse_ref[1, rows, :], far(far_lse_ref)]
        outs = [o_acc_ref[0, rows, :], o_acc_ref[1, rows, :], far(far_o_ref)]
        top = jnp.maximum(jnp.maximum(lses[0], lses[1]), lses[2])
        ws = [jnp.exp(x - top) for x in lses]
        num = ws[0] * outs[0] + ws[1] * outs[1] + ws[2] * outs[2]
        o_ref[rows, :] = (num / (ws[0] + ws[1] + ws[2])).astype(o_ref.dtype)
        return carry

    lax.fori_loop(0, SEQ // Q_TILE, merge, 0)


def _attn_call(qkv, bias_rows):
    batch = qkv[0].shape[0]

    def head_spec(which):
        return pl.BlockSpec((None, SEQ, HEAD_DIM), lambda h, b: (b, 0, which * HEADS_PER_GROUP + h))

    in_specs = [head_spec(which) for gi in range(3) for which in range(3)]
    in_specs.append(pl.BlockSpec((None, 3, BIAS_LANES), lambda h, b: (h, 0, 0)))
    scratch = [pltpu.VMEM((len(_tile_deltas(dil)), Q_TILE, _key_span(dil)), _F32) for dil in ATTN_DILATIONS]
    assert SEQ // ATTN_DILATIONS[2] == Q_TILE and all(SEQ // d > Q_TILE for d in ATTN_DILATIONS[:2])
    scratch += [pltpu.VMEM((2, SEQ, HEAD_DIM), _F32) for _ in range(2)]
    scratch += [pltpu.VMEM((ATTN_DILATIONS[2] * RESIDUE_PITCH, HEAD_DIM), _F32) for _ in range(2)]
    return pl.pallas_call(
        _attn_kernel,
        grid=(HEADS_PER_GROUP, batch),
        in_specs=in_specs,
        out_specs=pl.BlockSpec((None, SEQ, HEAD_DIM), lambda h, b: (b, 0, h)),
        out_shape=jax.ShapeDtypeStruct((batch, SEQ, ATTN_OUT), _BF16),
        scratch_shapes=scratch,
        compiler_params=pltpu.CompilerParams(dimension_semantics=("arbitrary", "arbitrary"),
                                             vmem_limit_bytes=VMEM_LIMIT),
        name="attn",
    )(*[qkv[gi] for gi in range(3) for _ in range(3)], bias_rows)


def _mix_kernel(xn_ref, pool_ref, attn_ref, h_ref, wg_ref, wa_ref, wb_ref, wo_ref, g_ref, h_out_ref, xn_out_ref):
    xn = xn_ref[...]
    gates = jnp.dot(xn, wg_ref[:, GATE_BLOCK - GATE_WIDTH:], preferred_element_type=_F32)
    a = jnp.dot(pool_ref[...], wa_ref[...], preferred_element_type=_F32)
    b = jnp.dot(attn_ref[...], wb_ref[...], preferred_element_type=_F32)
    merged = jax.nn.sigmoid(gates[:, :D_MODEL]) * a + jax.nn.sigmoid(gates[:, D_MODEL:]) * b
    h = h_ref[...] + jnp.dot(merged.astype(_BF16), wo_ref[...], preferred_element_type=_F32)
    h_out_ref[...] = h
    xn_out_ref[...] = _rms(h, g_ref[...]).astype(xn_out_ref.dtype)


def _mix_call(xn2d, pool2d, attn2d, h2d, w_in, w_a, w_b, w_o, g_next, layer):
    rows = xn2d.shape[0]
    tm = ROW_TILE
    assert IN_WIDTH % GATE_BLOCK == 0 and GATE_BLOCK >= GATE_WIDTH
    return pl.pallas_call(
        _mix_kernel,
        grid=(rows // tm,),
        in_specs=[pl.BlockSpec((tm, D_MODEL), lambda i: (i, 0)),
                  pl.BlockSpec((tm, POOL_WIDTH), lambda i: (i, 0)),
                  pl.BlockSpec((tm, ATTN_OUT), lambda i: (i, 0)),
                  pl.BlockSpec((tm, D_MODEL), lambda i: (i, 0)),
                  _layer_spec(layer, (D_MODEL, GATE_BLOCK), (0, IN_WIDTH // GATE_BLOCK - 1)),
                  _layer_spec(layer, (POOL_WIDTH, D_MODEL)),
                  _layer_spec(layer, (ATTN_OUT, D_MODEL)),
                  _layer_spec(layer, (D_MODEL, D_MODEL)),
                  _const_spec((1, D_MODEL))],
        out_specs=[pl.BlockSpec((tm, D_MODEL), lambda i: (i, 0)),
                   pl.BlockSpec((tm, D_MODEL), lambda i: (i, 0))],
        out_shape=[jax.ShapeDtypeStruct((rows, D_MODEL), _F32),
                   jax.ShapeDtypeStruct((rows, D_MODEL), _BF16)],
        compiler_params=pltpu.CompilerParams(dimension_semantics=("parallel",),
                                             vmem_limit_bytes=VMEM_LIMIT),
        name="mix",
    )(xn2d, pool2d, attn2d, h2d, w_in, w_a, w_b, w_o, g_next.reshape(1, D_MODEL))


def _gelu_tanh(x):
    return 0.5 * x * (1.0 + jnp.tanh(np.sqrt(2.0 / np.pi).astype(np.float32) * (x + 0.044715 * (x * x * x))))


def _ffn_kernel(xp_ref, x_ref, xnx_ref, h_ref, wup_ref, cw_ref, cb_ref, wdn_ref, g_ref, *out_refs, final):
    tm = x_ref.shape[0]
    tiles_per_seq = SEQ // tm
    t = pl.program_id(0) % tiles_per_seq
    x = x_ref[...]
    xe = jnp.concatenate([xp_ref[...], x, xnx_ref[...]], axis=0)
    pos = lax.broadcasted_iota(jnp.int32, (tm, 1), 0) + t * tm
    keep_prev = pos > 0
    keep_next = pos < SEQ - 1
    n_ext = tm + 2 * HALO
    acc = None
    for c in range(D_FF // FF_CHUNK):
        cols = slice(c * FF_CHUNK, (c + 1) * FF_CHUNK)
        gcols = slice(D_FF + c * FF_CHUNK, D_FF + (c + 1) * FF_CHUNK)
        a_ext = jnp.dot(xe, wup_ref[:, cols], preferred_element_type=_F32)
        gate = jnp.dot(x, wup_ref[:, gcols], preferred_element_type=_F32)
        a_mid = a_ext[HALO:HALO + tm]
        a_prev = pltpu.roll(a_ext, 1, axis=0)[HALO:HALO + tm]
        a_next = pltpu.roll(a_ext, n_ext - 1, axis=0)[HALO:HALO + tm]
        cw = cw_ref[:, cols]
        conv = (jnp.where(keep_prev, a_prev, 0.0) * cw[0:1] + a_mid * cw[1:2]
                + jnp.where(keep_next, a_next, 0.0) * cw[2:3] + cb_ref[:, cols])
        act = (_gelu_tanh(conv) * gate).astype(_BF16)
        part = jnp.dot(act, wdn_ref[cols, :], preferred_element_type=_F32)
        acc = part if acc is None else acc + part
    h = h_ref[...] + acc
    if final:
        out_refs[0][...] = _rms(h, g_ref[...])
    else:
        out_refs[0][...] = h
        out_refs[1][...] = _rms(h, g_ref[...]).astype(out_refs[1].dtype)


def _ffn_call(xn2d, h2d, w_up, conv_w, conv_b, w_down, g_next, layer, final):
    rows = xn2d.shape[0]
    tm = ROW_TILE
    per = tm // HALO
    last = rows // HALO - 1
    row_spec = pl.BlockSpec((tm, D_MODEL), lambda i: (i, 0))
    if final:
        out_specs = [row_spec]
        out_shape = [jax.ShapeDtypeStruct((rows, D_MODEL), _F32)]
    else:
        out_specs = [row_spec, row_spec]
        out_shape = [jax.ShapeDtypeStruct((rows, D_MODEL), _F32),
                     jax.ShapeDtypeStruct((rows, D_MODEL), _BF16)]
    return pl.pallas_call(
        functools.partial(_ffn_kernel, final=final),
        grid=(rows // tm,),
        in_specs=[pl.BlockSpec((HALO, D_MODEL), lambda i: (jnp.maximum(i * per - 1, 0), 0)),
                  row_spec,
                  pl.BlockSpec((HALO, D_MODEL), lambda i: (jnp.minimum((i + 1) * per, last), 0)),
                  row_spec,
                  _layer_spec(layer, (D_MODEL, 2 * D_FF)),
                  _const_spec((3, D_FF)),
                  _const_spec((1, D_FF)),
                  _layer_spec(layer, (D_FF, D_MODEL)),
                  _const_spec((1, D_MODEL))],
        out_specs=out_specs,
        out_shape=out_shape,
        compiler_params=pltpu.CompilerParams(dimension_semantics=("parallel",),
                                             vmem_limit_bytes=VMEM_LIMIT),
        name="ffn",
    )(xn2d, xn2d, xn2d, h2d, w_up, conv_w, conv_b.reshape(1, D_FF), w_down, g_next.reshape(1, D_MODEL))


def kernel(x, w_in, w_pool, pool_scale, w_a, w_b, w_o, norm1, norm2, w_up, conv_w, conv_b, w_down, rel_bias, norm_f):
    batch, seq, d = x.shape
    assert (seq, d) == (SEQ, D_MODEL)
    depth = w_in.shape[0]
    rows = batch * seq
    bias_rows = _bias_rows(rel_bias)
    h = x.reshape(rows, d)
    xn = None
    w_in, w_pool, w_a, w_b, w_o, w_up, w_down = (w.astype(_BF16) for w in (w_in, w_pool, w_a, w_b, w_o, w_up, w_down))
    for layer in range(depth):
        if xn is None:
            qkv0, xn3 = _qkv_call(x, w_in, 0, layer, norm_gain=norm1[layer])
            xn = xn3.reshape(rows, d)
        else:
            xn3 = xn.reshape(batch, seq, d)
            qkv0 = _qkv_call(xn3, w_in, 0, layer)
        pool = _pool_call(xn3, w_in, w_pool, pool_scale[layer], layer)
        attn = _attn_call([qkv0] + [_qkv_call(xn3, w_in, gi, layer) for gi in (1, 2)], bias_rows)
        h, xn = _mix_call(xn, pool.reshape(rows, POOL_WIDTH), attn.reshape(rows, ATTN_OUT), h,
                          w_in, w_a, w_b, w_o, norm2[layer], layer)
        final = layer == depth - 1
        g_next = norm_f if final else norm1[layer + 1]
        outs = _ffn_call(xn, h, w_up, conv_w[layer], conv_b[layer], w_down, g_next, layer, final)
        if final:
            return outs[0].reshape(batch, seq, d)
        h, xn = outs
```

```python
import functools

import numpy as np
import jax
import jax.numpy as jnp
from jax import lax
from jax.experimental import pallas as pl
from jax.experimental.pallas import tpu as pltpu

D_MODEL = 1024
SEQ = 2048
POOL_WINDOWS = (2, 4, 8, 16)
POOL_GROUP_DIM = 256
POOL_WIDTH = 1024
ATTN_DILATIONS = (1, 4, 16)
HALF_WINDOW = 64
HEADS_PER_GROUP = 4
N_HEADS = 12
HEAD_DIM = 128
ATTN_WIDTH = N_HEADS * HEAD_DIM
ATTN_OUT = HEADS_PER_GROUP * HEAD_DIM
NEG_INF = -1e30
N_BUCKETS = 32
MAX_DISTANCE = 1024
D_FF = 2816
EPS = 1e-6
PROJ_WIDTH = POOL_WIDTH + 3 * ATTN_WIDTH
GATE_WIDTH = 2 * D_MODEL
IN_WIDTH = PROJ_WIDTH + GATE_WIDTH
GATE_BLOCK = 2560

Q_TILE = 128
UNITS_PER_STEP = 8
LOG2_E = float(np.log2(np.e))
RESIDUE_PITCH = Q_TILE + 4
NORM_ROWS = 256
PROJ_TN = 512
ROW_TILE = 512
HALO = 16
FF_CHUNK = 1408
VMEM_LIMIT = 56 * 1024 * 1024
LANES = 128
REGROUP_TILE = 256

_F32 = jnp.float32
_BF16 = jnp.bfloat16


def _rms(x, g):
    return x * lax.rsqrt(jnp.mean(x * x, axis=-1, keepdims=True) + EPS) * g


def _const_spec(shape):
    return pl.BlockSpec(shape, lambda i: (0,) * len(shape), pipeline_mode=pl.Buffered(1))


def _layer_spec(layer, shape, index=None):
    index = tuple(index) if index is not None else (0,) * len(shape)
    return pl.BlockSpec((None,) + tuple(shape), lambda *_: (layer,) + index, pipeline_mode=pl.Buffered(1))


POOL_PAD = 16
POOL_EDGE = 8
assert POOL_EDGE >= max(POOL_WINDOWS) // 2 and POOL_PAD - POOL_EDGE >= max(POOL_WINDOWS) // 4


def _pool_kernel(xn_ref, w_ref, wpool_ref, pscale_ref, o_ref, u_ref, t_ref):
    n = SEQ + 2 * POOL_PAD
    slabs = POOL_GROUP_DIM // LANES

    def shifted(view, start, rows, k_back, k_fwd):
        return view[pl.ds(start - k_back, rows, stride=1), :] + view[pl.ds(start + k_fwd, rows, stride=1), :]

    def steps(w):
        return (1, 0) if w == 1 else (w // 2, w // 2)

    zeros = jnp.zeros((POOL_PAD, LANES), _F32)
    for t in range(2):
        for s in range(slabs):
            t_ref[t, s, 0:POOL_EDGE, :] = zeros[:POOL_EDGE]
            t_ref[t, s, n - POOL_EDGE:n, :] = zeros[:POOL_EDGE]
    xn = xn_ref[...]
    for gi in range(len(POOL_WINDOWS)):
        u = jnp.dot(xn, w_ref[:, gi * POOL_GROUP_DIM:(gi + 1) * POOL_GROUP_DIM], preferred_element_type=_F32)
        for s in range(slabs):
            u_ref[gi, s, 0:POOL_PAD, :] = zeros
            u_ref[gi, s, n - POOL_PAD:n, :] = zeros
            u_ref[gi, s, POOL_PAD:POOL_PAD + SEQ, :] = u[:, s * LANES:(s + 1) * LANES]
    edge_row = lax.broadcasted_iota(jnp.int32, (POOL_PAD, POOL_GROUP_DIM), 0)
    for gi, window in enumerate(POOL_WINDOWS):
        cols = slice(gi * POOL_GROUP_DIM, (gi + 1) * POOL_GROUP_DIM)
        totals = []
        for s in range(slabs):
            src, w, slot = u_ref.at[gi, s], 1, 0
            while 2 * w < window:
                t_ref[slot, s, pl.ds(POOL_EDGE, n - 2 * POOL_EDGE), :] = shifted(src, POOL_EDGE, n - 2 * POOL_EDGE,
                                                                              *steps(w))
                src, w, slot = t_ref.at[slot, s], 2 * w, 1 - slot
            totals.append(shifted(src, POOL_PAD, SEQ, *steps(w)))
        total = jnp.concatenate(totals, axis=1)
        u = jnp.concatenate([u_ref[gi, s, POOL_PAD:POOL_PAD + SEQ, :] for s in range(slabs)], axis=1)
        wpool = wpool_ref[gi]
        scale = pscale_ref[:, cols]
        pooled = total * (1.0 / window) - u
        z = jnp.dot(pooled.astype(_BF16), wpool, preferred_element_type=_F32)
        o_ref[:, cols] = (z * scale).astype(o_ref.dtype)
        for r0 in (0, SEQ - POOL_PAD):
            pos = edge_row + r0
            size = (jnp.minimum(pos + window // 2, SEQ) - jnp.maximum(pos - window // 2, 0)).astype(_F32)
            pooled = total[r0:r0 + POOL_PAD] / size - u[r0:r0 + POOL_PAD]
            z = jnp.dot(pooled.astype(_BF16), wpool, preferred_element_type=_F32)
            o_ref[r0:r0 + POOL_PAD, cols] = (z * scale).astype(o_ref.dtype)


def _pool_call(xn3, w_in, w_pool, pool_scale, layer):
    batch = xn3.shape[0]
    return pl.pallas_call(
        _pool_kernel,
        grid=(batch,),
        in_specs=[pl.BlockSpec((None, SEQ, D_MODEL), lambda b: (b, 0, 0)),
                  _layer_spec(layer, (D_MODEL, POOL_WIDTH)),
                  _layer_spec(layer, (len(POOL_WINDOWS), POOL_GROUP_DIM, POOL_GROUP_DIM)),
                  _const_spec((1, POOL_WIDTH))],
        out_specs=pl.BlockSpec((None, SEQ, POOL_WIDTH), lambda b: (b, 0, 0)),
        out_shape=jax.ShapeDtypeStruct((batch, SEQ, POOL_WIDTH), _BF16),
        scratch_shapes=[pltpu.VMEM((len(POOL_WINDOWS), POOL_GROUP_DIM // LANES, SEQ + 2 * POOL_PAD, LANES), _F32),
                        pltpu.VMEM((2, POOL_GROUP_DIM // LANES, SEQ + 2 * POOL_PAD, LANES), _F32)],
        compiler_params=pltpu.CompilerParams(dimension_semantics=("parallel",),
                                             vmem_limit_bytes=VMEM_LIMIT),
        name="pool",
    )(xn3, w_in, w_pool, pool_scale.reshape(1, POOL_WIDTH))


def _qkv_kernel(x_ref, *refs, dil, norm_input):
    if norm_input:
        g_ref, wq_ref, wk_ref, wv_ref, o_ref, xn_ref, *scratch = refs
        for r0 in range(0, SEQ, NORM_ROWS):
            xn_ref[r0:r0 + NORM_ROWS, :] = _rms(x_ref[r0:r0 + NORM_ROWS, :], g_ref[...]).astype(xn_ref.dtype)
        x_ref = xn_ref
    else:
        wq_ref, wk_ref, wv_ref, o_ref, *scratch = refs
    if dil == 1:
        x = x_ref[...]
    else:
        xp_ref, = scratch
        sub = SEQ // dil
        sub_t = REGROUP_TILE // dil
        i = lax.broadcasted_iota(jnp.int32, (REGROUP_TILE, REGROUP_TILE), 0)
        j = lax.broadcasted_iota(jnp.int32, (REGROUP_TILE, REGROUP_TILE), 1)
        pick = (j == (i % sub_t) * dil + i // sub_t).astype(_BF16)
        for t in range(SEQ // REGROUP_TILE):
            tile = x_ref[t * REGROUP_TILE:(t + 1) * REGROUP_TILE, :]
            srt = jnp.dot(pick, tile, preferred_element_type=_F32).astype(_BF16)
            for r in range(dil):
                xp_ref[r * sub + t * sub_t:r * sub + (t + 1) * sub_t, :] = srt[r * sub_t:(r + 1) * sub_t]
        x = xp_ref[...]
    for which, w_ref in enumerate((wq_ref, wk_ref, wv_ref)):
        y = jnp.dot(x, w_ref[...], preferred_element_type=_F32)
        if which == 0:
            y = y * HEAD_DIM ** -0.5
        o_ref[:, which * PROJ_TN:(which + 1) * PROJ_TN] = y.astype(o_ref.dtype)


def _qkv_call(x3, w_in, gi, layer, norm_gain=None):
    batch = x3.shape[0]
    dil = ATTN_DILATIONS[gi]
    first = POOL_WIDTH // PROJ_TN + gi
    norm_input = norm_gain is not None
    x_spec = pl.BlockSpec((None, SEQ, D_MODEL), lambda b: (b, 0, 0))
    in_specs = [x_spec] + ([_const_spec((1, D_MODEL))] if norm_input else [])
    in_specs += [_layer_spec(layer, (D_MODEL, PROJ_TN), (0, first + 3 * which)) for which in range(3)]
    out_specs = [pl.BlockSpec((None, SEQ, 3 * PROJ_TN), lambda b: (b, 0, 0))]
    out_shape = [jax.ShapeDtypeStruct((batch, SEQ, 3 * PROJ_TN), _BF16)]
    if norm_input:
        out_specs.append(x_spec)
        out_shape.append(jax.ShapeDtypeStruct((batch, SEQ, D_MODEL), _BF16))
    args = (x3,) + ((norm_gain.reshape(1, D_MODEL),) if norm_input else ()) + (w_in, w_in, w_in)
    outs = pl.pallas_call(
        functools.partial(_qkv_kernel, dil=dil, norm_input=norm_input),
        grid=(batch,),
        in_specs=in_specs,
        out_specs=out_specs,
        out_shape=out_shape,
        scratch_shapes=[] if dil == 1 else [pltpu.VMEM((SEQ, D_MODEL), _BF16)],
        compiler_params=pltpu.CompilerParams(dimension_semantics=("parallel",),
                                             vmem_limit_bytes=VMEM_LIMIT),
        name=f"qkv{gi}",
    )(*args)
    return outs if norm_input else outs[0]


def _t5_buckets_np(rel):
    n = -rel
    half = N_BUCKETS // 2
    ret = (n < 0).astype(np.int32) * half
    n = np.abs(n)
    max_exact = half // 2
    large = max_exact + (np.log(np.maximum(n, 1) / max_exact)
                         / np.log(MAX_DISTANCE / max_exact) * (half - max_exact)).astype(np.int32)
    large = np.minimum(large, half - 1)
    return (ret + np.where(n < max_exact, n, large)).astype(np.int32)


def _key_span(dil):
    return min(Q_TILE + 2 * HALF_WINDOW, SEQ // dil)


BIAS_LANES = 512


def _bias_rows(rel_bias):
    rows = []
    for gi, dil in enumerate(ATTN_DILATIONS):
        buckets = _t5_buckets_np(dil * np.arange(-HALF_WINDOW, HALF_WINDOW + 1))
        bias = rel_bias[buckets][:, gi * HEADS_PER_GROUP:(gi + 1) * HEADS_PER_GROUP].T.astype(_F32)
        rows.append(jnp.pad(bias, ((0, 0), (0, BIAS_LANES - bias.shape[1])), constant_values=NEG_INF))
    return jnp.stack(rows, axis=1)


def _tile_deltas(dil):
    return (0, HALF_WINDOW, 2 * HALF_WINDOW) if SEQ // dil > Q_TILE else (0,)


def _attn_kernel(q0_ref, k0_ref, v0_ref, q1_ref, k1_ref, v1_ref, q2_ref, k2_ref, v2_ref,
                 e_ref, o_ref, t0_ref, t1_ref, t2_ref, o_acc_ref, lse_ref, far_o_ref, far_lse_ref):
    groups = ((q0_ref, k0_ref, v0_ref, t0_ref), (q1_ref, k1_ref, v1_ref, t1_ref), (q2_ref, k2_ref, v2_ref, t2_ref))

    @pl.when(pl.program_id(1) == 0)
    def _():
        for gi, (_, _, _, t_ref) in enumerate(groups):
            dil = ATTN_DILATIONS[gi]
            row = jnp.broadcast_to(e_ref[gi:gi + 1, :], (Q_TILE, BIAS_LANES))
            for vi, delta in enumerate(_tile_deltas(dil)):
                skew = pltpu.roll(row, (delta - HALF_WINDOW) % BIAS_LANES, axis=1, stride=1, stride_axis=0)
                t_ref[vi] = skew[:, :_key_span(dil)]

    def scores(u, gi):
        q_ref, k_ref, _, t_ref = groups[gi]
        dil = ATTN_DILATIONS[gi]
        sub = SEQ // dil
        span = _key_span(dil)
        blocks = sub // Q_TILE
        r = u // blocks
        m0 = (u % blocks) * Q_TILE
        start = jnp.clip(m0 - HALF_WINDOW, 0, sub - span)
        variant = (m0 - start) // HALF_WINDOW
        q = q_ref[pl.ds(pl.multiple_of(u * Q_TILE, Q_TILE), Q_TILE), :]
        k0 = pl.multiple_of(r * sub + start, HALF_WINDOW)
        k = k_ref[pl.ds(k0, span), :]
        s = lax.dot_general(q, k, (((1,), (1,)), ((), ())), preferred_element_type=_F32)
        if dil == 1:
            dest = (gi, pl.ds(pl.multiple_of(m0, Q_TILE), Q_TILE))
        elif blocks > 1:
            dest = (gi, pl.ds(m0 * dil + r, Q_TILE, stride=dil))
        else:
            dest = (None, pl.ds(r * RESIDUE_PITCH, Q_TILE, stride=1))
        return s + t_ref[variant], k0, dest

    def weights(s):
        m = jnp.max(s, axis=-1, keepdims=True)
        return m, jnp.exp((s - m).astype(_BF16))

    ones = jnp.ones((Q_TILE + 2 * HALF_WINDOW, HEAD_DIM), _BF16)

    def values(gi, k0, dest, m, p):
        span = p.shape[1]
        v = groups[gi][2][pl.ds(k0, span), :]
        ov = jnp.dot(p, jnp.concatenate([v, ones[:span]], axis=1), preferred_element_type=_F32)
        l = ov[:, HEAD_DIM:]
        lse2 = m * LOG2_E + jnp.log2(l)
        slot, rows = dest
        if slot is None:
            far_o_ref[rows, :] = ov[:, :HEAD_DIM] / l
            far_lse_ref[rows, :] = lse2
        else:
            o_acc_ref[slot, rows, :] = ov[:, :HEAD_DIM] / l
            lse_ref[slot, rows, :] = lse2

    def units(it, carry):
        todo = [(it * UNITS_PER_STEP + j, gi) for j in range(UNITS_PER_STEP) for gi in range(3)]
        scored = [scores(u, gi) for u, gi in todo]
        soft = [weights(s) for s, _, _ in scored]
        for (_, gi), (_, k0, dest), (m, p) in zip(todo, scored, soft):
            values(gi, k0, dest, m, p)
        return carry

    lax.fori_loop(0, SEQ // Q_TILE // UNITS_PER_STEP, units, 0)

    far_dil = ATTN_DILATIONS[2]

    def merge(t, carry):
        rows = pl.ds(pl.multiple_of(t * Q_TILE, Q_TILE), Q_TILE)

        def far(ref):
            return jnp.concatenate([ref[pl.ds(t * (Q_TILE // far_dil) + j, far_dil, stride=RESIDUE_PITCH), :]
                                    for j in range(Q_TILE // far_dil)], axis=0)

        lses = [lse_ref[0, rows, :], lse_ref[1, rows, :], far(far_lse_ref)]
        outs = [o_acc_ref[0, rows, :], o_acc_ref[1, rows, :], far(far_o_ref)]
        top = jnp.maximum(jnp.maximum(lses[0], lses[1]), lses[2])
        ws = [jnp.exp2(x - top) for x in lses]
        num = ws[0] * outs[0] + ws[1] * outs[1] + ws[2] * outs[2]
        o_ref[rows, :] = (num / (ws[0] + ws[1] + ws[2])).astype(o_ref.dtype)
        return carry

    lax.fori_loop(0, SEQ // Q_TILE, merge, 0, unroll=4)


def _attn_call(qkv, bias_rows):
    batch = qkv[0].shape[0]

    def head_spec(which):
        return pl.BlockSpec((None, SEQ, HEAD_DIM), lambda h, b: (b, 0, which * HEADS_PER_GROUP + h))

    in_specs = [head_spec(which) for gi in range(3) for which in range(3)]
    in_specs.append(pl.BlockSpec((None, 3, BIAS_LANES), lambda h, b: (h, 0, 0)))
    scratch = [pltpu.VMEM((len(_tile_deltas(dil)), Q_TILE, _key_span(dil)), _F32) for dil in ATTN_DILATIONS]
    assert SEQ // ATTN_DILATIONS[2] == Q_TILE and all(SEQ // d > Q_TILE for d in ATTN_DILATIONS[:2])
    scratch += [pltpu.VMEM((2, SEQ, HEAD_DIM), _F32) for _ in range(2)]
    scratch += [pltpu.VMEM((ATTN_DILATIONS[2] * RESIDUE_PITCH, HEAD_DIM), _F32) for _ in range(2)]
    return pl.pallas_call(
        _attn_kernel,
        grid=(HEADS_PER_GROUP, batch),
        in_specs=in_specs,
        out_specs=pl.BlockSpec((None, SEQ, HEAD_DIM), lambda h, b: (b, 0, h)),
        out_shape=jax.ShapeDtypeStruct((batch, SEQ, ATTN_OUT), _BF16),
        scratch_shapes=scratch,
        compiler_params=pltpu.CompilerParams(dimension_semantics=("arbitrary", "arbitrary"),
                                             vmem_limit_bytes=VMEM_LIMIT),
        name="attn",
    )(*[qkv[gi] for gi in range(3) for _ in range(3)], bias_rows)


def _mix_kernel(xn_ref, pool_ref, attn_ref, h_ref, wg_ref, wa_ref, wb_ref, wo_ref, g_ref, h_out_ref, xn_out_ref):
    xn = xn_ref[...]
    gates = jnp.dot(xn, wg_ref[:, GATE_BLOCK - GATE_WIDTH:], preferred_element_type=_F32)
    a = jnp.dot(pool_ref[...], wa_ref[...], preferred_element_type=_F32)
    b = jnp.dot(attn_ref[...], wb_ref[...], preferred_element_type=_F32)
    merged = jax.nn.sigmoid(gates[:, :D_MODEL]) * a + jax.nn.sigmoid(gates[:, D_MODEL:]) * b
    h = h_ref[...] + jnp.dot(merged.astype(_BF16), wo_ref[...], preferred_element_type=_F32)
    h_out_ref[...] = h
    xn_out_ref[...] = _rms(h, g_ref[...]).astype(xn_out_ref.dtype)


def _mix_call(xn2d, pool2d, attn2d, h2d, w_in, w_a, w_b, w_o, g_next, layer):
    rows = xn2d.shape[0]
    tm = ROW_TILE
    assert IN_WIDTH % GATE_BLOCK == 0 and GATE_BLOCK >= GATE_WIDTH
    return pl.pallas_call(
        _mix_kernel,
        grid=(rows // tm,),
        in_specs=[pl.BlockSpec((tm, D_MODEL), lambda i: (i, 0)),
                  pl.BlockSpec((tm, POOL_WIDTH), lambda i: (i, 0)),
                  pl.BlockSpec((tm, ATTN_OUT), lambda i: (i, 0)),
                  pl.BlockSpec((tm, D_MODEL), lambda i: (i, 0)),
                  _layer_spec(layer, (D_MODEL, GATE_BLOCK), (0, IN_WIDTH // GATE_BLOCK - 1)),
                  _layer_spec(layer, (POOL_WIDTH, D_MODEL)),
                  _layer_spec(layer, (ATTN_OUT, D_MODEL)),
                  _layer_spec(layer, (D_MODEL, D_MODEL)),
                  _const_spec((1, D_MODEL))],
        out_specs=[pl.BlockSpec((tm, D_MODEL), lambda i: (i, 0)),
                   pl.BlockSpec((tm, D_MODEL), lambda i: (i, 0))],
        out_shape=[jax.ShapeDtypeStruct((rows, D_MODEL), _F32),
                   jax.ShapeDtypeStruct((rows, D_MODEL), _BF16)],
        compiler_params=pltpu.CompilerParams(dimension_semantics=("parallel",),
                                             vmem_limit_bytes=VMEM_LIMIT),
        name="mix",
    )(xn2d, pool2d, attn2d, h2d, w_in, w_a, w_b, w_o, g_next.reshape(1, D_MODEL))


def _gelu_tanh(x):
    return 0.5 * x * (1.0 + jnp.tanh(np.sqrt(2.0 / np.pi).astype(np.float32) * (x + 0.044715 * (x * x * x))))


def _ffn_kernel(xp_ref, x_ref, xnx_ref, h_ref, wup_ref, cw_ref, cb_ref, wdn_ref, g_ref, *out_refs, final):
    tm = x_ref.shape[0]
    tiles_per_seq = SEQ // tm
    t = pl.program_id(0) % tiles_per_seq
    x = x_ref[...]
    xe = jnp.concatenate([xp_ref[...], x, xnx_ref[...]], axis=0)
    pos = lax.broadcasted_iota(jnp.int32, (tm, 1), 0) + t * tm
    keep_prev = pos > 0
    keep_next = pos < SEQ - 1
    n_ext = tm + 2 * HALO
    acc = None
    for c in range(D_FF // FF_CHUNK):
        cols = slice(c * FF_CHUNK, (c + 1) * FF_CHUNK)
        gcols = slice(D_FF + c * FF_CHUNK, D_FF + (c + 1) * FF_CHUNK)
        a_ext = jnp.dot(xe, wup_ref[:, cols], preferred_element_type=_F32)
        gate = jnp.dot(x, wup_ref[:, gcols], preferred_element_type=_F32)
        a_mid = a_ext[HALO:HALO + tm]
        a_prev = pltpu.roll(a_ext, 1, axis=0)[HALO:HALO + tm]
        a_next = pltpu.roll(a_ext, n_ext - 1, axis=0)[HALO:HALO + tm]
        cw = cw_ref[:, cols]
        conv = (jnp.where(keep_prev, a_prev, 0.0) * cw[0:1] + a_mid * cw[1:2]
                + jnp.where(keep_next, a_next, 0.0) * cw[2:3] + cb_ref[:, cols])
        act = (_gelu_tanh(conv) * gate).astype(_BF16)
        part = jnp.dot(act, wdn_ref[cols, :], preferred_element_type=_F32)
        acc = part if acc is None else acc + part
    h = h_ref[...] + acc
    if final:
        out_refs[0][...] = _rms(h, g_ref[...])
    else:
        out_refs[0][...] = h
        out_refs[1][...] = _rms(h, g_ref[...]).astype(out_refs[1].dtype)


def _ffn_call(xn2d, h2d, w_up, conv_w, conv_b, w_down, g_next, layer, final):
    rows = xn2d.shape[0]
    tm = ROW_TILE
    per = tm // HALO
    last = rows // HALO - 1
    row_spec = pl.BlockSpec((tm, D_MODEL), lambda i: (i, 0))
    if final:
        out_specs = [row_spec]
        out_shape = [jax.ShapeDtypeStruct((rows, D_MODEL), _F32)]
    else:
        out_specs = [row_spec, row_spec]
        out_shape = [jax.ShapeDtypeStruct((rows, D_MODEL), _F32),
                     jax.ShapeDtypeStruct((rows, D_MODEL), _BF16)]
    return pl.pallas_call(
        functools.partial(_ffn_kernel, final=final),
        grid=(rows // tm,),
        in_specs=[pl.BlockSpec((HALO, D_MODEL), lambda i: (jnp.maximum(i * per - 1, 0), 0)),
                  row_spec,
                  pl.BlockSpec((HALO, D_MODEL), lambda i: (jnp.minimum((i + 1) * per, last), 0)),
                  row_spec,
                  _layer_spec(layer, (D_MODEL, 2 * D_FF)),
                  _const_spec((3, D_FF)),
                  _const_spec((1, D_FF)),
                  _layer_spec(layer, (D_FF, D_MODEL)),
                  _const_spec((1, D_MODEL))],
        out_specs=out_specs,
        out_shape=out_shape,
        compiler_params=pltpu.CompilerParams(dimension_semantics=("parallel",),
                                             vmem_limit_bytes=VMEM_LIMIT),
        name="ffn",
    )(xn2d, xn2d, xn2d, h2d, w_up, conv_w, conv_b.reshape(1, D_FF), w_down, g_next.reshape(1, D_MODEL))


def kernel(x, w_in, w_pool, pool_scale, w_a, w_b, w_o, norm1, norm2, w_up, conv_w, conv_b, w_down, rel_bias, norm_f):
    batch, seq, d = x.shape
    assert (seq, d) == (SEQ, D_MODEL)
    depth = w_in.shape[0]
    rows = batch * seq
    bias_rows = _bias_rows(rel_bias)
    h = x.reshape(rows, d)
    xn = None
    w_in, w_pool, w_a, w_b, w_o, w_up, w_down = (w.astype(_BF16) for w in (w_in, w_pool, w_a, w_b, w_o, w_up, w_down))
    for layer in range(depth):
        if xn is None:
            qkv0, xn3 = _qkv_call(x, w_in, 0, layer, norm_gain=norm1[layer])
            xn = xn3.reshape(rows, d)
        else:
            xn3 = xn.reshape(batch, seq, d)
            qkv0 = _qkv_call(xn3, w_in, 0, layer)
        pool = _pool_call(xn3, w_in, w_pool, pool_scale[layer], layer)
        attn = _attn_call([qkv0] + [_qkv_call(xn3, w_in, gi, layer) for gi in (1, 2)], bias_rows)
        h, xn = _mix_call(xn, pool.reshape(rows, POOL_WIDTH), attn.reshape(rows, ATTN_OUT), h,
                          w_in, w_a, w_b, w_o, norm2[layer], layer)
        final = layer == depth - 1
        g_next = norm_f if final else norm1[layer + 1]
        outs = _ffn_call(xn, h, w_up, conv_w[layer], conv_b[layer], w_down, g_next, layer, final)
        if final:
            return outs[0].reshape(batch, seq, d)
        h, xn = outs
```

```python
import functools

import numpy as np
import jax
import jax.numpy as jnp
from jax import lax
from jax.experimental import pallas as pl
from jax.experimental.pallas import tpu as pltpu

D_MODEL = 1024
SEQ = 2048
POOL_WINDOWS = (2, 4, 8, 16)
POOL_GROUP_DIM = 256
POOL_WIDTH = 1024
ATTN_DILATIONS = (1, 4, 16)
HALF_WINDOW = 64
HEADS_PER_GROUP = 4
N_HEADS = 12
HEAD_DIM = 128
ATTN_WIDTH = N_HEADS * HEAD_DIM
ATTN_OUT = HEADS_PER_GROUP * HEAD_DIM
NEG_INF = -1e30
N_BUCKETS = 32
MAX_DISTANCE = 1024
D_FF = 2816
EPS = 1e-6
PROJ_WIDTH = POOL_WIDTH + 3 * ATTN_WIDTH
GATE_WIDTH = 2 * D_MODEL
IN_WIDTH = PROJ_WIDTH + GATE_WIDTH
GATE_BLOCK = 2560

Q_TILE = 128
UNITS_PER_STEP = 8
LOG2_E = float(np.log2(np.e))
RESIDUE_PITCH = Q_TILE + 4
NORM_ROWS = 256
PROJ_TN = 512
ROW_TILE = 512
HALO = 16
FF_CHUNKS = (1536, 1280)
VMEM_LIMIT = 56 * 1024 * 1024
LANES = 128
REGROUP_TILE = 256

_F32 = jnp.float32
_BF16 = jnp.bfloat16


def _rms(x, g):
    return x * lax.rsqrt(jnp.mean(x * x, axis=-1, keepdims=True) + EPS) * g


def _const_spec(shape):
    return pl.BlockSpec(shape, lambda i: (0,) * len(shape), pipeline_mode=pl.Buffered(1))


def _layer_spec(layer, shape, index=None):
    index = tuple(index) if index is not None else (0,) * len(shape)
    return pl.BlockSpec((None,) + tuple(shape), lambda *_: (layer,) + index, pipeline_mode=pl.Buffered(1))


POOL_PAD = 16
POOL_EDGE = 8
assert POOL_EDGE >= max(POOL_WINDOWS) // 2 and POOL_PAD - POOL_EDGE >= max(POOL_WINDOWS) // 4


def _pool_kernel(xn_ref, w_ref, wpool_ref, pscale_ref, o_ref, u_ref, t_ref):
    n = SEQ + 2 * POOL_PAD
    slabs = POOL_GROUP_DIM // LANES

    def shifted(view, start, rows, k_back, k_fwd):
        return view[pl.ds(start - k_back, rows, stride=1), :] + view[pl.ds(start + k_fwd, rows, stride=1), :]

    def steps(w):
        return (1, 0) if w == 1 else (w // 2, w // 2)

    zeros = jnp.zeros((POOL_PAD, LANES), _F32)
    for t in range(2):
        for s in range(slabs):
            t_ref[t, s, 0:POOL_EDGE, :] = zeros[:POOL_EDGE]
            t_ref[t, s, n - POOL_EDGE:n, :] = zeros[:POOL_EDGE]
    xn = xn_ref[...]
    for gi in range(len(POOL_WINDOWS)):
        u = jnp.dot(xn, w_ref[:, gi * POOL_GROUP_DIM:(gi + 1) * POOL_GROUP_DIM], preferred_element_type=_F32)
        for s in range(slabs):
            u_ref[gi, s, 0:POOL_PAD, :] = zeros
            u_ref[gi, s, n - POOL_PAD:n, :] = zeros
            u_ref[gi, s, POOL_PAD:POOL_PAD + SEQ, :] = u[:, s * LANES:(s + 1) * LANES]
    edge_row = lax.broadcasted_iota(jnp.int32, (POOL_PAD, POOL_GROUP_DIM), 0)
    for gi, window in enumerate(POOL_WINDOWS):
        cols = slice(gi * POOL_GROUP_DIM, (gi + 1) * POOL_GROUP_DIM)
        totals = []
        for s in range(slabs):
            src, w, slot = u_ref.at[gi, s], 1, 0
            while 2 * w < window:
                t_ref[slot, s, pl.ds(POOL_EDGE, n - 2 * POOL_EDGE), :] = shifted(src, POOL_EDGE, n - 2 * POOL_EDGE,
                                                                              *steps(w))
                src, w, slot = t_ref.at[slot, s], 2 * w, 1 - slot
            totals.append(shifted(src, POOL_PAD, SEQ, *steps(w)))
        total = jnp.concatenate(totals, axis=1)
        u = jnp.concatenate([u_ref[gi, s, POOL_PAD:POOL_PAD + SEQ, :] for s in range(slabs)], axis=1)
        wpool = wpool_ref[gi]
        scale = pscale_ref[:, cols]
        pooled = total * (1.0 / window) - u
        z = jnp.dot(pooled.astype(_BF16), wpool, preferred_element_type=_F32)
        o_ref[:, cols] = (z * scale).astype(o_ref.dtype)
        for r0 in (0, SEQ - POOL_PAD):
            pos = edge_row + r0
            size = (jnp.minimum(pos + window // 2, SEQ) - jnp.maximum(pos - window // 2, 0)).astype(_F32)
            pooled = total[r0:r0 + POOL_PAD] / size - u[r0:r0 + POOL_PAD]
            z = jnp.dot(pooled.astype(_BF16), wpool, preferred_element_type=_F32)
            o_ref[r0:r0 + POOL_PAD, cols] = (z * scale).astype(o_ref.dtype)


def _pool_call(xn3, w_in, w_pool, pool_scale, layer):
    batch = xn3.shape[0]
    return pl.pallas_call(
        _pool_kernel,
        grid=(batch,),
        in_specs=[pl.BlockSpec((None, SEQ, D_MODEL), lambda b: (b, 0, 0)),
                  _layer_spec(layer, (D_MODEL, POOL_WIDTH)),
                  _layer_spec(layer, (len(POOL_WINDOWS), POOL_GROUP_DIM, POOL_GROUP_DIM)),
                  _const_spec((1, POOL_WIDTH))],
        out_specs=pl.BlockSpec((None, SEQ, POOL_WIDTH), lambda b: (b, 0, 0)),
        out_shape=jax.ShapeDtypeStruct((batch, SEQ, POOL_WIDTH), _BF16),
        scratch_shapes=[pltpu.VMEM((len(POOL_WINDOWS), POOL_GROUP_DIM // LANES, SEQ + 2 * POOL_PAD, LANES), _F32),
                        pltpu.VMEM((2, POOL_GROUP_DIM // LANES, SEQ + 2 * POOL_PAD, LANES), _F32)],
        compiler_params=pltpu.CompilerParams(dimension_semantics=("parallel",),
                                             vmem_limit_bytes=VMEM_LIMIT),
        name="pool",
    )(xn3, w_in, w_pool, pool_scale.reshape(1, POOL_WIDTH))


def _qkv_kernel(x_ref, *refs, dil, norm_input):
    if norm_input:
        g_ref, wq_ref, wk_ref, wv_ref, o_ref, xn_ref, *scratch = refs
        for r0 in range(0, SEQ, NORM_ROWS):
            xn_ref[r0:r0 + NORM_ROWS, :] = _rms(x_ref[r0:r0 + NORM_ROWS, :], g_ref[...]).astype(xn_ref.dtype)
        x_ref = xn_ref
    else:
        wq_ref, wk_ref, wv_ref, o_ref, *scratch = refs
    if dil == 1:
        x = x_ref[...]
    else:
        xp_ref, = scratch
        sub = SEQ // dil
        sub_t = REGROUP_TILE // dil
        i = lax.broadcasted_iota(jnp.int32, (REGROUP_TILE, REGROUP_TILE), 0)
        j = lax.broadcasted_iota(jnp.int32, (REGROUP_TILE, REGROUP_TILE), 1)
        pick = (j == (i % sub_t) * dil + i // sub_t).astype(_BF16)
        for t in range(SEQ // REGROUP_TILE):
            tile = x_ref[t * REGROUP_TILE:(t + 1) * REGROUP_TILE, :]
            srt = jnp.dot(pick, tile, preferred_element_type=_F32).astype(_BF16)
            for r in range(dil):
                xp_ref[r * sub + t * sub_t:r * sub + (t + 1) * sub_t, :] = srt[r * sub_t:(r + 1) * sub_t]
        x = xp_ref[...]
    for which, w_ref in enumerate((wq_ref, wk_ref, wv_ref)):
        y = jnp.dot(x, w_ref[...], preferred_element_type=_F32)
        if which == 0:
            y = y * HEAD_DIM ** -0.5
        o_ref[:, which * PROJ_TN:(which + 1) * PROJ_TN] = y.astype(o_ref.dtype)


def _qkv_call(x3, w_in, gi, layer, norm_gain=None):
    batch = x3.shape[0]
    dil = ATTN_DILATIONS[gi]
    first = POOL_WIDTH // PROJ_TN + gi
    norm_input = norm_gain is not None
    x_spec = pl.BlockSpec((None, SEQ, D_MODEL), lambda b: (b, 0, 0))
    in_specs = [x_spec] + ([_const_spec((1, D_MODEL))] if norm_input else [])
    in_specs += [_layer_spec(layer, (D_MODEL, PROJ_TN), (0, first + 3 * which)) for which in range(3)]
    out_specs = [pl.BlockSpec((None, SEQ, 3 * PROJ_TN), lambda b: (b, 0, 0))]
    out_shape = [jax.ShapeDtypeStruct((batch, SEQ, 3 * PROJ_TN), _BF16)]
    if norm_input:
        out_specs.append(x_spec)
        out_shape.append(jax.ShapeDtypeStruct((batch, SEQ, D_MODEL), _BF16))
    args = (x3,) + ((norm_gain.reshape(1, D_MODEL),) if norm_input else ()) + (w_in, w_in, w_in)
    outs = pl.pallas_call(
        functools.partial(_qkv_kernel, dil=dil, norm_input=norm_input),
        grid=(batch,),
        in_specs=in_specs,
        out_specs=out_specs,
        out_shape=out_shape,
        scratch_shapes=[] if dil == 1 else [pltpu.VMEM((SEQ, D_MODEL), _BF16)],
        compiler_params=pltpu.CompilerParams(dimension_semantics=("parallel",),
                                             vmem_limit_bytes=VMEM_LIMIT),
        name=f"qkv{gi}",
    )(*args)
    return outs if norm_input else outs[0]


def _t5_buckets_np(rel):
    n = -rel
    half = N_BUCKETS // 2
    ret = (n < 0).astype(np.int32) * half
    n = np.abs(n)
    max_exact = half // 2
    large = max_exact + (np.log(np.maximum(n, 1) / max_exact)
                         / np.log(MAX_DISTANCE / max_exact) * (half - max_exact)).astype(np.int32)
    large = np.minimum(large, half - 1)
    return (ret + np.where(n < max_exact, n, large)).astype(np.int32)


def _key_span(dil):
    return min(Q_TILE + 2 * HALF_WINDOW, SEQ // dil)


BIAS_LANES = 512


def _bias_rows(rel_bias):
    rows = []
    for gi, dil in enumerate(ATTN_DILATIONS):
        buckets = _t5_buckets_np(dil * np.arange(-HALF_WINDOW, HALF_WINDOW + 1))
        bias = rel_bias[buckets][:, gi * HEADS_PER_GROUP:(gi + 1) * HEADS_PER_GROUP].T.astype(_F32)
        rows.append(jnp.pad(bias, ((0, 0), (0, BIAS_LANES - bias.shape[1])), constant_values=NEG_INF))
    return jnp.stack(rows, axis=1)


def _tile_deltas(dil):
    return (0, HALF_WINDOW, 2 * HALF_WINDOW) if SEQ // dil > Q_TILE else (0,)


def _attn_kernel(q0_ref, k0_ref, v0_ref, q1_ref, k1_ref, v1_ref, q2_ref, k2_ref, v2_ref,
                 e_ref, o_ref, t0_ref, t1_ref, t2_ref, o_acc_ref, lse_ref, far_o_ref, far_lse_ref):
    groups = ((q0_ref, k0_ref, v0_ref, t0_ref), (q1_ref, k1_ref, v1_ref, t1_ref), (q2_ref, k2_ref, v2_ref, t2_ref))

    @pl.when(pl.program_id(1) == 0)
    def _():
        for gi, (_, _, _, t_ref) in enumerate(groups):
            dil = ATTN_DILATIONS[gi]
            row = jnp.broadcast_to(e_ref[gi:gi + 1, :], (Q_TILE, BIAS_LANES))
            for vi, delta in enumerate(_tile_deltas(dil)):
                skew = pltpu.roll(row, (delta - HALF_WINDOW) % BIAS_LANES, axis=1, stride=1, stride_axis=0)
                t_ref[vi] = skew[:, :_key_span(dil)]

    def scores(u, gi):
        q_ref, k_ref, _, t_ref = groups[gi]
        dil = ATTN_DILATIONS[gi]
        sub = SEQ // dil
        span = _key_span(dil)
        blocks = sub // Q_TILE
        r = u // blocks
        m0 = (u % blocks) * Q_TILE
        start = jnp.clip(m0 - HALF_WINDOW, 0, sub - span)
        variant = (m0 - start) // HALF_WINDOW
        q = q_ref[pl.ds(pl.multiple_of(u * Q_TILE, Q_TILE), Q_TILE), :]
        k0 = pl.multiple_of(r * sub + start, HALF_WINDOW)
        k = k_ref[pl.ds(k0, span), :]
        s = lax.dot_general(q, k, (((1,), (1,)), ((), ())), preferred_element_type=_F32)
        if dil == 1:
            dest = (gi, pl.ds(pl.multiple_of(m0, Q_TILE), Q_TILE))
        elif blocks > 1:
            dest = (gi, pl.ds(m0 * dil + r, Q_TILE, stride=dil))
        else:
            dest = (None, pl.ds(r * RESIDUE_PITCH, Q_TILE, stride=1))
        return s + t_ref[variant], k0, dest

    def weights(s):
        m = jnp.max(s, axis=-1, keepdims=True)
        return m, jnp.exp((s - m).astype(_BF16))

    ones = jnp.ones((Q_TILE + 2 * HALF_WINDOW, HEAD_DIM), _BF16)

    def values(gi, k0, dest, m, p):
        span = p.shape[1]
        v = groups[gi][2][pl.ds(k0, span), :]
        ov = jnp.dot(p, jnp.concatenate([v, ones[:span]], axis=1), preferred_element_type=_F32)
        l = ov[:, HEAD_DIM:]
        lse2 = m * LOG2_E + jnp.log2(l)
        slot, rows = dest
        if slot is None:
            far_o_ref[rows, :] = ov[:, :HEAD_DIM] / l
            far_lse_ref[rows, :] = lse2
        else:
            o_acc_ref[slot, rows, :] = ov[:, :HEAD_DIM] / l
            lse_ref[slot, rows, :] = lse2

    def units(it, carry):
        todo = [(it * UNITS_PER_STEP + j, gi) for j in range(UNITS_PER_STEP) for gi in range(3)]
        scored = [scores(u, gi) for u, gi in todo]
        soft = [weights(s) for s, _, _ in scored]
        for (_, gi), (_, k0, dest), (m, p) in zip(todo, scored, soft):
            values(gi, k0, dest, m, p)
        return carry

    lax.fori_loop(0, SEQ // Q_TILE // UNITS_PER_STEP, units, 0)

    far_dil = ATTN_DILATIONS[2]

    def merge(t, carry):
        rows = pl.ds(pl.multiple_of(t * Q_TILE, Q_TILE), Q_TILE)

        def far(ref):
            return jnp.concatenate([ref[pl.ds(t * (Q_TILE // far_dil) + j, far_dil, stride=RESIDUE_PITCH), :]
                                    for j in range(Q_TILE // far_dil)], axis=0)

        lses = [lse_ref[0, rows, :], lse_ref[1, rows, :], far(far_lse_ref)]
        outs = [o_acc_ref[0, rows, :], o_acc_ref[1, rows, :], far(far_o_ref)]
        top = jnp.maximum(jnp.maximum(lses[0], lses[1]), lses[2])
        ws = [jnp.exp2(x - top) for x in lses]
        num = ws[0] * outs[0] + ws[1] * outs[1] + ws[2] * outs[2]
        o_ref[rows, :] = (num / (ws[0] + ws[1] + ws[2])).astype(o_ref.dtype)
        return carry

    lax.fori_loop(0, SEQ // Q_TILE, merge, 0, unroll=4)


def _attn_call(qkv, bias_rows):
    batch = qkv[0].shape[0]

    def head_spec(which):
        return pl.BlockSpec((None, SEQ, HEAD_DIM), lambda h, b: (b, 0, which * HEADS_PER_GROUP + h))

    in_specs = [head_spec(which) for gi in range(3) for which in range(3)]
    in_specs.append(pl.BlockSpec((None, 3, BIAS_LANES), lambda h, b: (h, 0, 0)))
    scratch = [pltpu.VMEM((len(_tile_deltas(dil)), Q_TILE, _key_span(dil)), _F32) for dil in ATTN_DILATIONS]
    assert SEQ // ATTN_DILATIONS[2] == Q_TILE and all(SEQ // d > Q_TILE for d in ATTN_DILATIONS[:2])
    scratch += [pltpu.VMEM((2, SEQ, HEAD_DIM), _F32) for _ in range(2)]
    scratch += [pltpu.VMEM((ATTN_DILATIONS[2] * RESIDUE_PITCH, HEAD_DIM), _F32) for _ in range(2)]
    return pl.pallas_call(
        _attn_kernel,
        grid=(HEADS_PER_GROUP, batch),
        in_specs=in_specs,
        out_specs=pl.BlockSpec((None, SEQ, HEAD_DIM), lambda h, b: (b, 0, h)),
        out_shape=jax.ShapeDtypeStruct((batch, SEQ, ATTN_OUT), _BF16),
        scratch_shapes=scratch,
        compiler_params=pltpu.CompilerParams(dimension_semantics=("arbitrary", "arbitrary"),
                                             vmem_limit_bytes=VMEM_LIMIT),
        name="attn",
    )(*[qkv[gi] for gi in range(3) for _ in range(3)], bias_rows)


def _mix_kernel(xn_ref, pool_ref, attn_ref, h_ref, wg_ref, wa_ref, wb_ref, wo_ref, g_ref, h_out_ref, xn_out_ref):
    xn = xn_ref[...]
    gates = jnp.dot(xn, wg_ref[:, GATE_BLOCK - GATE_WIDTH:], preferred_element_type=_F32)
    a = jnp.dot(pool_ref[...], wa_ref[...], preferred_element_type=_F32)
    b = jnp.dot(attn_ref[...], wb_ref[...], preferred_element_type=_F32)
    merged = jax.nn.sigmoid(gates[:, :D_MODEL]) * a + jax.nn.sigmoid(gates[:, D_MODEL:]) * b
    h = h_ref[...] + jnp.dot(merged.astype(_BF16), wo_ref[...], preferred_element_type=_F32)
    h_out_ref[...] = h
    xn_out_ref[...] = _rms(h, g_ref[...]).astype(xn_out_ref.dtype)


def _mix_call(xn2d, pool2d, attn2d, h2d, w_in, w_a, w_b, w_o, g_next, layer):
    rows = xn2d.shape[0]
    tm = ROW_TILE
    assert IN_WIDTH % GATE_BLOCK == 0 and GATE_BLOCK >= GATE_WIDTH
    return pl.pallas_call(
        _mix_kernel,
        grid=(rows // tm,),
        in_specs=[pl.BlockSpec((tm, D_MODEL), lambda i: (i, 0)),
                  pl.BlockSpec((tm, POOL_WIDTH), lambda i: (i, 0)),
                  pl.BlockSpec((tm, ATTN_OUT), lambda i: (i, 0)),
                  pl.BlockSpec((tm, D_MODEL), lambda i: (i, 0)),
                  _layer_spec(layer, (D_MODEL, GATE_BLOCK), (0, IN_WIDTH // GATE_BLOCK - 1)),
                  _layer_spec(layer, (POOL_WIDTH, D_MODEL)),
                  _layer_spec(layer, (ATTN_OUT, D_MODEL)),
                  _layer_spec(layer, (D_MODEL, D_MODEL)),
                  _const_spec((1, D_MODEL))],
        out_specs=[pl.BlockSpec((tm, D_MODEL), lambda i: (i, 0)),
                   pl.BlockSpec((tm, D_MODEL), lambda i: (i, 0))],
        out_shape=[jax.ShapeDtypeStruct((rows, D_MODEL), _F32),
                   jax.ShapeDtypeStruct((rows, D_MODEL), _BF16)],
        compiler_params=pltpu.CompilerParams(dimension_semantics=("parallel",),
                                             vmem_limit_bytes=VMEM_LIMIT),
        name="mix",
    )(xn2d, pool2d, attn2d, h2d, w_in, w_a, w_b, w_o, g_next.reshape(1, D_MODEL))


def _gelu_tanh(x):
    return 0.5 * x * (1.0 + jnp.tanh(np.sqrt(2.0 / np.pi).astype(np.float32) * (x + 0.044715 * (x * x * x))))


def _ffn_kernel(xp_ref, x_ref, xnx_ref, h_ref, wup_ref, cw_ref, cb_ref, wdn_ref, g_ref, *out_refs, final):
    tm = x_ref.shape[0]
    tiles_per_seq = SEQ // tm
    t = pl.program_id(0) % tiles_per_seq
    x = x_ref[...]
    xe = jnp.concatenate([xp_ref[...], x, xnx_ref[...]], axis=0)
    pos = lax.broadcasted_iota(jnp.int32, (tm, 1), 0) + t * tm
    keep_prev = pos > 0
    keep_next = pos < SEQ - 1
    n_ext = tm + 2 * HALO
    acc = None
    assert sum(FF_CHUNKS) == D_FF
    for c, width in enumerate(FF_CHUNKS):
        c0 = sum(FF_CHUNKS[:c])
        cols = slice(c0, c0 + width)
        gcols = slice(D_FF + c0, D_FF + c0 + width)
        a_ext = jnp.dot(xe, wup_ref[:, cols], preferred_element_type=_F32)
        gate = jnp.dot(x, wup_ref[:, gcols], preferred_element_type=_F32)
        a_mid = a_ext[HALO:HALO + tm]
        a_prev = pltpu.roll(a_ext, 1, axis=0)[HALO:HALO + tm]
        a_next = pltpu.roll(a_ext, n_ext - 1, axis=0)[HALO:HALO + tm]
        cw = cw_ref[:, cols]
        conv = (jnp.where(keep_prev, a_prev, 0.0) * cw[0:1] + a_mid * cw[1:2]
                + jnp.where(keep_next, a_next, 0.0) * cw[2:3] + cb_ref[:, cols])
        act = (_gelu_tanh(conv) * gate).astype(_BF16)
        part = jnp.dot(act, wdn_ref[cols, :], preferred_element_type=_F32)
        acc = part if acc is None else acc + part
    h = h_ref[...] + acc
    if final:
        out_refs[0][...] = _rms(h, g_ref[...])
    else:
        out_refs[0][...] = h
        out_refs[1][...] = _rms(h, g_ref[...]).astype(out_refs[1].dtype)


def _ffn_call(xn2d, h2d, w_up, conv_w, conv_b, w_down, g_next, layer, final):
    rows = xn2d.shape[0]
    tm = ROW_TILE
    per = tm // HALO
    last = rows // HALO - 1
    row_spec = pl.BlockSpec((tm, D_MODEL), lambda i: (i, 0))
    if final:
        out_specs = [row_spec]
        out_shape = [jax.ShapeDtypeStruct((rows, D_MODEL), _F32)]
    else:
        out_specs = [row_spec, row_spec]
        out_shape = [jax.ShapeDtypeStruct((rows, D_MODEL), _F32),
                     jax.ShapeDtypeStruct((rows, D_MODEL), _BF16)]
    return pl.pallas_call(
        functools.partial(_ffn_kernel, final=final),
        grid=(rows // tm,),
        in_specs=[pl.BlockSpec((HALO, D_MODEL), lambda i: (jnp.maximum(i * per - 1, 0), 0)),
                  row_spec,
                  pl.BlockSpec((HALO, D_MODEL), lambda i: (jnp.minimum((i + 1) * per, last), 0)),
                  row_spec,
                  _layer_spec(layer, (D_MODEL, 2 * D_FF)),
                  _const_spec((3, D_FF)),
                  _const_spec((1, D_FF)),
                  _layer_spec(layer, (D_FF, D_MODEL)),
                  _const_spec((1, D_MODEL))],
        out_specs=out_specs,
        out_shape=out_shape,
        compiler_params=pltpu.CompilerParams(dimension_semantics=("parallel",),
                                             vmem_limit_bytes=VMEM_LIMIT),
        name="ffn",
    )(xn2d, xn2d, xn2d, h2d, w_up, conv_w, conv_b.reshape(1, D_FF), w_down, g_next.reshape(1, D_MODEL))


def kernel(x, w_in, w_pool, pool_scale, w_a, w_b, w_o, norm1, norm2, w_up, conv_w, conv_b, w_down, rel_bias, norm_f):
    batch, seq, d = x.shape
    assert (seq, d) == (SEQ, D_MODEL)
    depth = w_in.shape[0]
    rows = batch * seq
    bias_rows = _bias_rows(rel_bias)
    h = x.reshape(rows, d)
    xn = None
    w_in, w_pool, w_a, w_b, w_o, w_up, w_down = (w.astype(_BF16) for w in (w_in, w_pool, w_a, w_b, w_o, w_up, w_down))
    for layer in range(depth):
        if xn is None:
            qkv0, xn3 = _qkv_call(x, w_in, 0, layer, norm_gain=norm1[layer])
            xn = xn3.reshape(rows, d)
        else:
            xn3 = xn.reshape(batch, seq, d)
            qkv0 = _qkv_call(xn3, w_in, 0, layer)
        pool = _pool_call(xn3, w_in, w_pool, pool_scale[layer], layer)
        attn = _attn_call([qkv0] + [_qkv_call(xn3, w_in, gi, layer) for gi in (1, 2)], bias_rows)
        h, xn = _mix_call(xn, pool.reshape(rows, POOL_WIDTH), attn.reshape(rows, ATTN_OUT), h,
                          w_in, w_a, w_b, w_o, norm2[layer], layer)
        final = layer == depth - 1
        g_next = norm_f if final else norm1[layer + 1]
        outs = _ffn_call(xn, h, w_up, conv_w[layer], conv_b[layer], w_down, g_next, layer, final)
        if final:
            return outs[0].reshape(batch, seq, d)
        h, xn = outs
```

```python
import functools

import numpy as np
import jax
import jax.numpy as jnp
from jax import lax
from jax.experimental import pallas as pl
from jax.experimental.pallas import tpu as pltpu

D_MODEL = 1024
SEQ = 2048
POOL_WINDOWS = (2, 4, 8, 16)
POOL_GROUP_DIM = 256
POOL_WIDTH = 1024
ATTN_DILATIONS = (1, 4, 16)
HALF_WINDOW = 64
HEADS_PER_GROUP = 4
N_HEADS = 12
HEAD_DIM = 128
ATTN_WIDTH = N_HEADS * HEAD_DIM
ATTN_OUT = HEADS_PER_GROUP * HEAD_DIM
NEG_INF = -1e30
N_BUCKETS = 32
MAX_DISTANCE = 1024
D_FF = 2816
EPS = 1e-6
PROJ_WIDTH = POOL_WIDTH + 3 * ATTN_WIDTH
GATE_WIDTH = 2 * D_MODEL
IN_WIDTH = PROJ_WIDTH + GATE_WIDTH
GATE_BLOCK = 2560

Q_TILE = 128
UNITS_PER_STEP = 8
LOG2_E = float(np.log2(np.e))
RESIDUE_PITCH = Q_TILE + 4
NORM_ROWS = 256
PROJ_TN = 512
ROW_TILE = 512
BF16_TILE_ROWS = 16
HALO = BF16_TILE_ROWS
FF_CHUNKS = (1536, 1280)
VMEM_LIMIT = 56 * 1024 * 1024
LANES = 128
REGROUP_TILE = 256

_F32 = jnp.float32
_BF16 = jnp.bfloat16


def _rms(x, g):
    return x * lax.rsqrt(jnp.mean(x * x, axis=-1, keepdims=True) + EPS) * g


def _const_spec(shape, index=None):
    index = tuple(index) if index is not None else (0,) * len(shape)
    return pl.BlockSpec(tuple(shape), lambda *_: index, pipeline_mode=pl.Buffered(1))


def _layer_spec(layer, shape, index=None):
    index = tuple(index) if index is not None else (0,) * len(shape)
    return pl.BlockSpec((None,) + tuple(shape), lambda *_: (layer,) + index, pipeline_mode=pl.Buffered(1))


def _cast_rider(param, layer, n_steps, step_of):
    _, rows, cols = param.shape
    n_blocks = max(n for n in range(1, n_steps + 1) if rows % n == 0 and (rows // n) % BF16_TILE_ROWS == 0)
    rb = rows // n_blocks

    def block(*g):
        return step_of(*g) * n_blocks // n_steps

    return (pl.BlockSpec((None, rb, cols), lambda *g: (layer, block(*g), 0)),
            pl.BlockSpec((rb, cols), lambda *g: (block(*g), 0)),
            jax.ShapeDtypeStruct((rows, cols), _BF16))


def _cast_blocks(src_refs, dst_refs):
    for src, dst in zip(src_refs, dst_refs, strict=True):
        dst[...] = src[...].astype(dst.dtype)


POOL_PAD = 16
POOL_EDGE = 8
assert POOL_EDGE >= max(POOL_WINDOWS) // 2 and POOL_PAD - POOL_EDGE >= max(POOL_WINDOWS) // 4


def _pool_kernel(xn_ref, w_ref, wpool_ref, pscale_ref, o_ref, u_ref, t_ref):
    n = SEQ + 2 * POOL_PAD
    slabs = POOL_GROUP_DIM // LANES

    def shifted(view, start, rows, k_back, k_fwd):
        return view[pl.ds(start - k_back, rows, stride=1), :] + view[pl.ds(start + k_fwd, rows, stride=1), :]

    def steps(w):
        return (1, 0) if w == 1 else (w // 2, w // 2)

    zeros = jnp.zeros((POOL_PAD, LANES), _F32)
    for t in range(2):
        for s in range(slabs):
            t_ref[t, s, 0:POOL_EDGE, :] = zeros[:POOL_EDGE]
            t_ref[t, s, n - POOL_EDGE:n, :] = zeros[:POOL_EDGE]
    xn = xn_ref[...]
    for gi in range(len(POOL_WINDOWS)):
        u = jnp.dot(xn, w_ref[:, gi * POOL_GROUP_DIM:(gi + 1) * POOL_GROUP_DIM], preferred_element_type=_F32)
        for s in range(slabs):
            u_ref[gi, s, 0:POOL_PAD, :] = zeros
            u_ref[gi, s, n - POOL_PAD:n, :] = zeros
            u_ref[gi, s, POOL_PAD:POOL_PAD + SEQ, :] = u[:, s * LANES:(s + 1) * LANES]
    edge_row = lax.broadcasted_iota(jnp.int32, (POOL_PAD, POOL_GROUP_DIM), 0)
    for gi, window in enumerate(POOL_WINDOWS):
        cols = slice(gi * POOL_GROUP_DIM, (gi + 1) * POOL_GROUP_DIM)
        totals = []
        for s in range(slabs):
            src, w, slot = u_ref.at[gi, s], 1, 0
            while 2 * w < window:
                t_ref[slot, s, pl.ds(POOL_EDGE, n - 2 * POOL_EDGE), :] = shifted(src, POOL_EDGE, n - 2 * POOL_EDGE,
                                                                              *steps(w))
                src, w, slot = t_ref.at[slot, s], 2 * w, 1 - slot
            totals.append(shifted(src, POOL_PAD, SEQ, *steps(w)))
        total = jnp.concatenate(totals, axis=1)
        u = jnp.concatenate([u_ref[gi, s, POOL_PAD:POOL_PAD + SEQ, :] for s in range(slabs)], axis=1)
        wpool = wpool_ref[gi]
        scale = pscale_ref[:, cols]
        pooled = total * (1.0 / window) - u
        z = jnp.dot(pooled.astype(_BF16), wpool, preferred_element_type=_F32)
        o_ref[:, cols] = (z * scale).astype(o_ref.dtype)
        for r0 in (0, SEQ - POOL_PAD):
            pos = edge_row + r0
            size = (jnp.minimum(pos + window // 2, SEQ) - jnp.maximum(pos - window // 2, 0)).astype(_F32)
            pooled = total[r0:r0 + POOL_PAD] / size - u[r0:r0 + POOL_PAD]
            z = jnp.dot(pooled.astype(_BF16), wpool, preferred_element_type=_F32)
            o_ref[r0:r0 + POOL_PAD, cols] = (z * scale).astype(o_ref.dtype)


def _pool_call(xn3, w_in, w_pool, pool_scale, layer):
    batch = xn3.shape[0]
    return pl.pallas_call(
        _pool_kernel,
        grid=(batch,),
        in_specs=[pl.BlockSpec((None, SEQ, D_MODEL), lambda b: (b, 0, 0)),
                  _const_spec((D_MODEL, POOL_WIDTH)),
                  _layer_spec(layer, (len(POOL_WINDOWS), POOL_GROUP_DIM, POOL_GROUP_DIM)),
                  _const_spec((1, POOL_WIDTH))],
        out_specs=pl.BlockSpec((None, SEQ, POOL_WIDTH), lambda b: (b, 0, 0)),
        out_shape=jax.ShapeDtypeStruct((batch, SEQ, POOL_WIDTH), _BF16),
        scratch_shapes=[pltpu.VMEM((len(POOL_WINDOWS), POOL_GROUP_DIM // LANES, SEQ + 2 * POOL_PAD, LANES), _F32),
                        pltpu.VMEM((2, POOL_GROUP_DIM // LANES, SEQ + 2 * POOL_PAD, LANES), _F32)],
        compiler_params=pltpu.CompilerParams(dimension_semantics=("parallel",),
                                             vmem_limit_bytes=VMEM_LIMIT),
        name="pool",
    )(xn3, w_in, w_pool, pool_scale.reshape(1, POOL_WIDTH))


def _qkv_kernel(x_ref, *refs, dil, norm_input):
    if norm_input:
        g_ref, wq_ref, wk_ref, wv_ref, o_ref, xn_ref, *scratch = refs
        for r0 in range(0, SEQ, NORM_ROWS):
            xn_ref[r0:r0 + NORM_ROWS, :] = _rms(x_ref[r0:r0 + NORM_ROWS, :], g_ref[...]).astype(xn_ref.dtype)
        x_ref = xn_ref
    else:
        wq_ref, wk_ref, wv_ref, o_ref, *scratch = refs
    if dil == 1:
        x = x_ref[...]
    else:
        xp_ref, = scratch
        sub = SEQ // dil
        sub_t = REGROUP_TILE // dil
        i = lax.broadcasted_iota(jnp.int32, (REGROUP_TILE, REGROUP_TILE), 0)
        j = lax.broadcasted_iota(jnp.int32, (REGROUP_TILE, REGROUP_TILE), 1)
        pick = (j == (i % sub_t) * dil + i // sub_t).astype(_BF16)
        for t in range(SEQ // REGROUP_TILE):
            tile = x_ref[t * REGROUP_TILE:(t + 1) * REGROUP_TILE, :]
            srt = jnp.dot(pick, tile, preferred_element_type=_F32).astype(_BF16)
            for r in range(dil):
                xp_ref[r * sub + t * sub_t:r * sub + (t + 1) * sub_t, :] = srt[r * sub_t:(r + 1) * sub_t]
        x = xp_ref[...]
    for which, w_ref in enumerate((wq_ref, wk_ref, wv_ref)):
        y = jnp.dot(x, w_ref[...], preferred_element_type=_F32)
        if which == 0:
            y = y * HEAD_DIM ** -0.5
        o_ref[:, which * PROJ_TN:(which + 1) * PROJ_TN] = y.astype(o_ref.dtype)


def _qkv_call(x3, w_in, gi, norm_gain=None):
    batch = x3.shape[0]
    dil = ATTN_DILATIONS[gi]
    first = POOL_WIDTH // PROJ_TN + gi
    norm_input = norm_gain is not None
    x_spec = pl.BlockSpec((None, SEQ, D_MODEL), lambda b: (b, 0, 0))
    in_specs = [x_spec] + ([_const_spec((1, D_MODEL))] if norm_input else [])
    in_specs += [_const_spec((D_MODEL, PROJ_TN), (0, first + 3 * which)) for which in range(3)]
    out_specs = [pl.BlockSpec((None, SEQ, 3 * PROJ_TN), lambda b: (b, 0, 0))]
    out_shape = [jax.ShapeDtypeStruct((batch, SEQ, 3 * PROJ_TN), _BF16)]
    if norm_input:
        out_specs.append(x_spec)
        out_shape.append(jax.ShapeDtypeStruct((batch, SEQ, D_MODEL), _BF16))
    args = (x3,) + ((norm_gain.reshape(1, D_MODEL),) if norm_input else ()) + (w_in, w_in, w_in)
    outs = pl.pallas_call(
        functools.partial(_qkv_kernel, dil=dil, norm_input=norm_input),
        grid=(batch,),
        in_specs=in_specs,
        out_specs=out_specs,
        out_shape=out_shape,
        scratch_shapes=[] if dil == 1 else [pltpu.VMEM((SEQ, D_MODEL), _BF16)],
        compiler_params=pltpu.CompilerParams(dimension_semantics=("parallel",),
                                             vmem_limit_bytes=VMEM_LIMIT),
        name=f"qkv{gi}",
    )(*args)
    return outs if norm_input else outs[0]


def _t5_buckets_np(rel):
    n = -rel
    half = N_BUCKETS // 2
    ret = (n < 0).astype(np.int32) * half
    n = np.abs(n)
    max_exact = half // 2
    large = max_exact + (np.log(np.maximum(n, 1) / max_exact)
                         / np.log(MAX_DISTANCE / max_exact) * (half - max_exact)).astype(np.int32)
    large = np.minimum(large, half - 1)
    return (ret + np.where(n < max_exact, n, large)).astype(np.int32)


def _key_span(dil):
    return min(Q_TILE + 2 * HALF_WINDOW, SEQ // dil)


BIAS_LANES = 512


def _bias_rows(rel_bias):
    rows = []
    for gi, dil in enumerate(ATTN_DILATIONS):
        buckets = _t5_buckets_np(dil * np.arange(-HALF_WINDOW, HALF_WINDOW + 1))
        bias = rel_bias[buckets][:, gi * HEADS_PER_GROUP:(gi + 1) * HEADS_PER_GROUP].T.astype(_F32)
        rows.append(jnp.pad(bias, ((0, 0), (0, BIAS_LANES - bias.shape[1])), constant_values=NEG_INF))
    return jnp.stack(rows, axis=1)


def _tile_deltas(dil):
    return (0, HALF_WINDOW, 2 * HALF_WINDOW) if SEQ // dil > Q_TILE else (0,)


def _attn_kernel(q0_ref, k0_ref, v0_ref, q1_ref, k1_ref, v1_ref, q2_ref, k2_ref, v2_ref, e_ref, *rest, n_cast):
    cast_src, (o_ref, *rest) = rest[:n_cast], rest[n_cast:]
    cast_dst, (t0_ref, t1_ref, t2_ref, o_acc_ref, lse_ref, far_o_ref, far_lse_ref) = rest[:n_cast], rest[n_cast:]
    _cast_blocks(cast_src, cast_dst)
    groups = ((q0_ref, k0_ref, v0_ref, t0_ref), (q1_ref, k1_ref, v1_ref, t1_ref), (q2_ref, k2_ref, v2_ref, t2_ref))

    @pl.when(pl.program_id(1) == 0)
    def _():
        for gi, (_, _, _, t_ref) in enumerate(groups):
            dil = ATTN_DILATIONS[gi]
            row = jnp.broadcast_to(e_ref[gi:gi + 1, :], (Q_TILE, BIAS_LANES))
            for vi, delta in enumerate(_tile_deltas(dil)):
                skew = pltpu.roll(row, (delta - HALF_WINDOW) % BIAS_LANES, axis=1, stride=1, stride_axis=0)
                t_ref[vi] = skew[:, :_key_span(dil)]

    def scores(u, gi):
        q_ref, k_ref, _, t_ref = groups[gi]
        dil = ATTN_DILATIONS[gi]
        sub = SEQ // dil
        span = _key_span(dil)
        blocks = sub // Q_TILE
        r = u // blocks
        m0 = (u % blocks) * Q_TILE
        start = jnp.clip(m0 - HALF_WINDOW, 0, sub - span)
        variant = (m0 - start) // HALF_WINDOW
        q = q_ref[pl.ds(pl.multiple_of(u * Q_TILE, Q_TILE), Q_TILE), :]
        k0 = pl.multiple_of(r * sub + start, HALF_WINDOW)
        k = k_ref[pl.ds(k0, span), :]
        s = lax.dot_general(q, k, (((1,), (1,)), ((), ())), preferred_element_type=_F32)
        if dil == 1:
            dest = (gi, pl.ds(pl.multiple_of(m0, Q_TILE), Q_TILE))
        elif blocks > 1:
            dest = (gi, pl.ds(m0 * dil + r, Q_TILE, stride=dil))
        else:
            dest = (None, pl.ds(r * RESIDUE_PITCH, Q_TILE, stride=1))
        return s + t_ref[variant], k0, dest

    def weights(s):
        m = jnp.max(s, axis=-1, keepdims=True)
        return m, jnp.exp((s - m).astype(_BF16))

    ones = jnp.ones((Q_TILE + 2 * HALF_WINDOW, HEAD_DIM), _BF16)

    def values(gi, k0, dest, m, p):
        span = p.shape[1]
        v = groups[gi][2][pl.ds(k0, span), :]
        ov = jnp.dot(p, jnp.concatenate([v, ones[:span]], axis=1), preferred_element_type=_F32)
        l = ov[:, HEAD_DIM:]
        lse2 = m * LOG2_E + jnp.log2(l)
        slot, rows = dest
        if slot is None:
            far_o_ref[rows, :] = ov[:, :HEAD_DIM] / l
            far_lse_ref[rows, :] = lse2
        else:
            o_acc_ref[slot, rows, :] = ov[:, :HEAD_DIM] / l
            lse_ref[slot, rows, :] = lse2

    def units(it, carry):
        todo = [(it * UNITS_PER_STEP + j, gi) for j in range(UNITS_PER_STEP) for gi in range(3)]
        scored = [scores(u, gi) for u, gi in todo]
        soft = [weights(s) for s, _, _ in scored]
        for (_, gi), (_, k0, dest), (m, p) in zip(todo, scored, soft):
            values(gi, k0, dest, m, p)
        return carry

    lax.fori_loop(0, SEQ // Q_TILE // UNITS_PER_STEP, units, 0)

    far_dil = ATTN_DILATIONS[2]

    def merge(t, carry):
        rows = pl.ds(pl.multiple_of(t * Q_TILE, Q_TILE), Q_TILE)

        def far(ref):
            return jnp.concatenate([ref[pl.ds(t * (Q_TILE // far_dil) + j, far_dil, stride=RESIDUE_PITCH), :]
                                    for j in range(Q_TILE // far_dil)], axis=0)

        lses = [lse_ref[0, rows, :], lse_ref[1, rows, :], far(far_lse_ref)]
        outs = [o_acc_ref[0, rows, :], o_acc_ref[1, rows, :], far(far_o_ref)]
        top = jnp.maximum(jnp.maximum(lses[0], lses[1]), lses[2])
        ws = [jnp.exp2(x - top) for x in lses]
        num = ws[0] * outs[0] + ws[1] * outs[1] + ws[2] * outs[2]
        o_ref[rows, :] = (num / (ws[0] + ws[1] + ws[2])).astype(o_ref.dtype)
        return carry

    lax.fori_loop(0, SEQ // Q_TILE, merge, 0, unroll=4)


def _attn_call(qkv, bias_rows, to_cast, layer):
    batch = qkv[0].shape[0]

    def head_spec(which):
        return pl.BlockSpec((None, SEQ, HEAD_DIM), lambda h, b: (b, 0, which * HEADS_PER_GROUP + h))

    in_specs = [head_spec(which) for gi in range(3) for which in range(3)]
    in_specs.append(pl.BlockSpec((None, 3, BIAS_LANES), lambda h, b: (h, 0, 0)))
    riders = [_cast_rider(w, layer, HEADS_PER_GROUP * batch, lambda h, b: h * batch + b) for w in to_cast]
    in_specs += [r[0] for r in riders]
    scratch = [pltpu.VMEM((len(_tile_deltas(dil)), Q_TILE, _key_span(dil)), _F32) for dil in ATTN_DILATIONS]
    assert SEQ // ATTN_DILATIONS[2] == Q_TILE and all(SEQ // d > Q_TILE for d in ATTN_DILATIONS[:2])
    scratch += [pltpu.VMEM((2, SEQ, HEAD_DIM), _F32) for _ in range(2)]
    scratch += [pltpu.VMEM((ATTN_DILATIONS[2] * RESIDUE_PITCH, HEAD_DIM), _F32) for _ in range(2)]
    attn, *cast = pl.pallas_call(
        functools.partial(_attn_kernel, n_cast=len(riders)),
        grid=(HEADS_PER_GROUP, batch),
        in_specs=in_specs,
        out_specs=[pl.BlockSpec((None, SEQ, HEAD_DIM), lambda h, b: (b, 0, h))] + [r[1] for r in riders],
        out_shape=[jax.ShapeDtypeStruct((batch, SEQ, ATTN_OUT), _BF16)] + [r[2] for r in riders],
        scratch_shapes=scratch,
        compiler_params=pltpu.CompilerParams(dimension_semantics=("arbitrary", "arbitrary"),
                                             vmem_limit_bytes=VMEM_LIMIT),
        name="attn",
    )(*[qkv[gi] for gi in range(3) for _ in range(3)], bias_rows, *to_cast)
    return attn, cast


def _mix_kernel(xn_ref, pool_ref, attn_ref, h_ref, wg_ref, wa_ref, wb_ref, wo_ref, g_ref, h_out_ref, xn_out_ref):
    xn = xn_ref[...]
    gates = jnp.dot(xn, wg_ref[:, GATE_BLOCK - GATE_WIDTH:], preferred_element_type=_F32)
    a = jnp.dot(pool_ref[...], wa_ref[...], preferred_element_type=_F32)
    b = jnp.dot(attn_ref[...], wb_ref[...], preferred_element_type=_F32)
    merged = jax.nn.sigmoid(gates[:, :D_MODEL]) * a + jax.nn.sigmoid(gates[:, D_MODEL:]) * b
    h = h_ref[...] + jnp.dot(merged.astype(_BF16), wo_ref[...], preferred_element_type=_F32)
    h_out_ref[...] = h
    xn_out_ref[...] = _rms(h, g_ref[...]).astype(xn_out_ref.dtype)


def _mix_call(xn2d, pool2d, attn2d, h2d, w_in, w_a, w_b, w_o, g_next):
    rows = xn2d.shape[0]
    tm = ROW_TILE
    assert IN_WIDTH % GATE_BLOCK == 0 and GATE_BLOCK >= GATE_WIDTH
    return pl.pallas_call(
        _mix_kernel,
        grid=(rows // tm,),
        in_specs=[pl.BlockSpec((tm, D_MODEL), lambda i: (i, 0)),
                  pl.BlockSpec((tm, POOL_WIDTH), lambda i: (i, 0)),
                  pl.BlockSpec((tm, ATTN_OUT), lambda i: (i, 0)),
                  pl.BlockSpec((tm, D_MODEL), lambda i: (i, 0)),
                  _const_spec((D_MODEL, GATE_BLOCK), (0, IN_WIDTH // GATE_BLOCK - 1)),
                  _const_spec((POOL_WIDTH, D_MODEL)),
                  _const_spec((ATTN_OUT, D_MODEL)),
                  _const_spec((D_MODEL, D_MODEL)),
                  _const_spec((1, D_MODEL))],
        out_specs=[pl.BlockSpec((tm, D_MODEL), lambda i: (i, 0)),
                   pl.BlockSpec((tm, D_MODEL), lambda i: (i, 0))],
        out_shape=[jax.ShapeDtypeStruct((rows, D_MODEL), _F32),
                   jax.ShapeDtypeStruct((rows, D_MODEL), _BF16)],
        compiler_params=pltpu.CompilerParams(dimension_semantics=("parallel",),
                                             vmem_limit_bytes=VMEM_LIMIT),
        name="mix",
    )(xn2d, pool2d, attn2d, h2d, w_in, w_a, w_b, w_o, g_next.reshape(1, D_MODEL))


def _gelu_tanh(x):
    return 0.5 * x * (1.0 + jnp.tanh(np.sqrt(2.0 / np.pi).astype(np.float32) * (x + 0.044715 * (x * x * x))))


def _ffn_kernel(xp_ref, x_ref, xnx_ref, h_ref, wup_ref, cw_ref, cb_ref, wdn_ref, g_ref, *rest, final, n_cast):
    n_out = 1 if final else 2
    cast_src, out_refs, cast_dst = rest[:n_cast], rest[n_cast:n_cast + n_out], rest[n_cast + n_out:]
    _cast_blocks(cast_src, cast_dst)
    tm = x_ref.shape[0]
    tiles_per_seq = SEQ // tm
    t = pl.program_id(0) % tiles_per_seq
    x = x_ref[...]
    xe = jnp.concatenate([xp_ref[...], x, xnx_ref[...]], axis=0)
    pos = lax.broadcasted_iota(jnp.int32, (tm, 1), 0) + t * tm
    keep_prev = pos > 0
    keep_next = pos < SEQ - 1
    n_ext = tm + 2 * HALO
    acc = None
    assert sum(FF_CHUNKS) == D_FF
    for c, width in enumerate(FF_CHUNKS):
        c0 = sum(FF_CHUNKS[:c])
        cols = slice(c0, c0 + width)
        gcols = slice(D_FF + c0, D_FF + c0 + width)
        a_ext = jnp.dot(xe, wup_ref[:, cols], preferred_element_type=_F32)
        gate = jnp.dot(x, wup_ref[:, gcols], preferred_element_type=_F32)
        a_mid = a_ext[HALO:HALO + tm]
        a_prev = pltpu.roll(a_ext, 1, axis=0)[HALO:HALO + tm]
        a_next = pltpu.roll(a_ext, n_ext - 1, axis=0)[HALO:HALO + tm]
        cw = cw_ref[:, cols]
        conv = (jnp.where(keep_prev, a_prev, 0.0) * cw[0:1] + a_mid * cw[1:2]
                + jnp.where(keep_next, a_next, 0.0) * cw[2:3] + cb_ref[:, cols])
        act = (_gelu_tanh(conv) * gate).astype(_BF16)
        part = jnp.dot(act, wdn_ref[cols, :], preferred_element_type=_F32)
        acc = part if acc is None else acc + part
    h = h_ref[...] + acc
    if final:
        out_refs[0][...] = _rms(h, g_ref[...])
    else:
        out_refs[0][...] = h
        out_refs[1][...] = _rms(h, g_ref[...]).astype(out_refs[1].dtype)


def _ffn_call(xn2d, h2d, w_up, conv_w, conv_b, w_down, g_next, final, to_cast=(), cast_layer=0):
    rows = xn2d.shape[0]
    tm = ROW_TILE
    per = tm // HALO
    last = rows // HALO - 1
    row_spec = pl.BlockSpec((tm, D_MODEL), lambda i: (i, 0))
    if final:
        out_specs = [row_spec]
        out_shape = [jax.ShapeDtypeStruct((rows, D_MODEL), _F32)]
    else:
        out_specs = [row_spec, row_spec]
        out_shape = [jax.ShapeDtypeStruct((rows, D_MODEL), _F32),
                     jax.ShapeDtypeStruct((rows, D_MODEL), _BF16)]
    riders = [_cast_rider(w, cast_layer, rows // tm, lambda i: i) for w in to_cast]
    outs = pl.pallas_call(
        functools.partial(_ffn_kernel, final=final, n_cast=len(riders)),
        grid=(rows // tm,),
        in_specs=[pl.BlockSpec((HALO, D_MODEL), lambda i: (jnp.maximum(i * per - 1, 0), 0)),
                  row_spec,
                  pl.BlockSpec((HALO, D_MODEL), lambda i: (jnp.minimum((i + 1) * per, last), 0)),
                  row_spec,
                  _const_spec((D_MODEL, 2 * D_FF)),
                  _const_spec((3, D_FF)),
                  _const_spec((1, D_FF)),
                  _const_spec((D_FF, D_MODEL)),
                  _const_spec((1, D_MODEL))] + [r[0] for r in riders],
        out_specs=out_specs + [r[1] for r in riders],
        out_shape=out_shape + [r[2] for r in riders],
        compiler_params=pltpu.CompilerParams(dimension_semantics=("parallel",),
                                             vmem_limit_bytes=VMEM_LIMIT),
        name="ffn",
    )(xn2d, xn2d, xn2d, h2d, w_up, conv_w, conv_b.reshape(1, D_FF), w_down, g_next.reshape(1, D_MODEL), *to_cast)
    return outs[:len(out_shape)], outs[len(out_shape):]


def kernel(x, w_in, w_pool, pool_scale, w_a, w_b, w_o, norm1, norm2, w_up, conv_w, conv_b, w_down, rel_bias, norm_f):
    batch, seq, d = x.shape
    assert (seq, d) == (SEQ, D_MODEL)
    depth = w_in.shape[0]
    rows = batch * seq
    bias_rows = _bias_rows(rel_bias)
    h = x.reshape(rows, d)
    xn = None
    w_pool_bf = w_pool.astype(_BF16)
    w_in_bf = w_in[0].astype(_BF16)
    for layer in range(depth):
        if xn is None:
            qkv0, xn3 = _qkv_call(x, w_in_bf, 0, norm_gain=norm1[layer])
            xn = xn3.reshape(rows, d)
        else:
            xn3 = xn.reshape(batch, seq, d)
            qkv0 = _qkv_call(xn3, w_in_bf, 0)
        pool = _pool_call(xn3, w_in_bf, w_pool_bf, pool_scale[layer], layer)
        qkv = [qkv0] + [_qkv_call(xn3, w_in_bf, gi) for gi in (1, 2)]
        attn, (w_a_bf, w_b_bf, w_o_bf, w_up_bf, w_down_bf) = _attn_call(qkv, bias_rows, (w_a, w_b, w_o, w_up, w_down), layer)
        h, xn = _mix_call(xn, pool.reshape(rows, POOL_WIDTH), attn.reshape(rows, ATTN_OUT), h,
                          w_in_bf, w_a_bf, w_b_bf, w_o_bf, norm2[layer])
        final = layer == depth - 1
        g_next = norm_f if final else norm1[layer + 1]
        outs, cast = _ffn_call(xn, h, w_up_bf, conv_w[layer], conv_b[layer], w_down_bf, g_next, final,
                               to_cast=() if final else (w_in,), cast_layer=layer + 1)
        if final:
            return outs[0].reshape(batch, seq, d)
        h, xn = outs
        w_in_bf, = cast
```

```python
import functools

import numpy as np
import jax
import jax.numpy as jnp
from jax import lax
from jax.experimental import pallas as pl
from jax.experimental.pallas import tpu as pltpu

D_MODEL = 1024
SEQ = 2048
POOL_WINDOWS = (2, 4, 8, 16)
POOL_GROUP_DIM = 256
POOL_WIDTH = 1024
ATTN_DILATIONS = (1, 4, 16)
HALF_WINDOW = 64
HEADS_PER_GROUP = 4
N_HEADS = 12
HEAD_DIM = 128
ATTN_WIDTH = N_HEADS * HEAD_DIM
ATTN_OUT = HEADS_PER_GROUP * HEAD_DIM
NEG_INF = -1e30
N_BUCKETS = 32
MAX_DISTANCE = 1024
D_FF = 2816
EPS = 1e-6
PROJ_WIDTH = POOL_WIDTH + 3 * ATTN_WIDTH
GATE_WIDTH = 2 * D_MODEL
IN_WIDTH = PROJ_WIDTH + GATE_WIDTH
GATE_BLOCK = 2560

Q_TILE = 128
UNITS_PER_STEP = 8
LOG2_E = float(np.log2(np.e))
RESIDUE_PITCH = Q_TILE + 4
NORM_ROWS = 256
PROJ_TN = 512
ROW_TILE = 512
BF16_TILE_ROWS = 16
HALO = BF16_TILE_ROWS
FF_CHUNKS = (1536, 1280)
VMEM_LIMIT = 56 * 1024 * 1024
LANES = 128
REGROUP_TILE = 256

_F32 = jnp.float32
_BF16 = jnp.bfloat16


def _rms(x, g):
    return x * lax.rsqrt(jnp.mean(x * x, axis=-1, keepdims=True) + EPS) * g


def _const_spec(shape, index=None):
    index = tuple(index) if index is not None else (0,) * len(shape)
    return pl.BlockSpec(tuple(shape), lambda *_: index, pipeline_mode=pl.Buffered(1))


def _layer_spec(layer, shape, index=None):
    index = tuple(index) if index is not None else (0,) * len(shape)
    return pl.BlockSpec((None,) + tuple(shape), lambda *_: (layer,) + index, pipeline_mode=pl.Buffered(1))


def _cast_rider(param, layer, n_steps, step_of):
    _, rows, cols = param.shape
    n_blocks = max(n for n in range(1, n_steps + 1) if rows % n == 0 and (rows // n) % BF16_TILE_ROWS == 0)
    rb = rows // n_blocks

    def block(*g):
        return step_of(*g) * n_blocks // n_steps

    return (pl.BlockSpec((None, rb, cols), lambda *g: (layer, block(*g), 0)),
            pl.BlockSpec((rb, cols), lambda *g: (block(*g), 0)),
            jax.ShapeDtypeStruct((rows, cols), _BF16))


def _cast_blocks(src_refs, dst_refs):
    for src, dst in zip(src_refs, dst_refs, strict=True):
        dst[...] = src[...].astype(dst.dtype)


POOL_PAD = 16
POOL_EDGE = 8
assert POOL_EDGE >= max(POOL_WINDOWS) // 2 and POOL_PAD - POOL_EDGE >= max(POOL_WINDOWS) // 4


def _pool_kernel(xn_ref, w_ref, wpool_ref, pscale_ref, o_ref, u_ref, t_ref):
    n = SEQ + 2 * POOL_PAD
    slabs = POOL_GROUP_DIM // LANES

    def shifted(view, start, rows, k_back, k_fwd):
        return view[pl.ds(start - k_back, rows, stride=1), :] + view[pl.ds(start + k_fwd, rows, stride=1), :]

    def steps(w):
        return (1, 0) if w == 1 else (w // 2, w // 2)

    zeros = jnp.zeros((POOL_PAD, LANES), _F32)
    for t in range(2):
        for s in range(slabs):
            t_ref[t, s, 0:POOL_EDGE, :] = zeros[:POOL_EDGE]
            t_ref[t, s, n - POOL_EDGE:n, :] = zeros[:POOL_EDGE]
    xn = xn_ref[...]
    for gi in range(len(POOL_WINDOWS)):
        u = jnp.dot(xn, w_ref[:, gi * POOL_GROUP_DIM:(gi + 1) * POOL_GROUP_DIM], preferred_element_type=_F32)
        for s in range(slabs):
            u_ref[gi, s, 0:POOL_PAD, :] = zeros
            u_ref[gi, s, n - POOL_PAD:n, :] = zeros
            u_ref[gi, s, POOL_PAD:POOL_PAD + SEQ, :] = u[:, s * LANES:(s + 1) * LANES]
    edge_row = lax.broadcasted_iota(jnp.int32, (POOL_PAD, POOL_GROUP_DIM), 0)
    for gi, window in enumerate(POOL_WINDOWS):
        cols = slice(gi * POOL_GROUP_DIM, (gi + 1) * POOL_GROUP_DIM)
        totals = []
        for s in range(slabs):
            src, w, slot = u_ref.at[gi, s], 1, 0
            while 2 * w < window:
                t_ref[slot, s, pl.ds(POOL_EDGE, n - 2 * POOL_EDGE), :] = shifted(src, POOL_EDGE, n - 2 * POOL_EDGE,
                                                                              *steps(w))
                src, w, slot = t_ref.at[slot, s], 2 * w, 1 - slot
            totals.append(shifted(src, POOL_PAD, SEQ, *steps(w)))
        total = jnp.concatenate(totals, axis=1)
        u = jnp.concatenate([u_ref[gi, s, POOL_PAD:POOL_PAD + SEQ, :] for s in range(slabs)], axis=1)
        wpool = wpool_ref[gi]
        scale = pscale_ref[:, cols]
        pooled = total * (1.0 / window) - u
        z = jnp.dot(pooled.astype(_BF16), wpool, preferred_element_type=_F32)
        o_ref[:, cols] = (z * scale).astype(o_ref.dtype)
        for r0 in (0, SEQ - POOL_PAD):
            pos = edge_row + r0
            size = (jnp.minimum(pos + window // 2, SEQ) - jnp.maximum(pos - window // 2, 0)).astype(_F32)
            pooled = total[r0:r0 + POOL_PAD] / size - u[r0:r0 + POOL_PAD]
            z = jnp.dot(pooled.astype(_BF16), wpool, preferred_element_type=_F32)
            o_ref[r0:r0 + POOL_PAD, cols] = (z * scale).astype(o_ref.dtype)


def _pool_call(xn3, w_in, w_pool, pool_scale, layer):
    batch = xn3.shape[0]
    return pl.pallas_call(
        _pool_kernel,
        grid=(batch,),
        in_specs=[pl.BlockSpec((None, SEQ, D_MODEL), lambda b: (b, 0, 0)),
                  _const_spec((D_MODEL, POOL_WIDTH)),
                  _layer_spec(layer, (len(POOL_WINDOWS), POOL_GROUP_DIM, POOL_GROUP_DIM)),
                  _const_spec((1, POOL_WIDTH))],
        out_specs=pl.BlockSpec((None, SEQ, POOL_WIDTH), lambda b: (b, 0, 0)),
        out_shape=jax.ShapeDtypeStruct((batch, SEQ, POOL_WIDTH), _BF16),
        scratch_shapes=[pltpu.VMEM((len(POOL_WINDOWS), POOL_GROUP_DIM // LANES, SEQ + 2 * POOL_PAD, LANES), _F32),
                        pltpu.VMEM((2, POOL_GROUP_DIM // LANES, SEQ + 2 * POOL_PAD, LANES), _F32)],
        compiler_params=pltpu.CompilerParams(dimension_semantics=("parallel",),
                                             vmem_limit_bytes=VMEM_LIMIT),
        name="pool",
    )(xn3, w_in, w_pool, pool_scale.reshape(1, POOL_WIDTH))


def _qkv_kernel(x_ref, *refs, dil, norm_input):
    if norm_input:
        g_ref, wq_ref, wk_ref, wv_ref, o_ref, xn_ref, *scratch = refs
        for r0 in range(0, SEQ, NORM_ROWS):
            xn_ref[r0:r0 + NORM_ROWS, :] = _rms(x_ref[r0:r0 + NORM_ROWS, :], g_ref[...]).astype(xn_ref.dtype)
        x_ref = xn_ref
    else:
        wq_ref, wk_ref, wv_ref, o_ref, *scratch = refs
    if dil == 1:
        x = x_ref[...]
    else:
        xp_ref, = scratch
        sub = SEQ // dil
        sub_t = REGROUP_TILE // dil
        i = lax.broadcasted_iota(jnp.int32, (REGROUP_TILE, REGROUP_TILE), 0)
        j = lax.broadcasted_iota(jnp.int32, (REGROUP_TILE, REGROUP_TILE), 1)
        pick = (j == (i % sub_t) * dil + i // sub_t).astype(_BF16)
        for t in range(SEQ // REGROUP_TILE):
            tile = x_ref[t * REGROUP_TILE:(t + 1) * REGROUP_TILE, :]
            srt = jnp.dot(pick, tile, preferred_element_type=_F32).astype(_BF16)
            for r in range(dil):
                xp_ref[r * sub + t * sub_t:r * sub + (t + 1) * sub_t, :] = srt[r * sub_t:(r + 1) * sub_t]
        x = xp_ref[...]
    for which, w_ref in enumerate((wq_ref, wk_ref, wv_ref)):
        y = jnp.dot(x, w_ref[...], preferred_element_type=_F32)
        if which == 0:
            y = y * HEAD_DIM ** -0.5
        o_ref[:, which * PROJ_TN:(which + 1) * PROJ_TN] = y.astype(o_ref.dtype)


def _qkv_call(x3, w_in, gi, norm_gain=None):
    batch = x3.shape[0]
    dil = ATTN_DILATIONS[gi]
    first = POOL_WIDTH // PROJ_TN + gi
    norm_input = norm_gain is not None
    x_spec = pl.BlockSpec((None, SEQ, D_MODEL), lambda b: (b, 0, 0))
    in_specs = [x_spec] + ([_const_spec((1, D_MODEL))] if norm_input else [])
    in_specs += [_const_spec((D_MODEL, PROJ_TN), (0, first + 3 * which)) for which in range(3)]
    out_specs = [pl.BlockSpec((None, SEQ, 3 * PROJ_TN), lambda b: (b, 0, 0))]
    out_shape = [jax.ShapeDtypeStruct((batch, SEQ, 3 * PROJ_TN), _BF16)]
    if norm_input:
        out_specs.append(x_spec)
        out_shape.append(jax.ShapeDtypeStruct((batch, SEQ, D_MODEL), _BF16))
    args = (x3,) + ((norm_gain.reshape(1, D_MODEL),) if norm_input else ()) + (w_in, w_in, w_in)
    outs = pl.pallas_call(
        functools.partial(_qkv_kernel, dil=dil, norm_input=norm_input),
        grid=(batch,),
        in_specs=in_specs,
        out_specs=out_specs,
        out_shape=out_shape,
        scratch_shapes=[] if dil == 1 else [pltpu.VMEM((SEQ, D_MODEL), _BF16)],
        compiler_params=pltpu.CompilerParams(dimension_semantics=("parallel",),
                                             vmem_limit_bytes=VMEM_LIMIT),
        name=f"qkv{gi}",
    )(*args)
    return outs if norm_input else outs[0]


def _t5_buckets_np(rel):
    n = -rel
    half = N_BUCKETS // 2
    ret = (n < 0).astype(np.int32) * half
    n = np.abs(n)
    max_exact = half // 2
    large = max_exact + (np.log(np.maximum(n, 1) / max_exact)
                         / np.log(MAX_DISTANCE / max_exact) * (half - max_exact)).astype(np.int32)
    large = np.minimum(large, half - 1)
    return (ret + np.where(n < max_exact, n, large)).astype(np.int32)


def _key_span(dil):
    return min(Q_TILE + 2 * HALF_WINDOW, SEQ // dil)


BIAS_LANES = 512


def _bias_rows(rel_bias):
    rows = []
    for gi, dil in enumerate(ATTN_DILATIONS):
        buckets = _t5_buckets_np(dil * np.arange(-HALF_WINDOW, HALF_WINDOW + 1))
        bias = rel_bias[buckets][:, gi * HEADS_PER_GROUP:(gi + 1) * HEADS_PER_GROUP].T.astype(_F32)
        rows.append(jnp.pad(bias, ((0, 0), (0, BIAS_LANES - bias.shape[1])), constant_values=NEG_INF))
    return jnp.stack(rows, axis=1)


def _tile_deltas(dil):
    return (0, HALF_WINDOW, 2 * HALF_WINDOW) if SEQ // dil > Q_TILE else (0,)


def _attn_kernel(q0_ref, k0_ref, v0_ref, q1_ref, k1_ref, v1_ref, q2_ref, k2_ref, v2_ref, e_ref, *rest, n_cast):
    cast_src, (o_ref, *rest) = rest[:n_cast], rest[n_cast:]
    cast_dst, (t0_ref, t1_ref, t2_ref, o_acc_ref, lse_ref, far_o_ref, far_lse_ref) = rest[:n_cast], rest[n_cast:]
    _cast_blocks(cast_src, cast_dst)
    groups = ((q0_ref, k0_ref, v0_ref, t0_ref), (q1_ref, k1_ref, v1_ref, t1_ref), (q2_ref, k2_ref, v2_ref, t2_ref))

    @pl.when(pl.program_id(1) == 0)
    def _():
        for gi, (_, _, _, t_ref) in enumerate(groups):
            dil = ATTN_DILATIONS[gi]
            row = jnp.broadcast_to(e_ref[gi:gi + 1, :], (Q_TILE, BIAS_LANES))
            for vi, delta in enumerate(_tile_deltas(dil)):
                skew = pltpu.roll(row, (delta - HALF_WINDOW) % BIAS_LANES, axis=1, stride=1, stride_axis=0)
                t_ref[vi] = skew[:, :_key_span(dil)]

    def scores(u, gi):
        q_ref, k_ref, _, t_ref = groups[gi]
        dil = ATTN_DILATIONS[gi]
        sub = SEQ // dil
        span = _key_span(dil)
        blocks = sub // Q_TILE
        r = u // blocks
        m0 = (u % blocks) * Q_TILE
        start = jnp.clip(m0 - HALF_WINDOW, 0, sub - span)
        variant = (m0 - start) // HALF_WINDOW
        q = q_ref[pl.ds(pl.multiple_of(u * Q_TILE, Q_TILE), Q_TILE), :]
        k0 = pl.multiple_of(r * sub + start, HALF_WINDOW)
        k = k_ref[pl.ds(k0, span), :]
        s = lax.dot_general(q, k, (((1,), (1,)), ((), ())), preferred_element_type=_F32)
        if dil == 1:
            dest = (gi, pl.ds(pl.multiple_of(m0, Q_TILE), Q_TILE))
        elif blocks > 1:
            dest = (gi, pl.ds(m0 * dil + r, Q_TILE, stride=dil))
        else:
            dest = (None, pl.ds(r * RESIDUE_PITCH, Q_TILE, stride=1))
        return s + t_ref[variant], k0, dest

    def weights(s):
        m = jnp.max(s, axis=-1, keepdims=True)
        return m, jnp.exp((s - m).astype(_BF16))

    ones = jnp.ones((Q_TILE + 2 * HALF_WINDOW, HEAD_DIM), _BF16)

    def values(gi, k0, dest, m, p):
        span = p.shape[1]
        v = groups[gi][2][pl.ds(k0, span), :]
        ov = jnp.dot(p, jnp.concatenate([v, ones[:span]], axis=1), preferred_element_type=_F32)
        l = ov[:, HEAD_DIM:]
        lse2 = m * LOG2_E + jnp.log2(l)
        slot, rows = dest
        if slot is None:
            far_o_ref[rows, :] = ov[:, :HEAD_DIM] / l
            far_lse_ref[rows, :] = lse2
        else:
            o_acc_ref[slot, rows, :] = ov[:, :HEAD_DIM] / l
            lse_ref[slot, rows, :] = lse2

    def units(it, carry):
        todo = [(it * UNITS_PER_STEP + j, gi) for j in range(UNITS_PER_STEP) for gi in range(3)]
        scored = [scores(u, gi) for u, gi in todo]
        soft = [weights(s) for s, _, _ in scored]
        for (_, gi), (_, k0, dest), (m, p) in zip(todo, scored, soft):
            values(gi, k0, dest, m, p)
        return carry

    lax.fori_loop(0, SEQ // Q_TILE // UNITS_PER_STEP, units, 0)

    far_dil = ATTN_DILATIONS[2]

    def merge(t, carry):
        rows = pl.ds(pl.multiple_of(t * Q_TILE, Q_TILE), Q_TILE)

        def far(ref):
            return jnp.concatenate([ref[pl.ds(t * (Q_TILE // far_dil) + j, far_dil, stride=RESIDUE_PITCH), :]
                                    for j in range(Q_TILE // far_dil)], axis=0)

        lses = [lse_ref[0, rows, :], lse_ref[1, rows, :], far(far_lse_ref)]
        outs = [o_acc_ref[0, rows, :], o_acc_ref[1, rows, :], far(far_o_ref)]
        top = jnp.maximum(jnp.maximum(lses[0], lses[1]), lses[2])
        ws = [jnp.exp2(x - top) for x in lses]
        num = ws[0] * outs[0] + ws[1] * outs[1] + ws[2] * outs[2]
        o_ref[rows, :] = (num / (ws[0] + ws[1] + ws[2])).astype(o_ref.dtype)
        return carry

    lax.fori_loop(0, SEQ // Q_TILE, merge, 0, unroll=4)


def _attn_call(qkv, bias_rows, to_cast, layer):
    batch = qkv[0].shape[0]

    def head_spec(which):
        return pl.BlockSpec((None, SEQ, HEAD_DIM), lambda h, b: (b, 0, which * HEADS_PER_GROUP + h))

    in_specs = [head_spec(which) for gi in range(3) for which in range(3)]
    in_specs.append(pl.BlockSpec((None, 3, BIAS_LANES), lambda h, b: (h, 0, 0)))
    riders = [_cast_rider(w, layer, HEADS_PER_GROUP * batch, lambda h, b: h * batch + b) for w in to_cast]
    in_specs += [r[0] for r in riders]
    scratch = [pltpu.VMEM((len(_tile_deltas(dil)), Q_TILE, _key_span(dil)), _F32) for dil in ATTN_DILATIONS]
    assert SEQ // ATTN_DILATIONS[2] == Q_TILE and all(SEQ // d > Q_TILE for d in ATTN_DILATIONS[:2])
    scratch += [pltpu.VMEM((2, SEQ, HEAD_DIM), _F32) for _ in range(2)]
    scratch += [pltpu.VMEM((ATTN_DILATIONS[2] * RESIDUE_PITCH, HEAD_DIM), _F32) for _ in range(2)]
    attn, *cast = pl.pallas_call(
        functools.partial(_attn_kernel, n_cast=len(riders)),
        grid=(HEADS_PER_GROUP, batch),
        in_specs=in_specs,
        out_specs=[pl.BlockSpec((None, SEQ, HEAD_DIM), lambda h, b: (b, 0, h))] + [r[1] for r in riders],
        out_shape=[jax.ShapeDtypeStruct((batch, SEQ, ATTN_OUT), _BF16)] + [r[2] for r in riders],
        scratch_shapes=scratch,
        compiler_params=pltpu.CompilerParams(dimension_semantics=("arbitrary", "arbitrary"),
                                             vmem_limit_bytes=VMEM_LIMIT),
        name="attn",
    )(*[qkv[gi] for gi in range(3) for _ in range(3)], bias_rows, *to_cast)
    return attn, cast


def _mix_kernel(xn_ref, pool_ref, attn_ref, h_ref, wg_ref, wa_ref, wb_ref, wo_ref, g_ref, h_out_ref, xn_out_ref):
    xn = xn_ref[...]
    gates = jnp.dot(xn, wg_ref[:, GATE_BLOCK - GATE_WIDTH:], preferred_element_type=_F32)
    a = jnp.dot(pool_ref[...], wa_ref[...], preferred_element_type=_F32)
    b = jnp.dot(attn_ref[...], wb_ref[...], preferred_element_type=_F32)
    merged = jax.nn.sigmoid(gates[:, :D_MODEL]) * a + jax.nn.sigmoid(gates[:, D_MODEL:]) * b
    h = h_ref[...] + jnp.dot(merged.astype(_BF16), wo_ref[...], preferred_element_type=_F32)
    h_out_ref[...] = h
    xn_out_ref[...] = _rms(h, g_ref[...]).astype(xn_out_ref.dtype)


def _mix_call(xn2d, pool2d, attn2d, h2d, w_in, w_a, w_b, w_o, g_next):
    rows = xn2d.shape[0]
    tm = ROW_TILE
    assert IN_WIDTH % GATE_BLOCK == 0 and GATE_BLOCK >= GATE_WIDTH
    return pl.pallas_call(
        _mix_kernel,
        grid=(rows // tm,),
        in_specs=[pl.BlockSpec((tm, D_MODEL), lambda i: (i, 0)),
                  pl.BlockSpec((tm, POOL_WIDTH), lambda i: (i, 0)),
                  pl.BlockSpec((tm, ATTN_OUT), lambda i: (i, 0)),
                  pl.BlockSpec((tm, D_MODEL), lambda i: (i, 0)),
                  _const_spec((D_MODEL, GATE_BLOCK), (0, IN_WIDTH // GATE_BLOCK - 1)),
                  _const_spec((POOL_WIDTH, D_MODEL)),
                  _const_spec((ATTN_OUT, D_MODEL)),
                  _const_spec((D_MODEL, D_MODEL)),
                  _const_spec((1, D_MODEL))],
        out_specs=[pl.BlockSpec((tm, D_MODEL), lambda i: (i, 0)),
                   pl.BlockSpec((tm, D_MODEL), lambda i: (i, 0))],
        out_shape=[jax.ShapeDtypeStruct((rows, D_MODEL), _F32),
                   jax.ShapeDtypeStruct((rows, D_MODEL), _BF16)],
        compiler_params=pltpu.CompilerParams(dimension_semantics=("parallel",),
                                             vmem_limit_bytes=VMEM_LIMIT),
        name="mix",
    )(xn2d, pool2d, attn2d, h2d, w_in, w_a, w_b, w_o, g_next.reshape(1, D_MODEL))


def _gelu_tanh(x):
    return 0.5 * x * (1.0 + jnp.tanh(np.sqrt(2.0 / np.pi).astype(np.float32) * (x + 0.044715 * (x * x * x))))


def _ffn_kernel(xp_ref, x_ref, xnx_ref, h_ref, wup_ref, cw_ref, cb_ref, wdn_ref, g_ref, *rest, final, n_cast):
    n_out = 1 if final else 2
    cast_src, out_refs, rest = rest[:n_cast], rest[n_cast:n_cast + n_out], rest[n_cast + n_out:]
    cast_dst, a_refs = rest[:n_cast], rest[n_cast:]
    _cast_blocks(cast_src, cast_dst)
    tm = x_ref.shape[0]
    tiles_per_seq = SEQ // tm
    t = pl.program_id(0) % tiles_per_seq
    x = x_ref[...]
    x_prev = jnp.where(t > 0, xp_ref[...], jnp.zeros_like(xp_ref))
    x_next = jnp.where(t < tiles_per_seq - 1, xnx_ref[...], jnp.zeros_like(xnx_ref))
    xe = jnp.concatenate([x_prev, x, x_next], axis=0)
    acc = None
    assert sum(FF_CHUNKS) == D_FF
    for c, width in enumerate(FF_CHUNKS):
        c0 = sum(FF_CHUNKS[:c])
        cols = slice(c0, c0 + width)
        gcols = slice(D_FF + c0, D_FF + c0 + width)
        a_ext = jnp.dot(xe, wup_ref[:, cols], preferred_element_type=_F32)
        gate = jnp.dot(x, wup_ref[:, gcols], preferred_element_type=_F32)
        a_mid = a_ext[HALO:HALO + tm]
        a_ref = a_refs[c]
        for j in range(width // LANES):
            a_ref[j] = a_ext[:, j * LANES:(j + 1) * LANES]
        a_prev = jnp.concatenate([a_ref[j, pl.ds(HALO - 1, tm, stride=1), :] for j in range(width // LANES)], axis=1)
        a_next = jnp.concatenate([a_ref[j, pl.ds(HALO + 1, tm, stride=1), :] for j in range(width // LANES)], axis=1)
        cw = cw_ref[:, cols]
        conv = a_prev * cw[0:1] + a_mid * cw[1:2] + a_next * cw[2:3] + cb_ref[:, cols]
        act = (_gelu_tanh(conv) * gate).astype(_BF16)
        part = jnp.dot(act, wdn_ref[cols, :], preferred_element_type=_F32)
        acc = part if acc is None else acc + part
    h = h_ref[...] + acc
    if final:
        out_refs[0][...] = _rms(h, g_ref[...])
    else:
        out_refs[0][...] = h
        out_refs[1][...] = _rms(h, g_ref[...]).astype(out_refs[1].dtype)


def _ffn_call(xn2d, h2d, w_up, conv_w, conv_b, w_down, g_next, final, to_cast=(), cast_layer=0):
    rows = xn2d.shape[0]
    tm = ROW_TILE
    per = tm // HALO
    last = rows // HALO - 1
    row_spec = pl.BlockSpec((tm, D_MODEL), lambda i: (i, 0))
    if final:
        out_specs = [row_spec]
        out_shape = [jax.ShapeDtypeStruct((rows, D_MODEL), _F32)]
    else:
        out_specs = [row_spec, row_spec]
        out_shape = [jax.ShapeDtypeStruct((rows, D_MODEL), _F32),
                     jax.ShapeDtypeStruct((rows, D_MODEL), _BF16)]
    riders = [_cast_rider(w, cast_layer, rows // tm, lambda i: i) for w in to_cast]
    outs = pl.pallas_call(
        functools.partial(_ffn_kernel, final=final, n_cast=len(riders)),
        grid=(rows // tm,),
        in_specs=[pl.BlockSpec((HALO, D_MODEL), lambda i: (jnp.maximum(i * per - 1, 0), 0)),
                  row_spec,
                  pl.BlockSpec((HALO, D_MODEL), lambda i: (jnp.minimum((i + 1) * per, last), 0)),
                  row_spec,
                  _const_spec((D_MODEL, 2 * D_FF)),
                  _const_spec((3, D_FF)),
                  _const_spec((1, D_FF)),
                  _const_spec((D_FF, D_MODEL)),
                  _const_spec((1, D_MODEL))] + [r[0] for r in riders],
        out_specs=out_specs + [r[1] for r in riders],
        out_shape=out_shape + [r[2] for r in riders],
        scratch_shapes=[pltpu.VMEM((width // LANES, tm + 2 * HALO, LANES), _F32) for width in FF_CHUNKS],
        compiler_params=pltpu.CompilerParams(dimension_semantics=("parallel",),
                                             vmem_limit_bytes=VMEM_LIMIT),
        name="ffn",
    )(xn2d, xn2d, xn2d, h2d, w_up, conv_w, conv_b.reshape(1, D_FF), w_down, g_next.reshape(1, D_MODEL), *to_cast)
    return outs[:len(out_shape)], outs[len(out_shape):]


def kernel(x, w_in, w_pool, pool_scale, w_a, w_b, w_o, norm1, norm2, w_up, conv_w, conv_b, w_down, rel_bias, norm_f):
    batch, seq, d = x.shape
    assert (seq, d) == (SEQ, D_MODEL)
    depth = w_in.shape[0]
    rows = batch * seq
    bias_rows = _bias_rows(rel_bias)
    h = x.reshape(rows, d)
    xn = None
    w_pool_bf = w_pool.astype(_BF16)
    w_in_bf = w_in[0].astype(_BF16)
    for layer in range(depth):
        if xn is None:
            qkv0, xn3 = _qkv_call(x, w_in_bf, 0, norm_gain=norm1[layer])
            xn = xn3.reshape(rows, d)
        else:
            xn3 = xn.reshape(batch, seq, d)
            qkv0 = _qkv_call(xn3, w_in_bf, 0)
        pool = _pool_call(xn3, w_in_bf, w_pool_bf, pool_scale[layer], layer)
        qkv = [qkv0] + [_qkv_call(xn3, w_in_bf, gi) for gi in (1, 2)]
        attn, (w_a_bf, w_b_bf, w_o_bf, w_up_bf, w_down_bf) = _attn_call(qkv, bias_rows, (w_a, w_b, w_o, w_up, w_down), layer)
        h, xn = _mix_call(xn, pool.reshape(rows, POOL_WIDTH), attn.reshape(rows, ATTN_OUT), h,
                          w_in_bf, w_a_bf, w_b_bf, w_o_bf, norm2[layer])
        final = layer == depth - 1
        g_next = norm_f if final else norm1[layer + 1]
        outs, cast = _ffn_call(xn, h, w_up_bf, conv_w[layer], conv_b[layer], w_down_bf, g_next, final,
                               to_cast=() if final else (w_in,), cast_layer=layer + 1)
        if final:
            return outs[0].reshape(batch, seq, d)
        h, xn = outs
        w_in_bf, = cast
```

```python
import functools

import numpy as np
import jax
import jax.numpy as jnp
from jax import lax
from jax.experimental import pallas as pl
from jax.experimental.pallas import tpu as pltpu

D_MODEL = 1024
SEQ = 2048
POOL_WINDOWS = (2, 4, 8, 16)
POOL_GROUP_DIM = 256
POOL_WIDTH = 1024
ATTN_DILATIONS = (1, 4, 16)
HALF_WINDOW = 64
HEADS_PER_GROUP = 4
N_HEADS = 12
HEAD_DIM = 128
ATTN_WIDTH = N_HEADS * HEAD_DIM
ATTN_OUT = HEADS_PER_GROUP * HEAD_DIM
NEG_INF = -1e30
N_BUCKETS = 32
MAX_DISTANCE = 1024
D_FF = 2816
EPS = 1e-6
PROJ_WIDTH = POOL_WIDTH + 3 * ATTN_WIDTH
GATE_WIDTH = 2 * D_MODEL
IN_WIDTH = PROJ_WIDTH + GATE_WIDTH
GATE_BLOCK = 2560

Q_TILE = 128
UNITS_PER_STEP = 8
LOG2_E = float(np.log2(np.e))
RESIDUE_PITCH = Q_TILE + 4
NORM_ROWS = 256
PROJ_TN = 512
ROW_TILE = 512
MIX_SUB_TILES = 2
BF16_TILE_ROWS = 16
HALO = BF16_TILE_ROWS
FF_CHUNKS = (1536, 1280)
VMEM_LIMIT = 56 * 1024 * 1024
LANES = 128
REGROUP_TILE = 256

_F32 = jnp.float32
_BF16 = jnp.bfloat16


def _rms(x, g):
    return x * lax.rsqrt(jnp.mean(x * x, axis=-1, keepdims=True) + EPS) * g


def _const_spec(shape, index=None):
    index = tuple(index) if index is not None else (0,) * len(shape)
    return pl.BlockSpec(tuple(shape), lambda *_: index, pipeline_mode=pl.Buffered(1))


def _layer_spec(layer, shape, index=None):
    index = tuple(index) if index is not None else (0,) * len(shape)
    return pl.BlockSpec((None,) + tuple(shape), lambda *_: (layer,) + index, pipeline_mode=pl.Buffered(1))


def _cast_rider(param, layer, n_steps, step_of):
    _, rows, cols = param.shape
    n_blocks = max(n for n in range(1, n_steps + 1) if rows % n == 0 and (rows // n) % BF16_TILE_ROWS == 0)
    rb = rows // n_blocks

    def block(*g):
        return step_of(*g) * n_blocks // n_steps

    return (pl.BlockSpec((None, rb, cols), lambda *g: (layer, block(*g), 0)),
            pl.BlockSpec((rb, cols), lambda *g: (block(*g), 0)),
            jax.ShapeDtypeStruct((rows, cols), _BF16))


def _cast_blocks(src_refs, dst_refs):
    for src, dst in zip(src_refs, dst_refs, strict=True):
        dst[...] = src[...].astype(dst.dtype)


POOL_PAD = 16
POOL_EDGE = 8
assert POOL_EDGE >= max(POOL_WINDOWS) // 2 and POOL_PAD - POOL_EDGE >= max(POOL_WINDOWS) // 4


def _pool_kernel(xn_ref, w_ref, wpool_ref, pscale_ref, o_ref, *scratch):
    *u_refs, t_ref = scratch
    n = SEQ + 2 * POOL_PAD
    slabs = POOL_GROUP_DIM // LANES

    def shifted(view, start, rows, k_back, k_fwd):
        return view[pl.ds(start - k_back, rows, stride=1), :] + view[pl.ds(start + k_fwd, rows, stride=1), :]

    def steps(w):
        return (1, 0) if w == 1 else (w // 2, w // 2)

    zeros = jnp.zeros((POOL_PAD, LANES), _F32)
    for t in range(2):
        for s in range(slabs):
            t_ref[t, s, 0:POOL_EDGE, :] = zeros[:POOL_EDGE]
            t_ref[t, s, n - POOL_EDGE:n, :] = zeros[:POOL_EDGE]
    xn = xn_ref[...]

    def project(gi):
        u = jnp.dot(xn, w_ref[:, gi * POOL_GROUP_DIM:(gi + 1) * POOL_GROUP_DIM], preferred_element_type=_F32)
        for s in range(slabs):
            u_refs[gi][s, 0:POOL_PAD, :] = zeros
            u_refs[gi][s, n - POOL_PAD:n, :] = zeros
            u_refs[gi][s, POOL_PAD:POOL_PAD + SEQ, :] = u[:, s * LANES:(s + 1) * LANES]

    edge_row = lax.broadcasted_iota(jnp.int32, (POOL_PAD, POOL_GROUP_DIM), 0)

    def pool(gi):
        window = POOL_WINDOWS[gi]
        u_ref = u_refs[gi]
        cols = slice(gi * POOL_GROUP_DIM, (gi + 1) * POOL_GROUP_DIM)
        totals = []
        for s in range(slabs):
            src, w, slot = u_ref.at[s], 1, 0
            while 2 * w < window:
                t_ref[slot, s, pl.ds(POOL_EDGE, n - 2 * POOL_EDGE), :] = shifted(src, POOL_EDGE, n - 2 * POOL_EDGE,
                                                                              *steps(w))
                src, w, slot = t_ref.at[slot, s], 2 * w, 1 - slot
            totals.append(shifted(src, POOL_PAD, SEQ, *steps(w)))
        total = jnp.concatenate(totals, axis=1)
        u = jnp.concatenate([u_ref[s, POOL_PAD:POOL_PAD + SEQ, :] for s in range(slabs)], axis=1)
        wpool = wpool_ref[gi]
        scale = pscale_ref[:, cols]
        pooled = total * (1.0 / window) - u
        z = jnp.dot(pooled.astype(_BF16), wpool, preferred_element_type=_F32)
        o_ref[:, cols] = (z * scale).astype(o_ref.dtype)
        for r0 in (0, SEQ - POOL_PAD):
            pos = edge_row + r0
            size = (jnp.minimum(pos + window // 2, SEQ) - jnp.maximum(pos - window // 2, 0)).astype(_F32)
            pooled = total[r0:r0 + POOL_PAD] / size - u[r0:r0 + POOL_PAD]
            z = jnp.dot(pooled.astype(_BF16), wpool, preferred_element_type=_F32)
            o_ref[r0:r0 + POOL_PAD, cols] = (z * scale).astype(o_ref.dtype)

    n_groups = len(POOL_WINDOWS)
    project(0)
    for gi in range(n_groups):
        if gi + 1 < n_groups:
            project(gi + 1)
        pool(gi)


def _pool_call(xn3, w_in, w_pool, pool_scale, layer):
    batch = xn3.shape[0]
    return pl.pallas_call(
        _pool_kernel,
        grid=(batch,),
        in_specs=[pl.BlockSpec((None, SEQ, D_MODEL), lambda b: (b, 0, 0)),
                  _const_spec((D_MODEL, POOL_WIDTH)),
                  _layer_spec(layer, (len(POOL_WINDOWS), POOL_GROUP_DIM, POOL_GROUP_DIM)),
                  _const_spec((1, POOL_WIDTH))],
        out_specs=pl.BlockSpec((None, SEQ, POOL_WIDTH), lambda b: (b, 0, 0)),
        out_shape=jax.ShapeDtypeStruct((batch, SEQ, POOL_WIDTH), _BF16),
        scratch_shapes=[pltpu.VMEM((POOL_GROUP_DIM // LANES, SEQ + 2 * POOL_PAD, LANES), _F32) for _ in POOL_WINDOWS]
        + [pltpu.VMEM((2, POOL_GROUP_DIM // LANES, SEQ + 2 * POOL_PAD, LANES), _F32)],
        compiler_params=pltpu.CompilerParams(dimension_semantics=("parallel",),
                                             vmem_limit_bytes=VMEM_LIMIT),
        name="pool",
    )(xn3, w_in, w_pool, pool_scale.reshape(1, POOL_WIDTH))


def _qkv_kernel(x_ref, *refs, dil, norm_input):
    if norm_input:
        g_ref, wq_ref, wk_ref, wv_ref, o_ref, xn_ref, *scratch = refs
        for r0 in range(0, SEQ, NORM_ROWS):
            xn_ref[r0:r0 + NORM_ROWS, :] = _rms(x_ref[r0:r0 + NORM_ROWS, :], g_ref[...]).astype(xn_ref.dtype)
        x_ref = xn_ref
    else:
        wq_ref, wk_ref, wv_ref, o_ref, *scratch = refs
    if dil == 1:
        x = x_ref[...]
    else:
        xp_ref, = scratch
        sub = SEQ // dil
        sub_t = REGROUP_TILE // dil
        i = lax.broadcasted_iota(jnp.int32, (REGROUP_TILE, REGROUP_TILE), 0)
        j = lax.broadcasted_iota(jnp.int32, (REGROUP_TILE, REGROUP_TILE), 1)
        pick = (j == (i % sub_t) * dil + i // sub_t).astype(_BF16)
        for t in range(SEQ // REGROUP_TILE):
            tile = x_ref[t * REGROUP_TILE:(t + 1) * REGROUP_TILE, :]
            srt = jnp.dot(pick, tile, preferred_element_type=_F32).astype(_BF16)
            for r in range(dil):
                xp_ref[r * sub + t * sub_t:r * sub + (t + 1) * sub_t, :] = srt[r * sub_t:(r + 1) * sub_t]
        x = xp_ref[...]
    for which, w_ref in enumerate((wq_ref, wk_ref, wv_ref)):
        y = jnp.dot(x, w_ref[...], preferred_element_type=_F32)
        if which == 0:
            y = y * HEAD_DIM ** -0.5
        o_ref[:, which * PROJ_TN:(which + 1) * PROJ_TN] = y.astype(o_ref.dtype)


def _qkv_call(x3, w_in, gi, norm_gain=None):
    batch = x3.shape[0]
    dil = ATTN_DILATIONS[gi]
    first = POOL_WIDTH // PROJ_TN + gi
    norm_input = norm_gain is not None
    x_spec = pl.BlockSpec((None, SEQ, D_MODEL), lambda b: (b, 0, 0))
    in_specs = [x_spec] + ([_const_spec((1, D_MODEL))] if norm_input else [])
    in_specs += [_const_spec((D_MODEL, PROJ_TN), (0, first + 3 * which)) for which in range(3)]
    out_specs = [pl.BlockSpec((None, SEQ, 3 * PROJ_TN), lambda b: (b, 0, 0))]
    out_shape = [jax.ShapeDtypeStruct((batch, SEQ, 3 * PROJ_TN), _BF16)]
    if norm_input:
        out_specs.append(x_spec)
        out_shape.append(jax.ShapeDtypeStruct((batch, SEQ, D_MODEL), _BF16))
    args = (x3,) + ((norm_gain.reshape(1, D_MODEL),) if norm_input else ()) + (w_in, w_in, w_in)
    outs = pl.pallas_call(
        functools.partial(_qkv_kernel, dil=dil, norm_input=norm_input),
        grid=(batch,),
        in_specs=in_specs,
        out_specs=out_specs,
        out_shape=out_shape,
        scratch_shapes=[] if dil == 1 else [pltpu.VMEM((SEQ, D_MODEL), _BF16)],
        compiler_params=pltpu.CompilerParams(dimension_semantics=("parallel",),
                                             vmem_limit_bytes=VMEM_LIMIT),
        name=f"qkv{gi}",
    )(*args)
    return outs if norm_input else outs[0]


def _t5_buckets_np(rel):
    n = -rel
    half = N_BUCKETS // 2
    ret = (n < 0).astype(np.int32) * half
    n = np.abs(n)
    max_exact = half // 2
    large = max_exact + (np.log(np.maximum(n, 1) / max_exact)
                         / np.log(MAX_DISTANCE / max_exact) * (half - max_exact)).astype(np.int32)
    large = np.minimum(large, half - 1)
    return (ret + np.where(n < max_exact, n, large)).astype(np.int32)


def _key_span(dil):
    return min(Q_TILE + 2 * HALF_WINDOW, SEQ // dil)


BIAS_LANES = 512


def _bias_rows(rel_bias):
    rows = []
    for gi, dil in enumerate(ATTN_DILATIONS):
        buckets = _t5_buckets_np(dil * np.arange(-HALF_WINDOW, HALF_WINDOW + 1))
        bias = rel_bias[buckets][:, gi * HEADS_PER_GROUP:(gi + 1) * HEADS_PER_GROUP].T.astype(_F32)
        rows.append(jnp.pad(bias, ((0, 0), (0, BIAS_LANES - bias.shape[1])), constant_values=NEG_INF))
    return jnp.stack(rows, axis=1)


def _tile_deltas(dil):
    return (0, HALF_WINDOW, 2 * HALF_WINDOW) if SEQ // dil > Q_TILE else (0,)


def _attn_kernel(q0_ref, k0_ref, v0_ref, q1_ref, k1_ref, v1_ref, q2_ref, k2_ref, v2_ref, e_ref, *rest, n_cast):
    cast_src, (o_ref, *rest) = rest[:n_cast], rest[n_cast:]
    cast_dst, (t0_ref, t1_ref, t2_ref, o_acc_ref, lse_ref, far_o_ref, far_lse_ref) = rest[:n_cast], rest[n_cast:]
    _cast_blocks(cast_src, cast_dst)
    groups = ((q0_ref, k0_ref, v0_ref, t0_ref), (q1_ref, k1_ref, v1_ref, t1_ref), (q2_ref, k2_ref, v2_ref, t2_ref))

    @pl.when(pl.program_id(1) == 0)
    def _():
        for gi, (_, _, _, t_ref) in enumerate(groups):
            dil = ATTN_DILATIONS[gi]
            row = jnp.broadcast_to(e_ref[gi:gi + 1, :], (Q_TILE, BIAS_LANES))
            for vi, delta in enumerate(_tile_deltas(dil)):
                skew = pltpu.roll(row, (delta - HALF_WINDOW) % BIAS_LANES, axis=1, stride=1, stride_axis=0)
                t_ref[vi] = skew[:, :_key_span(dil)]

    def scores(u, gi):
        q_ref, k_ref, _, t_ref = groups[gi]
        dil = ATTN_DILATIONS[gi]
        sub = SEQ // dil
        span = _key_span(dil)
        blocks = sub // Q_TILE
        r = u // blocks
        m0 = (u % blocks) * Q_TILE
        start = jnp.clip(m0 - HALF_WINDOW, 0, sub - span)
        variant = (m0 - start) // HALF_WINDOW
        q = q_ref[pl.ds(pl.multiple_of(u * Q_TILE, Q_TILE), Q_TILE), :]
        k0 = pl.multiple_of(r * sub + start, HALF_WINDOW)
        k = k_ref[pl.ds(k0, span), :]
        s = lax.dot_general(q, k, (((1,), (1,)), ((), ())), preferred_element_type=_F32)
        if dil == 1:
            dest = (gi, pl.ds(pl.multiple_of(m0, Q_TILE), Q_TILE))
        elif blocks > 1:
            dest = (gi, pl.ds(m0 * dil + r, Q_TILE, stride=dil))
        else:
            dest = (None, pl.ds(r * RESIDUE_PITCH, Q_TILE, stride=1))
        return s + t_ref[variant], k0, dest

    def weights(s):
        m = jnp.max(s, axis=-1, keepdims=True)
        return m, jnp.exp((s - m).astype(_BF16))

    ones = jnp.ones((Q_TILE + 2 * HALF_WINDOW, HEAD_DIM), _BF16)

    def values(gi, k0, dest, m, p):
        span = p.shape[1]
        v = groups[gi][2][pl.ds(k0, span), :]
        ov = jnp.dot(p, jnp.concatenate([v, ones[:span]], axis=1), preferred_element_type=_F32)
        l = ov[:, HEAD_DIM:]
        lse2 = m * LOG2_E + jnp.log2(l)
        slot, rows = dest
        if slot is None:
            far_o_ref[rows, :] = ov[:, :HEAD_DIM] / l
            far_lse_ref[rows, :] = lse2
        else:
            o_acc_ref[slot, rows, :] = ov[:, :HEAD_DIM] / l
            lse_ref[slot, rows, :] = lse2

    def units(it, carry):
        todo = [(it * UNITS_PER_STEP + j, gi) for j in range(UNITS_PER_STEP) for gi in range(3)]
        scored = [scores(u, gi) for u, gi in todo]
        soft = [weights(s) for s, _, _ in scored]
        for (_, gi), (_, k0, dest), (m, p) in zip(todo, scored, soft):
            values(gi, k0, dest, m, p)
        return carry

    lax.fori_loop(0, SEQ // Q_TILE // UNITS_PER_STEP, units, 0)

    far_dil = ATTN_DILATIONS[2]

    def merge(t, carry):
        rows = pl.ds(pl.multiple_of(t * Q_TILE, Q_TILE), Q_TILE)

        def far(ref):
            return jnp.concatenate([ref[pl.ds(t * (Q_TILE // far_dil) + j, far_dil, stride=RESIDUE_PITCH), :]
                                    for j in range(Q_TILE // far_dil)], axis=0)

        lses = [lse_ref[0, rows, :], lse_ref[1, rows, :], far(far_lse_ref)]
        outs = [o_acc_ref[0, rows, :], o_acc_ref[1, rows, :], far(far_o_ref)]
        top = jnp.maximum(jnp.maximum(lses[0], lses[1]), lses[2])
        ws = [jnp.exp2(x - top) for x in lses]
        num = ws[0] * outs[0] + ws[1] * outs[1] + ws[2] * outs[2]
        o_ref[rows, :] = (num / (ws[0] + ws[1] + ws[2])).astype(o_ref.dtype)
        return carry

    lax.fori_loop(0, SEQ // Q_TILE, merge, 0, unroll=4)


def _attn_call(qkv, bias_rows, to_cast, layer):
    batch = qkv[0].shape[0]

    def head_spec(which):
        return pl.BlockSpec((None, SEQ, HEAD_DIM), lambda h, b: (b, 0, which * HEADS_PER_GROUP + h))

    in_specs = [head_spec(which) for gi in range(3) for which in range(3)]
    in_specs.append(pl.BlockSpec((None, 3, BIAS_LANES), lambda h, b: (h, 0, 0)))
    riders = [_cast_rider(w, layer, HEADS_PER_GROUP * batch, lambda h, b: h * batch + b) for w in to_cast]
    in_specs += [r[0] for r in riders]
    scratch = [pltpu.VMEM((len(_tile_deltas(dil)), Q_TILE, _key_span(dil)), _F32) for dil in ATTN_DILATIONS]
    assert SEQ // ATTN_DILATIONS[2] == Q_TILE and all(SEQ // d > Q_TILE for d in ATTN_DILATIONS[:2])
    scratch += [pltpu.VMEM((2, SEQ, HEAD_DIM), _F32) for _ in range(2)]
    scratch += [pltpu.VMEM((ATTN_DILATIONS[2] * RESIDUE_PITCH, HEAD_DIM), _F32) for _ in range(2)]
    attn, *cast = pl.pallas_call(
        functools.partial(_attn_kernel, n_cast=len(riders)),
        grid=(HEADS_PER_GROUP, batch),
        in_specs=in_specs,
        out_specs=[pl.BlockSpec((None, SEQ, HEAD_DIM), lambda h, b: (b, 0, h))] + [r[1] for r in riders],
        out_shape=[jax.ShapeDtypeStruct((batch, SEQ, ATTN_OUT), _BF16)] + [r[2] for r in riders],
        scratch_shapes=scratch,
        compiler_params=pltpu.CompilerParams(dimension_semantics=("arbitrary", "arbitrary"),
                                             vmem_limit_bytes=VMEM_LIMIT),
        name="attn",
    )(*[qkv[gi] for gi in range(3) for _ in range(3)], bias_rows, *to_cast)
    return attn, cast


def _mix_kernel(xn_ref, pool_ref, attn_ref, h_ref, wg_ref, wa_ref, wb_ref, wo_ref, g_ref, h_out_ref, xn_out_ref):
    for r0 in range(0, xn_ref.shape[0], ROW_TILE):
        rows = slice(r0, r0 + ROW_TILE)
        xn = xn_ref[rows, :]
        gates = jnp.dot(xn, wg_ref[:, GATE_BLOCK - GATE_WIDTH:], preferred_element_type=_F32)
        a = jnp.dot(pool_ref[rows, :], wa_ref[...], preferred_element_type=_F32)
        b = jnp.dot(attn_ref[rows, :], wb_ref[...], preferred_element_type=_F32)
        merged = jax.nn.sigmoid(gates[:, :D_MODEL]) * a + jax.nn.sigmoid(gates[:, D_MODEL:]) * b
        h = h_ref[rows, :] + jnp.dot(merged.astype(_BF16), wo_ref[...], preferred_element_type=_F32)
        h_out_ref[rows, :] = h
        xn_out_ref[rows, :] = _rms(h, g_ref[...]).astype(xn_out_ref.dtype)


def _mix_call(xn2d, pool2d, attn2d, h2d, w_in, w_a, w_b, w_o, g_next):
    rows = xn2d.shape[0]
    tm = MIX_SUB_TILES * ROW_TILE
    assert IN_WIDTH % GATE_BLOCK == 0 and GATE_BLOCK >= GATE_WIDTH
    return pl.pallas_call(
        _mix_kernel,
        grid=(rows // tm,),
        in_specs=[pl.BlockSpec((tm, D_MODEL), lambda i: (i, 0)),
                  pl.BlockSpec((tm, POOL_WIDTH), lambda i: (i, 0)),
                  pl.BlockSpec((tm, ATTN_OUT), lambda i: (i, 0)),
                  pl.BlockSpec((tm, D_MODEL), lambda i: (i, 0)),
                  _const_spec((D_MODEL, GATE_BLOCK), (0, IN_WIDTH // GATE_BLOCK - 1)),
                  _const_spec((POOL_WIDTH, D_MODEL)),
                  _const_spec((ATTN_OUT, D_MODEL)),
                  _const_spec((D_MODEL, D_MODEL)),
                  _const_spec((1, D_MODEL))],
        out_specs=[pl.BlockSpec((tm, D_MODEL), lambda i: (i, 0)),
                   pl.BlockSpec((tm, D_MODEL), lambda i: (i, 0))],
        out_shape=[jax.ShapeDtypeStruct((rows, D_MODEL), _F32),
                   jax.ShapeDtypeStruct((rows, D_MODEL), _BF16)],
        compiler_params=pltpu.CompilerParams(dimension_semantics=("parallel",),
                                             vmem_limit_bytes=VMEM_LIMIT),
        name="mix",
    )(xn2d, pool2d, attn2d, h2d, w_in, w_a, w_b, w_o, g_next.reshape(1, D_MODEL))


def _gelu_tanh(x):
    return 0.5 * x * (1.0 + jnp.tanh(np.sqrt(2.0 / np.pi).astype(np.float32) * (x + 0.044715 * (x * x * x))))


def _ffn_kernel(xp_ref, x_ref, xnx_ref, h_ref, wup_ref, cw_ref, cb_ref, wdn_ref, g_ref, *rest, final, n_cast):
    n_out = 1 if final else 2
    cast_src, out_refs, rest = rest[:n_cast], rest[n_cast:n_cast + n_out], rest[n_cast + n_out:]
    cast_dst, a_refs = rest[:n_cast], rest[n_cast:]
    _cast_blocks(cast_src, cast_dst)
    tm = x_ref.shape[0]
    tiles_per_seq = SEQ // tm
    t = pl.program_id(0) % tiles_per_seq
    x = x_ref[...]
    x_prev = jnp.where(t > 0, xp_ref[...], jnp.zeros_like(xp_ref))
    x_next = jnp.where(t < tiles_per_seq - 1, xnx_ref[...], jnp.zeros_like(xnx_ref))
    xe = jnp.concatenate([x_prev, x, x_next], axis=0)
    acc = None
    assert sum(FF_CHUNKS) == D_FF
    for c, width in enumerate(FF_CHUNKS):
        c0 = sum(FF_CHUNKS[:c])
        cols = slice(c0, c0 + width)
        gcols = slice(D_FF + c0, D_FF + c0 + width)
        a_ext = jnp.dot(xe, wup_ref[:, cols], preferred_element_type=_F32)
        gate = jnp.dot(x, wup_ref[:, gcols], preferred_element_type=_F32)
        a_mid = a_ext[HALO:HALO + tm]
        a_ref = a_refs[c]
        for j in range(width // LANES):
            a_ref[j] = a_ext[:, j * LANES:(j + 1) * LANES]
        a_prev = jnp.concatenate([a_ref[j, pl.ds(HALO - 1, tm, stride=1), :] for j in range(width // LANES)], axis=1)
        a_next = jnp.concatenate([a_ref[j, pl.ds(HALO + 1, tm, stride=1), :] for j in range(width // LANES)], axis=1)
        cw = cw_ref[:, cols]
        conv = a_prev * cw[0:1] + a_mid * cw[1:2] + a_next * cw[2:3] + cb_ref[:, cols]
        act = (_gelu_tanh(conv) * gate).astype(_BF16)
        part = jnp.dot(act, wdn_ref[cols, :], preferred_element_type=_F32)
        acc = part if acc is None else acc + part
    h = h_ref[...] + acc
    if final:
        out_refs[0][...] = _rms(h, g_ref[...])
    else:
        out_refs[0][...] = h
        out_refs[1][...] = _rms(h, g_ref[...]).astype(out_refs[1].dtype)


def _ffn_call(xn2d, h2d, w_up, conv_w, conv_b, w_down, g_next, final, to_cast=(), cast_layer=0):
    rows = xn2d.shape[0]
    tm = ROW_TILE
    per = tm // HALO
    last = rows // HALO - 1
    row_spec = pl.BlockSpec((tm, D_MODEL), lambda i: (i, 0))
    if final:
        out_specs = [row_spec]
        out_shape = [jax.ShapeDtypeStruct((rows, D_MODEL), _F32)]
    else:
        out_specs = [row_spec, row_spec]
        out_shape = [jax.ShapeDtypeStruct((rows, D_MODEL), _F32),
                     jax.ShapeDtypeStruct((rows, D_MODEL), _BF16)]
    riders = [_cast_rider(w, cast_layer, rows // tm, lambda i: i) for w in to_cast]
    outs = pl.pallas_call(
        functools.partial(_ffn_kernel, final=final, n_cast=len(riders)),
        grid=(rows // tm,),
        in_specs=[pl.BlockSpec((HALO, D_MODEL), lambda i: (jnp.maximum(i * per - 1, 0), 0)),
                  row_spec,
                  pl.BlockSpec((HALO, D_MODEL), lambda i: (jnp.minimum((i + 1) * per, last), 0)),
                  row_spec,
                  _const_spec((D_MODEL, 2 * D_FF)),
                  _const_spec((3, D_FF)),
                  _const_spec((1, D_FF)),
                  _const_spec((D_FF, D_MODEL)),
                  _const_spec((1, D_MODEL))] + [r[0] for r in riders],
        out_specs=out_specs + [r[1] for r in riders],
        out_shape=out_shape + [r[2] for r in riders],
        scratch_shapes=[pltpu.VMEM((width // LANES, tm + 2 * HALO, LANES), _F32) for width in FF_CHUNKS],
        compiler_params=pltpu.CompilerParams(dimension_semantics=("parallel",),
                                             vmem_limit_bytes=VMEM_LIMIT),
        name="ffn",
    )(xn2d, xn2d, xn2d, h2d, w_up, conv_w, conv_b.reshape(1, D_FF), w_down, g_next.reshape(1, D_MODEL), *to_cast)
    return outs[:len(out_shape)], outs[len(out_shape):]


def kernel(x, w_in, w_pool, pool_scale, w_a, w_b, w_o, norm1, norm2, w_up, conv_w, conv_b, w_down, rel_bias, norm_f):
    batch, seq, d = x.shape
    assert (seq, d) == (SEQ, D_MODEL)
    depth = w_in.shape[0]
    rows = batch * seq
    bias_rows = _bias_rows(rel_bias)
    h = x.reshape(rows, d)
    xn = None
    w_pool_bf = w_pool.astype(_BF16)
    w_in_bf = w_in[0].astype(_BF16)
    for layer in range(depth):
        if xn is None:
            qkv0, xn3 = _qkv_call(x, w_in_bf, 0, norm_gain=norm1[layer])
            xn = xn3.reshape(rows, d)
        else:
            xn3 = xn.reshape(batch, seq, d)
            qkv0 = _qkv_call(xn3, w_in_bf, 0)
        pool = _pool_call(xn3, w_in_bf, w_pool_bf, pool_scale[layer], layer)
        qkv = [qkv0] + [_qkv_call(xn3, w_in_bf, gi) for gi in (1, 2)]
        attn, (w_a_bf, w_b_bf, w_o_bf, w_up_bf, w_down_bf) = _attn_call(qkv, bias_rows, (w_a, w_b, w_o, w_up, w_down), layer)
        h, xn = _mix_call(xn, pool.reshape(rows, POOL_WIDTH), attn.reshape(rows, ATTN_OUT), h,
                          w_in_bf, w_a_bf, w_b_bf, w_o_bf, norm2[layer])
        final = layer == depth - 1
        g_next = norm_f if final else norm1[layer + 1]
        outs, cast = _ffn_call(xn, h, w_up_bf, conv_w[layer], conv_b[layer], w_down_bf, g_next, final,
                               to_cast=() if final else (w_in,), cast_layer=layer + 1)
        if final:
            return outs[0].reshape(batch, seq, d)
        h, xn = outs
        w_in_bf, = cast
```

```python
import functools

import numpy as np
import jax
import jax.numpy as jnp
from jax import lax
from jax.experimental import pallas as pl
from jax.experimental.pallas import tpu as pltpu

D_MODEL = 1024
SEQ = 2048
POOL_WINDOWS = (2, 4, 8, 16)
POOL_GROUP_DIM = 256
POOL_WIDTH = 1024
ATTN_DILATIONS = (1, 4, 16)
HALF_WINDOW = 64
HEADS_PER_GROUP = 4
N_HEADS = 12
HEAD_DIM = 128
ATTN_WIDTH = N_HEADS * HEAD_DIM
ATTN_OUT = HEADS_PER_GROUP * HEAD_DIM
NEG_INF = -1e30
N_BUCKETS = 32
MAX_DISTANCE = 1024
D_FF = 2816
EPS = 1e-6
PROJ_WIDTH = POOL_WIDTH + 3 * ATTN_WIDTH
GATE_WIDTH = 2 * D_MODEL
IN_WIDTH = PROJ_WIDTH + GATE_WIDTH
GATE_BLOCK = 2560

Q_TILE = 128
UNITS_PER_STEP = 8
LOG2_E = float(np.log2(np.e))
RESIDUE_PITCH = Q_TILE + 4
NORM_ROWS = 256
PROJ_TN = 512
ROW_TILE = 512
BF16_TILE_ROWS = 16
HALO = BF16_TILE_ROWS
FF_CHUNKS = (1536, 1280)
VMEM_LIMIT = 56 * 1024 * 1024
LANES = 128
REGROUP_TILE = 256

_F32 = jnp.float32
_BF16 = jnp.bfloat16


def _rms(x, g):
    return x * lax.rsqrt(jnp.mean(x * x, axis=-1, keepdims=True) + EPS) * g


def _const_spec(shape, index=None):
    index = tuple(index) if index is not None else (0,) * len(shape)
    return pl.BlockSpec(tuple(shape), lambda *_: index, pipeline_mode=pl.Buffered(1))


def _layer_spec(layer, shape, index=None):
    index = tuple(index) if index is not None else (0,) * len(shape)
    return pl.BlockSpec((None,) + tuple(shape), lambda *_: (layer,) + index, pipeline_mode=pl.Buffered(1))


def _cast_rider(param, layer, n_steps, step_of):
    _, rows, cols = param.shape
    n_blocks = max(n for n in range(1, n_steps + 1) if rows % n == 0 and (rows // n) % BF16_TILE_ROWS == 0)
    rb = rows // n_blocks

    def block(*g):
        return step_of(*g) * n_blocks // n_steps

    return (pl.BlockSpec((None, rb, cols), lambda *g: (layer, block(*g), 0)),
            pl.BlockSpec((rb, cols), lambda *g: (block(*g), 0)),
            jax.ShapeDtypeStruct((rows, cols), _BF16))


def _cast_blocks(src_refs, dst_refs):
    for src, dst in zip(src_refs, dst_refs, strict=True):
        dst[...] = src[...].astype(dst.dtype)


POOL_PAD = 16
POOL_EDGE = 8
assert POOL_EDGE >= max(POOL_WINDOWS) // 2 and POOL_PAD - POOL_EDGE >= max(POOL_WINDOWS) // 4


def _pool_kernel(xn_ref, w_ref, wpool_ref, pscale_ref, o_ref, *scratch):
    *u_refs, t_ref = scratch
    n = SEQ + 2 * POOL_PAD
    slabs = POOL_GROUP_DIM // LANES

    def shifted(view, start, rows, k_back, k_fwd):
        return view[pl.ds(start - k_back, rows, stride=1), :] + view[pl.ds(start + k_fwd, rows, stride=1), :]

    def steps(w):
        return (1, 0) if w == 1 else (w // 2, w // 2)

    zeros = jnp.zeros((POOL_PAD, LANES), _F32)
    for t in range(2):
        for s in range(slabs):
            t_ref[t, s, 0:POOL_EDGE, :] = zeros[:POOL_EDGE]
            t_ref[t, s, n - POOL_EDGE:n, :] = zeros[:POOL_EDGE]
    xn = xn_ref[...]

    def project(gi):
        u = jnp.dot(xn, w_ref[:, gi * POOL_GROUP_DIM:(gi + 1) * POOL_GROUP_DIM], preferred_element_type=_F32)
        for s in range(slabs):
            u_refs[gi][s, 0:POOL_PAD, :] = zeros
            u_refs[gi][s, n - POOL_PAD:n, :] = zeros
            u_refs[gi][s, POOL_PAD:POOL_PAD + SEQ, :] = u[:, s * LANES:(s + 1) * LANES]

    edge_row = lax.broadcasted_iota(jnp.int32, (POOL_PAD, POOL_GROUP_DIM), 0)

    def pool(gi):
        window = POOL_WINDOWS[gi]
        u_ref = u_refs[gi]
        cols = slice(gi * POOL_GROUP_DIM, (gi + 1) * POOL_GROUP_DIM)
        totals = []
        for s in range(slabs):
            src, w, slot = u_ref.at[s], 1, 0
            while 2 * w < window:
                t_ref[slot, s, pl.ds(POOL_EDGE, n - 2 * POOL_EDGE), :] = shifted(src, POOL_EDGE, n - 2 * POOL_EDGE,
                                                                              *steps(w))
                src, w, slot = t_ref.at[slot, s], 2 * w, 1 - slot
            totals.append(shifted(src, POOL_PAD, SEQ, *steps(w)))
        total = jnp.concatenate(totals, axis=1)
        u = jnp.concatenate([u_ref[s, POOL_PAD:POOL_PAD + SEQ, :] for s in range(slabs)], axis=1)
        wpool = wpool_ref[gi]
        scale = pscale_ref[:, cols]
        pooled = total * (1.0 / window) - u
        z = jnp.dot(pooled.astype(_BF16), wpool, preferred_element_type=_F32)
        o_ref[:, cols] = (z * scale).astype(o_ref.dtype)
        for r0 in (0, SEQ - POOL_PAD):
            pos = edge_row + r0
            size = (jnp.minimum(pos + window // 2, SEQ) - jnp.maximum(pos - window // 2, 0)).astype(_F32)
            pooled = total[r0:r0 + POOL_PAD] / size - u[r0:r0 + POOL_PAD]
            z = jnp.dot(pooled.astype(_BF16), wpool, preferred_element_type=_F32)
            o_ref[r0:r0 + POOL_PAD, cols] = (z * scale).astype(o_ref.dtype)

    n_groups = len(POOL_WINDOWS)
    project(0)
    for gi in range(n_groups):
        if gi + 1 < n_groups:
            project(gi + 1)
        pool(gi)


def _pool_call(xn3, w_in, w_pool, pool_scale, layer):
    batch = xn3.shape[0]
    return pl.pallas_call(
        _pool_kernel,
        grid=(batch,),
        in_specs=[pl.BlockSpec((None, SEQ, D_MODEL), lambda b: (b, 0, 0)),
                  _const_spec((D_MODEL, POOL_WIDTH)),
                  _layer_spec(layer, (len(POOL_WINDOWS), POOL_GROUP_DIM, POOL_GROUP_DIM)),
                  _const_spec((1, POOL_WIDTH))],
        out_specs=pl.BlockSpec((None, SEQ, POOL_WIDTH), lambda b: (b, 0, 0)),
        out_shape=jax.ShapeDtypeStruct((batch, SEQ, POOL_WIDTH), _BF16),
        scratch_shapes=[pltpu.VMEM((POOL_GROUP_DIM // LANES, SEQ + 2 * POOL_PAD, LANES), _F32) for _ in POOL_WINDOWS]
        + [pltpu.VMEM((2, POOL_GROUP_DIM // LANES, SEQ + 2 * POOL_PAD, LANES), _F32)],
        compiler_params=pltpu.CompilerParams(dimension_semantics=("parallel",),
                                             vmem_limit_bytes=VMEM_LIMIT),
        name="pool",
    )(xn3, w_in, w_pool, pool_scale.reshape(1, POOL_WIDTH))


def _qkv_kernel(x_ref, *refs, dil, norm_input):
    if norm_input:
        g_ref, wq_ref, wk_ref, wv_ref, o_ref, xn_ref, *scratch = refs
        for r0 in range(0, SEQ, NORM_ROWS):
            xn_ref[r0:r0 + NORM_ROWS, :] = _rms(x_ref[r0:r0 + NORM_ROWS, :], g_ref[...]).astype(xn_ref.dtype)
        x_ref = xn_ref
    else:
        wq_ref, wk_ref, wv_ref, o_ref, *scratch = refs
    if dil == 1:
        x = x_ref[...]
    else:
        xp_ref, = scratch
        sub = SEQ // dil
        sub_t = REGROUP_TILE // dil
        i = lax.broadcasted_iota(jnp.int32, (REGROUP_TILE, REGROUP_TILE), 0)
        j = lax.broadcasted_iota(jnp.int32, (REGROUP_TILE, REGROUP_TILE), 1)
        pick = (j == (i % sub_t) * dil + i // sub_t).astype(_BF16)
        for t in range(SEQ // REGROUP_TILE):
            tile = x_ref[t * REGROUP_TILE:(t + 1) * REGROUP_TILE, :]
            srt = jnp.dot(pick, tile, preferred_element_type=_F32).astype(_BF16)
            for r in range(dil):
                xp_ref[r * sub + t * sub_t:r * sub + (t + 1) * sub_t, :] = srt[r * sub_t:(r + 1) * sub_t]
        x = xp_ref[...]
    for which, w_ref in enumerate((wq_ref, wk_ref, wv_ref)):
        y = jnp.dot(x, w_ref[...], preferred_element_type=_F32)
        if which == 0:
            y = y * HEAD_DIM ** -0.5
        for hd in range(HEADS_PER_GROUP):
            o_ref[which * HEADS_PER_GROUP + hd] = y[:, hd * HEAD_DIM:(hd + 1) * HEAD_DIM].astype(o_ref.dtype)


def _qkv_call(x3, w_in, gi, norm_gain=None):
    batch = x3.shape[0]
    dil = ATTN_DILATIONS[gi]
    first = POOL_WIDTH // PROJ_TN + gi
    norm_input = norm_gain is not None
    x_spec = pl.BlockSpec((None, SEQ, D_MODEL), lambda b: (b, 0, 0))
    in_specs = [x_spec] + ([_const_spec((1, D_MODEL))] if norm_input else [])
    in_specs += [_const_spec((D_MODEL, PROJ_TN), (0, first + 3 * which)) for which in range(3)]
    out_specs = [pl.BlockSpec((None, 3 * HEADS_PER_GROUP, SEQ, HEAD_DIM), lambda b: (b, 0, 0, 0))]
    out_shape = [jax.ShapeDtypeStruct((batch, 3 * HEADS_PER_GROUP, SEQ, HEAD_DIM), _BF16)]
    if norm_input:
        out_specs.append(x_spec)
        out_shape.append(jax.ShapeDtypeStruct((batch, SEQ, D_MODEL), _BF16))
    args = (x3,) + ((norm_gain.reshape(1, D_MODEL),) if norm_input else ()) + (w_in, w_in, w_in)
    outs = pl.pallas_call(
        functools.partial(_qkv_kernel, dil=dil, norm_input=norm_input),
        grid=(batch,),
        in_specs=in_specs,
        out_specs=out_specs,
        out_shape=out_shape,
        scratch_shapes=[] if dil == 1 else [pltpu.VMEM((SEQ, D_MODEL), _BF16)],
        compiler_params=pltpu.CompilerParams(dimension_semantics=("parallel",),
                                             vmem_limit_bytes=VMEM_LIMIT),
        name=f"qkv{gi}",
    )(*args)
    return outs if norm_input else outs[0]


def _t5_buckets_np(rel):
    n = -rel
    half = N_BUCKETS // 2
    ret = (n < 0).astype(np.int32) * half
    n = np.abs(n)
    max_exact = half // 2
    large = max_exact + (np.log(np.maximum(n, 1) / max_exact)
                         / np.log(MAX_DISTANCE / max_exact) * (half - max_exact)).astype(np.int32)
    large = np.minimum(large, half - 1)
    return (ret + np.where(n < max_exact, n, large)).astype(np.int32)


def _key_span(dil):
    return min(Q_TILE + 2 * HALF_WINDOW, SEQ // dil)


BIAS_LANES = 512


def _bias_rows(rel_bias):
    rows = []
    for gi, dil in enumerate(ATTN_DILATIONS):
        buckets = _t5_buckets_np(dil * np.arange(-HALF_WINDOW, HALF_WINDOW + 1))
        bias = rel_bias[buckets][:, gi * HEADS_PER_GROUP:(gi + 1) * HEADS_PER_GROUP].T.astype(_F32)
        rows.append(jnp.pad(bias, ((0, 0), (0, BIAS_LANES - bias.shape[1])), constant_values=NEG_INF))
    return jnp.stack(rows, axis=1)


def _tile_deltas(dil):
    return (0, HALF_WINDOW, 2 * HALF_WINDOW) if SEQ // dil > Q_TILE else (0,)


def _attn_kernel(q0_ref, k0_ref, v0_ref, q1_ref, k1_ref, v1_ref, q2_ref, k2_ref, v2_ref, e_ref, *rest, n_cast):
    cast_src, (o_ref, *rest) = rest[:n_cast], rest[n_cast:]
    cast_dst, (t0_ref, t1_ref, t2_ref, o_acc_ref, lse_ref, far_o_ref, far_lse_ref) = rest[:n_cast], rest[n_cast:]
    _cast_blocks(cast_src, cast_dst)
    groups = ((q0_ref, k0_ref, v0_ref, t0_ref), (q1_ref, k1_ref, v1_ref, t1_ref), (q2_ref, k2_ref, v2_ref, t2_ref))

    @pl.when(pl.program_id(1) == 0)
    def _():
        for gi, (_, _, _, t_ref) in enumerate(groups):
            dil = ATTN_DILATIONS[gi]
            row = jnp.broadcast_to(e_ref[gi:gi + 1, :], (Q_TILE, BIAS_LANES))
            for vi, delta in enumerate(_tile_deltas(dil)):
                skew = pltpu.roll(row, (delta - HALF_WINDOW) % BIAS_LANES, axis=1, stride=1, stride_axis=0)
                t_ref[vi] = skew[:, :_key_span(dil)]

    def scores(u, gi):
        q_ref, k_ref, _, t_ref = groups[gi]
        dil = ATTN_DILATIONS[gi]
        sub = SEQ // dil
        span = _key_span(dil)
        blocks = sub // Q_TILE
        r = u // blocks
        m0 = (u % blocks) * Q_TILE
        start = jnp.clip(m0 - HALF_WINDOW, 0, sub - span)
        variant = (m0 - start) // HALF_WINDOW
        q = q_ref[pl.ds(pl.multiple_of(u * Q_TILE, Q_TILE), Q_TILE), :]
        k0 = pl.multiple_of(r * sub + start, HALF_WINDOW)
        k = k_ref[pl.ds(k0, span), :]
        s = lax.dot_general(q, k, (((1,), (1,)), ((), ())), preferred_element_type=_F32)
        if dil == 1:
            dest = (gi, pl.ds(pl.multiple_of(m0, Q_TILE), Q_TILE))
        elif blocks > 1:
            dest = (gi, pl.ds(m0 * dil + r, Q_TILE, stride=dil))
        else:
            dest = (None, pl.ds(r * RESIDUE_PITCH, Q_TILE, stride=1))
        return s + t_ref[variant], k0, dest

    def weights(s):
        m = jnp.max(s, axis=-1, keepdims=True)
        return m, jnp.exp((s - m).astype(_BF16))

    ones = jnp.ones((Q_TILE + 2 * HALF_WINDOW, HEAD_DIM), _BF16)

    def values(gi, k0, dest, m, p):
        span = p.shape[1]
        v = groups[gi][2][pl.ds(k0, span), :]
        ov = jnp.dot(p, jnp.concatenate([v, ones[:span]], axis=1), preferred_element_type=_F32)
        l = ov[:, HEAD_DIM:]
        lse2 = m * LOG2_E + jnp.log2(l)
        slot, rows = dest
        if slot is None:
            far_o_ref[rows, :] = ov[:, :HEAD_DIM] / l
            far_lse_ref[rows, :] = lse2
        else:
            o_acc_ref[slot, rows, :] = ov[:, :HEAD_DIM] / l
            lse_ref[slot, rows, :] = lse2

    def units(it, carry):
        todo = [(it * UNITS_PER_STEP + j, gi) for j in range(UNITS_PER_STEP) for gi in range(3)]
        scored = [scores(u, gi) for u, gi in todo]
        soft = [weights(s) for s, _, _ in scored]
        for (_, gi), (_, k0, dest), (m, p) in zip(todo, scored, soft):
            values(gi, k0, dest, m, p)
        return carry

    lax.fori_loop(0, SEQ // Q_TILE // UNITS_PER_STEP, units, 0)

    far_dil = ATTN_DILATIONS[2]

    def merge(t, carry):
        rows = pl.ds(pl.multiple_of(t * Q_TILE, Q_TILE), Q_TILE)

        def far(ref):
            return jnp.concatenate([ref[pl.ds(t * (Q_TILE // far_dil) + j, far_dil, stride=RESIDUE_PITCH), :]
                                    for j in range(Q_TILE // far_dil)], axis=0)

        lses = [lse_ref[0, rows, :], lse_ref[1, rows, :], far(far_lse_ref)]
        outs = [o_acc_ref[0, rows, :], o_acc_ref[1, rows, :], far(far_o_ref)]
        top = jnp.maximum(jnp.maximum(lses[0], lses[1]), lses[2])
        ws = [jnp.exp2(x - top) for x in lses]
        num = ws[0] * outs[0] + ws[1] * outs[1] + ws[2] * outs[2]
        o_ref[rows, :] = (num / (ws[0] + ws[1] + ws[2])).astype(o_ref.dtype)
        return carry

    lax.fori_loop(0, SEQ // Q_TILE, merge, 0, unroll=4)


def _attn_call(qkv, bias_rows, to_cast, layer):
    batch = qkv[0].shape[0]

    def head_spec(which):
        return pl.BlockSpec((None, None, SEQ, HEAD_DIM), lambda h, b: (b, which * HEADS_PER_GROUP + h, 0, 0))

    in_specs = [head_spec(which) for gi in range(3) for which in range(3)]
    in_specs.append(pl.BlockSpec((None, 3, BIAS_LANES), lambda h, b: (h, 0, 0)))
    riders = [_cast_rider(w, layer, HEADS_PER_GROUP * batch, lambda h, b: h * batch + b) for w in to_cast]
    in_specs += [r[0] for r in riders]
    scratch = [pltpu.VMEM((len(_tile_deltas(dil)), Q_TILE, _key_span(dil)), _F32) for dil in ATTN_DILATIONS]
    assert SEQ // ATTN_DILATIONS[2] == Q_TILE and all(SEQ // d > Q_TILE for d in ATTN_DILATIONS[:2])
    scratch += [pltpu.VMEM((2, SEQ, HEAD_DIM), _F32) for _ in range(2)]
    scratch += [pltpu.VMEM((ATTN_DILATIONS[2] * RESIDUE_PITCH, HEAD_DIM), _F32) for _ in range(2)]
    attn, *cast = pl.pallas_call(
        functools.partial(_attn_kernel, n_cast=len(riders)),
        grid=(HEADS_PER_GROUP, batch),
        in_specs=in_specs,
        out_specs=[pl.BlockSpec((None, SEQ, HEAD_DIM), lambda h, b: (b, 0, h))] + [r[1] for r in riders],
        out_shape=[jax.ShapeDtypeStruct((batch, SEQ, ATTN_OUT), _BF16)] + [r[2] for r in riders],
        scratch_shapes=scratch,
        compiler_params=pltpu.CompilerParams(dimension_semantics=("arbitrary", "arbitrary"),
                                             vmem_limit_bytes=VMEM_LIMIT),
        name="attn",
    )(*[qkv[gi] for gi in range(3) for _ in range(3)], bias_rows, *to_cast)
    return attn, cast


def _mix_kernel(xn_ref, pool_ref, attn_ref, h_ref, wg_ref, wa_ref, wb_ref, wo_ref, g_ref, h_out_ref, xn_out_ref):
    xn = xn_ref[...]
    gates = jnp.dot(xn, wg_ref[:, GATE_BLOCK - GATE_WIDTH:], preferred_element_type=_F32)
    a = jnp.dot(pool_ref[...], wa_ref[...], preferred_element_type=_F32)
    b = jnp.dot(attn_ref[...], wb_ref[...], preferred_element_type=_F32)
    merged = jax.nn.sigmoid(gates[:, :D_MODEL]) * a + jax.nn.sigmoid(gates[:, D_MODEL:]) * b
    h = h_ref[...] + jnp.dot(merged.astype(_BF16), wo_ref[...], preferred_element_type=_F32)
    h_out_ref[...] = h
    xn_out_ref[...] = _rms(h, g_ref[...]).astype(xn_out_ref.dtype)


def _mix_call(xn2d, pool2d, attn2d, h2d, w_in, w_a, w_b, w_o, g_next):
    rows = xn2d.shape[0]
    tm = ROW_TILE
    assert IN_WIDTH % GATE_BLOCK == 0 and GATE_BLOCK >= GATE_WIDTH
    return pl.pallas_call(
        _mix_kernel,
        grid=(rows // tm,),
        in_specs=[pl.BlockSpec((tm, D_MODEL), lambda i: (i, 0)),
                  pl.BlockSpec((tm, POOL_WIDTH), lambda i: (i, 0)),
                  pl.BlockSpec((tm, ATTN_OUT), lambda i: (i, 0)),
                  pl.BlockSpec((tm, D_MODEL), lambda i: (i, 0)),
                  _const_spec((D_MODEL, GATE_BLOCK), (0, IN_WIDTH // GATE_BLOCK - 1)),
                  _const_spec((POOL_WIDTH, D_MODEL)),
                  _const_spec((ATTN_OUT, D_MODEL)),
                  _const_spec((D_MODEL, D_MODEL)),
                  _const_spec((1, D_MODEL))],
        out_specs=[pl.BlockSpec((tm, D_MODEL), lambda i: (i, 0)),
                   pl.BlockSpec((tm, D_MODEL), lambda i: (i, 0))],
        out_shape=[jax.ShapeDtypeStruct((rows, D_MODEL), _F32),
                   jax.ShapeDtypeStruct((rows, D_MODEL), _BF16)],
        compiler_params=pltpu.CompilerParams(dimension_semantics=("parallel",),
                                             vmem_limit_bytes=VMEM_LIMIT),
        name="mix",
    )(xn2d, pool2d, attn2d, h2d, w_in, w_a, w_b, w_o, g_next.reshape(1, D_MODEL))


def _gelu_tanh(x):
    return 0.5 * x * (1.0 + jnp.tanh(np.sqrt(2.0 / np.pi).astype(np.float32) * (x + 0.044715 * (x * x * x))))


def _ffn_kernel(xp_ref, x_ref, xnx_ref, h_ref, wup_ref, cw_ref, cb_ref, wdn_ref, g_ref, *rest, final, n_cast):
    n_out = 1 if final else 2
    cast_src, out_refs, rest = rest[:n_cast], rest[n_cast:n_cast + n_out], rest[n_cast + n_out:]
    cast_dst, a_refs = rest[:n_cast], rest[n_cast:]
    _cast_blocks(cast_src, cast_dst)
    tm = x_ref.shape[0]
    tiles_per_seq = SEQ // tm
    t = pl.program_id(0) % tiles_per_seq
    x = x_ref[...]
    x_prev = jnp.where(t > 0, xp_ref[...], jnp.zeros_like(xp_ref))
    x_next = jnp.where(t < tiles_per_seq - 1, xnx_ref[...], jnp.zeros_like(xnx_ref))
    xe = jnp.concatenate([x_prev, x, x_next], axis=0)
    acc = None
    assert sum(FF_CHUNKS) == D_FF
    for c, width in enumerate(FF_CHUNKS):
        c0 = sum(FF_CHUNKS[:c])
        cols = slice(c0, c0 + width)
        gcols = slice(D_FF + c0, D_FF + c0 + width)
        a_ext = jnp.dot(xe, wup_ref[:, cols], preferred_element_type=_F32)
        gate = jnp.dot(x, wup_ref[:, gcols], preferred_element_type=_F32)
        a_mid = a_ext[HALO:HALO + tm]
        a_ref = a_refs[c]
        for j in range(width // LANES):
            a_ref[j] = a_ext[:, j * LANES:(j + 1) * LANES]
        a_prev = jnp.concatenate([a_ref[j, pl.ds(HALO - 1, tm, stride=1), :] for j in range(width // LANES)], axis=1)
        a_next = jnp.concatenate([a_ref[j, pl.ds(HALO + 1, tm, stride=1), :] for j in range(width // LANES)], axis=1)
        cw = cw_ref[:, cols]
        conv = a_prev * cw[0:1] + a_mid * cw[1:2] + a_next * cw[2:3] + cb_ref[:, cols]
        act = (_gelu_tanh(conv) * gate).astype(_BF16)
        part = jnp.dot(act, wdn_ref[cols, :], preferred_element_type=_F32)
        acc = part if acc is None else acc + part
    h = h_ref[...] + acc
    if final:
        out_refs[0][...] = _rms(h, g_ref[...])
    else:
        out_refs[0][...] = h
        out_refs[1][...] = _rms(h, g_ref[...]).astype(out_refs[1].dtype)


def _ffn_call(xn2d, h2d, w_up, conv_w, conv_b, w_down, g_next, final, to_cast=(), cast_layer=0):
    rows = xn2d.shape[0]
    tm = ROW_TILE
    per = tm // HALO
    last = rows // HALO - 1
    row_spec = pl.BlockSpec((tm, D_MODEL), lambda i: (i, 0))
    if final:
        out_specs = [row_spec]
        out_shape = [jax.ShapeDtypeStruct((rows, D_MODEL), _F32)]
    else:
        out_specs = [row_spec, row_spec]
        out_shape = [jax.ShapeDtypeStruct((rows, D_MODEL), _F32),
                     jax.ShapeDtypeStruct((rows, D_MODEL), _BF16)]
    riders = [_cast_rider(w, cast_layer, rows // tm, lambda i: i) for w in to_cast]
    outs = pl.pallas_call(
        functools.partial(_ffn_kernel, final=final, n_cast=len(riders)),
        grid=(rows // tm,),
        in_specs=[pl.BlockSpec((HALO, D_MODEL), lambda i: (jnp.maximum(i * per - 1, 0), 0)),
                  row_spec,
                  pl.BlockSpec((HALO, D_MODEL), lambda i: (jnp.minimum((i + 1) * per, last), 0)),
                  row_spec,
                  _const_spec((D_MODEL, 2 * D_FF)),
                  _const_spec((3, D_FF)),
                  _const_spec((1, D_FF)),
                  _const_spec((D_FF, D_MODEL)),
                  _const_spec((1, D_MODEL))] + [r[0] for r in riders],
        out_specs=out_specs + [r[1] for r in riders],
        out_shape=out_shape + [r[2] for r in riders],
        scratch_shapes=[pltpu.VMEM((width // LANES, tm + 2 * HALO, LANES), _F32) for width in FF_CHUNKS],
        compiler_params=pltpu.CompilerParams(dimension_semantics=("parallel",),
                                             vmem_limit_bytes=VMEM_LIMIT),
        name="ffn",
    )(xn2d, xn2d, xn2d, h2d, w_up, conv_w, conv_b.reshape(1, D_FF), w_down, g_next.reshape(1, D_MODEL), *to_cast)
    return outs[:len(out_shape)], outs[len(out_shape):]


def kernel(x, w_in, w_pool, pool_scale, w_a, w_b, w_o, norm1, norm2, w_up, conv_w, conv_b, w_down, rel_bias, norm_f):
    batch, seq, d = x.shape
    assert (seq, d) == (SEQ, D_MODEL)
    depth = w_in.shape[0]
    rows = batch * seq
    bias_rows = _bias_rows(rel_bias)
    h = x.reshape(rows, d)
    xn = None
    w_pool_bf = w_pool.astype(_BF16)
    w_in_bf = w_in[0].astype(_BF16)
    for layer in range(depth):
        if xn is None:
            qkv0, xn3 = _qkv_call(x, w_in_bf, 0, norm_gain=norm1[layer])
            xn = xn3.reshape(rows, d)
        else:
            xn3 = xn.reshape(batch, seq, d)
            qkv0 = _qkv_call(xn3, w_in_bf, 0)
        pool = _pool_call(xn3, w_in_bf, w_pool_bf, pool_scale[layer], layer)
        qkv = [qkv0] + [_qkv_call(xn3, w_in_bf, gi) for gi in (1, 2)]
        attn, (w_a_bf, w_b_bf, w_o_bf, w_up_bf, w_down_bf) = _attn_call(qkv, bias_rows, (w_a, w_b, w_o, w_up, w_down), layer)
        h, xn = _mix_call(xn, pool.reshape(rows, POOL_WIDTH), attn.reshape(rows, ATTN_OUT), h,
                          w_in_bf, w_a_bf, w_b_bf, w_o_bf, norm2[layer])
        final = layer == depth - 1
        g_next = norm_f if final else norm1[layer + 1]
        outs, cast = _ffn_call(xn, h, w_up_bf, conv_w[layer], conv_b[layer], w_down_bf, g_next, final,
                               to_cast=() if final else (w_in,), cast_layer=layer + 1)
        if final:
            return outs[0].reshape(batch, seq, d)
        h, xn = outs
        w_in_bf, = cast
```

```python
import functools

import numpy as np
import jax
import jax.numpy as jnp
from jax import lax
from jax.experimental import pallas as pl
from jax.experimental.pallas import tpu as pltpu

D_MODEL = 1024
SEQ = 2048
POOL_WINDOWS = (2, 4, 8, 16)
POOL_GROUP_DIM = 256
POOL_WIDTH = 1024
ATTN_DILATIONS = (1, 4, 16)
HALF_WINDOW = 64
HEADS_PER_GROUP = 4
N_HEADS = 12
HEAD_DIM = 128
ATTN_WIDTH = N_HEADS * HEAD_DIM
ATTN_OUT = HEADS_PER_GROUP * HEAD_DIM
NEG_INF = -1e30
N_BUCKETS = 32
MAX_DISTANCE = 1024
D_FF = 2816
EPS = 1e-6
PROJ_WIDTH = POOL_WIDTH + 3 * ATTN_WIDTH
GATE_WIDTH = 2 * D_MODEL
IN_WIDTH = PROJ_WIDTH + GATE_WIDTH
GATE_BLOCK = 2560

Q_TILE = 128
UNITS_PER_STEP = 8
LOG2_E = float(np.log2(np.e))
RESIDUE_PITCH = Q_TILE + 4
NORM_ROWS = 256
PROJ_TN = 512
ROW_TILE = 512
BF16_TILE_ROWS = 16
HALO = BF16_TILE_ROWS
FF_CHUNKS = (1536, 1280)
VMEM_LIMIT = 56 * 1024 * 1024
LANES = 128
REGROUP_TILE = 256

_F32 = jnp.float32
_BF16 = jnp.bfloat16


def _rms(x, g):
    return x * lax.rsqrt(jnp.mean(x * x, axis=-1, keepdims=True) + EPS) * g


def _const_spec(shape, index=None):
    index = tuple(index) if index is not None else (0,) * len(shape)
    return pl.BlockSpec(tuple(shape), lambda *_: index, pipeline_mode=pl.Buffered(1))


def _layer_spec(layer, shape, index=None):
    index = tuple(index) if index is not None else (0,) * len(shape)
    return pl.BlockSpec((None,) + tuple(shape), lambda *_: (layer,) + index, pipeline_mode=pl.Buffered(1))


def _cast_rider(param, layer, n_steps, step_of):
    _, rows, cols = param.shape
    n_blocks = max(n for n in range(1, n_steps + 1) if rows % n == 0 and (rows // n) % BF16_TILE_ROWS == 0)
    rb = rows // n_blocks

    def block(*g):
        return step_of(*g) * n_blocks // n_steps

    return (pl.BlockSpec((None, rb, cols), lambda *g: (layer, block(*g), 0)),
            pl.BlockSpec((rb, cols), lambda *g: (block(*g), 0)),
            jax.ShapeDtypeStruct((rows, cols), _BF16))


def _cast_blocks(src_refs, dst_refs):
    for src, dst in zip(src_refs, dst_refs, strict=True):
        dst[...] = src[...].astype(dst.dtype)


POOL_PAD = 16
POOL_EDGE = 8
assert POOL_EDGE >= max(POOL_WINDOWS) // 2 and POOL_PAD - POOL_EDGE >= max(POOL_WINDOWS) // 4


def _pool_kernel(xn_ref, w_ref, wpool_ref, pscale_ref, o_ref, *scratch):
    *u_refs, t_ref = scratch
    n = SEQ + 2 * POOL_PAD
    slabs = POOL_GROUP_DIM // LANES

    def shifted(view, start, rows, k_back, k_fwd):
        return view[pl.ds(start - k_back, rows, stride=1), :] + view[pl.ds(start + k_fwd, rows, stride=1), :]

    def steps(w):
        return (1, 0) if w == 1 else (w // 2, w // 2)

    zeros = jnp.zeros((POOL_PAD, LANES), _F32)
    for t in range(2):
        for s in range(slabs):
            t_ref[t, s, 0:POOL_EDGE, :] = zeros[:POOL_EDGE]
            t_ref[t, s, n - POOL_EDGE:n, :] = zeros[:POOL_EDGE]
    xn = xn_ref[...]

    def project(gi):
        u = jnp.dot(xn, w_ref[:, gi * POOL_GROUP_DIM:(gi + 1) * POOL_GROUP_DIM], preferred_element_type=_F32)
        for s in range(slabs):
            u_refs[gi][s, 0:POOL_PAD, :] = zeros
            u_refs[gi][s, n - POOL_PAD:n, :] = zeros
            u_refs[gi][s, POOL_PAD:POOL_PAD + SEQ, :] = u[:, s * LANES:(s + 1) * LANES]

    edge_row = lax.broadcasted_iota(jnp.int32, (POOL_PAD, POOL_GROUP_DIM), 0)

    def pool(gi):
        window = POOL_WINDOWS[gi]
        u_ref = u_refs[gi]
        cols = slice(gi * POOL_GROUP_DIM, (gi + 1) * POOL_GROUP_DIM)
        totals = []
        for s in range(slabs):
            src, w, slot = u_ref.at[s], 1, 0
            while 2 * w < window:
                t_ref[slot, s, pl.ds(POOL_EDGE, n - 2 * POOL_EDGE), :] = shifted(src, POOL_EDGE, n - 2 * POOL_EDGE,
                                                                              *steps(w))
                src, w, slot = t_ref.at[slot, s], 2 * w, 1 - slot
            totals.append(shifted(src, POOL_PAD, SEQ, *steps(w)))
        total = jnp.concatenate(totals, axis=1)
        u = jnp.concatenate([u_ref[s, POOL_PAD:POOL_PAD + SEQ, :] for s in range(slabs)], axis=1)
        wpool = wpool_ref[gi]
        scale = pscale_ref[:, cols]
        pooled = total * (1.0 / window) - u
        z = jnp.dot(pooled.astype(_BF16), wpool, preferred_element_type=_F32)
        o_ref[:, cols] = (z * scale).astype(o_ref.dtype)
        for r0 in (0, SEQ - POOL_PAD):
            pos = edge_row + r0
            size = (jnp.minimum(pos + window // 2, SEQ) - jnp.maximum(pos - window // 2, 0)).astype(_F32)
            pooled = total[r0:r0 + POOL_PAD] / size - u[r0:r0 + POOL_PAD]
            z = jnp.dot(pooled.astype(_BF16), wpool, preferred_element_type=_F32)
            o_ref[r0:r0 + POOL_PAD, cols] = (z * scale).astype(o_ref.dtype)

    n_groups = len(POOL_WINDOWS)
    project(0)
    for gi in range(n_groups):
        if gi + 1 < n_groups:
            project(gi + 1)
        pool(gi)


def _pool_call(xn3, w_in, w_pool, pool_scale, layer):
    batch = xn3.shape[0]
    return pl.pallas_call(
        _pool_kernel,
        grid=(batch,),
        in_specs=[pl.BlockSpec((None, SEQ, D_MODEL), lambda b: (b, 0, 0)),
                  _const_spec((D_MODEL, POOL_WIDTH)),
                  _layer_spec(layer, (len(POOL_WINDOWS), POOL_GROUP_DIM, POOL_GROUP_DIM)),
                  _const_spec((1, POOL_WIDTH))],
        out_specs=pl.BlockSpec((None, SEQ, POOL_WIDTH), lambda b: (b, 0, 0)),
        out_shape=jax.ShapeDtypeStruct((batch, SEQ, POOL_WIDTH), _BF16),
        scratch_shapes=[pltpu.VMEM((POOL_GROUP_DIM // LANES, SEQ + 2 * POOL_PAD, LANES), _F32) for _ in POOL_WINDOWS]
        + [pltpu.VMEM((2, POOL_GROUP_DIM // LANES, SEQ + 2 * POOL_PAD, LANES), _F32)],
        compiler_params=pltpu.CompilerParams(dimension_semantics=("parallel",),
                                             vmem_limit_bytes=VMEM_LIMIT),
        name="pool",
    )(xn3, w_in, w_pool, pool_scale.reshape(1, POOL_WIDTH))


def _qkv_kernel(x_ref, *refs, dil, norm_input):
    if norm_input:
        g_ref, wq_ref, wk_ref, wv_ref, o_ref, xn_ref, *scratch = refs
        for r0 in range(0, SEQ, NORM_ROWS):
            xn_ref[r0:r0 + NORM_ROWS, :] = _rms(x_ref[r0:r0 + NORM_ROWS, :], g_ref[...]).astype(xn_ref.dtype)
        x_ref = xn_ref
    else:
        wq_ref, wk_ref, wv_ref, o_ref, *scratch = refs
    if dil == 1:
        x = x_ref[...]
    else:
        xp_ref, = scratch
        sub = SEQ // dil
        sub_t = REGROUP_TILE // dil
        i = lax.broadcasted_iota(jnp.int32, (REGROUP_TILE, REGROUP_TILE), 0)
        j = lax.broadcasted_iota(jnp.int32, (REGROUP_TILE, REGROUP_TILE), 1)
        pick = (j == (i % sub_t) * dil + i // sub_t).astype(_BF16)
        for t in range(SEQ // REGROUP_TILE):
            tile = x_ref[t * REGROUP_TILE:(t + 1) * REGROUP_TILE, :]
            srt = jnp.dot(pick, tile, preferred_element_type=_F32).astype(_BF16)
            for r in range(dil):
                xp_ref[r * sub + t * sub_t:r * sub + (t + 1) * sub_t, :] = srt[r * sub_t:(r + 1) * sub_t]
        x = xp_ref[...]
    for which, w_ref in enumerate((wq_ref, wk_ref, wv_ref)):
        y = jnp.dot(x, w_ref[...], preferred_element_type=_F32)
        if which == 0:
            y = y * HEAD_DIM ** -0.5
        for hd in range(HEADS_PER_GROUP):
            o_ref[which * HEADS_PER_GROUP + hd] = y[:, hd * HEAD_DIM:(hd + 1) * HEAD_DIM].astype(o_ref.dtype)


def _qkv_call(x3, w_in, gi, norm_gain=None):
    batch = x3.shape[0]
    dil = ATTN_DILATIONS[gi]
    first = POOL_WIDTH // PROJ_TN + gi
    norm_input = norm_gain is not None
    x_spec = pl.BlockSpec((None, SEQ, D_MODEL), lambda b: (b, 0, 0))
    in_specs = [x_spec] + ([_const_spec((1, D_MODEL))] if norm_input else [])
    in_specs += [_const_spec((D_MODEL, PROJ_TN), (0, first + 3 * which)) for which in range(3)]
    out_specs = [pl.BlockSpec((None, 3 * HEADS_PER_GROUP, SEQ, HEAD_DIM), lambda b: (b, 0, 0, 0))]
    out_shape = [jax.ShapeDtypeStruct((batch, 3 * HEADS_PER_GROUP, SEQ, HEAD_DIM), _BF16)]
    if norm_input:
        out_specs.append(x_spec)
        out_shape.append(jax.ShapeDtypeStruct((batch, SEQ, D_MODEL), _BF16))
    args = (x3,) + ((norm_gain.reshape(1, D_MODEL),) if norm_input else ()) + (w_in, w_in, w_in)
    outs = pl.pallas_call(
        functools.partial(_qkv_kernel, dil=dil, norm_input=norm_input),
        grid=(batch,),
        in_specs=in_specs,
        out_specs=out_specs,
        out_shape=out_shape,
        scratch_shapes=[] if dil == 1 else [pltpu.VMEM((SEQ, D_MODEL), _BF16)],
        compiler_params=pltpu.CompilerParams(dimension_semantics=("parallel",),
                                             vmem_limit_bytes=VMEM_LIMIT),
        name=f"qkv{gi}",
    )(*args)
    return outs if norm_input else outs[0]


def _t5_buckets_np(rel):
    n = -rel
    half = N_BUCKETS // 2
    ret = (n < 0).astype(np.int32) * half
    n = np.abs(n)
    max_exact = half // 2
    large = max_exact + (np.log(np.maximum(n, 1) / max_exact)
                         / np.log(MAX_DISTANCE / max_exact) * (half - max_exact)).astype(np.int32)
    large = np.minimum(large, half - 1)
    return (ret + np.where(n < max_exact, n, large)).astype(np.int32)


def _key_span(dil):
    return min(Q_TILE + 2 * HALF_WINDOW, SEQ // dil)


BIAS_LANES = 512


def _bias_rows(rel_bias):
    rows = []
    for gi, dil in enumerate(ATTN_DILATIONS):
        buckets = _t5_buckets_np(dil * np.arange(-HALF_WINDOW, HALF_WINDOW + 1))
        bias = rel_bias[buckets][:, gi * HEADS_PER_GROUP:(gi + 1) * HEADS_PER_GROUP].T.astype(_F32)
        rows.append(jnp.pad(bias, ((0, 0), (0, BIAS_LANES - bias.shape[1])), constant_values=NEG_INF))
    return jnp.stack(rows, axis=1)


def _tile_deltas(dil):
    return (0, HALF_WINDOW, 2 * HALF_WINDOW) if SEQ // dil > Q_TILE else (0,)


def _attn_kernel(q0_ref, k0_ref, v0_ref, q1_ref, k1_ref, v1_ref, q2_ref, k2_ref, v2_ref, e_ref, *rest, n_cast):
    cast_src, (o_ref, *rest) = rest[:n_cast], rest[n_cast:]
    cast_dst, (t0_ref, t1_ref, t2_ref, *stats) = rest[:n_cast], rest[n_cast:]
    near_o_ref, near_m_ref, near_l_ref, far_o_ref, far_m_ref, far_l_ref = stats
    _cast_blocks(cast_src, cast_dst)
    groups = ((q0_ref, k0_ref, v0_ref, t0_ref), (q1_ref, k1_ref, v1_ref, t1_ref), (q2_ref, k2_ref, v2_ref, t2_ref))

    @pl.when(pl.program_id(1) == 0)
    def _():
        for gi, (_, _, _, t_ref) in enumerate(groups):
            dil = ATTN_DILATIONS[gi]
            row = jnp.broadcast_to(e_ref[gi:gi + 1, :], (Q_TILE, BIAS_LANES))
            for vi, delta in enumerate(_tile_deltas(dil)):
                skew = pltpu.roll(row, (delta - HALF_WINDOW) % BIAS_LANES, axis=1, stride=1, stride_axis=0)
                t_ref[vi] = skew[:, :_key_span(dil)]

    def scores(u, gi):
        q_ref, k_ref, _, t_ref = groups[gi]
        dil = ATTN_DILATIONS[gi]
        sub = SEQ // dil
        span = _key_span(dil)
        blocks = sub // Q_TILE
        r = u // blocks
        m0 = (u % blocks) * Q_TILE
        start = jnp.clip(m0 - HALF_WINDOW, 0, sub - span)
        variant = (m0 - start) // HALF_WINDOW
        q = q_ref[pl.ds(pl.multiple_of(u * Q_TILE, Q_TILE), Q_TILE), :]
        k0 = pl.multiple_of(r * sub + start, HALF_WINDOW)
        k = k_ref[pl.ds(k0, span), :]
        s = lax.dot_general(q, k, (((1,), (1,)), ((), ())), preferred_element_type=_F32)
        if dil == 1:
            dest = (gi, pl.ds(pl.multiple_of(m0, Q_TILE), Q_TILE))
        elif blocks > 1:
            dest = (gi, pl.ds(m0 * dil + r, Q_TILE, stride=dil))
        else:
            dest = (None, pl.ds(r * RESIDUE_PITCH, Q_TILE, stride=1))
        return s + t_ref[variant], k0, dest

    def weights(s):
        m = jnp.max(s, axis=-1, keepdims=True)
        return m, jnp.exp((s - m).astype(_BF16))

    ones = jnp.ones((Q_TILE + 2 * HALF_WINDOW, HEAD_DIM), _BF16)

    def values(gi, k0, dest, m, p):
        span = p.shape[1]
        v = groups[gi][2][pl.ds(k0, span), :]
        ov = jnp.dot(p, jnp.concatenate([v, ones[:span]], axis=1), preferred_element_type=_F32)
        m2 = jnp.broadcast_to(m * LOG2_E, (Q_TILE, HEAD_DIM))
        slot, rows = dest
        if slot is None:
            far_o_ref[rows, :] = ov[:, :HEAD_DIM]
            far_m_ref[rows, :] = m2
            far_l_ref[rows, :] = ov[:, HEAD_DIM:]
        else:
            near_o_ref[slot, rows, :] = ov[:, :HEAD_DIM]
            near_m_ref[slot, rows, :] = m2
            near_l_ref[slot, rows, :] = ov[:, HEAD_DIM:]

    def units(it, carry):
        todo = [(it * UNITS_PER_STEP + j, gi) for j in range(UNITS_PER_STEP) for gi in range(3)]
        scored = [scores(u, gi) for u, gi in todo]
        soft = [weights(s) for s, _, _ in scored]
        for (_, gi), (_, k0, dest), (m, p) in zip(todo, scored, soft):
            values(gi, k0, dest, m, p)
        return carry

    lax.fori_loop(0, SEQ // Q_TILE // UNITS_PER_STEP, units, 0)

    far_dil = ATTN_DILATIONS[2]

    def merge(t, carry):
        rows = pl.ds(pl.multiple_of(t * Q_TILE, Q_TILE), Q_TILE)

        def far(ref):
            return jnp.concatenate([ref[pl.ds(t * (Q_TILE // far_dil) + j, far_dil, stride=RESIDUE_PITCH), :]
                                    for j in range(Q_TILE // far_dil)], axis=0)

        ms = [near_m_ref[0, rows, :], near_m_ref[1, rows, :], far(far_m_ref)]
        outs = [near_o_ref[0, rows, :], near_o_ref[1, rows, :], far(far_o_ref)]
        ls = [near_l_ref[0, rows, :], near_l_ref[1, rows, :], far(far_l_ref)]
        top = jnp.maximum(jnp.maximum(ms[0], ms[1]), ms[2])
        ws = [jnp.exp2(x - top) for x in ms]
        num = ws[0] * outs[0] + ws[1] * outs[1] + ws[2] * outs[2]
        den = ws[0] * ls[0] + ws[1] * ls[1] + ws[2] * ls[2]
        o_ref[rows, :] = (num / den).astype(o_ref.dtype)
        return carry

    lax.fori_loop(0, SEQ // Q_TILE, merge, 0, unroll=4)


def _attn_call(qkv, bias_rows, to_cast, layer):
    batch = qkv[0].shape[0]

    def head_spec(which):
        return pl.BlockSpec((None, None, SEQ, HEAD_DIM), lambda h, b: (b, which * HEADS_PER_GROUP + h, 0, 0))

    in_specs = [head_spec(which) for gi in range(3) for which in range(3)]
    in_specs.append(pl.BlockSpec((None, 3, BIAS_LANES), lambda h, b: (h, 0, 0)))
    riders = [_cast_rider(w, layer, HEADS_PER_GROUP * batch, lambda h, b: h * batch + b) for w in to_cast]
    in_specs += [r[0] for r in riders]
    scratch = [pltpu.VMEM((len(_tile_deltas(dil)), Q_TILE, _key_span(dil)), _F32) for dil in ATTN_DILATIONS]
    assert SEQ // ATTN_DILATIONS[2] == Q_TILE and all(SEQ // d > Q_TILE for d in ATTN_DILATIONS[:2])
    scratch += [pltpu.VMEM((2, SEQ, HEAD_DIM), _F32) for _ in range(3)]
    scratch += [pltpu.VMEM((ATTN_DILATIONS[2] * RESIDUE_PITCH, HEAD_DIM), _F32) for _ in range(3)]
    attn, *cast = pl.pallas_call(
        functools.partial(_attn_kernel, n_cast=len(riders)),
        grid=(HEADS_PER_GROUP, batch),
        in_specs=in_specs,
        out_specs=[pl.BlockSpec((None, SEQ, HEAD_DIM), lambda h, b: (b, 0, h))] + [r[1] for r in riders],
        out_shape=[jax.ShapeDtypeStruct((batch, SEQ, ATTN_OUT), _BF16)] + [r[2] for r in riders],
        scratch_shapes=scratch,
        compiler_params=pltpu.CompilerParams(dimension_semantics=("arbitrary", "arbitrary"),
                                             vmem_limit_bytes=VMEM_LIMIT),
        name="attn",
    )(*[qkv[gi] for gi in range(3) for _ in range(3)], bias_rows, *to_cast)
    return attn, cast


def _mix_kernel(xn_ref, pool_ref, attn_ref, h_ref, wg_ref, wa_ref, wb_ref, wo_ref, g_ref, h_out_ref, xn_out_ref):
    xn = xn_ref[...]
    gates = jnp.dot(xn, wg_ref[:, GATE_BLOCK - GATE_WIDTH:], preferred_element_type=_F32)
    a = jnp.dot(pool_ref[...], wa_ref[...], preferred_element_type=_F32)
    b = jnp.dot(attn_ref[...], wb_ref[...], preferred_element_type=_F32)
    merged = jax.nn.sigmoid(gates[:, :D_MODEL]) * a + jax.nn.sigmoid(gates[:, D_MODEL:]) * b
    h = h_ref[...] + jnp.dot(merged.astype(_BF16), wo_ref[...], preferred_element_type=_F32)
    h_out_ref[...] = h
    xn_out_ref[...] = _rms(h, g_ref[...]).astype(xn_out_ref.dtype)


def _mix_call(xn2d, pool2d, attn2d, h2d, w_in, w_a, w_b, w_o, g_next):
    rows = xn2d.shape[0]
    tm = ROW_TILE
    assert IN_WIDTH % GATE_BLOCK == 0 and GATE_BLOCK >= GATE_WIDTH
    return pl.pallas_call(
        _mix_kernel,
        grid=(rows // tm,),
        in_specs=[pl.BlockSpec((tm, D_MODEL), lambda i: (i, 0)),
                  pl.BlockSpec((tm, POOL_WIDTH), lambda i: (i, 0)),
                  pl.BlockSpec((tm, ATTN_OUT), lambda i: (i, 0)),
                  pl.BlockSpec((tm, D_MODEL), lambda i: (i, 0)),
                  _const_spec((D_MODEL, GATE_BLOCK), (0, IN_WIDTH // GATE_BLOCK - 1)),
                  _const_spec((POOL_WIDTH, D_MODEL)),
                  _const_spec((ATTN_OUT, D_MODEL)),
                  _const_spec((D_MODEL, D_MODEL)),
                  _const_spec((1, D_MODEL))],
        out_specs=[pl.BlockSpec((tm, D_MODEL), lambda i: (i, 0)),
                   pl.BlockSpec((tm, D_MODEL), lambda i: (i, 0))],
        out_shape=[jax.ShapeDtypeStruct((rows, D_MODEL), _F32),
                   jax.ShapeDtypeStruct((rows, D_MODEL), _BF16)],
        compiler_params=pltpu.CompilerParams(dimension_semantics=("parallel",),
                                             vmem_limit_bytes=VMEM_LIMIT),
        name="mix",
    )(xn2d, pool2d, attn2d, h2d, w_in, w_a, w_b, w_o, g_next.reshape(1, D_MODEL))


def _gelu_tanh(x):
    return 0.5 * x * (1.0 + jnp.tanh(np.sqrt(2.0 / np.pi).astype(np.float32) * (x + 0.044715 * (x * x * x))))


def _ffn_kernel(xp_ref, x_ref, xnx_ref, h_ref, wup_ref, cw_ref, cb_ref, wdn_ref, g_ref, *rest, final, n_cast):
    n_out = 1 if final else 2
    cast_src, out_refs, rest = rest[:n_cast], rest[n_cast:n_cast + n_out], rest[n_cast + n_out:]
    cast_dst, a_refs = rest[:n_cast], rest[n_cast:]
    _cast_blocks(cast_src, cast_dst)
    tm = x_ref.shape[0]
    tiles_per_seq = SEQ // tm
    t = pl.program_id(0) % tiles_per_seq
    x = x_ref[...]
    x_prev = jnp.where(t > 0, xp_ref[...], jnp.zeros_like(xp_ref))
    x_next = jnp.where(t < tiles_per_seq - 1, xnx_ref[...], jnp.zeros_like(xnx_ref))
    xe = jnp.concatenate([x_prev, x, x_next], axis=0)
    acc = None
    assert sum(FF_CHUNKS) == D_FF
    for c, width in enumerate(FF_CHUNKS):
        c0 = sum(FF_CHUNKS[:c])
        cols = slice(c0, c0 + width)
        gcols = slice(D_FF + c0, D_FF + c0 + width)
        a_ext = jnp.dot(xe, wup_ref[:, cols], preferred_element_type=_F32)
        gate = jnp.dot(x, wup_ref[:, gcols], preferred_element_type=_F32)
        a_mid = a_ext[HALO:HALO + tm]
        a_ref = a_refs[c]
        for j in range(width // LANES):
            a_ref[j] = a_ext[:, j * LANES:(j + 1) * LANES]
        a_prev = jnp.concatenate([a_ref[j, pl.ds(HALO - 1, tm, stride=1), :] for j in range(width // LANES)], axis=1)
        a_next = jnp.concatenate([a_ref[j, pl.ds(HALO + 1, tm, stride=1), :] for j in range(width // LANES)], axis=1)
        cw = cw_ref[:, cols]
        conv = a_prev * cw[0:1] + a_mid * cw[1:2] + a_next * cw[2:3] + cb_ref[:, cols]
        act = (_gelu_tanh(conv) * gate).astype(_BF16)
        part = jnp.dot(act, wdn_ref[cols, :], preferred_element_type=_F32)
        acc = part if acc is None else acc + part
    h = h_ref[...] + acc
    if final:
        out_refs[0][...] = _rms(h, g_ref[...])
    else:
        out_refs[0][...] = h
        out_refs[1][...] = _rms(h, g_ref[...]).astype(out_refs[1].dtype)


def _ffn_call(xn2d, h2d, w_up, conv_w, conv_b, w_down, g_next, final, to_cast=(), cast_layer=0):
    rows = xn2d.shape[0]
    tm = ROW_TILE
    per = tm // HALO
    last = rows // HALO - 1
    row_spec = pl.BlockSpec((tm, D_MODEL), lambda i: (i, 0))
    if final:
        out_specs = [row_spec]
        out_shape = [jax.ShapeDtypeStruct((rows, D_MODEL), _F32)]
    else:
        out_specs = [row_spec, row_spec]
        out_shape = [jax.ShapeDtypeStruct((rows, D_MODEL), _F32),
                     jax.ShapeDtypeStruct((rows, D_MODEL), _BF16)]
    riders = [_cast_rider(w, cast_layer, rows // tm, lambda i: i) for w in to_cast]
    outs = pl.pallas_call(
        functools.partial(_ffn_kernel, final=final, n_cast=len(riders)),
        grid=(rows // tm,),
        in_specs=[pl.BlockSpec((HALO, D_MODEL), lambda i: (jnp.maximum(i * per - 1, 0), 0)),
                  row_spec,
                  pl.BlockSpec((HALO, D_MODEL), lambda i: (jnp.minimum((i + 1) * per, last), 0)),
                  row_spec,
                  _const_spec((D_MODEL, 2 * D_FF)),
                  _const_spec((3, D_FF)),
                  _const_spec((1, D_FF)),
                  _const_spec((D_FF, D_MODEL)),
                  _const_spec((1, D_MODEL))] + [r[0] for r in riders],
        out_specs=out_specs + [r[1] for r in riders],
        out_shape=out_shape + [r[2] for r in riders],
        scratch_shapes=[pltpu.VMEM((width // LANES, tm + 2 * HALO, LANES), _F32) for width in FF_CHUNKS],
        compiler_params=pltpu.CompilerParams(dimension_semantics=("parallel",),
                                             vmem_limit_bytes=VMEM_LIMIT),
        name="ffn",
    )(xn2d, xn2d, xn2d, h2d, w_up, conv_w, conv_b.reshape(1, D_FF), w_down, g_next.reshape(1, D_MODEL), *to_cast)
    return outs[:len(out_shape)], outs[len(out_shape):]


def kernel(x, w_in, w_pool, pool_scale, w_a, w_b, w_o, norm1, norm2, w_up, conv_w, conv_b, w_down, rel_bias, norm_f):
    batch, seq, d = x.shape
    assert (seq, d) == (SEQ, D_MODEL)
    depth = w_in.shape[0]
    rows = batch * seq
    bias_rows = _bias_rows(rel_bias)
    h = x.reshape(rows, d)
    xn = None
    w_pool_bf = w_pool.astype(_BF16)
    w_in_bf = w_in[0].astype(_BF16)
    for layer in range(depth):
        if xn is None:
            qkv0, xn3 = _qkv_call(x, w_in_bf, 0, norm_gain=norm1[layer])
            xn = xn3.reshape(rows, d)
        else:
            xn3 = xn.reshape(batch, seq, d)
            qkv0 = _qkv_call(xn3, w_in_bf, 0)
        pool = _pool_call(xn3, w_in_bf, w_pool_bf, pool_scale[layer], layer)
        qkv = [qkv0] + [_qkv_call(xn3, w_in_bf, gi) for gi in (1, 2)]
        attn, (w_a_bf, w_b_bf, w_o_bf, w_up_bf, w_down_bf) = _attn_call(qkv, bias_rows, (w_a, w_b, w_o, w_up, w_down), layer)
        h, xn = _mix_call(xn, pool.reshape(rows, POOL_WIDTH), attn.reshape(rows, ATTN_OUT), h,
                          w_in_bf, w_a_bf, w_b_bf, w_o_bf, norm2[layer])
        final = layer == depth - 1
        g_next = norm_f if final else norm1[layer + 1]
        outs, cast = _ffn_call(xn, h, w_up_bf, conv_w[layer], conv_b[layer], w_down_bf, g_next, final,
                               to_cast=() if final else (w_in,), cast_layer=layer + 1)
        if final:
            return outs[0].reshape(batch, seq, d)
        h, xn = outs
        w_in_bf, = cast
```

```python
import functools

import numpy as np
import jax
import jax.numpy as jnp
from jax import lax
from jax.experimental import pallas as pl
from jax.experimental.pallas import tpu as pltpu

D_MODEL = 1024
SEQ = 2048
POOL_WINDOWS = (2, 4, 8, 16)
POOL_GROUP_DIM = 256
POOL_WIDTH = 1024
ATTN_DILATIONS = (1, 4, 16)
HALF_WINDOW = 64
HEADS_PER_GROUP = 4
N_HEADS = 12
HEAD_DIM = 128
ATTN_WIDTH = N_HEADS * HEAD_DIM
ATTN_OUT = HEADS_PER_GROUP * HEAD_DIM
NEG_INF = -1e30
N_BUCKETS = 32
MAX_DISTANCE = 1024
D_FF = 2816
EPS = 1e-6
PROJ_WIDTH = POOL_WIDTH + 3 * ATTN_WIDTH
GATE_WIDTH = 2 * D_MODEL
IN_WIDTH = PROJ_WIDTH + GATE_WIDTH
GATE_BLOCK = 2560

Q_TILE = 128
UNITS_PER_STEP = 16
LOG2_E = float(np.log2(np.e))
RESIDUE_PITCH = Q_TILE + 4
NORM_ROWS = 256
PROJ_TN = 512
ROW_TILE = 512
BF16_TILE_ROWS = 16
HALO = BF16_TILE_ROWS
FF_CHUNKS = (1536, 1280)
VMEM_LIMIT = 56 * 1024 * 1024
LANES = 128
REGROUP_TILE = 256

_F32 = jnp.float32
_BF16 = jnp.bfloat16


def _rms(x, g):
    return x * lax.rsqrt(jnp.mean(x * x, axis=-1, keepdims=True) + EPS) * g


def _const_spec(shape, index=None):
    index = tuple(index) if index is not None else (0,) * len(shape)
    return pl.BlockSpec(tuple(shape), lambda *_: index, pipeline_mode=pl.Buffered(1))


def _layer_spec(layer, shape, index=None):
    index = tuple(index) if index is not None else (0,) * len(shape)
    return pl.BlockSpec((None,) + tuple(shape), lambda *_: (layer,) + index, pipeline_mode=pl.Buffered(1))


def _cast_rider(param, layer, n_steps, step_of):
    _, rows, cols = param.shape
    n_blocks = max(n for n in range(1, n_steps + 1) if rows % n == 0 and (rows // n) % BF16_TILE_ROWS == 0)
    rb = rows // n_blocks

    def block(*g):
        return step_of(*g) * n_blocks // n_steps

    return (pl.BlockSpec((None, rb, cols), lambda *g: (layer, block(*g), 0)),
            pl.BlockSpec((rb, cols), lambda *g: (block(*g), 0)),
            jax.ShapeDtypeStruct((rows, cols), _BF16))


def _cast_blocks(src_refs, dst_refs):
    for src, dst in zip(src_refs, dst_refs, strict=True):
        dst[...] = src[...].astype(dst.dtype)


POOL_PAD = 16
POOL_EDGE = 8
assert POOL_EDGE >= max(POOL_WINDOWS) // 2 and POOL_PAD - POOL_EDGE >= max(POOL_WINDOWS) // 4


def _pool_kernel(xn_ref, w_ref, wpool_ref, pscale_ref, o_ref, *scratch):
    *u_refs, t_ref = scratch
    n = SEQ + 2 * POOL_PAD
    slabs = POOL_GROUP_DIM // LANES

    def shifted(view, start, rows, k_back, k_fwd):
        return view[pl.ds(start - k_back, rows, stride=1), :] + view[pl.ds(start + k_fwd, rows, stride=1), :]

    def steps(w):
        return (1, 0) if w == 1 else (w // 2, w // 2)

    zeros = jnp.zeros((POOL_PAD, LANES), _F32)
    for t in range(2):
        for s in range(slabs):
            t_ref[t, s, 0:POOL_EDGE, :] = zeros[:POOL_EDGE]
            t_ref[t, s, n - POOL_EDGE:n, :] = zeros[:POOL_EDGE]
    xn = xn_ref[...]

    def project(gi):
        u = jnp.dot(xn, w_ref[:, gi * POOL_GROUP_DIM:(gi + 1) * POOL_GROUP_DIM], preferred_element_type=_F32)
        for s in range(slabs):
            u_refs[gi][s, 0:POOL_PAD, :] = zeros
            u_refs[gi][s, n - POOL_PAD:n, :] = zeros
            u_refs[gi][s, POOL_PAD:POOL_PAD + SEQ, :] = u[:, s * LANES:(s + 1) * LANES]

    edge_row = lax.broadcasted_iota(jnp.int32, (POOL_PAD, POOL_GROUP_DIM), 0)

    def pool(gi):
        window = POOL_WINDOWS[gi]
        u_ref = u_refs[gi]
        cols = slice(gi * POOL_GROUP_DIM, (gi + 1) * POOL_GROUP_DIM)
        totals = []
        for s in range(slabs):
            src, w, slot = u_ref.at[s], 1, 0
            while 2 * w < window:
                t_ref[slot, s, pl.ds(POOL_EDGE, n - 2 * POOL_EDGE), :] = shifted(src, POOL_EDGE, n - 2 * POOL_EDGE,
                                                                              *steps(w))
                src, w, slot = t_ref.at[slot, s], 2 * w, 1 - slot
            totals.append(shifted(src, POOL_PAD, SEQ, *steps(w)))
        total = jnp.concatenate(totals, axis=1)
        u = jnp.concatenate([u_ref[s, POOL_PAD:POOL_PAD + SEQ, :] for s in range(slabs)], axis=1)
        wpool = wpool_ref[gi]
        scale = pscale_ref[:, cols]
        pooled = total * (1.0 / window) - u
        z = jnp.dot(pooled.astype(_BF16), wpool, preferred_element_type=_F32)
        o_ref[:, cols] = (z * scale).astype(o_ref.dtype)
        for r0 in (0, SEQ - POOL_PAD):
            pos = edge_row + r0
            size = (jnp.minimum(pos + window // 2, SEQ) - jnp.maximum(pos - window // 2, 0)).astype(_F32)
            pooled = total[r0:r0 + POOL_PAD] / size - u[r0:r0 + POOL_PAD]
            z = jnp.dot(pooled.astype(_BF16), wpool, preferred_element_type=_F32)
            o_ref[r0:r0 + POOL_PAD, cols] = (z * scale).astype(o_ref.dtype)

    n_groups = len(POOL_WINDOWS)
    project(0)
    for gi in range(n_groups):
        if gi + 1 < n_groups:
            project(gi + 1)
        pool(gi)


def _pool_call(xn3, w_in, w_pool, pool_scale, layer):
    batch = xn3.shape[0]
    return pl.pallas_call(
        _pool_kernel,
        grid=(batch,),
        in_specs=[pl.BlockSpec((None, SEQ, D_MODEL), lambda b: (b, 0, 0)),
                  _const_spec((D_MODEL, POOL_WIDTH)),
                  _layer_spec(layer, (len(POOL_WINDOWS), POOL_GROUP_DIM, POOL_GROUP_DIM)),
                  _const_spec((1, POOL_WIDTH))],
        out_specs=pl.BlockSpec((None, SEQ, POOL_WIDTH), lambda b: (b, 0, 0)),
        out_shape=jax.ShapeDtypeStruct((batch, SEQ, POOL_WIDTH), _BF16),
        scratch_shapes=[pltpu.VMEM((POOL_GROUP_DIM // LANES, SEQ + 2 * POOL_PAD, LANES), _F32) for _ in POOL_WINDOWS]
        + [pltpu.VMEM((2, POOL_GROUP_DIM // LANES, SEQ + 2 * POOL_PAD, LANES), _F32)],
        compiler_params=pltpu.CompilerParams(dimension_semantics=("parallel",),
                                             vmem_limit_bytes=VMEM_LIMIT),
        name="pool",
    )(xn3, w_in, w_pool, pool_scale.reshape(1, POOL_WIDTH))


def _qkv_kernel(x_ref, *refs, dil, norm_input):
    if norm_input:
        g_ref, wq_ref, wk_ref, wv_ref, o_ref, xn_ref, *scratch = refs
        for r0 in range(0, SEQ, NORM_ROWS):
            xn_ref[r0:r0 + NORM_ROWS, :] = _rms(x_ref[r0:r0 + NORM_ROWS, :], g_ref[...]).astype(xn_ref.dtype)
        x_ref = xn_ref
    else:
        wq_ref, wk_ref, wv_ref, o_ref, *scratch = refs
    if dil == 1:
        x = x_ref[...]
    else:
        xp_ref, = scratch
        sub = SEQ // dil
        sub_t = REGROUP_TILE // dil
        for t in range(SEQ // REGROUP_TILE):
            tile = x_ref[t * REGROUP_TILE:(t + 1) * REGROUP_TILE, :].astype(_F32)
            by_residue = tile.reshape(sub_t, dil, D_MODEL).swapaxes(0, 1)
            for r in range(dil):
                xp_ref[r * sub + t * sub_t:r * sub + (t + 1) * sub_t, :] = by_residue[r].astype(_BF16)
        x = xp_ref[...]
    for which, w_ref in enumerate((wq_ref, wk_ref, wv_ref)):
        y = jnp.dot(x, w_ref[...], preferred_element_type=_F32)
        if which == 0:
            y = y * HEAD_DIM ** -0.5
        for hd in range(HEADS_PER_GROUP):
            o_ref[which * HEADS_PER_GROUP + hd] = y[:, hd * HEAD_DIM:(hd + 1) * HEAD_DIM].astype(o_ref.dtype)


def _qkv_call(x3, w_in, gi, norm_gain=None):
    batch = x3.shape[0]
    dil = ATTN_DILATIONS[gi]
    first = POOL_WIDTH // PROJ_TN + gi
    norm_input = norm_gain is not None
    x_spec = pl.BlockSpec((None, SEQ, D_MODEL), lambda b: (b, 0, 0))
    in_specs = [x_spec] + ([_const_spec((1, D_MODEL))] if norm_input else [])
    in_specs += [_const_spec((D_MODEL, PROJ_TN), (0, first + 3 * which)) for which in range(3)]
    out_specs = [pl.BlockSpec((None, 3 * HEADS_PER_GROUP, SEQ, HEAD_DIM), lambda b: (b, 0, 0, 0))]
    out_shape = [jax.ShapeDtypeStruct((batch, 3 * HEADS_PER_GROUP, SEQ, HEAD_DIM), _BF16)]
    if norm_input:
        out_specs.append(x_spec)
        out_shape.append(jax.ShapeDtypeStruct((batch, SEQ, D_MODEL), _BF16))
    args = (x3,) + ((norm_gain.reshape(1, D_MODEL),) if norm_input else ()) + (w_in, w_in, w_in)
    outs = pl.pallas_call(
        functools.partial(_qkv_kernel, dil=dil, norm_input=norm_input),
        grid=(batch,),
        in_specs=in_specs,
        out_specs=out_specs,
        out_shape=out_shape,
        scratch_shapes=[] if dil == 1 else [pltpu.VMEM((SEQ, D_MODEL), _BF16)],
        compiler_params=pltpu.CompilerParams(dimension_semantics=("parallel",),
                                             vmem_limit_bytes=VMEM_LIMIT),
        name=f"qkv{gi}",
    )(*args)
    return outs if norm_input else outs[0]


def _t5_buckets_np(rel):
    n = -rel
    half = N_BUCKETS // 2
    ret = (n < 0).astype(np.int32) * half
    n = np.abs(n)
    max_exact = half // 2
    large = max_exact + (np.log(np.maximum(n, 1) / max_exact)
                         / np.log(MAX_DISTANCE / max_exact) * (half - max_exact)).astype(np.int32)
    large = np.minimum(large, half - 1)
    return (ret + np.where(n < max_exact, n, large)).astype(np.int32)


def _key_span(dil):
    return min(Q_TILE + 2 * HALF_WINDOW, SEQ // dil)


BIAS_LANES = 512


def _bias_rows(rel_bias):
    rows = []
    for gi, dil in enumerate(ATTN_DILATIONS):
        buckets = _t5_buckets_np(dil * np.arange(-HALF_WINDOW, HALF_WINDOW + 1))
        bias = rel_bias[buckets][:, gi * HEADS_PER_GROUP:(gi + 1) * HEADS_PER_GROUP].T.astype(_F32)
        rows.append(jnp.pad(bias, ((0, 0), (0, BIAS_LANES - bias.shape[1])), constant_values=NEG_INF))
    return jnp.stack(rows, axis=1)


def _tile_deltas(dil):
    return (0, HALF_WINDOW, 2 * HALF_WINDOW) if SEQ // dil > Q_TILE else (0,)


def _attn_kernel(q0_ref, k0_ref, v0_ref, q1_ref, k1_ref, v1_ref, q2_ref, k2_ref, v2_ref, e_ref, *rest, n_cast):
    cast_src, (o_ref, *rest) = rest[:n_cast], rest[n_cast:]
    cast_dst, (t0_ref, t1_ref, t2_ref, *stats) = rest[:n_cast], rest[n_cast:]
    near_o_ref, near_m_ref, near_l_ref, far_o_ref, far_m_ref, far_l_ref = stats
    _cast_blocks(cast_src, cast_dst)
    groups = ((q0_ref, k0_ref, v0_ref, t0_ref), (q1_ref, k1_ref, v1_ref, t1_ref), (q2_ref, k2_ref, v2_ref, t2_ref))

    @pl.when(pl.program_id(1) == 0)
    def _():
        for gi, (_, _, _, t_ref) in enumerate(groups):
            dil = ATTN_DILATIONS[gi]
            row = jnp.broadcast_to(e_ref[gi:gi + 1, :], (Q_TILE, BIAS_LANES))
            for vi, delta in enumerate(_tile_deltas(dil)):
                skew = pltpu.roll(row, (delta - HALF_WINDOW) % BIAS_LANES, axis=1, stride=1, stride_axis=0)
                t_ref[vi] = skew[:, :_key_span(dil)]

    def scores(u, gi):
        q_ref, k_ref, _, t_ref = groups[gi]
        dil = ATTN_DILATIONS[gi]
        sub = SEQ // dil
        span = _key_span(dil)
        blocks = sub // Q_TILE
        r = u // blocks
        m0 = (u % blocks) * Q_TILE
        start = jnp.clip(m0 - HALF_WINDOW, 0, sub - span)
        variant = (m0 - start) // HALF_WINDOW
        q = q_ref[pl.ds(pl.multiple_of(u * Q_TILE, Q_TILE), Q_TILE), :]
        k0 = pl.multiple_of(r * sub + start, HALF_WINDOW)
        k = k_ref[pl.ds(k0, span), :]
        s = lax.dot_general(q, k, (((1,), (1,)), ((), ())), preferred_element_type=_F32)
        if dil == 1:
            dest = (gi, pl.ds(pl.multiple_of(m0, Q_TILE), Q_TILE))
        elif blocks > 1:
            dest = (gi, pl.ds(m0 * dil + r, Q_TILE, stride=dil))
        else:
            dest = (None, pl.ds(r * RESIDUE_PITCH, Q_TILE, stride=1))
        return s + t_ref[variant], k0, dest

    def weights(s):
        m = jnp.max(s, axis=-1, keepdims=True)
        return m, jnp.exp((s - m).astype(_BF16))

    ones = jnp.ones((Q_TILE + 2 * HALF_WINDOW, HEAD_DIM), _BF16)

    def values(gi, k0, dest, m, p):
        span = p.shape[1]
        v = groups[gi][2][pl.ds(k0, span), :]
        ov = jnp.dot(p, jnp.concatenate([v, ones[:span]], axis=1), preferred_element_type=_F32)
        m2 = jnp.broadcast_to(m * LOG2_E, (Q_TILE, HEAD_DIM))
        slot, rows = dest
        if slot is None:
            far_o_ref[rows, :] = ov[:, :HEAD_DIM]
            far_m_ref[rows, :] = m2
            far_l_ref[rows, :] = ov[:, HEAD_DIM:]
        else:
            near_o_ref[slot, rows, :] = ov[:, :HEAD_DIM]
            near_m_ref[slot, rows, :] = m2
            near_l_ref[slot, rows, :] = ov[:, HEAD_DIM:]

    def units(it, carry):
        todo = [(it * UNITS_PER_STEP + j, gi) for j in range(UNITS_PER_STEP) for gi in range(3)]
        scored = [scores(u, gi) for u, gi in todo]
        soft = [weights(s) for s, _, _ in scored]
        for (_, gi), (_, k0, dest), (m, p) in zip(todo, scored, soft):
            values(gi, k0, dest, m, p)
        return carry

    lax.fori_loop(0, SEQ // Q_TILE // UNITS_PER_STEP, units, 0)

    far_dil = ATTN_DILATIONS[2]

    def merge(t, carry):
        rows = pl.ds(pl.multiple_of(t * Q_TILE, Q_TILE), Q_TILE)

        def far(ref):
            return jnp.concatenate([ref[pl.ds(t * (Q_TILE // far_dil) + j, far_dil, stride=RESIDUE_PITCH), :]
                                    for j in range(Q_TILE // far_dil)], axis=0)

        ms = [near_m_ref[0, rows, :], near_m_ref[1, rows, :], far(far_m_ref)]
        outs = [near_o_ref[0, rows, :], near_o_ref[1, rows, :], far(far_o_ref)]
        ls = [near_l_ref[0, rows, :], near_l_ref[1, rows, :], far(far_l_ref)]
        top = jnp.maximum(jnp.maximum(ms[0], ms[1]), ms[2])
        ws = [jnp.exp2(x - top) for x in ms]
        num = ws[0] * outs[0] + ws[1] * outs[1] + ws[2] * outs[2]
        den = ws[0] * ls[0] + ws[1] * ls[1] + ws[2] * ls[2]
        o_ref[rows, :] = (num / den).astype(o_ref.dtype)
        return carry

    lax.fori_loop(0, SEQ // Q_TILE, merge, 0, unroll=4)


def _attn_call(qkv, bias_rows, to_cast, layer):
    batch = qkv[0].shape[0]

    def head_spec(which):
        return pl.BlockSpec((None, None, SEQ, HEAD_DIM), lambda h, b: (b, which * HEADS_PER_GROUP + h, 0, 0))

    in_specs = [head_spec(which) for gi in range(3) for which in range(3)]
    in_specs.append(pl.BlockSpec((None, 3, BIAS_LANES), lambda h, b: (h, 0, 0)))
    riders = [_cast_rider(w, layer, HEADS_PER_GROUP * batch, lambda h, b: h * batch + b) for w in to_cast]
    in_specs += [r[0] for r in riders]
    scratch = [pltpu.VMEM((len(_tile_deltas(dil)), Q_TILE, _key_span(dil)), _F32) for dil in ATTN_DILATIONS]
    assert SEQ // ATTN_DILATIONS[2] == Q_TILE and all(SEQ // d > Q_TILE for d in ATTN_DILATIONS[:2])
    scratch += [pltpu.VMEM((2, SEQ, HEAD_DIM), _F32) for _ in range(3)]
    scratch += [pltpu.VMEM((ATTN_DILATIONS[2] * RESIDUE_PITCH, HEAD_DIM), _F32) for _ in range(3)]
    attn, *cast = pl.pallas_call(
        functools.partial(_attn_kernel, n_cast=len(riders)),
        grid=(HEADS_PER_GROUP, batch),
        in_specs=in_specs,
        out_specs=[pl.BlockSpec((None, SEQ, HEAD_DIM), lambda h, b: (b, 0, h))] + [r[1] for r in riders],
        out_shape=[jax.ShapeDtypeStruct((batch, SEQ, ATTN_OUT), _BF16)] + [r[2] for r in riders],
        scratch_shapes=scratch,
        compiler_params=pltpu.CompilerParams(dimension_semantics=("arbitrary", "arbitrary"),
                                             vmem_limit_bytes=VMEM_LIMIT),
        name="attn",
    )(*[qkv[gi] for gi in range(3) for _ in range(3)], bias_rows, *to_cast)
    return attn, cast


def _mix_kernel(xn_ref, pool_ref, attn_ref, h_ref, wg_ref, wa_ref, wb_ref, wo_ref, g_ref, h_out_ref, xn_out_ref):
    xn = xn_ref[...]
    gates = jnp.dot(xn, wg_ref[:, GATE_BLOCK - GATE_WIDTH:], preferred_element_type=_F32)
    a = jnp.dot(pool_ref[...], wa_ref[...], preferred_element_type=_F32)
    b = jnp.dot(attn_ref[...], wb_ref[...], preferred_element_type=_F32)
    merged = jax.nn.sigmoid(gates[:, :D_MODEL]) * a + jax.nn.sigmoid(gates[:, D_MODEL:]) * b
    h = h_ref[...] + jnp.dot(merged.astype(_BF16), wo_ref[...], preferred_element_type=_F32)
    h_out_ref[...] = h
    xn_out_ref[...] = _rms(h, g_ref[...]).astype(xn_out_ref.dtype)


def _mix_call(xn2d, pool2d, attn2d, h2d, w_in, w_a, w_b, w_o, g_next):
    rows = xn2d.shape[0]
    tm = ROW_TILE
    assert IN_WIDTH % GATE_BLOCK == 0 and GATE_BLOCK >= GATE_WIDTH
    return pl.pallas_call(
        _mix_kernel,
        grid=(rows // tm,),
        in_specs=[pl.BlockSpec((tm, D_MODEL), lambda i: (i, 0)),
                  pl.BlockSpec((tm, POOL_WIDTH), lambda i: (i, 0)),
                  pl.BlockSpec((tm, ATTN_OUT), lambda i: (i, 0)),
                  pl.BlockSpec((tm, D_MODEL), lambda i: (i, 0)),
                  _const_spec((D_MODEL, GATE_BLOCK), (0, IN_WIDTH // GATE_BLOCK - 1)),
                  _const_spec((POOL_WIDTH, D_MODEL)),
                  _const_spec((ATTN_OUT, D_MODEL)),
                  _const_spec((D_MODEL, D_MODEL)),
                  _const_spec((1, D_MODEL))],
        out_specs=[pl.BlockSpec((tm, D_MODEL), lambda i: (i, 0)),
                   pl.BlockSpec((tm, D_MODEL), lambda i: (i, 0))],
        out_shape=[jax.ShapeDtypeStruct((rows, D_MODEL), _F32),
                   jax.ShapeDtypeStruct((rows, D_MODEL), _BF16)],
        compiler_params=pltpu.CompilerParams(dimension_semantics=("parallel",),
                                             vmem_limit_bytes=VMEM_LIMIT),
        name="mix",
    )(xn2d, pool2d, attn2d, h2d, w_in, w_a, w_b, w_o, g_next.reshape(1, D_MODEL))


def _gelu_tanh(x):
    return 0.5 * x * (1.0 + jnp.tanh(np.sqrt(2.0 / np.pi).astype(np.float32) * (x + 0.044715 * (x * x * x))))


def _ffn_kernel(xp_ref, x_ref, xnx_ref, h_ref, wup_ref, cw_ref, cb_ref, wdn_ref, g_ref, *rest, final, n_cast):
    n_out = 1 if final else 2
    cast_src, out_refs, rest = rest[:n_cast], rest[n_cast:n_cast + n_out], rest[n_cast + n_out:]
    cast_dst, a_refs = rest[:n_cast], rest[n_cast:]
    _cast_blocks(cast_src, cast_dst)
    tm = x_ref.shape[0]
    tiles_per_seq = SEQ // tm
    t = pl.program_id(0) % tiles_per_seq
    x = x_ref[...]
    x_prev = jnp.where(t > 0, xp_ref[...], jnp.zeros_like(xp_ref))
    x_next = jnp.where(t < tiles_per_seq - 1, xnx_ref[...], jnp.zeros_like(xnx_ref))
    xe = jnp.concatenate([x_prev, x, x_next], axis=0)
    acc = None
    assert sum(FF_CHUNKS) == D_FF
    for c, width in enumerate(FF_CHUNKS):
        c0 = sum(FF_CHUNKS[:c])
        cols = slice(c0, c0 + width)
        gcols = slice(D_FF + c0, D_FF + c0 + width)
        a_ext = jnp.dot(xe, wup_ref[:, cols], preferred_element_type=_F32)
        gate = jnp.dot(x, wup_ref[:, gcols], preferred_element_type=_F32)
        a_mid = a_ext[HALO:HALO + tm]
        a_ref = a_refs[c]
        for j in range(width // LANES):
            a_ref[j] = a_ext[:, j * LANES:(j + 1) * LANES]
        a_prev = jnp.concatenate([a_ref[j, pl.ds(HALO - 1, tm, stride=1), :] for j in range(width // LANES)], axis=1)
        a_next = jnp.concatenate([a_ref[j, pl.ds(HALO + 1, tm, stride=1), :] for j in range(width // LANES)], axis=1)
        cw = cw_ref[:, cols]
        conv = a_prev * cw[0:1] + a_mid * cw[1:2] + a_next * cw[2:3] + cb_ref[:, cols]
        act = (_gelu_tanh(conv) * gate).astype(_BF16)
        part = jnp.dot(act, wdn_ref[cols, :], preferred_element_type=_F32)
        acc = part if acc is None else acc + part
    h = h_ref[...] + acc
    if final:
        out_refs[0][...] = _rms(h, g_ref[...])
    else:
        out_refs[0][...] = h
        out_refs[1][...] = _rms(h, g_ref[...]).astype(out_refs[1].dtype)


def _ffn_call(xn2d, h2d, w_up, conv_w, conv_b, w_down, g_next, final, to_cast=(), cast_layer=0):
    rows = xn2d.shape[0]
    tm = ROW_TILE
    per = tm // HALO
    last = rows // HALO - 1
    row_spec = pl.BlockSpec((tm, D_MODEL), lambda i: (i, 0))
    if final:
        out_specs = [row_spec]
        out_shape = [jax.ShapeDtypeStruct((rows, D_MODEL), _F32)]
    else:
        out_specs = [row_spec, row_spec]
        out_shape = [jax.ShapeDtypeStruct((rows, D_MODEL), _F32),
                     jax.ShapeDtypeStruct((rows, D_MODEL), _BF16)]
    riders = [_cast_rider(w, cast_layer, rows // tm, lambda i: i) for w in to_cast]
    outs = pl.pallas_call(
        functools.partial(_ffn_kernel, final=final, n_cast=len(riders)),
        grid=(rows // tm,),
        in_specs=[pl.BlockSpec((HALO, D_MODEL), lambda i: (jnp.maximum(i * per - 1, 0), 0)),
                  row_spec,
                  pl.BlockSpec((HALO, D_MODEL), lambda i: (jnp.minimum((i + 1) * per, last), 0)),
                  row_spec,
                  _const_spec((D_MODEL, 2 * D_FF)),
                  _const_spec((3, D_FF)),
                  _const_spec((1, D_FF)),
                  _const_spec((D_FF, D_MODEL)),
                  _const_spec((1, D_MODEL))] + [r[0] for r in riders],
        out_specs=out_specs + [r[1] for r in riders],
        out_shape=out_shape + [r[2] for r in riders],
        scratch_shapes=[pltpu.VMEM((width // LANES, tm + 2 * HALO, LANES), _F32) for width in FF_CHUNKS],
        compiler_params=pltpu.CompilerParams(dimension_semantics=("parallel",),
                                             vmem_limit_bytes=VMEM_LIMIT),
        name="ffn",
    )(xn2d, xn2d, xn2d, h2d, w_up, conv_w, conv_b.reshape(1, D_FF), w_down, g_next.reshape(1, D_MODEL), *to_cast)
    return outs[:len(out_shape)], outs[len(out_shape):]


def kernel(x, w_in, w_pool, pool_scale, w_a, w_b, w_o, norm1, norm2, w_up, conv_w, conv_b, w_down, rel_bias, norm_f):
    batch, seq, d = x.shape
    assert (seq, d) == (SEQ, D_MODEL)
    depth = w_in.shape[0]
    rows = batch * seq
    bias_rows = _bias_rows(rel_bias)
    h = x.reshape(rows, d)
    xn = None
    w_pool_bf = w_pool.astype(_BF16)
    w_in_bf = w_in[0].astype(_BF16)
    for layer in range(depth):
        if xn is None:
            qkv0, xn3 = _qkv_call(x, w_in_bf, 0, norm_gain=norm1[layer])
            xn = xn3.reshape(rows, d)
        else:
            xn3 = xn.reshape(batch, seq, d)
            qkv0 = _qkv_call(xn3, w_in_bf, 0)
        pool = _pool_call(xn3, w_in_bf, w_pool_bf, pool_scale[layer], layer)
        qkv = [qkv0] + [_qkv_call(xn3, w_in_bf, gi) for gi in (1, 2)]
        attn, (w_a_bf, w_b_bf, w_o_bf, w_up_bf, w_down_bf) = _attn_call(qkv, bias_rows, (w_a, w_b, w_o, w_up, w_down), layer)
        h, xn = _mix_call(xn, pool.reshape(rows, POOL_WIDTH), attn.reshape(rows, ATTN_OUT), h,
                          w_in_bf, w_a_bf, w_b_bf, w_o_bf, norm2[layer])
        final = layer == depth - 1
        g_next = norm_f if final else norm1[layer + 1]
        outs, cast = _ffn_call(xn, h, w_up_bf, conv_w[layer], conv_b[layer], w_down_bf, g_next, final,
                               to_cast=() if final else (w_in,), cast_layer=layer + 1)
        if final:
            return outs[0].reshape(batch, seq, d)
        h, xn = outs
        w_in_bf, = cast
```

```python
import functools

import numpy as np
import jax
import jax.numpy as jnp
from jax import lax
from jax.experimental import pallas as pl
from jax.experimental.pallas import tpu as pltpu

D_MODEL = 1024
SEQ = 2048
POOL_WINDOWS = (2, 4, 8, 16)
POOL_GROUP_DIM = 256
POOL_WIDTH = 1024
ATTN_DILATIONS = (1, 4, 16)
HALF_WINDOW = 64
HEADS_PER_GROUP = 4
N_HEADS = 12
HEAD_DIM = 128
ATTN_WIDTH = N_HEADS * HEAD_DIM
ATTN_OUT = HEADS_PER_GROUP * HEAD_DIM
NEG_INF = -1e30
N_BUCKETS = 32
MAX_DISTANCE = 1024
D_FF = 2816
EPS = 1e-6
PROJ_WIDTH = POOL_WIDTH + 3 * ATTN_WIDTH
GATE_WIDTH = 2 * D_MODEL
IN_WIDTH = PROJ_WIDTH + GATE_WIDTH
GATE_BLOCK = 2560

Q_TILE = 128
UNITS_PER_STEP = 16
LOG2_E = float(np.log2(np.e))
RESIDUE_PITCH = Q_TILE + 4
NORM_ROWS = 256
PROJ_TN = 512
ROW_TILE = 512
BF16_TILE_ROWS = 16
HALO = BF16_TILE_ROWS
FF_CHUNKS = (1536, 1280)
VMEM_LIMIT = 56 * 1024 * 1024
LANES = 128
REGROUP_TILE = 256

_F32 = jnp.float32
_BF16 = jnp.bfloat16


def _rms(x, g):
    return x * lax.rsqrt(jnp.mean(x * x, axis=-1, keepdims=True) + EPS) * g


def _const_spec(shape, index=None):
    index = tuple(index) if index is not None else (0,) * len(shape)
    return pl.BlockSpec(tuple(shape), lambda *_: index, pipeline_mode=pl.Buffered(1))


def _layer_spec(layer, shape, index=None):
    index = tuple(index) if index is not None else (0,) * len(shape)
    return pl.BlockSpec((None,) + tuple(shape), lambda *_: (layer,) + index, pipeline_mode=pl.Buffered(1))


def _cast_rider(param, layer, n_steps, step_of):
    _, rows, cols = param.shape
    n_blocks = max(n for n in range(1, n_steps + 1) if rows % n == 0 and (rows // n) % BF16_TILE_ROWS == 0)
    rb = rows // n_blocks

    def block(*g):
        return step_of(*g) * n_blocks // n_steps

    return (pl.BlockSpec((None, rb, cols), lambda *g: (layer, block(*g), 0)),
            pl.BlockSpec((rb, cols), lambda *g: (block(*g), 0)),
            jax.ShapeDtypeStruct((rows, cols), _BF16))


def _cast_blocks(src_refs, dst_refs):
    for src, dst in zip(src_refs, dst_refs, strict=True):
        dst[...] = src[...].astype(dst.dtype)


POOL_PAD = 16
POOL_EDGE = 8
assert POOL_EDGE >= max(POOL_WINDOWS) // 2 and POOL_PAD - POOL_EDGE >= max(POOL_WINDOWS) // 4


def _pool_kernel(xn_ref, w_ref, wpool_ref, pscale_ref, o_ref, *scratch):
    *u_refs, t_ref = scratch
    n = SEQ + 2 * POOL_PAD
    slabs = POOL_GROUP_DIM // LANES

    def shifted(view, start, rows, k_back, k_fwd):
        return view[pl.ds(start - k_back, rows, stride=1), :] + view[pl.ds(start + k_fwd, rows, stride=1), :]

    def steps(w):
        return (1, 0) if w == 1 else (w // 2, w // 2)

    zeros = jnp.zeros((POOL_PAD, LANES), _F32)
    for t in range(2):
        for s in range(slabs):
            t_ref[t, s, 0:POOL_EDGE, :] = zeros[:POOL_EDGE]
            t_ref[t, s, n - POOL_EDGE:n, :] = zeros[:POOL_EDGE]
    xn = xn_ref[...]

    def project(gi):
        u = jnp.dot(xn, w_ref[:, gi * POOL_GROUP_DIM:(gi + 1) * POOL_GROUP_DIM], preferred_element_type=_F32)
        for s in range(slabs):
            u_refs[gi][s, 0:POOL_PAD, :] = zeros
            u_refs[gi][s, n - POOL_PAD:n, :] = zeros
            u_refs[gi][s, POOL_PAD:POOL_PAD + SEQ, :] = u[:, s * LANES:(s + 1) * LANES]

    edge_row = lax.broadcasted_iota(jnp.int32, (POOL_PAD, POOL_GROUP_DIM), 0)

    def pool(gi):
        window = POOL_WINDOWS[gi]
        u_ref = u_refs[gi]
        cols = slice(gi * POOL_GROUP_DIM, (gi + 1) * POOL_GROUP_DIM)
        totals = []
        for s in range(slabs):
            src, w, slot = u_ref.at[s], 1, 0
            while 2 * w < window:
                t_ref[slot, s, pl.ds(POOL_EDGE, n - 2 * POOL_EDGE), :] = shifted(src, POOL_EDGE, n - 2 * POOL_EDGE,
                                                                              *steps(w))
                src, w, slot = t_ref.at[slot, s], 2 * w, 1 - slot
            totals.append(shifted(src, POOL_PAD, SEQ, *steps(w)))
        total = jnp.concatenate(totals, axis=1)
        u = jnp.concatenate([u_ref[s, POOL_PAD:POOL_PAD + SEQ, :] for s in range(slabs)], axis=1)
        wpool = wpool_ref[gi]
        scale = pscale_ref[:, cols]
        pooled = total * (1.0 / window) - u
        z = jnp.dot(pooled.astype(_BF16), wpool, preferred_element_type=_F32)
        o_ref[:, cols] = (z * scale).astype(o_ref.dtype)
        for r0 in (0, SEQ - POOL_PAD):
            pos = edge_row + r0
            size = (jnp.minimum(pos + window // 2, SEQ) - jnp.maximum(pos - window // 2, 0)).astype(_F32)
            pooled = total[r0:r0 + POOL_PAD] / size - u[r0:r0 + POOL_PAD]
            z = jnp.dot(pooled.astype(_BF16), wpool, preferred_element_type=_F32)
            o_ref[r0:r0 + POOL_PAD, cols] = (z * scale).astype(o_ref.dtype)

    n_groups = len(POOL_WINDOWS)
    project(0)
    for gi in range(n_groups):
        if gi + 1 < n_groups:
            project(gi + 1)
        pool(gi)


def _pool_call(xn3, w_in, w_pool, pool_scale, layer):
    batch = xn3.shape[0]
    return pl.pallas_call(
        _pool_kernel,
        grid=(batch,),
        in_specs=[pl.BlockSpec((None, SEQ, D_MODEL), lambda b: (b, 0, 0)),
                  _const_spec((D_MODEL, POOL_WIDTH)),
                  _layer_spec(layer, (len(POOL_WINDOWS), POOL_GROUP_DIM, POOL_GROUP_DIM)),
                  _const_spec((1, POOL_WIDTH))],
        out_specs=pl.BlockSpec((None, SEQ, POOL_WIDTH), lambda b: (b, 0, 0)),
        out_shape=jax.ShapeDtypeStruct((batch, SEQ, POOL_WIDTH), _BF16),
        scratch_shapes=[pltpu.VMEM((POOL_GROUP_DIM // LANES, SEQ + 2 * POOL_PAD, LANES), _F32) for _ in POOL_WINDOWS]
        + [pltpu.VMEM((2, POOL_GROUP_DIM // LANES, SEQ + 2 * POOL_PAD, LANES), _F32)],
        compiler_params=pltpu.CompilerParams(dimension_semantics=("parallel",),
                                             vmem_limit_bytes=VMEM_LIMIT),
        name="pool",
    )(xn3, w_in, w_pool, pool_scale.reshape(1, POOL_WIDTH))


def _qkv_kernel(x_ref, *refs, dil, norm_input):
    if norm_input:
        g_ref, wq_ref, wk_ref, wv_ref, o_ref, xn_ref, *scratch = refs
        for r0 in range(0, SEQ, NORM_ROWS):
            xn_ref[r0:r0 + NORM_ROWS, :] = _rms(x_ref[r0:r0 + NORM_ROWS, :], g_ref[...]).astype(xn_ref.dtype)
        x_ref = xn_ref
    else:
        wq_ref, wk_ref, wv_ref, o_ref, *scratch = refs
    if dil == 1:
        x = x_ref[...]
    else:
        xp_ref, = scratch
        sub = SEQ // dil
        sub_t = REGROUP_TILE // dil
        i = lax.broadcasted_iota(jnp.int32, (REGROUP_TILE, REGROUP_TILE), 0)
        j = lax.broadcasted_iota(jnp.int32, (REGROUP_TILE, REGROUP_TILE), 1)
        pick = (j == (i % sub_t) * dil + i // sub_t).astype(_BF16)
        for t in range(SEQ // REGROUP_TILE):
            tile = x_ref[t * REGROUP_TILE:(t + 1) * REGROUP_TILE, :]
            srt = jnp.dot(pick, tile, preferred_element_type=_F32).astype(_BF16)
            for r in range(dil):
                xp_ref[r * sub + t * sub_t:r * sub + (t + 1) * sub_t, :] = srt[r * sub_t:(r + 1) * sub_t]
        x = xp_ref[...]
    for which, w_ref in enumerate((wq_ref, wk_ref, wv_ref)):
        y = jnp.dot(x, w_ref[...], preferred_element_type=_F32)
        if which == 0:
            y = y * HEAD_DIM ** -0.5
        for hd in range(HEADS_PER_GROUP):
            o_ref[which * HEADS_PER_GROUP + hd] = y[:, hd * HEAD_DIM:(hd + 1) * HEAD_DIM].astype(o_ref.dtype)


def _qkv_call(x3, w_in, gi, norm_gain=None):
    batch = x3.shape[0]
    dil = ATTN_DILATIONS[gi]
    first = POOL_WIDTH // PROJ_TN + gi
    norm_input = norm_gain is not None
    x_spec = pl.BlockSpec((None, SEQ, D_MODEL), lambda b: (b, 0, 0))
    in_specs = [x_spec] + ([_const_spec((1, D_MODEL))] if norm_input else [])
    in_specs += [_const_spec((D_MODEL, PROJ_TN), (0, first + 3 * which)) for which in range(3)]
    out_specs = [pl.BlockSpec((None, 3 * HEADS_PER_GROUP, SEQ, HEAD_DIM), lambda b: (b, 0, 0, 0))]
    out_shape = [jax.ShapeDtypeStruct((batch, 3 * HEADS_PER_GROUP, SEQ, HEAD_DIM), _BF16)]
    if norm_input:
        out_specs.append(x_spec)
        out_shape.append(jax.ShapeDtypeStruct((batch, SEQ, D_MODEL), _BF16))
    args = (x3,) + ((norm_gain.reshape(1, D_MODEL),) if norm_input else ()) + (w_in, w_in, w_in)
    outs = pl.pallas_call(
        functools.partial(_qkv_kernel, dil=dil, norm_input=norm_input),
        grid=(batch,),
        in_specs=in_specs,
        out_specs=out_specs,
        out_shape=out_shape,
        scratch_shapes=[] if dil == 1 else [pltpu.VMEM((SEQ, D_MODEL), _BF16)],
        compiler_params=pltpu.CompilerParams(dimension_semantics=("parallel",),
                                             vmem_limit_bytes=VMEM_LIMIT),
        name=f"qkv{gi}",
    )(*args)
    return outs if norm_input else outs[0]


def _t5_buckets_np(rel):
    n = -rel
    half = N_BUCKETS // 2
    ret = (n < 0).astype(np.int32) * half
    n = np.abs(n)
    max_exact = half // 2
    large = max_exact + (np.log(np.maximum(n, 1) / max_exact)
                         / np.log(MAX_DISTANCE / max_exact) * (half - max_exact)).astype(np.int32)
    large = np.minimum(large, half - 1)
    return (ret + np.where(n < max_exact, n, large)).astype(np.int32)


def _key_span(dil):
    return min(Q_TILE + 2 * HALF_WINDOW, SEQ // dil)


BIAS_LANES = 512


def _bias_rows(rel_bias):
    rows = []
    for gi, dil in enumerate(ATTN_DILATIONS):
        buckets = _t5_buckets_np(dil * np.arange(-HALF_WINDOW, HALF_WINDOW + 1))
        bias = rel_bias[buckets][:, gi * HEADS_PER_GROUP:(gi + 1) * HEADS_PER_GROUP].T.astype(_F32)
        rows.append(jnp.pad(bias, ((0, 0), (0, BIAS_LANES - bias.shape[1])), constant_values=NEG_INF))
    return jnp.stack(rows, axis=1)


def _tile_deltas(dil):
    return (0, HALF_WINDOW, 2 * HALF_WINDOW) if SEQ // dil > Q_TILE else (0,)


def _attn_kernel(q0_ref, k0_ref, v0_ref, q1_ref, k1_ref, v1_ref, q2_ref, k2_ref, v2_ref, e_ref, *rest, n_cast):
    cast_src, (o_ref, *rest) = rest[:n_cast], rest[n_cast:]
    cast_dst, (t0_ref, t1_ref, t2_ref, *stats) = rest[:n_cast], rest[n_cast:]
    near_o_ref, near_m_ref, near_l_ref, far_o_ref, far_m_ref, far_l_ref = stats
    _cast_blocks(cast_src, cast_dst)
    groups = ((q0_ref, k0_ref, v0_ref, t0_ref), (q1_ref, k1_ref, v1_ref, t1_ref), (q2_ref, k2_ref, v2_ref, t2_ref))

    @pl.when(pl.program_id(1) == 0)
    def _():
        for gi, (_, _, _, t_ref) in enumerate(groups):
            dil = ATTN_DILATIONS[gi]
            row = jnp.broadcast_to(e_ref[gi:gi + 1, :], (Q_TILE, BIAS_LANES))
            for vi, delta in enumerate(_tile_deltas(dil)):
                skew = pltpu.roll(row, (delta - HALF_WINDOW) % BIAS_LANES, axis=1, stride=1, stride_axis=0)
                t_ref[vi] = skew[:, :_key_span(dil)]

    def scores(u, gi):
        q_ref, k_ref, _, t_ref = groups[gi]
        dil = ATTN_DILATIONS[gi]
        sub = SEQ // dil
        span = _key_span(dil)
        blocks = sub // Q_TILE
        r = u // blocks
        m0 = (u % blocks) * Q_TILE
        start = jnp.clip(m0 - HALF_WINDOW, 0, sub - span)
        variant = (m0 - start) // HALF_WINDOW
        q = q_ref[pl.ds(pl.multiple_of(u * Q_TILE, Q_TILE), Q_TILE), :]
        k0 = pl.multiple_of(r * sub + start, HALF_WINDOW)
        k = k_ref[pl.ds(k0, span), :]
        s = lax.dot_general(q, k, (((1,), (1,)), ((), ())), preferred_element_type=_F32)
        if dil == 1:
            dest = (gi, pl.ds(pl.multiple_of(m0, Q_TILE), Q_TILE))
        elif blocks > 1:
            dest = (gi, pl.ds(m0 * dil + r, Q_TILE, stride=dil))
        else:
            dest = (None, pl.ds(r * RESIDUE_PITCH, Q_TILE, stride=1))
        return s + t_ref[variant], k0, dest

    def weights(s):
        m = jnp.max(s, axis=-1, keepdims=True)
        return m, jnp.exp((s - m).astype(_BF16))

    ones = jnp.ones((Q_TILE + 2 * HALF_WINDOW, HEAD_DIM), _BF16)

    def values(gi, k0, dest, m, p):
        span = p.shape[1]
        v = groups[gi][2][pl.ds(k0, span), :]
        ov = jnp.dot(p, jnp.concatenate([v, ones[:span]], axis=1), preferred_element_type=_F32)
        m2 = jnp.broadcast_to(m * LOG2_E, (Q_TILE, HEAD_DIM))
        slot, rows = dest
        if slot is None:
            far_o_ref[rows, :] = ov[:, :HEAD_DIM]
            far_m_ref[rows, :] = m2
            far_l_ref[rows, :] = ov[:, HEAD_DIM:]
        else:
            near_o_ref[slot, rows, :] = ov[:, :HEAD_DIM]
            near_m_ref[slot, rows, :] = m2
            near_l_ref[slot, rows, :] = ov[:, HEAD_DIM:]

    def units(it, carry):
        todo = [(it * UNITS_PER_STEP + j, gi) for j in range(UNITS_PER_STEP) for gi in range(3)]
        scored = [scores(u, gi) for u, gi in todo]
        soft = [weights(s) for s, _, _ in scored]
        for (_, gi), (_, k0, dest), (m, p) in zip(todo, scored, soft):
            values(gi, k0, dest, m, p)
        return carry

    lax.fori_loop(0, SEQ // Q_TILE // UNITS_PER_STEP, units, 0)

    far_dil = ATTN_DILATIONS[2]

    def merge(t, carry):
        rows = pl.ds(pl.multiple_of(t * Q_TILE, Q_TILE), Q_TILE)

        def far(ref):
            return jnp.concatenate([ref[pl.ds(t * (Q_TILE // far_dil) + j, far_dil, stride=RESIDUE_PITCH), :]
                                    for j in range(Q_TILE // far_dil)], axis=0)

        ms = [near_m_ref[0, rows, :], near_m_ref[1, rows, :], far(far_m_ref)]
        outs = [near_o_ref[0, rows, :], near_o_ref[1, rows, :], far(far_o_ref)]
        ls = [near_l_ref[0, rows, :], near_l_ref[1, rows, :], far(far_l_ref)]
        top = jnp.maximum(jnp.maximum(ms[0], ms[1]), ms[2])
        ws = [jnp.exp2(x - top) for x in ms]
        num = ws[0] * outs[0] + ws[1] * outs[1] + ws[2] * outs[2]
        den = ws[0] * ls[0] + ws[1] * ls[1] + ws[2] * ls[2]
        o_ref[rows, :] = (num / den).astype(o_ref.dtype)
        return carry

    lax.fori_loop(0, SEQ // Q_TILE, merge, 0, unroll=4)


def _attn_call(qkv, bias_rows, to_cast, layer):
    batch = qkv[0].shape[0]

    def head_spec(which):
        return pl.BlockSpec((None, None, SEQ, HEAD_DIM), lambda h, b: (b, which * HEADS_PER_GROUP + h, 0, 0))

    in_specs = [head_spec(which) for gi in range(3) for which in range(3)]
    in_specs.append(pl.BlockSpec((None, 3, BIAS_LANES), lambda h, b: (h, 0, 0)))
    riders = [_cast_rider(w, layer, HEADS_PER_GROUP * batch, lambda h, b: h * batch + b) for w in to_cast]
    in_specs += [r[0] for r in riders]
    scratch = [pltpu.VMEM((len(_tile_deltas(dil)), Q_TILE, _key_span(dil)), _F32) for dil in ATTN_DILATIONS]
    assert SEQ // ATTN_DILATIONS[2] == Q_TILE and all(SEQ // d > Q_TILE for d in ATTN_DILATIONS[:2])
    scratch += [pltpu.VMEM((2, SEQ, HEAD_DIM), _F32) for _ in range(3)]
    scratch += [pltpu.VMEM((ATTN_DILATIONS[2] * RESIDUE_PITCH, HEAD_DIM), _F32) for _ in range(3)]
    attn, *cast = pl.pallas_call(
        functools.partial(_attn_kernel, n_cast=len(riders)),
        grid=(HEADS_PER_GROUP, batch),
        in_specs=in_specs,
        out_specs=[pl.BlockSpec((None, SEQ, HEAD_DIM), lambda h, b: (b, 0, h))] + [r[1] for r in riders],
        out_shape=[jax.ShapeDtypeStruct((batch, SEQ, ATTN_OUT), _BF16)] + [r[2] for r in riders],
        scratch_shapes=scratch,
        compiler_params=pltpu.CompilerParams(dimension_semantics=("arbitrary", "arbitrary"),
                                             vmem_limit_bytes=VMEM_LIMIT),
        name="attn",
    )(*[qkv[gi] for gi in range(3) for _ in range(3)], bias_rows, *to_cast)
    return attn, cast


def _mix_kernel(xn_ref, pool_ref, attn_ref, h_ref, wg_ref, wa_ref, wb_ref, wo_ref, g_ref, h_out_ref, xn_out_ref):
    xn = xn_ref[...]
    gates = jnp.dot(xn, wg_ref[:, GATE_BLOCK - GATE_WIDTH:], preferred_element_type=_F32)
    a = jnp.dot(pool_ref[...], wa_ref[...], preferred_element_type=_F32)
    b = jnp.dot(attn_ref[...], wb_ref[...], preferred_element_type=_F32)
    merged = jax.nn.sigmoid(gates[:, :D_MODEL]) * a + jax.nn.sigmoid(gates[:, D_MODEL:]) * b
    h = h_ref[...] + jnp.dot(merged.astype(_BF16), wo_ref[...], preferred_element_type=_F32)
    h_out_ref[...] = h
    xn_out_ref[...] = _rms(h, g_ref[...]).astype(xn_out_ref.dtype)


def _mix_call(xn2d, pool2d, attn2d, h2d, w_in, w_a, w_b, w_o, g_next):
    rows = xn2d.shape[0]
    tm = ROW_TILE
    assert IN_WIDTH % GATE_BLOCK == 0 and GATE_BLOCK >= GATE_WIDTH
    return pl.pallas_call(
        _mix_kernel,
        grid=(rows // tm,),
        in_specs=[pl.BlockSpec((tm, D_MODEL), lambda i: (i, 0)),
                  pl.BlockSpec((tm, POOL_WIDTH), lambda i: (i, 0)),
                  pl.BlockSpec((tm, ATTN_OUT), lambda i: (i, 0)),
                  pl.BlockSpec((tm, D_MODEL), lambda i: (i, 0)),
                  _const_spec((D_MODEL, GATE_BLOCK), (0, IN_WIDTH // GATE_BLOCK - 1)),
                  _const_spec((POOL_WIDTH, D_MODEL)),
                  _const_spec((ATTN_OUT, D_MODEL)),
                  _const_spec((D_MODEL, D_MODEL)),
                  _const_spec((1, D_MODEL))],
        out_specs=[pl.BlockSpec((tm, D_MODEL), lambda i: (i, 0)),
                   pl.BlockSpec((tm, D_MODEL), lambda i: (i, 0))],
        out_shape=[jax.ShapeDtypeStruct((rows, D_MODEL), _F32),
                   jax.ShapeDtypeStruct((rows, D_MODEL), _BF16)],
        compiler_params=pltpu.CompilerParams(dimension_semantics=("parallel",),
                                             vmem_limit_bytes=VMEM_LIMIT),
        name="mix",
    )(xn2d, pool2d, attn2d, h2d, w_in, w_a, w_b, w_o, g_next.reshape(1, D_MODEL))


def _gelu_tanh(x):
    return 0.5 * x * (1.0 + jnp.tanh(np.sqrt(2.0 / np.pi).astype(np.float32) * (x + 0.044715 * (x * x * x))))


def _ffn_kernel(xp_ref, x_ref, xnx_ref, h_ref, wup_ref, cw_ref, cb_ref, wdn_ref, g_ref, *rest, final, n_cast):
    n_out = 1 if final else 2
    cast_src, out_refs, rest = rest[:n_cast], rest[n_cast:n_cast + n_out], rest[n_cast + n_out:]
    cast_dst, a_refs = rest[:n_cast], rest[n_cast:]
    _cast_blocks(cast_src, cast_dst)
    tm = x_ref.shape[0]
    tiles_per_seq = SEQ // tm
    t = pl.program_id(0) % tiles_per_seq
    x = x_ref[...]
    x_prev = jnp.where(t > 0, xp_ref[...], jnp.zeros_like(xp_ref))
    x_next = jnp.where(t < tiles_per_seq - 1, xnx_ref[...], jnp.zeros_like(xnx_ref))
    xe = jnp.concatenate([x_prev, x, x_next], axis=0)
    acc = None
    assert sum(FF_CHUNKS) == D_FF
    for c, width in enumerate(FF_CHUNKS):
        c0 = sum(FF_CHUNKS[:c])
        cols = slice(c0, c0 + width)
        gcols = slice(D_FF + c0, D_FF + c0 + width)
        a_ext = jnp.dot(xe, wup_ref[:, cols], preferred_element_type=_F32)
        gate = jnp.dot(x, wup_ref[:, gcols], preferred_element_type=_F32)
        a_mid = a_ext[HALO:HALO + tm]
        a_ref = a_refs[c]
        for j in range(width // LANES):
            a_ref[j] = a_ext[:, j * LANES:(j + 1) * LANES]
        a_prev = jnp.concatenate([a_ref[j, pl.ds(HALO - 1, tm, stride=1), :] for j in range(width // LANES)], axis=1)
        a_next = jnp.concatenate([a_ref[j, pl.ds(HALO + 1, tm, stride=1), :] for j in range(width // LANES)], axis=1)
        cw = cw_ref[:, cols]
        conv = a_prev * cw[0:1] + a_mid * cw[1:2] + a_next * cw[2:3] + cb_ref[:, cols]
        act = (_gelu_tanh(conv) * gate).astype(_BF16)
        part = jnp.dot(act, wdn_ref[cols, :], preferred_element_type=_F32)
        acc = part if acc is None else acc + part
    h = h_ref[...] + acc
    if final:
        out_refs[0][...] = _rms(h, g_ref[...])
    else:
        out_refs[0][...] = h
        out_refs[1][...] = _rms(h, g_ref[...]).astype(out_refs[1].dtype)


def _ffn_call(xn2d, h2d, w_up, conv_w, conv_b, w_down, g_next, final, to_cast=(), cast_layer=0):
    rows = xn2d.shape[0]
    tm = ROW_TILE
    per = tm // HALO
    last = rows // HALO - 1
    row_spec = pl.BlockSpec((tm, D_MODEL), lambda i: (i, 0))
    if final:
        out_specs = [row_spec]
        out_shape = [jax.ShapeDtypeStruct((rows, D_MODEL), _F32)]
    else:
        out_specs = [row_spec, row_spec]
        out_shape = [jax.ShapeDtypeStruct((rows, D_MODEL), _F32),
                     jax.ShapeDtypeStruct((rows, D_MODEL), _BF16)]
    riders = [_cast_rider(w, cast_layer, rows // tm, lambda i: i) for w in to_cast]
    outs = pl.pallas_call(
        functools.partial(_ffn_kernel, final=final, n_cast=len(riders)),
        grid=(rows // tm,),
        in_specs=[pl.BlockSpec((HALO, D_MODEL), lambda i: (jnp.maximum(i * per - 1, 0), 0)),
                  row_spec,
                  pl.BlockSpec((HALO, D_MODEL), lambda i: (jnp.minimum((i + 1) * per, last), 0)),
                  row_spec,
                  _const_spec((D_MODEL, 2 * D_FF)),
                  _const_spec((3, D_FF)),
                  _const_spec((1, D_FF)),
                  _const_spec((D_FF, D_MODEL)),
                  _const_spec((1, D_MODEL))] + [r[0] for r in riders],
        out_specs=out_specs + [r[1] for r in riders],
        out_shape=out_shape + [r[2] for r in riders],
        scratch_shapes=[pltpu.VMEM((width // LANES, tm + 2 * HALO, LANES), _F32) for width in FF_CHUNKS],
        compiler_params=pltpu.CompilerParams(dimension_semantics=("parallel",),
                                             vmem_limit_bytes=VMEM_LIMIT),
        name="ffn",
    )(xn2d, xn2d, xn2d, h2d, w_up, conv_w, conv_b.reshape(1, D_FF), w_down, g_next.reshape(1, D_MODEL), *to_cast)
    return outs[:len(out_shape)], outs[len(out_shape):]


def kernel(x, w_in, w_pool, pool_scale, w_a, w_b, w_o, norm1, norm2, w_up, conv_w, conv_b, w_down, rel_bias, norm_f):
    batch, seq, d = x.shape
    assert (seq, d) == (SEQ, D_MODEL)
    depth = w_in.shape[0]
    rows = batch * seq
    bias_rows = _bias_rows(rel_bias)
    h = x.reshape(rows, d)
    xn = None
    w_pool_bf = w_pool.astype(_BF16)
    w_in_bf = w_in[0].astype(_BF16)
    for layer in range(depth):
        if xn is None:
            qkv0, xn3 = _qkv_call(x, w_in_bf, 0, norm_gain=norm1[layer])
            xn = xn3.reshape(rows, d)
        else:
            xn3 = xn.reshape(batch, seq, d)
            qkv0 = _qkv_call(xn3, w_in_bf, 0)
        pool = _pool_call(xn3, w_in_bf, w_pool_bf, pool_scale[layer], layer)
        qkv = [qkv0] + [_qkv_call(xn3, w_in_bf, gi) for gi in (1, 2)]
        attn, (w_a_bf, w_b_bf, w_o_bf, w_up_bf, w_down_bf) = _attn_call(qkv, bias_rows, (w_a, w_b, w_o, w_up, w_down), layer)
        h, xn = _mix_call(xn, pool.reshape(rows, POOL_WIDTH), attn.reshape(rows, ATTN_OUT), h,
                          w_in_bf, w_a_bf, w_b_bf, w_o_bf, norm2[layer])
        final = layer == depth - 1
        g_next = norm_f if final else norm1[layer + 1]
        outs, cast = _ffn_call(xn, h, w_up_bf, conv_w[layer], conv_b[layer], w_down_bf, g_next, final,
                               to_cast=() if final else (w_in,), cast_layer=layer + 1)
        if final:
            return outs[0].reshape(batch, seq, d)
        h, xn = outs
        w_in_bf, = cast
```

```python
import functools

import numpy as np
import jax
import jax.numpy as jnp
from jax import lax
from jax.experimental import pallas as pl
from jax.experimental.pallas import tpu as pltpu

D_MODEL = 1024
SEQ = 2048
POOL_WINDOWS = (2, 4, 8, 16)
POOL_GROUP_DIM = 256
POOL_WIDTH = 1024
ATTN_DILATIONS = (1, 4, 16)
HALF_WINDOW = 64
HEADS_PER_GROUP = 4
N_HEADS = 12
HEAD_DIM = 128
ATTN_WIDTH = N_HEADS * HEAD_DIM
ATTN_OUT = HEADS_PER_GROUP * HEAD_DIM
NEG_INF = -1e30
N_BUCKETS = 32
MAX_DISTANCE = 1024
D_FF = 2816
EPS = 1e-6
PROJ_WIDTH = POOL_WIDTH + 3 * ATTN_WIDTH
GATE_WIDTH = 2 * D_MODEL
IN_WIDTH = PROJ_WIDTH + GATE_WIDTH
GATE_BLOCK = 2560

Q_TILE = 128
UNITS_PER_STEP = 16
LOG2_E = float(np.log2(np.e))
RESIDUE_PITCH = Q_TILE + 4
NORM_ROWS = 256
PROJ_TN = 512
ROW_TILE = 512
BF16_TILE_ROWS = 16
HALO = BF16_TILE_ROWS
FF_CHUNKS = (1536, 1280)
VMEM_LIMIT = 56 * 1024 * 1024
LANES = 128
REGROUP_TILE = 256

_F32 = jnp.float32
_BF16 = jnp.bfloat16


def _rms(x, g):
    return x * lax.rsqrt(jnp.mean(x * x, axis=-1, keepdims=True) + EPS) * g


def _const_spec(shape, index=None):
    index = tuple(index) if index is not None else (0,) * len(shape)
    return pl.BlockSpec(tuple(shape), lambda *_: index, pipeline_mode=pl.Buffered(1))


def _layer_spec(layer, shape, index=None):
    index = tuple(index) if index is not None else (0,) * len(shape)
    return pl.BlockSpec((None,) + tuple(shape), lambda *_: (layer,) + index, pipeline_mode=pl.Buffered(1))


def _cast_rider(param, layer, n_steps, step_of):
    _, rows, cols = param.shape
    n_blocks = max(n for n in range(1, n_steps + 1) if rows % n == 0 and (rows // n) % BF16_TILE_ROWS == 0)
    rb = rows // n_blocks

    def block(*g):
        return step_of(*g) * n_blocks // n_steps

    return (pl.BlockSpec((None, rb, cols), lambda *g: (layer, block(*g), 0)),
            pl.BlockSpec((rb, cols), lambda *g: (block(*g), 0)),
            jax.ShapeDtypeStruct((rows, cols), _BF16))


def _cast_blocks(src_refs, dst_refs):
    for src, dst in zip(src_refs, dst_refs, strict=True):
        dst[...] = src[...].astype(dst.dtype)


POOL_PAD = 16
POOL_EDGE = 8
assert POOL_EDGE >= max(POOL_WINDOWS) // 2 and POOL_PAD - POOL_EDGE >= max(POOL_WINDOWS) // 4


def _pool_kernel(xn_ref, w_ref, wpool_ref, pscale_ref, o_ref, *scratch):
    *u_refs, t_ref = scratch
    n = SEQ + 2 * POOL_PAD
    slabs = POOL_GROUP_DIM // LANES

    def shifted(view, start, rows, k_back, k_fwd):
        lo = view[pl.ds(start - k_back, rows, stride=1), :]
        if k_fwd == 0:
            return lo + view[start:start + rows, :]
        hi = pltpu.roll(view[start:start + rows + 8, :], rows + 8 - k_fwd, axis=0)[:rows]
        return lo + hi

    def steps(w):
        return (1, 0) if w == 1 else (w // 2, w // 2)

    zeros = jnp.zeros((POOL_PAD, LANES), _F32)
    for t in range(2):
        for s in range(slabs):
            t_ref[t, s, 0:POOL_EDGE, :] = zeros[:POOL_EDGE]
            t_ref[t, s, n - POOL_EDGE:n, :] = zeros[:POOL_EDGE]
    xn = xn_ref[...]

    def project(gi):
        u = jnp.dot(xn, w_ref[:, gi * POOL_GROUP_DIM:(gi + 1) * POOL_GROUP_DIM], preferred_element_type=_F32)
        for s in range(slabs):
            u_refs[gi][s, 0:POOL_PAD, :] = zeros
            u_refs[gi][s, n - POOL_PAD:n, :] = zeros
            u_refs[gi][s, POOL_PAD:POOL_PAD + SEQ, :] = u[:, s * LANES:(s + 1) * LANES]

    edge_row = lax.broadcasted_iota(jnp.int32, (POOL_PAD, POOL_GROUP_DIM), 0)

    def pool(gi):
        window = POOL_WINDOWS[gi]
        u_ref = u_refs[gi]
        cols = slice(gi * POOL_GROUP_DIM, (gi + 1) * POOL_GROUP_DIM)
        totals = []
        for s in range(slabs):
            src, w, slot = u_ref.at[s], 1, 0
            while 2 * w < window:
                t_ref[slot, s, pl.ds(POOL_EDGE, n - 2 * POOL_EDGE), :] = shifted(src, POOL_EDGE, n - 2 * POOL_EDGE,
                                                                              *steps(w))
                src, w, slot = t_ref.at[slot, s], 2 * w, 1 - slot
            totals.append(shifted(src, POOL_PAD, SEQ, *steps(w)))
        total = jnp.concatenate(totals, axis=1)
        u = jnp.concatenate([u_ref[s, POOL_PAD:POOL_PAD + SEQ, :] for s in range(slabs)], axis=1)
        wpool = wpool_ref[gi]
        scale = pscale_ref[:, cols]
        pooled = total * (1.0 / window) - u
        z = jnp.dot(pooled.astype(_BF16), wpool, preferred_element_type=_F32)
        o_ref[:, cols] = (z * scale).astype(o_ref.dtype)
        for r0 in (0, SEQ - POOL_PAD):
            pos = edge_row + r0
            size = (jnp.minimum(pos + window // 2, SEQ) - jnp.maximum(pos - window // 2, 0)).astype(_F32)
            pooled = total[r0:r0 + POOL_PAD] / size - u[r0:r0 + POOL_PAD]
            z = jnp.dot(pooled.astype(_BF16), wpool, preferred_element_type=_F32)
            o_ref[r0:r0 + POOL_PAD, cols] = (z * scale).astype(o_ref.dtype)

    n_groups = len(POOL_WINDOWS)
    project(0)
    for gi in range(n_groups):
        if gi + 1 < n_groups:
            project(gi + 1)
        pool(gi)


def _pool_call(xn3, w_in, w_pool, pool_scale, layer):
    batch = xn3.shape[0]
    return pl.pallas_call(
        _pool_kernel,
        grid=(batch,),
        in_specs=[pl.BlockSpec((None, SEQ, D_MODEL), lambda b: (b, 0, 0)),
                  _const_spec((D_MODEL, POOL_WIDTH)),
                  _layer_spec(layer, (len(POOL_WINDOWS), POOL_GROUP_DIM, POOL_GROUP_DIM)),
                  _const_spec((1, POOL_WIDTH))],
        out_specs=pl.BlockSpec((None, SEQ, POOL_WIDTH), lambda b: (b, 0, 0)),
        out_shape=jax.ShapeDtypeStruct((batch, SEQ, POOL_WIDTH), _BF16),
        scratch_shapes=[pltpu.VMEM((POOL_GROUP_DIM // LANES, SEQ + 2 * POOL_PAD, LANES), _F32) for _ in POOL_WINDOWS]
        + [pltpu.VMEM((2, POOL_GROUP_DIM // LANES, SEQ + 2 * POOL_PAD, LANES), _F32)],
        compiler_params=pltpu.CompilerParams(dimension_semantics=("parallel",),
                                             vmem_limit_bytes=VMEM_LIMIT),
        name="pool",
    )(xn3, w_in, w_pool, pool_scale.reshape(1, POOL_WIDTH))


def _qkv_kernel(x_ref, *refs, dil, norm_input):
    if norm_input:
        g_ref, wq_ref, wk_ref, wv_ref, o_ref, xn_ref, *scratch = refs
        for r0 in range(0, SEQ, NORM_ROWS):
            xn_ref[r0:r0 + NORM_ROWS, :] = _rms(x_ref[r0:r0 + NORM_ROWS, :], g_ref[...]).astype(xn_ref.dtype)
        x_ref = xn_ref
    else:
        wq_ref, wk_ref, wv_ref, o_ref, *scratch = refs
    if dil == 1:
        x = x_ref[...]
    else:
        xp_ref, = scratch
        sub = SEQ // dil
        sub_t = REGROUP_TILE // dil
        i = lax.broadcasted_iota(jnp.int32, (REGROUP_TILE, REGROUP_TILE), 0)
        j = lax.broadcasted_iota(jnp.int32, (REGROUP_TILE, REGROUP_TILE), 1)
        pick = (j == (i % sub_t) * dil + i // sub_t).astype(_BF16)
        for t in range(SEQ // REGROUP_TILE):
            tile = x_ref[t * REGROUP_TILE:(t + 1) * REGROUP_TILE, :]
            srt = jnp.dot(pick, tile, preferred_element_type=_F32).astype(_BF16)
            for r in range(dil):
                xp_ref[r * sub + t * sub_t:r * sub + (t + 1) * sub_t, :] = srt[r * sub_t:(r + 1) * sub_t]
        x = xp_ref[...]
    for which, w_ref in enumerate((wq_ref, wk_ref, wv_ref)):
        y = jnp.dot(x, w_ref[...], preferred_element_type=_F32)
        if which == 0:
            y = y * HEAD_DIM ** -0.5
        for hd in range(HEADS_PER_GROUP):
            o_ref[which * HEADS_PER_GROUP + hd] = y[:, hd * HEAD_DIM:(hd + 1) * HEAD_DIM].astype(o_ref.dtype)


def _qkv_call(x3, w_in, gi, norm_gain=None):
    batch = x3.shape[0]
    dil = ATTN_DILATIONS[gi]
    first = POOL_WIDTH // PROJ_TN + gi
    norm_input = norm_gain is not None
    x_spec = pl.BlockSpec((None, SEQ, D_MODEL), lambda b: (b, 0, 0))
    in_specs = [x_spec] + ([_const_spec((1, D_MODEL))] if norm_input else [])
    in_specs += [_const_spec((D_MODEL, PROJ_TN), (0, first + 3 * which)) for which in range(3)]
    out_specs = [pl.BlockSpec((None, 3 * HEADS_PER_GROUP, SEQ, HEAD_DIM), lambda b: (b, 0, 0, 0))]
    out_shape = [jax.ShapeDtypeStruct((batch, 3 * HEADS_PER_GROUP, SEQ, HEAD_DIM), _BF16)]
    if norm_input:
        out_specs.append(x_spec)
        out_shape.append(jax.ShapeDtypeStruct((batch, SEQ, D_MODEL), _BF16))
    args = (x3,) + ((norm_gain.reshape(1, D_MODEL),) if norm_input else ()) + (w_in, w_in, w_in)
    outs = pl.pallas_call(
        functools.partial(_qkv_kernel, dil=dil, norm_input=norm_input),
        grid=(batch,),
        in_specs=in_specs,
        out_specs=out_specs,
        out_shape=out_shape,
        scratch_shapes=[] if dil == 1 else [pltpu.VMEM((SEQ, D_MODEL), _BF16)],
        compiler_params=pltpu.CompilerParams(dimension_semantics=("parallel",),
                                             vmem_limit_bytes=VMEM_LIMIT),
        name=f"qkv{gi}",
    )(*args)
    return outs if norm_input else outs[0]


def _t5_buckets_np(rel):
    n = -rel
    half = N_BUCKETS // 2
    ret = (n < 0).astype(np.int32) * half
    n = np.abs(n)
    max_exact = half // 2
    large = max_exact + (np.log(np.maximum(n, 1) / max_exact)
                         / np.log(MAX_DISTANCE / max_exact) * (half - max_exact)).astype(np.int32)
    large = np.minimum(large, half - 1)
    return (ret + np.where(n < max_exact, n, large)).astype(np.int32)


def _key_span(dil):
    return min(Q_TILE + 2 * HALF_WINDOW, SEQ // dil)


BIAS_LANES = 512


def _bias_rows(rel_bias):
    rows = []
    for gi, dil in enumerate(ATTN_DILATIONS):
        buckets = _t5_buckets_np(dil * np.arange(-HALF_WINDOW, HALF_WINDOW + 1))
        bias = rel_bias[buckets][:, gi * HEADS_PER_GROUP:(gi + 1) * HEADS_PER_GROUP].T.astype(_F32)
        rows.append(jnp.pad(bias, ((0, 0), (0, BIAS_LANES - bias.shape[1])), constant_values=NEG_INF))
    return jnp.stack(rows, axis=1)


def _tile_deltas(dil):
    return (0, HALF_WINDOW, 2 * HALF_WINDOW) if SEQ // dil > Q_TILE else (0,)


def _attn_kernel(q0_ref, k0_ref, v0_ref, q1_ref, k1_ref, v1_ref, q2_ref, k2_ref, v2_ref, e_ref, *rest, n_cast):
    cast_src, (o_ref, *rest) = rest[:n_cast], rest[n_cast:]
    cast_dst, (t0_ref, t1_ref, t2_ref, *stats) = rest[:n_cast], rest[n_cast:]
    near_o_ref, near_m_ref, near_l_ref, far_o_ref, far_m_ref, far_l_ref = stats
    _cast_blocks(cast_src, cast_dst)
    groups = ((q0_ref, k0_ref, v0_ref, t0_ref), (q1_ref, k1_ref, v1_ref, t1_ref), (q2_ref, k2_ref, v2_ref, t2_ref))

    @pl.when(pl.program_id(1) == 0)
    def _():
        for gi, (_, _, _, t_ref) in enumerate(groups):
            dil = ATTN_DILATIONS[gi]
            row = jnp.broadcast_to(e_ref[gi:gi + 1, :], (Q_TILE, BIAS_LANES))
            for vi, delta in enumerate(_tile_deltas(dil)):
                skew = pltpu.roll(row, (delta - HALF_WINDOW) % BIAS_LANES, axis=1, stride=1, stride_axis=0)
                t_ref[vi] = skew[:, :_key_span(dil)]

    def scores(u, gi):
        q_ref, k_ref, _, t_ref = groups[gi]
        dil = ATTN_DILATIONS[gi]
        sub = SEQ // dil
        span = _key_span(dil)
        blocks = sub // Q_TILE
        r = u // blocks
        m0 = (u % blocks) * Q_TILE
        start = jnp.clip(m0 - HALF_WINDOW, 0, sub - span)
        variant = (m0 - start) // HALF_WINDOW
        q = q_ref[pl.ds(pl.multiple_of(u * Q_TILE, Q_TILE), Q_TILE), :]
        k0 = pl.multiple_of(r * sub + start, HALF_WINDOW)
        k = k_ref[pl.ds(k0, span), :]
        s = lax.dot_general(q, k, (((1,), (1,)), ((), ())), preferred_element_type=_F32)
        if dil == 1:
            dest = (gi, pl.ds(pl.multiple_of(m0, Q_TILE), Q_TILE))
        elif blocks > 1:
            dest = (gi, pl.ds(m0 * dil + r, Q_TILE, stride=dil))
        else:
            dest = (None, pl.ds(r * RESIDUE_PITCH, Q_TILE, stride=1))
        return s + t_ref[variant], k0, dest

    def weights(s):
        m = jnp.max(s, axis=-1, keepdims=True)
        return m, jnp.exp((s - m).astype(_BF16))

    ones = jnp.ones((Q_TILE + 2 * HALF_WINDOW, HEAD_DIM), _BF16)

    def values(gi, k0, dest, m, p):
        span = p.shape[1]
        v = groups[gi][2][pl.ds(k0, span), :]
        ov = jnp.dot(p, jnp.concatenate([v, ones[:span]], axis=1), preferred_element_type=_F32)
        m2 = jnp.broadcast_to(m * LOG2_E, (Q_TILE, HEAD_DIM))
        slot, rows = dest
        if slot is None:
            far_o_ref[rows, :] = ov[:, :HEAD_DIM]
            far_m_ref[rows, :] = m2
            far_l_ref[rows, :] = ov[:, HEAD_DIM:]
        else:
            near_o_ref[slot, rows, :] = ov[:, :HEAD_DIM]
            near_m_ref[slot, rows, :] = m2
            near_l_ref[slot, rows, :] = ov[:, HEAD_DIM:]

    def units(it, carry):
        todo = [(it * UNITS_PER_STEP + j, gi) for j in range(UNITS_PER_STEP) for gi in range(3)]
        scored = [scores(u, gi) for u, gi in todo]
        soft = [weights(s) for s, _, _ in scored]
        for (_, gi), (_, k0, dest), (m, p) in zip(todo, scored, soft):
            values(gi, k0, dest, m, p)
        return carry

    lax.fori_loop(0, SEQ // Q_TILE // UNITS_PER_STEP, units, 0)

    far_dil = ATTN_DILATIONS[2]

    def merge(t, carry):
        rows = pl.ds(pl.multiple_of(t * Q_TILE, Q_TILE), Q_TILE)

        def far(ref):
            return jnp.concatenate([ref[pl.ds(t * (Q_TILE // far_dil) + j, far_dil, stride=RESIDUE_PITCH), :]
                                    for j in range(Q_TILE // far_dil)], axis=0)

        ms = [near_m_ref[0, rows, :], near_m_ref[1, rows, :], far(far_m_ref)]
        outs = [near_o_ref[0, rows, :], near_o_ref[1, rows, :], far(far_o_ref)]
        ls = [near_l_ref[0, rows, :], near_l_ref[1, rows, :], far(far_l_ref)]
        top = jnp.maximum(jnp.maximum(ms[0], ms[1]), ms[2])
        ws = [jnp.exp2(x - top) for x in ms]
        num = ws[0] * outs[0] + ws[1] * outs[1] + ws[2] * outs[2]
        den = ws[0] * ls[0] + ws[1] * ls[1] + ws[2] * ls[2]
        o_ref[rows, :] = (num / den).astype(o_ref.dtype)
        return carry

    lax.fori_loop(0, SEQ // Q_TILE, merge, 0, unroll=4)


def _attn_call(qkv, bias_rows, to_cast, layer):
    batch = qkv[0].shape[0]

    def head_spec(which):
        return pl.BlockSpec((None, None, SEQ, HEAD_DIM), lambda h, b: (b, which * HEADS_PER_GROUP + h, 0, 0))

    in_specs = [head_spec(which) for gi in range(3) for which in range(3)]
    in_specs.append(pl.BlockSpec((None, 3, BIAS_LANES), lambda h, b: (h, 0, 0)))
    riders = [_cast_rider(w, layer, HEADS_PER_GROUP * batch, lambda h, b: h * batch + b) for w in to_cast]
    in_specs += [r[0] for r in riders]
    scratch = [pltpu.VMEM((len(_tile_deltas(dil)), Q_TILE, _key_span(dil)), _F32) for dil in ATTN_DILATIONS]
    assert SEQ // ATTN_DILATIONS[2] == Q_TILE and all(SEQ // d > Q_TILE for d in ATTN_DILATIONS[:2])
    scratch += [pltpu.VMEM((2, SEQ, HEAD_DIM), _F32) for _ in range(3)]
    scratch += [pltpu.VMEM((ATTN_DILATIONS[2] * RESIDUE_PITCH, HEAD_DIM), _F32) for _ in range(3)]
    attn, *cast = pl.pallas_call(
        functools.partial(_attn_kernel, n_cast=len(riders)),
        grid=(HEADS_PER_GROUP, batch),
        in_specs=in_specs,
        out_specs=[pl.BlockSpec((None, SEQ, HEAD_DIM), lambda h, b: (b, 0, h))] + [r[1] for r in riders],
        out_shape=[jax.ShapeDtypeStruct((batch, SEQ, ATTN_OUT), _BF16)] + [r[2] for r in riders],
        scratch_shapes=scratch,
        compiler_params=pltpu.CompilerParams(dimension_semantics=("arbitrary", "arbitrary"),
                                             vmem_limit_bytes=VMEM_LIMIT),
        name="attn",
    )(*[qkv[gi] for gi in range(3) for _ in range(3)], bias_rows, *to_cast)
    return attn, cast


def _mix_kernel(xn_ref, pool_ref, attn_ref, h_ref, wg_ref, wa_ref, wb_ref, wo_ref, g_ref, h_out_ref, xn_out_ref):
    xn = xn_ref[...]
    gates = jnp.dot(xn, wg_ref[:, GATE_BLOCK - GATE_WIDTH:], preferred_element_type=_F32)
    a = jnp.dot(pool_ref[...], wa_ref[...], preferred_element_type=_F32)
    b = jnp.dot(attn_ref[...], wb_ref[...], preferred_element_type=_F32)
    merged = jax.nn.sigmoid(gates[:, :D_MODEL]) * a + jax.nn.sigmoid(gates[:, D_MODEL:]) * b
    h = h_ref[...] + jnp.dot(merged.astype(_BF16), wo_ref[...], preferred_element_type=_F32)
    h_out_ref[...] = h
    xn_out_ref[...] = _rms(h, g_ref[...]).astype(xn_out_ref.dtype)


def _mix_call(xn2d, pool2d, attn2d, h2d, w_in, w_a, w_b, w_o, g_next):
    rows = xn2d.shape[0]
    tm = ROW_TILE
    assert IN_WIDTH % GATE_BLOCK == 0 and GATE_BLOCK >= GATE_WIDTH
    return pl.pallas_call(
        _mix_kernel,
        grid=(rows // tm,),
        in_specs=[pl.BlockSpec((tm, D_MODEL), lambda i: (i, 0)),
                  pl.BlockSpec((tm, POOL_WIDTH), lambda i: (i, 0)),
                  pl.BlockSpec((tm, ATTN_OUT), lambda i: (i, 0)),
                  pl.BlockSpec((tm, D_MODEL), lambda i: (i, 0)),
                  _const_spec((D_MODEL, GATE_BLOCK), (0, IN_WIDTH // GATE_BLOCK - 1)),
                  _const_spec((POOL_WIDTH, D_MODEL)),
                  _const_spec((ATTN_OUT, D_MODEL)),
                  _const_spec((D_MODEL, D_MODEL)),
                  _const_spec((1, D_MODEL))],
        out_specs=[pl.BlockSpec((tm, D_MODEL), lambda i: (i, 0)),
                   pl.BlockSpec((tm, D_MODEL), lambda i: (i, 0))],
        out_shape=[jax.ShapeDtypeStruct((rows, D_MODEL), _F32),
                   jax.ShapeDtypeStruct((rows, D_MODEL), _BF16)],
        compiler_params=pltpu.CompilerParams(dimension_semantics=("parallel",),
                                             vmem_limit_bytes=VMEM_LIMIT),
        name="mix",
    )(xn2d, pool2d, attn2d, h2d, w_in, w_a, w_b, w_o, g_next.reshape(1, D_MODEL))


def _gelu_tanh(x):
    return 0.5 * x * (1.0 + jnp.tanh(np.sqrt(2.0 / np.pi).astype(np.float32) * (x + 0.044715 * (x * x * x))))


def _ffn_kernel(xp_ref, x_ref, xnx_ref, h_ref, wup_ref, cw_ref, cb_ref, wdn_ref, g_ref, *rest, final, n_cast):
    n_out = 1 if final else 2
    cast_src, out_refs, rest = rest[:n_cast], rest[n_cast:n_cast + n_out], rest[n_cast + n_out:]
    cast_dst, a_refs = rest[:n_cast], rest[n_cast:]
    _cast_blocks(cast_src, cast_dst)
    tm = x_ref.shape[0]
    tiles_per_seq = SEQ // tm
    t = pl.program_id(0) % tiles_per_seq
    x = x_ref[...]
    x_prev = jnp.where(t > 0, xp_ref[...], jnp.zeros_like(xp_ref))
    x_next = jnp.where(t < tiles_per_seq - 1, xnx_ref[...], jnp.zeros_like(xnx_ref))
    xe = jnp.concatenate([x_prev, x, x_next], axis=0)
    acc = None
    assert sum(FF_CHUNKS) == D_FF
    for c, width in enumerate(FF_CHUNKS):
        c0 = sum(FF_CHUNKS[:c])
        cols = slice(c0, c0 + width)
        gcols = slice(D_FF + c0, D_FF + c0 + width)
        a_ext = jnp.dot(xe, wup_ref[:, cols], preferred_element_type=_F32)
        gate = jnp.dot(x, wup_ref[:, gcols], preferred_element_type=_F32)
        a_mid = a_ext[HALO:HALO + tm]
        a_ref = a_refs[c]
        for j in range(width // LANES):
            a_ref[j] = a_ext[:, j * LANES:(j + 1) * LANES]
        a_prev = jnp.concatenate([a_ref[j, pl.ds(HALO - 1, tm, stride=1), :] for j in range(width // LANES)], axis=1)
        a_next = jnp.concatenate([a_ref[j, pl.ds(HALO + 1, tm, stride=1), :] for j in range(width // LANES)], axis=1)
        cw = cw_ref[:, cols]
        conv = a_prev * cw[0:1] + a_mid * cw[1:2] + a_next * cw[2:3] + cb_ref[:, cols]
        act = (_gelu_tanh(conv) * gate).astype(_BF16)
        part = jnp.dot(act, wdn_ref[cols, :], preferred_element_type=_F32)
        acc = part if acc is None else acc + part
    h = h_ref[...] + acc
    if final:
        out_refs[0][...] = _rms(h, g_ref[...])
    else:
        out_refs[0][...] = h
        out_refs[1][...] = _rms(h, g_ref[...]).astype(out_refs[1].dtype)


def _ffn_call(xn2d, h2d, w_up, conv_w, conv_b, w_down, g_next, final, to_cast=(), cast_layer=0):
    rows = xn2d.shape[0]
    tm = ROW_TILE
    per = tm // HALO
    last = rows // HALO - 1
    row_spec = pl.BlockSpec((tm, D_MODEL), lambda i: (i, 0))
    if final:
        out_specs = [row_spec]
        out_shape = [jax.ShapeDtypeStruct((rows, D_MODEL), _F32)]
    else:
        out_specs = [row_spec, row_spec]
        out_shape = [jax.ShapeDtypeStruct((rows, D_MODEL), _F32),
                     jax.ShapeDtypeStruct((rows, D_MODEL), _BF16)]
    riders = [_cast_rider(w, cast_layer, rows // tm, lambda i: i) for w in to_cast]
    outs = pl.pallas_call(
        functools.partial(_ffn_kernel, final=final, n_cast=len(riders)),
        grid=(rows // tm,),
        in_specs=[pl.BlockSpec((HALO, D_MODEL), lambda i: (jnp.maximum(i * per - 1, 0), 0)),
                  row_spec,
                  pl.BlockSpec((HALO, D_MODEL), lambda i: (jnp.minimum((i + 1) * per, last), 0)),
                  row_spec,
                  _const_spec((D_MODEL, 2 * D_FF)),
                  _const_spec((3, D_FF)),
                  _const_spec((1, D_FF)),
                  _const_spec((D_FF, D_MODEL)),
                  _const_spec((1, D_MODEL))] + [r[0] for r in riders],
        out_specs=out_specs + [r[1] for r in riders],
        out_shape=out_shape + [r[2] for r in riders],
        scratch_shapes=[pltpu.VMEM((width // LANES, tm + 2 * HALO, LANES), _F32) for width in FF_CHUNKS],
        compiler_params=pltpu.CompilerParams(dimension_semantics=("parallel",),
                                             vmem_limit_bytes=VMEM_LIMIT),
        name="ffn",
    )(xn2d, xn2d, xn2d, h2d, w_up, conv_w, conv_b.reshape(1, D_FF), w_down, g_next.reshape(1, D_MODEL), *to_cast)
    return outs[:len(out_shape)], outs[len(out_shape):]


def kernel(x, w_in, w_pool, pool_scale, w_a, w_b, w_o, norm1, norm2, w_up, conv_w, conv_b, w_down, rel_bias, norm_f):
    batch, seq, d = x.shape
    assert (seq, d) == (SEQ, D_MODEL)
    depth = w_in.shape[0]
    rows = batch * seq
    bias_rows = _bias_rows(rel_bias)
    h = x.reshape(rows, d)
    xn = None
    w_pool_bf = w_pool.astype(_BF16)
    w_in_bf = w_in[0].astype(_BF16)
    for layer in range(depth):
        if xn is None:
            qkv0, xn3 = _qkv_call(x, w_in_bf, 0, norm_gain=norm1[layer])
            xn = xn3.reshape(rows, d)
        else:
            xn3 = xn.reshape(batch, seq, d)
            qkv0 = _qkv_call(xn3, w_in_bf, 0)
        pool = _pool_call(xn3, w_in_bf, w_pool_bf, pool_scale[layer], layer)
        qkv = [qkv0] + [_qkv_call(xn3, w_in_bf, gi) for gi in (1, 2)]
        attn, (w_a_bf, w_b_bf, w_o_bf, w_up_bf, w_down_bf) = _attn_call(qkv, bias_rows, (w_a, w_b, w_o, w_up, w_down), layer)
        h, xn = _mix_call(xn, pool.reshape(rows, POOL_WIDTH), attn.reshape(rows, ATTN_OUT), h,
                          w_in_bf, w_a_bf, w_b_bf, w_o_bf, norm2[layer])
        final = layer == depth - 1
        g_next = norm_f if final else norm1[layer + 1]
        outs, cast = _ffn_call(xn, h, w_up_bf, conv_w[layer], conv_b[layer], w_down_bf, g_next, final,
                               to_cast=() if final else (w_in,), cast_layer=layer + 1)
        if final:
            return outs[0].reshape(batch, seq, d)
        h, xn = outs
        w_in_bf, = cast
```

```python
import functools

import numpy as np
import jax
import jax.numpy as jnp
from jax import lax
from jax.experimental import pallas as pl
from jax.experimental.pallas import tpu as pltpu

D_MODEL = 1024
SEQ = 2048
POOL_WINDOWS = (2, 4, 8, 16)
POOL_GROUP_DIM = 256
POOL_WIDTH = 1024
ATTN_DILATIONS = (1, 4, 16)
HALF_WINDOW = 64
HEADS_PER_GROUP = 4
N_HEADS = 12
HEAD_DIM = 128
ATTN_WIDTH = N_HEADS * HEAD_DIM
ATTN_OUT = HEADS_PER_GROUP * HEAD_DIM
NEG_INF = -1e30
N_BUCKETS = 32
MAX_DISTANCE = 1024
D_FF = 2816
EPS = 1e-6
PROJ_WIDTH = POOL_WIDTH + 3 * ATTN_WIDTH
GATE_WIDTH = 2 * D_MODEL
IN_WIDTH = PROJ_WIDTH + GATE_WIDTH
GATE_BLOCK = 2560

Q_TILE = 128
UNITS_PER_STEP = 16
LOG2_E = float(np.log2(np.e))
RESIDUE_PITCH = Q_TILE + 4
NORM_ROWS = 256
PROJ_TN = 512
ROW_TILE = 512
BF16_TILE_ROWS = 16
HALO = BF16_TILE_ROWS
FF_CHUNKS = (1536, 1280)
VMEM_LIMIT = 56 * 1024 * 1024
LANES = 128
REGROUP_TILE = 256

_F32 = jnp.float32
_BF16 = jnp.bfloat16


def _rms(x, g):
    return x * lax.rsqrt(jnp.mean(x * x, axis=-1, keepdims=True) + EPS) * g


def _const_spec(shape, index=None):
    index = tuple(index) if index is not None else (0,) * len(shape)
    return pl.BlockSpec(tuple(shape), lambda *_: index, pipeline_mode=pl.Buffered(1))


def _layer_spec(layer, shape, index=None):
    index = tuple(index) if index is not None else (0,) * len(shape)
    return pl.BlockSpec((None,) + tuple(shape), lambda *_: (layer,) + index, pipeline_mode=pl.Buffered(1))


def _cast_rider(param, layer, n_steps, step_of):
    _, rows, cols = param.shape
    n_blocks = max(n for n in range(1, n_steps + 1) if rows % n == 0 and (rows // n) % BF16_TILE_ROWS == 0)
    rb = rows // n_blocks

    def block(*g):
        return step_of(*g) * n_blocks // n_steps

    return (pl.BlockSpec((None, rb, cols), lambda *g: (layer, block(*g), 0)),
            pl.BlockSpec((rb, cols), lambda *g: (block(*g), 0)),
            jax.ShapeDtypeStruct((rows, cols), _BF16))


def _cast_blocks(src_refs, dst_refs):
    for src, dst in zip(src_refs, dst_refs, strict=True):
        dst[...] = src[...].astype(dst.dtype)


POOL_PAD = 16
POOL_EDGE = 8
assert POOL_EDGE >= max(POOL_WINDOWS) // 2 and POOL_PAD - POOL_EDGE >= max(POOL_WINDOWS) // 4


def _pool_kernel(xn_ref, w_ref, o_ref, *scratch):
    *u_refs, t_ref = scratch
    n = SEQ + 2 * POOL_PAD
    slabs = POOL_GROUP_DIM // LANES

    def shifted(view, start, rows, k_back, k_fwd):
        return view[pl.ds(start - k_back, rows, stride=1), :] + view[pl.ds(start + k_fwd, rows, stride=1), :]

    def steps(w):
        return (1, 0) if w == 1 else (w // 2, w // 2)

    zeros = jnp.zeros((POOL_PAD, LANES), _F32)
    for t in range(2):
        for s in range(slabs):
            t_ref[t, s, 0:POOL_EDGE, :] = zeros[:POOL_EDGE]
            t_ref[t, s, n - POOL_EDGE:n, :] = zeros[:POOL_EDGE]
    xn = xn_ref[...]

    def project(gi):
        u = jnp.dot(xn, w_ref[:, gi * POOL_GROUP_DIM:(gi + 1) * POOL_GROUP_DIM], preferred_element_type=_F32)
        for s in range(slabs):
            u_refs[gi][s, 0:POOL_PAD, :] = zeros
            u_refs[gi][s, n - POOL_PAD:n, :] = zeros
            u_refs[gi][s, POOL_PAD:POOL_PAD + SEQ, :] = u[:, s * LANES:(s + 1) * LANES]

    edge_row = lax.broadcasted_iota(jnp.int32, (POOL_PAD, POOL_GROUP_DIM), 0)

    def pool(gi):
        window = POOL_WINDOWS[gi]
        u_ref = u_refs[gi]
        cols = slice(gi * POOL_GROUP_DIM, (gi + 1) * POOL_GROUP_DIM)
        totals = []
        for s in range(slabs):
            src, w, slot = u_ref.at[s], 1, 0
            while 2 * w < window:
                t_ref[slot, s, pl.ds(POOL_EDGE, n - 2 * POOL_EDGE), :] = shifted(src, POOL_EDGE, n - 2 * POOL_EDGE,
                                                                              *steps(w))
                src, w, slot = t_ref.at[slot, s], 2 * w, 1 - slot
            totals.append(shifted(src, POOL_PAD, SEQ, *steps(w)))
        total = jnp.concatenate(totals, axis=1)
        u = jnp.concatenate([u_ref[s, POOL_PAD:POOL_PAD + SEQ, :] for s in range(slabs)], axis=1)
        o_ref[:, cols] = (total * (1.0 / window) - u).astype(o_ref.dtype)
        for r0 in (0, SEQ - POOL_PAD):
            pos = edge_row + r0
            size = (jnp.minimum(pos + window // 2, SEQ) - jnp.maximum(pos - window // 2, 0)).astype(_F32)
            o_ref[r0:r0 + POOL_PAD, cols] = (total[r0:r0 + POOL_PAD] / size - u[r0:r0 + POOL_PAD]).astype(o_ref.dtype)

    n_groups = len(POOL_WINDOWS)
    project(0)
    for gi in range(n_groups):
        if gi + 1 < n_groups:
            project(gi + 1)
        pool(gi)


def _pool_call(xn3, w_in):
    batch = xn3.shape[0]
    return pl.pallas_call(
        _pool_kernel,
        grid=(batch,),
        in_specs=[pl.BlockSpec((None, SEQ, D_MODEL), lambda b: (b, 0, 0)),
                  _const_spec((D_MODEL, POOL_WIDTH))],
        out_specs=pl.BlockSpec((None, SEQ, POOL_WIDTH), lambda b: (b, 0, 0)),
        out_shape=jax.ShapeDtypeStruct((batch, SEQ, POOL_WIDTH), _BF16),
        scratch_shapes=[pltpu.VMEM((POOL_GROUP_DIM // LANES, SEQ + 2 * POOL_PAD, LANES), _F32) for _ in POOL_WINDOWS]
        + [pltpu.VMEM((2, POOL_GROUP_DIM // LANES, SEQ + 2 * POOL_PAD, LANES), _F32)],
        compiler_params=pltpu.CompilerParams(dimension_semantics=("parallel",),
                                             vmem_limit_bytes=VMEM_LIMIT),
        name="pool",
    )(xn3, w_in)


def _qkv_kernel(x_ref, *refs, dil, norm_input):
    if norm_input:
        g_ref, wq_ref, wk_ref, wv_ref, o_ref, xn_ref, *scratch = refs
        for r0 in range(0, SEQ, NORM_ROWS):
            xn_ref[r0:r0 + NORM_ROWS, :] = _rms(x_ref[r0:r0 + NORM_ROWS, :], g_ref[...]).astype(xn_ref.dtype)
        x_ref = xn_ref
    else:
        wq_ref, wk_ref, wv_ref, o_ref, *scratch = refs
    if dil == 1:
        x = x_ref[...]
    else:
        xp_ref, = scratch
        sub = SEQ // dil
        sub_t = REGROUP_TILE // dil
        i = lax.broadcasted_iota(jnp.int32, (REGROUP_TILE, REGROUP_TILE), 0)
        j = lax.broadcasted_iota(jnp.int32, (REGROUP_TILE, REGROUP_TILE), 1)
        pick = (j == (i % sub_t) * dil + i // sub_t).astype(_BF16)
        for t in range(SEQ // REGROUP_TILE):
            tile = x_ref[t * REGROUP_TILE:(t + 1) * REGROUP_TILE, :]
            srt = jnp.dot(pick, tile, preferred_element_type=_F32).astype(_BF16)
            for r in range(dil):
                xp_ref[r * sub + t * sub_t:r * sub + (t + 1) * sub_t, :] = srt[r * sub_t:(r + 1) * sub_t]
        x = xp_ref[...]
    for which, w_ref in enumerate((wq_ref, wk_ref, wv_ref)):
        y = jnp.dot(x, w_ref[...], preferred_element_type=_F32)
        if which == 0:
            y = y * HEAD_DIM ** -0.5
        for hd in range(HEADS_PER_GROUP):
            o_ref[which * HEADS_PER_GROUP + hd] = y[:, hd * HEAD_DIM:(hd + 1) * HEAD_DIM].astype(o_ref.dtype)


def _qkv_call(x3, w_in, gi, norm_gain=None):
    batch = x3.shape[0]
    dil = ATTN_DILATIONS[gi]
    first = POOL_WIDTH // PROJ_TN + gi
    norm_input = norm_gain is not None
    x_spec = pl.BlockSpec((None, SEQ, D_MODEL), lambda b: (b, 0, 0))
    in_specs = [x_spec] + ([_const_spec((1, D_MODEL))] if norm_input else [])
    in_specs += [_const_spec((D_MODEL, PROJ_TN), (0, first + 3 * which)) for which in range(3)]
    out_specs = [pl.BlockSpec((None, 3 * HEADS_PER_GROUP, SEQ, HEAD_DIM), lambda b: (b, 0, 0, 0))]
    out_shape = [jax.ShapeDtypeStruct((batch, 3 * HEADS_PER_GROUP, SEQ, HEAD_DIM), _BF16)]
    if norm_input:
        out_specs.append(x_spec)
        out_shape.append(jax.ShapeDtypeStruct((batch, SEQ, D_MODEL), _BF16))
    args = (x3,) + ((norm_gain.reshape(1, D_MODEL),) if norm_input else ()) + (w_in, w_in, w_in)
    outs = pl.pallas_call(
        functools.partial(_qkv_kernel, dil=dil, norm_input=norm_input),
        grid=(batch,),
        in_specs=in_specs,
        out_specs=out_specs,
        out_shape=out_shape,
        scratch_shapes=[] if dil == 1 else [pltpu.VMEM((SEQ, D_MODEL), _BF16)],
        compiler_params=pltpu.CompilerParams(dimension_semantics=("parallel",),
                                             vmem_limit_bytes=VMEM_LIMIT),
        name=f"qkv{gi}",
    )(*args)
    return outs if norm_input else outs[0]


def _t5_buckets_np(rel):
    n = -rel
    half = N_BUCKETS // 2
    ret = (n < 0).astype(np.int32) * half
    n = np.abs(n)
    max_exact = half // 2
    large = max_exact + (np.log(np.maximum(n, 1) / max_exact)
                         / np.log(MAX_DISTANCE / max_exact) * (half - max_exact)).astype(np.int32)
    large = np.minimum(large, half - 1)
    return (ret + np.where(n < max_exact, n, large)).astype(np.int32)


def _key_span(dil):
    return min(Q_TILE + 2 * HALF_WINDOW, SEQ // dil)


BIAS_LANES = 512


def _bias_rows(rel_bias):
    rows = []
    for gi, dil in enumerate(ATTN_DILATIONS):
        buckets = _t5_buckets_np(dil * np.arange(-HALF_WINDOW, HALF_WINDOW + 1))
        bias = rel_bias[buckets][:, gi * HEADS_PER_GROUP:(gi + 1) * HEADS_PER_GROUP].T.astype(_F32)
        rows.append(jnp.pad(bias, ((0, 0), (0, BIAS_LANES - bias.shape[1])), constant_values=NEG_INF))
    return jnp.stack(rows, axis=1)


def _tile_deltas(dil):
    return (0, HALF_WINDOW, 2 * HALF_WINDOW) if SEQ // dil > Q_TILE else (0,)


def _attn_kernel(q0_ref, k0_ref, v0_ref, q1_ref, k1_ref, v1_ref, q2_ref, k2_ref, v2_ref, e_ref, *rest, n_cast):
    cast_src, (o_ref, *rest) = rest[:n_cast], rest[n_cast:]
    cast_dst, (t0_ref, t1_ref, t2_ref, *stats) = rest[:n_cast], rest[n_cast:]
    near_o_ref, near_m_ref, near_l_ref, far_o_ref, far_m_ref, far_l_ref = stats
    _cast_blocks(cast_src, cast_dst)
    groups = ((q0_ref, k0_ref, v0_ref, t0_ref), (q1_ref, k1_ref, v1_ref, t1_ref), (q2_ref, k2_ref, v2_ref, t2_ref))

    @pl.when(pl.program_id(1) == 0)
    def _():
        for gi, (_, _, _, t_ref) in enumerate(groups):
            dil = ATTN_DILATIONS[gi]
            row = jnp.broadcast_to(e_ref[gi:gi + 1, :], (Q_TILE, BIAS_LANES))
            for vi, delta in enumerate(_tile_deltas(dil)):
                skew = pltpu.roll(row, (delta - HALF_WINDOW) % BIAS_LANES, axis=1, stride=1, stride_axis=0)
                t_ref[vi] = skew[:, :_key_span(dil)]

    def scores(u, gi):
        q_ref, k_ref, _, t_ref = groups[gi]
        dil = ATTN_DILATIONS[gi]
        sub = SEQ // dil
        span = _key_span(dil)
        blocks = sub // Q_TILE
        r = u // blocks
        m0 = (u % blocks) * Q_TILE
        start = jnp.clip(m0 - HALF_WINDOW, 0, sub - span)
        variant = (m0 - start) // HALF_WINDOW
        q = q_ref[pl.ds(pl.multiple_of(u * Q_TILE, Q_TILE), Q_TILE), :]
        k0 = pl.multiple_of(r * sub + start, HALF_WINDOW)
        k = k_ref[pl.ds(k0, span), :]
        s = lax.dot_general(q, k, (((1,), (1,)), ((), ())), preferred_element_type=_F32)
        if dil == 1:
            dest = (gi, pl.ds(pl.multiple_of(m0, Q_TILE), Q_TILE))
        elif blocks > 1:
            dest = (gi, pl.ds(m0 * dil + r, Q_TILE, stride=dil))
        else:
            dest = (None, pl.ds(r * RESIDUE_PITCH, Q_TILE, stride=1))
        return s + t_ref[variant], k0, dest

    def weights(s):
        m = jnp.max(s, axis=-1, keepdims=True)
        return m, jnp.exp((s - m).astype(_BF16))

    ones = jnp.ones((Q_TILE + 2 * HALF_WINDOW, HEAD_DIM), _BF16)

    def values(gi, k0, dest, m, p):
        span = p.shape[1]
        v = groups[gi][2][pl.ds(k0, span), :]
        ov = jnp.dot(p, jnp.concatenate([v, ones[:span]], axis=1), preferred_element_type=_F32)
        m2 = jnp.broadcast_to(m * LOG2_E, (Q_TILE, HEAD_DIM))
        slot, rows = dest
        if slot is None:
            far_o_ref[rows, :] = ov[:, :HEAD_DIM]
            far_m_ref[rows, :] = m2
            far_l_ref[rows, :] = ov[:, HEAD_DIM:]
        else:
            near_o_ref[slot, rows, :] = ov[:, :HEAD_DIM]
            near_m_ref[slot, rows, :] = m2
            near_l_ref[slot, rows, :] = ov[:, HEAD_DIM:]

    def units(it, carry):
        todo = [(it * UNITS_PER_STEP + j, gi) for j in range(UNITS_PER_STEP) for gi in range(3)]
        scored = [scores(u, gi) for u, gi in todo]
        soft = [weights(s) for s, _, _ in scored]
        for (_, gi), (_, k0, dest), (m, p) in zip(todo, scored, soft):
            values(gi, k0, dest, m, p)
        return carry

    lax.fori_loop(0, SEQ // Q_TILE // UNITS_PER_STEP, units, 0)

    far_dil = ATTN_DILATIONS[2]

    def merge(t, carry):
        rows = pl.ds(pl.multiple_of(t * Q_TILE, Q_TILE), Q_TILE)

        def far(ref):
            return jnp.concatenate([ref[pl.ds(t * (Q_TILE // far_dil) + j, far_dil, stride=RESIDUE_PITCH), :]
                                    for j in range(Q_TILE // far_dil)], axis=0)

        ms = [near_m_ref[0, rows, :], near_m_ref[1, rows, :], far(far_m_ref)]
        outs = [near_o_ref[0, rows, :], near_o_ref[1, rows, :], far(far_o_ref)]
        ls = [near_l_ref[0, rows, :], near_l_ref[1, rows, :], far(far_l_ref)]
        top = jnp.maximum(jnp.maximum(ms[0], ms[1]), ms[2])
        ws = [jnp.exp2(x - top) for x in ms]
        num = ws[0] * outs[0] + ws[1] * outs[1] + ws[2] * outs[2]
        den = ws[0] * ls[0] + ws[1] * ls[1] + ws[2] * ls[2]
        o_ref[rows, :] = (num / den).astype(o_ref.dtype)
        return carry

    lax.fori_loop(0, SEQ // Q_TILE, merge, 0, unroll=4)


def _attn_call(qkv, bias_rows, to_cast, layer):
    batch = qkv[0].shape[0]

    def head_spec(which):
        return pl.BlockSpec((None, None, SEQ, HEAD_DIM), lambda h, b: (b, which * HEADS_PER_GROUP + h, 0, 0))

    in_specs = [head_spec(which) for gi in range(3) for which in range(3)]
    in_specs.append(pl.BlockSpec((None, 3, BIAS_LANES), lambda h, b: (h, 0, 0)))
    riders = [_cast_rider(w, layer, HEADS_PER_GROUP * batch, lambda h, b: h * batch + b) for w in to_cast]
    in_specs += [r[0] for r in riders]
    scratch = [pltpu.VMEM((len(_tile_deltas(dil)), Q_TILE, _key_span(dil)), _F32) for dil in ATTN_DILATIONS]
    assert SEQ // ATTN_DILATIONS[2] == Q_TILE and all(SEQ // d > Q_TILE for d in ATTN_DILATIONS[:2])
    scratch += [pltpu.VMEM((2, SEQ, HEAD_DIM), _F32) for _ in range(3)]
    scratch += [pltpu.VMEM((ATTN_DILATIONS[2] * RESIDUE_PITCH, HEAD_DIM), _F32) for _ in range(3)]
    attn, *cast = pl.pallas_call(
        functools.partial(_attn_kernel, n_cast=len(riders)),
        grid=(HEADS_PER_GROUP, batch),
        in_specs=in_specs,
        out_specs=[pl.BlockSpec((None, SEQ, HEAD_DIM), lambda h, b: (b, 0, h))] + [r[1] for r in riders],
        out_shape=[jax.ShapeDtypeStruct((batch, SEQ, ATTN_OUT), _BF16)] + [r[2] for r in riders],
        scratch_shapes=scratch,
        compiler_params=pltpu.CompilerParams(dimension_semantics=("arbitrary", "arbitrary"),
                                             vmem_limit_bytes=VMEM_LIMIT),
        name="attn",
    )(*[qkv[gi] for gi in range(3) for _ in range(3)], bias_rows, *to_cast)
    return attn, cast


def _mix_kernel(xn_ref, pool_ref, attn_ref, h_ref, wg_ref, wpool_ref, pscale_ref, wa_ref, wb_ref, wo_ref, g_ref,
                h_out_ref, xn_out_ref, wpa_ref):
    @pl.when(pl.program_id(0) == 0)
    def _():
        for gi in range(len(POOL_WINDOWS)):
            rows = slice(gi * POOL_GROUP_DIM, (gi + 1) * POOL_GROUP_DIM)
            scaled = (wpool_ref[gi] * pscale_ref[:, rows]).astype(_BF16)
            wpa_ref[rows, :] = jnp.dot(scaled, wa_ref[rows, :], preferred_element_type=_F32).astype(_BF16)

    xn = xn_ref[...]
    gates = jnp.dot(xn, wg_ref[:, GATE_BLOCK - GATE_WIDTH:], preferred_element_type=_F32)
    a = jnp.dot(pool_ref[...], wpa_ref[...], preferred_element_type=_F32)
    b = jnp.dot(attn_ref[...], wb_ref[...], preferred_element_type=_F32)
    merged = jax.nn.sigmoid(gates[:, :D_MODEL]) * a + jax.nn.sigmoid(gates[:, D_MODEL:]) * b
    h = h_ref[...] + jnp.dot(merged.astype(_BF16), wo_ref[...], preferred_element_type=_F32)
    h_out_ref[...] = h
    xn_out_ref[...] = _rms(h, g_ref[...]).astype(xn_out_ref.dtype)


def _mix_call(xn2d, pool2d, attn2d, h2d, w_in, w_pool, pool_scale, w_a, w_b, w_o, g_next, layer):
    rows = xn2d.shape[0]
    tm = ROW_TILE
    assert IN_WIDTH % GATE_BLOCK == 0 and GATE_BLOCK >= GATE_WIDTH
    return pl.pallas_call(
        _mix_kernel,
        grid=(rows // tm,),
        in_specs=[pl.BlockSpec((tm, D_MODEL), lambda i: (i, 0)),
                  pl.BlockSpec((tm, POOL_WIDTH), lambda i: (i, 0)),
                  pl.BlockSpec((tm, ATTN_OUT), lambda i: (i, 0)),
                  pl.BlockSpec((tm, D_MODEL), lambda i: (i, 0)),
                  _const_spec((D_MODEL, GATE_BLOCK), (0, IN_WIDTH // GATE_BLOCK - 1)),
                  _layer_spec(layer, (len(POOL_WINDOWS), POOL_GROUP_DIM, POOL_GROUP_DIM)),
                  _const_spec((1, POOL_WIDTH)),
                  _const_spec((POOL_WIDTH, D_MODEL)),
                  _const_spec((ATTN_OUT, D_MODEL)),
                  _const_spec((D_MODEL, D_MODEL)),
                  _const_spec((1, D_MODEL))],
        out_specs=[pl.BlockSpec((tm, D_MODEL), lambda i: (i, 0)),
                   pl.BlockSpec((tm, D_MODEL), lambda i: (i, 0))],
        out_shape=[jax.ShapeDtypeStruct((rows, D_MODEL), _F32),
                   jax.ShapeDtypeStruct((rows, D_MODEL), _BF16)],
        scratch_shapes=[pltpu.VMEM((POOL_WIDTH, D_MODEL), _BF16)],
        compiler_params=pltpu.CompilerParams(dimension_semantics=("arbitrary",),
                                             vmem_limit_bytes=VMEM_LIMIT),
        name="mix",
    )(xn2d, pool2d, attn2d, h2d, w_in, w_pool, pool_scale.reshape(1, POOL_WIDTH), w_a, w_b, w_o,
      g_next.reshape(1, D_MODEL))


def _gelu_tanh(x):
    return 0.5 * x * (1.0 + jnp.tanh(np.sqrt(2.0 / np.pi).astype(np.float32) * (x + 0.044715 * (x * x * x))))


def _ffn_kernel(xp_ref, x_ref, xnx_ref, h_ref, wup_ref, cw_ref, cb_ref, wdn_ref, g_ref, *rest, final, n_cast):
    n_out = 1 if final else 2
    cast_src, out_refs, rest = rest[:n_cast], rest[n_cast:n_cast + n_out], rest[n_cast + n_out:]
    cast_dst, a_refs = rest[:n_cast], rest[n_cast:]
    _cast_blocks(cast_src, cast_dst)
    tm = x_ref.shape[0]
    tiles_per_seq = SEQ // tm
    t = pl.program_id(0) % tiles_per_seq
    x = x_ref[...]
    x_prev = jnp.where(t > 0, xp_ref[...], jnp.zeros_like(xp_ref))
    x_next = jnp.where(t < tiles_per_seq - 1, xnx_ref[...], jnp.zeros_like(xnx_ref))
    xe = jnp.concatenate([x_prev, x, x_next], axis=0)
    acc = None
    assert sum(FF_CHUNKS) == D_FF
    for c, width in enumerate(FF_CHUNKS):
        c0 = sum(FF_CHUNKS[:c])
        cols = slice(c0, c0 + width)
        gcols = slice(D_FF + c0, D_FF + c0 + width)
        a_ext = jnp.dot(xe, wup_ref[:, cols], preferred_element_type=_F32)
        gate = jnp.dot(x, wup_ref[:, gcols], preferred_element_type=_F32)
        a_mid = a_ext[HALO:HALO + tm]
        a_ref = a_refs[c]
        for j in range(width // LANES):
            a_ref[j] = a_ext[:, j * LANES:(j + 1) * LANES]
        a_prev = jnp.concatenate([a_ref[j, pl.ds(HALO - 1, tm, stride=1), :] for j in range(width // LANES)], axis=1)
        a_next = jnp.concatenate([a_ref[j, pl.ds(HALO + 1, tm, stride=1), :] for j in range(width // LANES)], axis=1)
        cw = cw_ref[:, cols]
        conv = a_prev * cw[0:1] + a_mid * cw[1:2] + a_next * cw[2:3] + cb_ref[:, cols]
        act = (_gelu_tanh(conv) * gate).astype(_BF16)
        part = jnp.dot(act, wdn_ref[cols, :], preferred_element_type=_F32)
        acc = part if acc is None else acc + part
    h = h_ref[...] + acc
    if final:
        out_refs[0][...] = _rms(h, g_ref[...])
    else:
        out_refs[0][...] = h
        out_refs[1][...] = _rms(h, g_ref[...]).astype(out_refs[1].dtype)


def _ffn_call(xn2d, h2d, w_up, conv_w, conv_b, w_down, g_next, final, to_cast=(), cast_layer=0):
    rows = xn2d.shape[0]
    tm = ROW_TILE
    per = tm // HALO
    last = rows // HALO - 1
    row_spec = pl.BlockSpec((tm, D_MODEL), lambda i: (i, 0))
    if final:
        out_specs = [row_spec]
        out_shape = [jax.ShapeDtypeStruct((rows, D_MODEL), _F32)]
    else:
        out_specs = [row_spec, row_spec]
        out_shape = [jax.ShapeDtypeStruct((rows, D_MODEL), _F32),
                     jax.ShapeDtypeStruct((rows, D_MODEL), _BF16)]
    riders = [_cast_rider(w, cast_layer, rows // tm, lambda i: i) for w in to_cast]
    outs = pl.pallas_call(
        functools.partial(_ffn_kernel, final=final, n_cast=len(riders)),
        grid=(rows // tm,),
        in_specs=[pl.BlockSpec((HALO, D_MODEL), lambda i: (jnp.maximum(i * per - 1, 0), 0)),
                  row_spec,
                  pl.BlockSpec((HALO, D_MODEL), lambda i: (jnp.minimum((i + 1) * per, last), 0)),
                  row_spec,
                  _const_spec((D_MODEL, 2 * D_FF)),
                  _const_spec((3, D_FF)),
                  _const_spec((1, D_FF)),
                  _const_spec((D_FF, D_MODEL)),
                  _const_spec((1, D_MODEL))] + [r[0] for r in riders],
        out_specs=out_specs + [r[1] for r in riders],
        out_shape=out_shape + [r[2] for r in riders],
        scratch_shapes=[pltpu.VMEM((width // LANES, tm + 2 * HALO, LANES), _F32) for width in FF_CHUNKS],
        compiler_params=pltpu.CompilerParams(dimension_semantics=("parallel",),
                                             vmem_limit_bytes=VMEM_LIMIT),
        name="ffn",
    )(xn2d, xn2d, xn2d, h2d, w_up, conv_w, conv_b.reshape(1, D_FF), w_down, g_next.reshape(1, D_MODEL), *to_cast)
    return outs[:len(out_shape)], outs[len(out_shape):]


def kernel(x, w_in, w_pool, pool_scale, w_a, w_b, w_o, norm1, norm2, w_up, conv_w, conv_b, w_down, rel_bias, norm_f):
    batch, seq, d = x.shape
    assert (seq, d) == (SEQ, D_MODEL)
    depth = w_in.shape[0]
    rows = batch * seq
    bias_rows = _bias_rows(rel_bias)
    h = x.reshape(rows, d)
    xn = None
    w_in_bf = w_in[0].astype(_BF16)
    for layer in range(depth):
        if xn is None:
            qkv0, xn3 = _qkv_call(x, w_in_bf, 0, norm_gain=norm1[layer])
            xn = xn3.reshape(rows, d)
        else:
            xn3 = xn.reshape(batch, seq, d)
            qkv0 = _qkv_call(xn3, w_in_bf, 0)
        pool = _pool_call(xn3, w_in_bf)
        qkv = [qkv0] + [_qkv_call(xn3, w_in_bf, gi) for gi in (1, 2)]
        attn, (w_a_bf, w_b_bf, w_o_bf, w_up_bf, w_down_bf) = _attn_call(qkv, bias_rows, (w_a, w_b, w_o, w_up, w_down), layer)
        h, xn = _mix_call(xn, pool.reshape(rows, POOL_WIDTH), attn.reshape(rows, ATTN_OUT), h,
                          w_in_bf, w_pool, pool_scale[layer], w_a_bf, w_b_bf, w_o_bf, norm2[layer], layer)
        final = layer == depth - 1
        g_next = norm_f if final else norm1[layer + 1]
        outs, cast = _ffn_call(xn, h, w_up_bf, conv_w[layer], conv_b[layer], w_down_bf, g_next, final,
                               to_cast=() if final else (w_in,), cast_layer=layer + 1)
        if final:
            return outs[0].reshape(batch, seq, d)
        h, xn = outs
        w_in_bf, = cast
```

```python
import functools

import numpy as np
import jax
import jax.numpy as jnp
from jax import lax
from jax.experimental import pallas as pl
from jax.experimental.pallas import tpu as pltpu

D_MODEL = 1024
SEQ = 2048
POOL_WINDOWS = (2, 4, 8, 16)
POOL_GROUP_DIM = 256
POOL_WIDTH = 1024
ATTN_DILATIONS = (1, 4, 16)
HALF_WINDOW = 64
HEADS_PER_GROUP = 4
N_HEADS = 12
HEAD_DIM = 128
ATTN_WIDTH = N_HEADS * HEAD_DIM
ATTN_OUT = HEADS_PER_GROUP * HEAD_DIM
NEG_INF = -1e30
N_BUCKETS = 32
MAX_DISTANCE = 1024
D_FF = 2816
EPS = 1e-6
PROJ_WIDTH = POOL_WIDTH + 3 * ATTN_WIDTH
GATE_WIDTH = 2 * D_MODEL
IN_WIDTH = PROJ_WIDTH + GATE_WIDTH
GATE_BLOCK = 2560

Q_TILE = 128
UNITS_PER_STEP = 16
LOG2_E = float(np.log2(np.e))
RESIDUE_PITCH = Q_TILE + 4
NORM_ROWS = 256
PROJ_TN = 512
ROW_TILE = 512
BF16_TILE_ROWS = 16
HALO = BF16_TILE_ROWS
FF_CHUNKS = (1536, 1280)
VMEM_LIMIT = 56 * 1024 * 1024
LANES = 128
REGROUP_TILE = 256

_F32 = jnp.float32
_BF16 = jnp.bfloat16


def _rms(x, g):
    return x * lax.rsqrt(jnp.mean(x * x, axis=-1, keepdims=True) + EPS) * g


def _const_spec(shape, index=None):
    index = tuple(index) if index is not None else (0,) * len(shape)
    return pl.BlockSpec(tuple(shape), lambda *_: index, pipeline_mode=pl.Buffered(1))


def _layer_spec(layer, shape, index=None):
    index = tuple(index) if index is not None else (0,) * len(shape)
    return pl.BlockSpec((None,) + tuple(shape), lambda *_: (layer,) + index, pipeline_mode=pl.Buffered(1))


def _cast_rider(param, layer, n_steps, step_of):
    _, rows, cols = param.shape
    n_blocks = max(n for n in range(1, n_steps + 1) if rows % n == 0 and (rows // n) % BF16_TILE_ROWS == 0)
    rb = rows // n_blocks

    def block(*g):
        return step_of(*g) * n_blocks // n_steps

    return (pl.BlockSpec((None, rb, cols), lambda *g: (layer, block(*g), 0)),
            pl.BlockSpec((rb, cols), lambda *g: (block(*g), 0)),
            jax.ShapeDtypeStruct((rows, cols), _BF16))


def _cast_blocks(src_refs, dst_refs):
    for src, dst in zip(src_refs, dst_refs, strict=True):
        dst[...] = src[...].astype(dst.dtype)


POOL_PAD = 16
POOL_EDGE = 8
assert POOL_EDGE >= max(POOL_WINDOWS) // 2 and POOL_PAD - POOL_EDGE >= max(POOL_WINDOWS) // 4


def _pool_kernel(xn_ref, w_ref, wpool_ref, pscale_ref, o_ref, *scratch):
    *u_refs, t_ref = scratch
    n = SEQ + 2 * POOL_PAD
    slabs = POOL_GROUP_DIM // LANES

    def shifted(view, start, rows, k_back, k_fwd):
        return view[pl.ds(start - k_back, rows, stride=1), :] + view[pl.ds(start + k_fwd, rows, stride=1), :]

    def steps(w):
        return (1, 0) if w == 1 else (w // 2, w // 2)

    zeros = jnp.zeros((POOL_PAD, LANES), _F32)
    for t in range(2):
        for s in range(slabs):
            t_ref[t, s, 0:POOL_EDGE, :] = zeros[:POOL_EDGE]
            t_ref[t, s, n - POOL_EDGE:n, :] = zeros[:POOL_EDGE]
    xn = xn_ref[...]

    def project(gi):
        u = jnp.dot(xn, w_ref[:, gi * POOL_GROUP_DIM:(gi + 1) * POOL_GROUP_DIM], preferred_element_type=_F32)
        for s in range(slabs):
            u_refs[gi][s, 0:POOL_PAD, :] = zeros
            u_refs[gi][s, n - POOL_PAD:n, :] = zeros
            u_refs[gi][s, POOL_PAD:POOL_PAD + SEQ, :] = u[:, s * LANES:(s + 1) * LANES]

    edge_row = lax.broadcasted_iota(jnp.int32, (POOL_PAD, POOL_GROUP_DIM), 0)

    def pool(gi):
        window = POOL_WINDOWS[gi]
        u_ref = u_refs[gi]
        cols = slice(gi * POOL_GROUP_DIM, (gi + 1) * POOL_GROUP_DIM)
        totals = []
        for s in range(slabs):
            src, w, slot = u_ref.at[s], 1, 0
            while 2 * w < window:
                t_ref[slot, s, pl.ds(POOL_EDGE, n - 2 * POOL_EDGE), :] = shifted(src, POOL_EDGE, n - 2 * POOL_EDGE,
                                                                              *steps(w))
                src, w, slot = t_ref.at[slot, s], 2 * w, 1 - slot
            totals.append(shifted(src, POOL_PAD, SEQ, *steps(w)))
        total = jnp.concatenate(totals, axis=1)
        u = jnp.concatenate([u_ref[s, POOL_PAD:POOL_PAD + SEQ, :] for s in range(slabs)], axis=1)
        wpool = wpool_ref[gi]
        scale = pscale_ref[:, cols]
        pooled = total * (1.0 / window) - u
        z = jnp.dot(pooled.astype(_BF16), wpool, preferred_element_type=_F32)
        o_ref[:, cols] = (z * scale).astype(o_ref.dtype)
        for r0 in (0, SEQ - POOL_PAD):
            pos = edge_row + r0
            size = (jnp.minimum(pos + window // 2, SEQ) - jnp.maximum(pos - window // 2, 0)).astype(_F32)
            pooled = total[r0:r0 + POOL_PAD] / size - u[r0:r0 + POOL_PAD]
            z = jnp.dot(pooled.astype(_BF16), wpool, preferred_element_type=_F32)
            o_ref[r0:r0 + POOL_PAD, cols] = (z * scale).astype(o_ref.dtype)

    n_groups = len(POOL_WINDOWS)
    project(0)
    for gi in range(n_groups):
        if gi + 1 < n_groups:
            project(gi + 1)
        pool(gi)


def _pool_call(xn3, w_in, w_pool, pool_scale, layer):
    batch = xn3.shape[0]
    return pl.pallas_call(
        _pool_kernel,
        grid=(batch,),
        in_specs=[pl.BlockSpec((None, SEQ, D_MODEL), lambda b: (b, 0, 0)),
                  _const_spec((D_MODEL, POOL_WIDTH)),
                  _layer_spec(layer, (len(POOL_WINDOWS), POOL_GROUP_DIM, POOL_GROUP_DIM)),
                  _const_spec((1, POOL_WIDTH))],
        out_specs=pl.BlockSpec((None, SEQ, POOL_WIDTH), lambda b: (b, 0, 0)),
        out_shape=jax.ShapeDtypeStruct((batch, SEQ, POOL_WIDTH), _BF16),
        scratch_shapes=[pltpu.VMEM((POOL_GROUP_DIM // LANES, SEQ + 2 * POOL_PAD, LANES), _F32) for _ in POOL_WINDOWS]
        + [pltpu.VMEM((2, POOL_GROUP_DIM // LANES, SEQ + 2 * POOL_PAD, LANES), _F32)],
        compiler_params=pltpu.CompilerParams(dimension_semantics=("parallel",),
                                             vmem_limit_bytes=VMEM_LIMIT),
        name="pool",
    )(xn3, w_in, w_pool, pool_scale.reshape(1, POOL_WIDTH))


def _qkv_kernel(x_ref, *refs, dils, norm_input):
    if norm_input:
        g_ref, wq_ref, wk_ref, wv_ref, o_ref, xn_ref, *scratch = refs
        for r0 in range(0, SEQ, NORM_ROWS):
            xn_ref[r0:r0 + NORM_ROWS, :] = _rms(x_ref[r0:r0 + NORM_ROWS, :], g_ref[...]).astype(xn_ref.dtype)
        x_ref = xn_ref
    else:
        wq_ref, wk_ref, wv_ref, o_ref, *scratch = refs
    if dils == (1,):
        x = x_ref[...]
    else:
        xp_ref, = scratch
        for step, dil in enumerate(dils):
            @pl.when(pl.program_id(1) == step)
            def _(dil=dil):
                sub = SEQ // dil
                sub_t = REGROUP_TILE // dil
                i = lax.broadcasted_iota(jnp.int32, (REGROUP_TILE, REGROUP_TILE), 0)
                j = lax.broadcasted_iota(jnp.int32, (REGROUP_TILE, REGROUP_TILE), 1)
                pick = (j == (i % sub_t) * dil + i // sub_t).astype(_BF16)
                for t in range(SEQ // REGROUP_TILE):
                    tile = x_ref[t * REGROUP_TILE:(t + 1) * REGROUP_TILE, :]
                    srt = jnp.dot(pick, tile, preferred_element_type=_F32).astype(_BF16)
                    for r in range(dil):
                        xp_ref[r * sub + t * sub_t:r * sub + (t + 1) * sub_t, :] = srt[r * sub_t:(r + 1) * sub_t]
        x = xp_ref[...]
    for which, w_ref in enumerate((wq_ref, wk_ref, wv_ref)):
        y = jnp.dot(x, w_ref[...], preferred_element_type=_F32)
        if which == 0:
            y = y * HEAD_DIM ** -0.5
        for hd in range(HEADS_PER_GROUP):
            o_ref[which * HEADS_PER_GROUP + hd] = y[:, hd * HEAD_DIM:(hd + 1) * HEAD_DIM].astype(o_ref.dtype)


def _qkv_call(x3, w_in, gis, norm_gain=None):
    batch = x3.shape[0]
    dils = tuple(ATTN_DILATIONS[gi] for gi in gis)
    assert tuple(gis) == tuple(range(gis[0], gis[0] + len(gis)))
    first = POOL_WIDTH // PROJ_TN + gis[0]
    norm_input = norm_gain is not None
    x_spec = pl.BlockSpec((None, SEQ, D_MODEL), lambda b, g: (b, 0, 0))
    in_specs = [x_spec] + ([_const_spec((1, D_MODEL))] if norm_input else [])
    if len(gis) == 1:
        in_specs += [_const_spec((D_MODEL, PROJ_TN), (0, first + 3 * which)) for which in range(3)]
    else:
        in_specs += [pl.BlockSpec((D_MODEL, PROJ_TN), lambda b, g, which=which: (0, first + g + 3 * which))
                     for which in range(3)]
    out_specs = [pl.BlockSpec((None, None, 3 * HEADS_PER_GROUP, SEQ, HEAD_DIM), lambda b, g: (b, g, 0, 0, 0))]
    out_shape = [jax.ShapeDtypeStruct((batch, len(gis), 3 * HEADS_PER_GROUP, SEQ, HEAD_DIM), _BF16)]
    if norm_input:
        out_specs.append(x_spec)
        out_shape.append(jax.ShapeDtypeStruct((batch, SEQ, D_MODEL), _BF16))
    args = (x3,) + ((norm_gain.reshape(1, D_MODEL),) if norm_input else ()) + (w_in, w_in, w_in)
    outs = pl.pallas_call(
        functools.partial(_qkv_kernel, dils=dils, norm_input=norm_input),
        grid=(batch, len(gis)),
        in_specs=in_specs,
        out_specs=out_specs,
        out_shape=out_shape,
        scratch_shapes=[] if dils == (1,) else [pltpu.VMEM((SEQ, D_MODEL), _BF16)],
        compiler_params=pltpu.CompilerParams(dimension_semantics=("parallel", "arbitrary"),
                                             vmem_limit_bytes=VMEM_LIMIT),
        name="qkv" + "".join(str(gi) for gi in gis),
    )(*args)
    return outs if norm_input else outs[0]


def _t5_buckets_np(rel):
    n = -rel
    half = N_BUCKETS // 2
    ret = (n < 0).astype(np.int32) * half
    n = np.abs(n)
    max_exact = half // 2
    large = max_exact + (np.log(np.maximum(n, 1) / max_exact)
                         / np.log(MAX_DISTANCE / max_exact) * (half - max_exact)).astype(np.int32)
    large = np.minimum(large, half - 1)
    return (ret + np.where(n < max_exact, n, large)).astype(np.int32)


def _key_span(dil):
    return min(Q_TILE + 2 * HALF_WINDOW, SEQ // dil)


BIAS_LANES = 512


def _bias_rows(rel_bias):
    rows = []
    for gi, dil in enumerate(ATTN_DILATIONS):
        buckets = _t5_buckets_np(dil * np.arange(-HALF_WINDOW, HALF_WINDOW + 1))
        bias = rel_bias[buckets][:, gi * HEADS_PER_GROUP:(gi + 1) * HEADS_PER_GROUP].T.astype(_F32)
        rows.append(jnp.pad(bias, ((0, 0), (0, BIAS_LANES - bias.shape[1])), constant_values=NEG_INF))
    return jnp.stack(rows, axis=1)


def _tile_deltas(dil):
    return (0, HALF_WINDOW, 2 * HALF_WINDOW) if SEQ // dil > Q_TILE else (0,)


def _attn_kernel(q0_ref, k0_ref, v0_ref, q1_ref, k1_ref, v1_ref, q2_ref, k2_ref, v2_ref, e_ref, *rest, n_cast):
    cast_src, (o_ref, *rest) = rest[:n_cast], rest[n_cast:]
    cast_dst, (t0_ref, t1_ref, t2_ref, *stats) = rest[:n_cast], rest[n_cast:]
    near_o_ref, near_m_ref, near_l_ref, far_o_ref, far_m_ref, far_l_ref = stats
    _cast_blocks(cast_src, cast_dst)
    groups = ((q0_ref, k0_ref, v0_ref, t0_ref), (q1_ref, k1_ref, v1_ref, t1_ref), (q2_ref, k2_ref, v2_ref, t2_ref))

    @pl.when(pl.program_id(1) == 0)
    def _():
        for gi, (_, _, _, t_ref) in enumerate(groups):
            dil = ATTN_DILATIONS[gi]
            row = jnp.broadcast_to(e_ref[gi:gi + 1, :], (Q_TILE, BIAS_LANES))
            for vi, delta in enumerate(_tile_deltas(dil)):
                skew = pltpu.roll(row, (delta - HALF_WINDOW) % BIAS_LANES, axis=1, stride=1, stride_axis=0)
                t_ref[vi] = skew[:, :_key_span(dil)]

    def scores(u, gi):
        q_ref, k_ref, _, t_ref = groups[gi]
        dil = ATTN_DILATIONS[gi]
        sub = SEQ // dil
        span = _key_span(dil)
        blocks = sub // Q_TILE
        r = u // blocks
        m0 = (u % blocks) * Q_TILE
        start = jnp.clip(m0 - HALF_WINDOW, 0, sub - span)
        variant = (m0 - start) // HALF_WINDOW
        q = q_ref[pl.ds(pl.multiple_of(u * Q_TILE, Q_TILE), Q_TILE), :]
        k0 = pl.multiple_of(r * sub + start, HALF_WINDOW)
        k = k_ref[pl.ds(k0, span), :]
        s = lax.dot_general(q, k, (((1,), (1,)), ((), ())), preferred_element_type=_F32)
        if dil == 1:
            dest = (gi, pl.ds(pl.multiple_of(m0, Q_TILE), Q_TILE))
        elif blocks > 1:
            dest = (gi, pl.ds(m0 * dil + r, Q_TILE, stride=dil))
        else:
            dest = (None, pl.ds(r * RESIDUE_PITCH, Q_TILE, stride=1))
        return s + t_ref[variant], k0, dest

    def weights(s):
        m = jnp.max(s, axis=-1, keepdims=True)
        return m, jnp.exp((s - m).astype(_BF16))

    ones = jnp.ones((Q_TILE + 2 * HALF_WINDOW, HEAD_DIM), _BF16)

    def values(gi, k0, dest, m, p):
        span = p.shape[1]
        v = groups[gi][2][pl.ds(k0, span), :]
        ov = jnp.dot(p, jnp.concatenate([v, ones[:span]], axis=1), preferred_element_type=_F32)
        m2 = jnp.broadcast_to(m * LOG2_E, (Q_TILE, HEAD_DIM))
        slot, rows = dest
        if slot is None:
            far_o_ref[rows, :] = ov[:, :HEAD_DIM]
            far_m_ref[rows, :] = m2
            far_l_ref[rows, :] = ov[:, HEAD_DIM:]
        else:
            near_o_ref[slot, rows, :] = ov[:, :HEAD_DIM]
            near_m_ref[slot, rows, :] = m2
            near_l_ref[slot, rows, :] = ov[:, HEAD_DIM:]

    def units(it, carry):
        todo = [(it * UNITS_PER_STEP + j, gi) for j in range(UNITS_PER_STEP) for gi in range(3)]
        scored = [scores(u, gi) for u, gi in todo]
        soft = [weights(s) for s, _, _ in scored]
        for (_, gi), (_, k0, dest), (m, p) in zip(todo, scored, soft):
            values(gi, k0, dest, m, p)
        return carry

    lax.fori_loop(0, SEQ // Q_TILE // UNITS_PER_STEP, units, 0)

    far_dil = ATTN_DILATIONS[2]

    def merge(t, carry):
        rows = pl.ds(pl.multiple_of(t * Q_TILE, Q_TILE), Q_TILE)

        def far(ref):
            return jnp.concatenate([ref[pl.ds(t * (Q_TILE // far_dil) + j, far_dil, stride=RESIDUE_PITCH), :]
                                    for j in range(Q_TILE // far_dil)], axis=0)

        ms = [near_m_ref[0, rows, :], near_m_ref[1, rows, :], far(far_m_ref)]
        outs = [near_o_ref[0, rows, :], near_o_ref[1, rows, :], far(far_o_ref)]
        ls = [near_l_ref[0, rows, :], near_l_ref[1, rows, :], far(far_l_ref)]
        top = jnp.maximum(jnp.maximum(ms[0], ms[1]), ms[2])
        ws = [jnp.exp2(x - top) for x in ms]
        num = ws[0] * outs[0] + ws[1] * outs[1] + ws[2] * outs[2]
        den = ws[0] * ls[0] + ws[1] * ls[1] + ws[2] * ls[2]
        o_ref[rows, :] = (num / den).astype(o_ref.dtype)
        return carry

    lax.fori_loop(0, SEQ // Q_TILE, merge, 0, unroll=4)


def _attn_call(qkv, bias_rows, to_cast, layer):
    batch = qkv[0][0].shape[0]

    def head_spec(slot, which):
        return pl.BlockSpec((None, None, None, SEQ, HEAD_DIM),
                            lambda h, b: (b, slot, which * HEADS_PER_GROUP + h, 0, 0))

    in_specs = [head_spec(slot, which) for _, slot in qkv for which in range(3)]
    in_specs.append(pl.BlockSpec((None, 3, BIAS_LANES), lambda h, b: (h, 0, 0)))
    riders = [_cast_rider(w, layer, HEADS_PER_GROUP * batch, lambda h, b: h * batch + b) for w in to_cast]
    in_specs += [r[0] for r in riders]
    scratch = [pltpu.VMEM((len(_tile_deltas(dil)), Q_TILE, _key_span(dil)), _F32) for dil in ATTN_DILATIONS]
    assert SEQ // ATTN_DILATIONS[2] == Q_TILE and all(SEQ // d > Q_TILE for d in ATTN_DILATIONS[:2])
    scratch += [pltpu.VMEM((2, SEQ, HEAD_DIM), _F32) for _ in range(3)]
    scratch += [pltpu.VMEM((ATTN_DILATIONS[2] * RESIDUE_PITCH, HEAD_DIM), _F32) for _ in range(3)]
    attn, *cast = pl.pallas_call(
        functools.partial(_attn_kernel, n_cast=len(riders)),
        grid=(HEADS_PER_GROUP, batch),
        in_specs=in_specs,
        out_specs=[pl.BlockSpec((None, SEQ, HEAD_DIM), lambda h, b: (b, 0, h))] + [r[1] for r in riders],
        out_shape=[jax.ShapeDtypeStruct((batch, SEQ, ATTN_OUT), _BF16)] + [r[2] for r in riders],
        scratch_shapes=scratch,
        compiler_params=pltpu.CompilerParams(dimension_semantics=("arbitrary", "arbitrary"),
                                             vmem_limit_bytes=VMEM_LIMIT),
        name="attn",
    )(*[arr for arr, _ in qkv for _ in range(3)], bias_rows, *to_cast)
    return attn, cast


def _mix_kernel(xn_ref, pool_ref, attn_ref, h_ref, wg_ref, wa_ref, wb_ref, wo_ref, g_ref, h_out_ref, xn_out_ref):
    xn = xn_ref[...]
    gates = jnp.dot(xn, wg_ref[:, GATE_BLOCK - GATE_WIDTH:], preferred_element_type=_F32)
    a = jnp.dot(pool_ref[...], wa_ref[...], preferred_element_type=_F32)
    b = jnp.dot(attn_ref[...], wb_ref[...], preferred_element_type=_F32)
    merged = jax.nn.sigmoid(gates[:, :D_MODEL]) * a + jax.nn.sigmoid(gates[:, D_MODEL:]) * b
    h = h_ref[...] + jnp.dot(merged.astype(_BF16), wo_ref[...], preferred_element_type=_F32)
    h_out_ref[...] = h
    xn_out_ref[...] = _rms(h, g_ref[...]).astype(xn_out_ref.dtype)


def _mix_call(xn2d, pool2d, attn2d, h2d, w_in, w_a, w_b, w_o, g_next):
    rows = xn2d.shape[0]
    tm = ROW_TILE
    assert IN_WIDTH % GATE_BLOCK == 0 and GATE_BLOCK >= GATE_WIDTH
    return pl.pallas_call(
        _mix_kernel,
        grid=(rows // tm,),
        in_specs=[pl.BlockSpec((tm, D_MODEL), lambda i: (i, 0)),
                  pl.BlockSpec((tm, POOL_WIDTH), lambda i: (i, 0)),
                  pl.BlockSpec((tm, ATTN_OUT), lambda i: (i, 0)),
                  pl.BlockSpec((tm, D_MODEL), lambda i: (i, 0)),
                  _const_spec((D_MODEL, GATE_BLOCK), (0, IN_WIDTH // GATE_BLOCK - 1)),
                  _const_spec((POOL_WIDTH, D_MODEL)),
                  _const_spec((ATTN_OUT, D_MODEL)),
                  _const_spec((D_MODEL, D_MODEL)),
                  _const_spec((1, D_MODEL))],
        out_specs=[pl.BlockSpec((tm, D_MODEL), lambda i: (i, 0)),
                   pl.BlockSpec((tm, D_MODEL), lambda i: (i, 0))],
        out_shape=[jax.ShapeDtypeStruct((rows, D_MODEL), _F32),
                   jax.ShapeDtypeStruct((rows, D_MODEL), _BF16)],
        compiler_params=pltpu.CompilerParams(dimension_semantics=("parallel",),
                                             vmem_limit_bytes=VMEM_LIMIT),
        name="mix",
    )(xn2d, pool2d, attn2d, h2d, w_in, w_a, w_b, w_o, g_next.reshape(1, D_MODEL))


def _gelu_tanh(x):
    return 0.5 * x * (1.0 + jnp.tanh(np.sqrt(2.0 / np.pi).astype(np.float32) * (x + 0.044715 * (x * x * x))))


def _ffn_kernel(xp_ref, x_ref, xnx_ref, h_ref, wup_ref, cw_ref, cb_ref, wdn_ref, g_ref, *rest, final, n_cast):
    n_out = 1 if final else 2
    cast_src, out_refs, rest = rest[:n_cast], rest[n_cast:n_cast + n_out], rest[n_cast + n_out:]
    cast_dst, a_refs = rest[:n_cast], rest[n_cast:]
    _cast_blocks(cast_src, cast_dst)
    tm = x_ref.shape[0]
    tiles_per_seq = SEQ // tm
    t = pl.program_id(0) % tiles_per_seq
    x = x_ref[...]
    x_prev = jnp.where(t > 0, xp_ref[...], jnp.zeros_like(xp_ref))
    x_next = jnp.where(t < tiles_per_seq - 1, xnx_ref[...], jnp.zeros_like(xnx_ref))
    xe = jnp.concatenate([x_prev, x, x_next], axis=0)
    acc = None
    assert sum(FF_CHUNKS) == D_FF
    for c, width in enumerate(FF_CHUNKS):
        c0 = sum(FF_CHUNKS[:c])
        cols = slice(c0, c0 + width)
        gcols = slice(D_FF + c0, D_FF + c0 + width)
        a_ext = jnp.dot(xe, wup_ref[:, cols], preferred_element_type=_F32)
        gate = jnp.dot(x, wup_ref[:, gcols], preferred_element_type=_F32)
        a_mid = a_ext[HALO:HALO + tm]
        a_ref = a_refs[c]
        for j in range(width // LANES):
            a_ref[j] = a_ext[:, j * LANES:(j + 1) * LANES]
        a_prev = jnp.concatenate([a_ref[j, pl.ds(HALO - 1, tm, stride=1), :] for j in range(width // LANES)], axis=1)
        a_next = jnp.concatenate([a_ref[j, pl.ds(HALO + 1, tm, stride=1), :] for j in range(width // LANES)], axis=1)
        cw = cw_ref[:, cols]
        conv = a_prev * cw[0:1] + a_mid * cw[1:2] + a_next * cw[2:3] + cb_ref[:, cols]
        act = (_gelu_tanh(conv) * gate).astype(_BF16)
        part = jnp.dot(act, wdn_ref[cols, :], preferred_element_type=_F32)
        acc = part if acc is None else acc + part
    h = h_ref[...] + acc
    if final:
        out_refs[0][...] = _rms(h, g_ref[...])
    else:
        out_refs[0][...] = h
        out_refs[1][...] = _rms(h, g_ref[...]).astype(out_refs[1].dtype)


def _ffn_call(xn2d, h2d, w_up, conv_w, conv_b, w_down, g_next, final, to_cast=(), cast_layer=0):
    rows = xn2d.shape[0]
    tm = ROW_TILE
    per = tm // HALO
    last = rows // HALO - 1
    row_spec = pl.BlockSpec((tm, D_MODEL), lambda i: (i, 0))
    if final:
        out_specs = [row_spec]
        out_shape = [jax.ShapeDtypeStruct((rows, D_MODEL), _F32)]
    else:
        out_specs = [row_spec, row_spec]
        out_shape = [jax.ShapeDtypeStruct((rows, D_MODEL), _F32),
                     jax.ShapeDtypeStruct((rows, D_MODEL), _BF16)]
    riders = [_cast_rider(w, cast_layer, rows // tm, lambda i: i) for w in to_cast]
    outs = pl.pallas_call(
        functools.partial(_ffn_kernel, final=final, n_cast=len(riders)),
        grid=(rows // tm,),
        in_specs=[pl.BlockSpec((HALO, D_MODEL), lambda i: (jnp.maximum(i * per - 1, 0), 0)),
                  row_spec,
                  pl.BlockSpec((HALO, D_MODEL), lambda i: (jnp.minimum((i + 1) * per, last), 0)),
                  row_spec,
                  _const_spec((D_MODEL, 2 * D_FF)),
                  _const_spec((3, D_FF)),
                  _const_spec((1, D_FF)),
                  _const_spec((D_FF, D_MODEL)),
                  _const_spec((1, D_MODEL))] + [r[0] for r in riders],
        out_specs=out_specs + [r[1] for r in riders],
        out_shape=out_shape + [r[2] for r in riders],
        scratch_shapes=[pltpu.VMEM((width // LANES, tm + 2 * HALO, LANES), _F32) for width in FF_CHUNKS],
        compiler_params=pltpu.CompilerParams(dimension_semantics=("parallel",),
                                             vmem_limit_bytes=VMEM_LIMIT),
        name="ffn",
    )(xn2d, xn2d, xn2d, h2d, w_up, conv_w, conv_b.reshape(1, D_FF), w_down, g_next.reshape(1, D_MODEL), *to_cast)
    return outs[:len(out_shape)], outs[len(out_shape):]


def kernel(x, w_in, w_pool, pool_scale, w_a, w_b, w_o, norm1, norm2, w_up, conv_w, conv_b, w_down, rel_bias, norm_f):
    batch, seq, d = x.shape
    assert (seq, d) == (SEQ, D_MODEL)
    depth = w_in.shape[0]
    rows = batch * seq
    bias_rows = _bias_rows(rel_bias)
    h = x.reshape(rows, d)
    xn = None
    w_pool_bf = w_pool.astype(_BF16)
    w_in_bf = w_in[0].astype(_BF16)
    for layer in range(depth):
        if xn is None:
            qkv0, xn3 = _qkv_call(x, w_in_bf, (0,), norm_gain=norm1[layer])
            xn = xn3.reshape(rows, d)
        else:
            xn3 = xn.reshape(batch, seq, d)
            qkv0 = _qkv_call(xn3, w_in_bf, (0,))
        pool = _pool_call(xn3, w_in_bf, w_pool_bf, pool_scale[layer], layer)
        qkv12 = _qkv_call(xn3, w_in_bf, (1, 2))
        qkv = [(qkv0, 0), (qkv12, 0), (qkv12, 1)]
        attn, (w_a_bf, w_b_bf, w_o_bf, w_up_bf, w_down_bf) = _attn_call(qkv, bias_rows, (w_a, w_b, w_o, w_up, w_down), layer)
        h, xn = _mix_call(xn, pool.reshape(rows, POOL_WIDTH), attn.reshape(rows, ATTN_OUT), h,
                          w_in_bf, w_a_bf, w_b_bf, w_o_bf, norm2[layer])
        final = layer == depth - 1
        g_next = norm_f if final else norm1[layer + 1]
        outs, cast = _ffn_call(xn, h, w_up_bf, conv_w[layer], conv_b[layer], w_down_bf, g_next, final,
                               to_cast=() if final else (w_in,), cast_layer=layer + 1)
        if final:
            return outs[0].reshape(batch, seq, d)
        h, xn = outs
        w_in_bf, = cast
```

```python
import functools

import numpy as np
import jax
import jax.numpy as jnp
from jax import lax
from jax.experimental import pallas as pl
from jax.experimental.pallas import tpu as pltpu

D_MODEL = 1024
SEQ = 2048
POOL_WINDOWS = (2, 4, 8, 16)
POOL_GROUP_DIM = 256
POOL_WIDTH = 1024
ATTN_DILATIONS = (1, 4, 16)
HALF_WINDOW = 64
HEADS_PER_GROUP = 4
N_HEADS = 12
HEAD_DIM = 128
ATTN_WIDTH = N_HEADS * HEAD_DIM
ATTN_OUT = HEADS_PER_GROUP * HEAD_DIM
NEG_INF = -1e30
N_BUCKETS = 32
MAX_DISTANCE = 1024
D_FF = 2816
EPS = 1e-6
PROJ_WIDTH = POOL_WIDTH + 3 * ATTN_WIDTH
GATE_WIDTH = 2 * D_MODEL
IN_WIDTH = PROJ_WIDTH + GATE_WIDTH
GATE_BLOCK = 2560

Q_TILE = 128
UNITS_PER_STEP = 16
LOG2_E = float(np.log2(np.e))
RESIDUE_PITCH = Q_TILE + 4
NORM_ROWS = 256
PROJ_TN = 512
ROW_TILE = 512
BF16_TILE_ROWS = 16
HALO = BF16_TILE_ROWS
FF_CHUNKS = (1536, 1280)
VMEM_LIMIT = 56 * 1024 * 1024
LANES = 128
REGROUP_TILE = 256

_F32 = jnp.float32
_BF16 = jnp.bfloat16


def _rms(x, g):
    return x * lax.rsqrt(jnp.mean(x * x, axis=-1, keepdims=True) + EPS) * g


def _const_spec(shape, index=None):
    index = tuple(index) if index is not None else (0,) * len(shape)
    return pl.BlockSpec(tuple(shape), lambda *_: index, pipeline_mode=pl.Buffered(1))


def _layer_spec(layer, shape, index=None):
    index = tuple(index) if index is not None else (0,) * len(shape)
    return pl.BlockSpec((None,) + tuple(shape), lambda *_: (layer,) + index, pipeline_mode=pl.Buffered(1))


def _cast_rider(param, layer, n_steps, step_of):
    _, rows, cols = param.shape
    n_blocks = max(n for n in range(1, n_steps + 1) if rows % n == 0 and (rows // n) % BF16_TILE_ROWS == 0)
    rb = rows // n_blocks

    def block(*g):
        return step_of(*g) * n_blocks // n_steps

    return (pl.BlockSpec((None, rb, cols), lambda *g: (layer, block(*g), 0)),
            pl.BlockSpec((rb, cols), lambda *g: (block(*g), 0)),
            jax.ShapeDtypeStruct((rows, cols), _BF16))


def _cast_blocks(src_refs, dst_refs):
    for src, dst in zip(src_refs, dst_refs, strict=True):
        dst[...] = src[...].astype(dst.dtype)


POOL_PAD = 16
POOL_EDGE = 8
assert POOL_EDGE >= max(POOL_WINDOWS) // 2 and POOL_PAD - POOL_EDGE >= max(POOL_WINDOWS) // 4


def _pool_kernel(xn_ref, w_ref, wpool_ref, pscale_ref, o_ref, *scratch):
    *u_refs, t_ref = scratch
    n = SEQ + 2 * POOL_PAD
    slabs = POOL_GROUP_DIM // LANES

    def shifted(view, start, rows, k_back, k_fwd):
        return view[pl.ds(start - k_back, rows, stride=1), :] + view[pl.ds(start + k_fwd, rows, stride=1), :]

    def steps(w):
        return (1, 0) if w == 1 else (w // 2, w // 2)

    zeros = jnp.zeros((POOL_PAD, LANES), _F32)
    for t in range(2):
        for s in range(slabs):
            t_ref[t, s, 0:POOL_EDGE, :] = zeros[:POOL_EDGE]
            t_ref[t, s, n - POOL_EDGE:n, :] = zeros[:POOL_EDGE]
    xn = xn_ref[...]

    def project(gi):
        u = jnp.dot(xn, w_ref[:, gi * POOL_GROUP_DIM:(gi + 1) * POOL_GROUP_DIM], preferred_element_type=_F32)
        for s in range(slabs):
            u_refs[gi][s, 0:POOL_PAD, :] = zeros
            u_refs[gi][s, n - POOL_PAD:n, :] = zeros
            u_refs[gi][s, POOL_PAD:POOL_PAD + SEQ, :] = u[:, s * LANES:(s + 1) * LANES]

    edge_row = lax.broadcasted_iota(jnp.int32, (POOL_PAD, POOL_GROUP_DIM), 0)

    def pool(gi):
        window = POOL_WINDOWS[gi]
        u_ref = u_refs[gi]
        cols = slice(gi * POOL_GROUP_DIM, (gi + 1) * POOL_GROUP_DIM)
        totals = []
        for s in range(slabs):
            src, w, slot = u_ref.at[s], 1, 0
            while 2 * w < window:
                t_ref[slot, s, pl.ds(POOL_EDGE, n - 2 * POOL_EDGE), :] = shifted(src, POOL_EDGE, n - 2 * POOL_EDGE,
                                                                              *steps(w))
                src, w, slot = t_ref.at[slot, s], 2 * w, 1 - slot
            totals.append(shifted(src, POOL_PAD, SEQ, *steps(w)))
        total = jnp.concatenate(totals, axis=1)
        u = jnp.concatenate([u_ref[s, POOL_PAD:POOL_PAD + SEQ, :] for s in range(slabs)], axis=1)
        wpool = wpool_ref[gi]
        scale = pscale_ref[:, cols]
        pooled = total * (1.0 / window) - u
        z = jnp.dot(pooled.astype(_BF16), wpool, preferred_element_type=_F32)
        o_ref[:, cols] = (z * scale).astype(o_ref.dtype)
        for r0 in (0, SEQ - POOL_PAD):
            pos = edge_row + r0
            size = (jnp.minimum(pos + window // 2, SEQ) - jnp.maximum(pos - window // 2, 0)).astype(_F32)
            pooled = total[r0:r0 + POOL_PAD] / size - u[r0:r0 + POOL_PAD]
            z = jnp.dot(pooled.astype(_BF16), wpool, preferred_element_type=_F32)
            o_ref[r0:r0 + POOL_PAD, cols] = (z * scale).astype(o_ref.dtype)

    n_groups = len(POOL_WINDOWS)
    project(0)
    for gi in range(n_groups):
        if gi + 1 < n_groups:
            project(gi + 1)
        pool(gi)


def _pool_call(xn3, w_in, w_pool, pool_scale, layer):
    batch = xn3.shape[0]
    return pl.pallas_call(
        _pool_kernel,
        grid=(batch,),
        in_specs=[pl.BlockSpec((None, SEQ, D_MODEL), lambda b: (b, 0, 0)),
                  _const_spec((D_MODEL, POOL_WIDTH)),
                  _layer_spec(layer, (len(POOL_WINDOWS), POOL_GROUP_DIM, POOL_GROUP_DIM)),
                  _const_spec((1, POOL_WIDTH))],
        out_specs=pl.BlockSpec((None, SEQ, POOL_WIDTH), lambda b: (b, 0, 0)),
        out_shape=jax.ShapeDtypeStruct((batch, SEQ, POOL_WIDTH), _BF16),
        scratch_shapes=[pltpu.VMEM((POOL_GROUP_DIM // LANES, SEQ + 2 * POOL_PAD, LANES), _F32) for _ in POOL_WINDOWS]
        + [pltpu.VMEM((2, POOL_GROUP_DIM // LANES, SEQ + 2 * POOL_PAD, LANES), _F32)],
        compiler_params=pltpu.CompilerParams(dimension_semantics=("parallel",),
                                             vmem_limit_bytes=VMEM_LIMIT),
        name="pool",
    )(xn3, w_in, w_pool, pool_scale.reshape(1, POOL_WIDTH))


def _qkv_kernel(x_ref, *refs, dils, norm_input):
    if norm_input:
        g_ref, wq_ref, wk_ref, wv_ref, o_ref, xn_ref, *scratch = refs
        for r0 in range(0, SEQ, NORM_ROWS):
            xn_ref[r0:r0 + NORM_ROWS, :] = _rms(x_ref[r0:r0 + NORM_ROWS, :], g_ref[...]).astype(xn_ref.dtype)
        x_ref = xn_ref
    else:
        wq_ref, wk_ref, wv_ref, o_ref, *scratch = refs
    if dils == (1,):
        x = x_ref[...]
    else:
        xp_ref, = scratch
        for step, dil in enumerate(dils):
            @pl.when(pl.program_id(1) == step)
            def _(dil=dil):
                if dil == 1:
                    xp_ref[...] = x_ref[...]
                    return
                sub = SEQ // dil
                sub_t = REGROUP_TILE // dil
                i = lax.broadcasted_iota(jnp.int32, (REGROUP_TILE, REGROUP_TILE), 0)
                j = lax.broadcasted_iota(jnp.int32, (REGROUP_TILE, REGROUP_TILE), 1)
                pick = (j == (i % sub_t) * dil + i // sub_t).astype(_BF16)
                for t in range(SEQ // REGROUP_TILE):
                    tile = x_ref[t * REGROUP_TILE:(t + 1) * REGROUP_TILE, :]
                    srt = jnp.dot(pick, tile, preferred_element_type=_F32).astype(_BF16)
                    for r in range(dil):
                        xp_ref[r * sub + t * sub_t:r * sub + (t + 1) * sub_t, :] = srt[r * sub_t:(r + 1) * sub_t]
        x = xp_ref[...]
    for which, w_ref in enumerate((wq_ref, wk_ref, wv_ref)):
        y = jnp.dot(x, w_ref[...], preferred_element_type=_F32)
        if which == 0:
            y = y * HEAD_DIM ** -0.5
        for hd in range(HEADS_PER_GROUP):
            o_ref[which * HEADS_PER_GROUP + hd] = y[:, hd * HEAD_DIM:(hd + 1) * HEAD_DIM].astype(o_ref.dtype)


def _qkv_call(x3, w_in, gis, norm_gain=None):
    batch = x3.shape[0]
    dils = tuple(ATTN_DILATIONS[gi] for gi in gis)
    assert tuple(gis) == tuple(range(gis[0], gis[0] + len(gis)))
    first = POOL_WIDTH // PROJ_TN + gis[0]
    norm_input = norm_gain is not None
    x_spec = pl.BlockSpec((None, SEQ, D_MODEL), lambda b, g: (b, 0, 0))
    in_specs = [x_spec] + ([_const_spec((1, D_MODEL))] if norm_input else [])
    if len(gis) == 1:
        in_specs += [_const_spec((D_MODEL, PROJ_TN), (0, first + 3 * which)) for which in range(3)]
    else:
        in_specs += [pl.BlockSpec((D_MODEL, PROJ_TN), lambda b, g, which=which: (0, first + g + 3 * which))
                     for which in range(3)]
    out_specs = [pl.BlockSpec((None, None, 3 * HEADS_PER_GROUP, SEQ, HEAD_DIM), lambda b, g: (b, g, 0, 0, 0))]
    out_shape = [jax.ShapeDtypeStruct((batch, len(gis), 3 * HEADS_PER_GROUP, SEQ, HEAD_DIM), _BF16)]
    if norm_input:
        out_specs.append(x_spec)
        out_shape.append(jax.ShapeDtypeStruct((batch, SEQ, D_MODEL), _BF16))
    args = (x3,) + ((norm_gain.reshape(1, D_MODEL),) if norm_input else ()) + (w_in, w_in, w_in)
    outs = pl.pallas_call(
        functools.partial(_qkv_kernel, dils=dils, norm_input=norm_input),
        grid=(batch, len(gis)),
        in_specs=in_specs,
        out_specs=out_specs,
        out_shape=out_shape,
        scratch_shapes=[] if dils == (1,) else [pltpu.VMEM((SEQ, D_MODEL), _BF16)],
        compiler_params=pltpu.CompilerParams(dimension_semantics=("parallel", "arbitrary"),
                                             vmem_limit_bytes=VMEM_LIMIT),
        name="qkv" + "".join(str(gi) for gi in gis),
    )(*args)
    return outs if norm_input else outs[0]


def _t5_buckets_np(rel):
    n = -rel
    half = N_BUCKETS // 2
    ret = (n < 0).astype(np.int32) * half
    n = np.abs(n)
    max_exact = half // 2
    large = max_exact + (np.log(np.maximum(n, 1) / max_exact)
                         / np.log(MAX_DISTANCE / max_exact) * (half - max_exact)).astype(np.int32)
    large = np.minimum(large, half - 1)
    return (ret + np.where(n < max_exact, n, large)).astype(np.int32)


def _key_span(dil):
    return min(Q_TILE + 2 * HALF_WINDOW, SEQ // dil)


BIAS_LANES = 512


def _bias_rows(rel_bias):
    rows = []
    for gi, dil in enumerate(ATTN_DILATIONS):
        buckets = _t5_buckets_np(dil * np.arange(-HALF_WINDOW, HALF_WINDOW + 1))
        bias = rel_bias[buckets][:, gi * HEADS_PER_GROUP:(gi + 1) * HEADS_PER_GROUP].T.astype(_F32)
        rows.append(jnp.pad(bias, ((0, 0), (0, BIAS_LANES - bias.shape[1])), constant_values=NEG_INF))
    return jnp.stack(rows, axis=1)


def _tile_deltas(dil):
    return (0, HALF_WINDOW, 2 * HALF_WINDOW) if SEQ // dil > Q_TILE else (0,)


def _attn_kernel(q0_ref, k0_ref, v0_ref, q1_ref, k1_ref, v1_ref, q2_ref, k2_ref, v2_ref, e_ref, *rest, n_cast):
    cast_src, (o_ref, *rest) = rest[:n_cast], rest[n_cast:]
    cast_dst, (t0_ref, t1_ref, t2_ref, *stats) = rest[:n_cast], rest[n_cast:]
    near_o_ref, near_m_ref, near_l_ref, far_o_ref, far_m_ref, far_l_ref = stats
    _cast_blocks(cast_src, cast_dst)
    groups = ((q0_ref, k0_ref, v0_ref, t0_ref), (q1_ref, k1_ref, v1_ref, t1_ref), (q2_ref, k2_ref, v2_ref, t2_ref))

    @pl.when(pl.program_id(1) == 0)
    def _():
        for gi, (_, _, _, t_ref) in enumerate(groups):
            dil = ATTN_DILATIONS[gi]
            row = jnp.broadcast_to(e_ref[gi:gi + 1, :], (Q_TILE, BIAS_LANES))
            for vi, delta in enumerate(_tile_deltas(dil)):
                skew = pltpu.roll(row, (delta - HALF_WINDOW) % BIAS_LANES, axis=1, stride=1, stride_axis=0)
                t_ref[vi] = skew[:, :_key_span(dil)]

    def scores(u, gi):
        q_ref, k_ref, _, t_ref = groups[gi]
        dil = ATTN_DILATIONS[gi]
        sub = SEQ // dil
        span = _key_span(dil)
        blocks = sub // Q_TILE
        r = u // blocks
        m0 = (u % blocks) * Q_TILE
        start = jnp.clip(m0 - HALF_WINDOW, 0, sub - span)
        variant = (m0 - start) // HALF_WINDOW
        q = q_ref[pl.ds(pl.multiple_of(u * Q_TILE, Q_TILE), Q_TILE), :]
        k0 = pl.multiple_of(r * sub + start, HALF_WINDOW)
        k = k_ref[pl.ds(k0, span), :]
        s = lax.dot_general(q, k, (((1,), (1,)), ((), ())), preferred_element_type=_F32)
        if dil == 1:
            dest = (gi, pl.ds(pl.multiple_of(m0, Q_TILE), Q_TILE))
        elif blocks > 1:
            dest = (gi, pl.ds(m0 * dil + r, Q_TILE, stride=dil))
        else:
            dest = (None, pl.ds(r * RESIDUE_PITCH, Q_TILE, stride=1))
        return s + t_ref[variant], k0, dest

    def weights(s):
        m = jnp.max(s, axis=-1, keepdims=True)
        return m, jnp.exp((s - m).astype(_BF16))

    ones = jnp.ones((Q_TILE + 2 * HALF_WINDOW, HEAD_DIM), _BF16)

    def values(gi, k0, dest, m, p):
        span = p.shape[1]
        v = groups[gi][2][pl.ds(k0, span), :]
        ov = jnp.dot(p, jnp.concatenate([v, ones[:span]], axis=1), preferred_element_type=_F32)
        m2 = jnp.broadcast_to(m * LOG2_E, (Q_TILE, HEAD_DIM))
        slot, rows = dest
        if slot is None:
            far_o_ref[rows, :] = ov[:, :HEAD_DIM]
            far_m_ref[rows, :] = m2
            far_l_ref[rows, :] = ov[:, HEAD_DIM:]
        else:
            near_o_ref[slot, rows, :] = ov[:, :HEAD_DIM]
            near_m_ref[slot, rows, :] = m2
            near_l_ref[slot, rows, :] = ov[:, HEAD_DIM:]

    def units(it, carry):
        todo = [(it * UNITS_PER_STEP + j, gi) for j in range(UNITS_PER_STEP) for gi in range(3)]
        scored = [scores(u, gi) for u, gi in todo]
        soft = [weights(s) for s, _, _ in scored]
        for (_, gi), (_, k0, dest), (m, p) in zip(todo, scored, soft):
            values(gi, k0, dest, m, p)
        return carry

    lax.fori_loop(0, SEQ // Q_TILE // UNITS_PER_STEP, units, 0)

    far_dil = ATTN_DILATIONS[2]

    def merge(t, carry):
        rows = pl.ds(pl.multiple_of(t * Q_TILE, Q_TILE), Q_TILE)

        def far(ref):
            return jnp.concatenate([ref[pl.ds(t * (Q_TILE // far_dil) + j, far_dil, stride=RESIDUE_PITCH), :]
                                    for j in range(Q_TILE // far_dil)], axis=0)

        ms = [near_m_ref[0, rows, :], near_m_ref[1, rows, :], far(far_m_ref)]
        outs = [near_o_ref[0, rows, :], near_o_ref[1, rows, :], far(far_o_ref)]
        ls = [near_l_ref[0, rows, :], near_l_ref[1, rows, :], far(far_l_ref)]
        top = jnp.maximum(jnp.maximum(ms[0], ms[1]), ms[2])
        ws = [jnp.exp2(x - top) for x in ms]
        num = ws[0] * outs[0] + ws[1] * outs[1] + ws[2] * outs[2]
        den = ws[0] * ls[0] + ws[1] * ls[1] + ws[2] * ls[2]
        o_ref[rows, :] = (num / den).astype(o_ref.dtype)
        return carry

    lax.fori_loop(0, SEQ // Q_TILE, merge, 0, unroll=4)


def _attn_call(qkv, bias_rows, to_cast, layer):
    batch = qkv[0][0].shape[0]

    def head_spec(slot, which):
        return pl.BlockSpec((None, None, None, SEQ, HEAD_DIM),
                            lambda h, b: (b, slot, which * HEADS_PER_GROUP + h, 0, 0))

    in_specs = [head_spec(slot, which) for _, slot in qkv for which in range(3)]
    in_specs.append(pl.BlockSpec((None, 3, BIAS_LANES), lambda h, b: (h, 0, 0)))
    riders = [_cast_rider(w, layer, HEADS_PER_GROUP * batch, lambda h, b: h * batch + b) for w in to_cast]
    in_specs += [r[0] for r in riders]
    scratch = [pltpu.VMEM((len(_tile_deltas(dil)), Q_TILE, _key_span(dil)), _F32) for dil in ATTN_DILATIONS]
    assert SEQ // ATTN_DILATIONS[2] == Q_TILE and all(SEQ // d > Q_TILE for d in ATTN_DILATIONS[:2])
    scratch += [pltpu.VMEM((2, SEQ, HEAD_DIM), _F32) for _ in range(3)]
    scratch += [pltpu.VMEM((ATTN_DILATIONS[2] * RESIDUE_PITCH, HEAD_DIM), _F32) for _ in range(3)]
    attn, *cast = pl.pallas_call(
        functools.partial(_attn_kernel, n_cast=len(riders)),
        grid=(HEADS_PER_GROUP, batch),
        in_specs=in_specs,
        out_specs=[pl.BlockSpec((None, SEQ, HEAD_DIM), lambda h, b: (b, 0, h))] + [r[1] for r in riders],
        out_shape=[jax.ShapeDtypeStruct((batch, SEQ, ATTN_OUT), _BF16)] + [r[2] for r in riders],
        scratch_shapes=scratch,
        compiler_params=pltpu.CompilerParams(dimension_semantics=("arbitrary", "arbitrary"),
                                             vmem_limit_bytes=VMEM_LIMIT),
        name="attn",
    )(*[arr for arr, _ in qkv for _ in range(3)], bias_rows, *to_cast)
    return attn, cast


def _mix_kernel(xn_ref, pool_ref, attn_ref, h_ref, wg_ref, wa_ref, wb_ref, wo_ref, g_ref, h_out_ref, xn_out_ref):
    xn = xn_ref[...]
    gates = jnp.dot(xn, wg_ref[:, GATE_BLOCK - GATE_WIDTH:], preferred_element_type=_F32)
    a = jnp.dot(pool_ref[...], wa_ref[...], preferred_element_type=_F32)
    b = jnp.dot(attn_ref[...], wb_ref[...], preferred_element_type=_F32)
    merged = jax.nn.sigmoid(gates[:, :D_MODEL]) * a + jax.nn.sigmoid(gates[:, D_MODEL:]) * b
    h = h_ref[...] + jnp.dot(merged.astype(_BF16), wo_ref[...], preferred_element_type=_F32)
    h_out_ref[...] = h
    xn_out_ref[...] = _rms(h, g_ref[...]).astype(xn_out_ref.dtype)


def _mix_call(xn2d, pool2d, attn2d, h2d, w_in, w_a, w_b, w_o, g_next):
    rows = xn2d.shape[0]
    tm = ROW_TILE
    assert IN_WIDTH % GATE_BLOCK == 0 and GATE_BLOCK >= GATE_WIDTH
    return pl.pallas_call(
        _mix_kernel,
        grid=(rows // tm,),
        in_specs=[pl.BlockSpec((tm, D_MODEL), lambda i: (i, 0)),
                  pl.BlockSpec((tm, POOL_WIDTH), lambda i: (i, 0)),
                  pl.BlockSpec((tm, ATTN_OUT), lambda i: (i, 0)),
                  pl.BlockSpec((tm, D_MODEL), lambda i: (i, 0)),
                  _const_spec((D_MODEL, GATE_BLOCK), (0, IN_WIDTH // GATE_BLOCK - 1)),
                  _const_spec((POOL_WIDTH, D_MODEL)),
                  _const_spec((ATTN_OUT, D_MODEL)),
                  _const_spec((D_MODEL, D_MODEL)),
                  _const_spec((1, D_MODEL))],
        out_specs=[pl.BlockSpec((tm, D_MODEL), lambda i: (i, 0)),
                   pl.BlockSpec((tm, D_MODEL), lambda i: (i, 0))],
        out_shape=[jax.ShapeDtypeStruct((rows, D_MODEL), _F32),
                   jax.ShapeDtypeStruct((rows, D_MODEL), _BF16)],
        compiler_params=pltpu.CompilerParams(dimension_semantics=("parallel",),
                                             vmem_limit_bytes=VMEM_LIMIT),
        name="mix",
    )(xn2d, pool2d, attn2d, h2d, w_in, w_a, w_b, w_o, g_next.reshape(1, D_MODEL))


def _gelu_tanh(x):
    return 0.5 * x * (1.0 + jnp.tanh(np.sqrt(2.0 / np.pi).astype(np.float32) * (x + 0.044715 * (x * x * x))))


def _ffn_kernel(xp_ref, x_ref, xnx_ref, h_ref, wup_ref, cw_ref, cb_ref, wdn_ref, g_ref, *rest, final, n_cast):
    n_out = 1 if final else 2
    cast_src, out_refs, rest = rest[:n_cast], rest[n_cast:n_cast + n_out], rest[n_cast + n_out:]
    cast_dst, a_refs = rest[:n_cast], rest[n_cast:]
    _cast_blocks(cast_src, cast_dst)
    tm = x_ref.shape[0]
    tiles_per_seq = SEQ // tm
    t = pl.program_id(0) % tiles_per_seq
    x = x_ref[...]
    x_prev = jnp.where(t > 0, xp_ref[...], jnp.zeros_like(xp_ref))
    x_next = jnp.where(t < tiles_per_seq - 1, xnx_ref[...], jnp.zeros_like(xnx_ref))
    xe = jnp.concatenate([x_prev, x, x_next], axis=0)
    acc = None
    assert sum(FF_CHUNKS) == D_FF
    for c, width in enumerate(FF_CHUNKS):
        c0 = sum(FF_CHUNKS[:c])
        cols = slice(c0, c0 + width)
        gcols = slice(D_FF + c0, D_FF + c0 + width)
        a_ext = jnp.dot(xe, wup_ref[:, cols], preferred_element_type=_F32)
        gate = jnp.dot(x, wup_ref[:, gcols], preferred_element_type=_F32)
        a_mid = a_ext[HALO:HALO + tm]
        a_ref = a_refs[c]
        for j in range(width // LANES):
            a_ref[j] = a_ext[:, j * LANES:(j + 1) * LANES]
        a_prev = jnp.concatenate([a_ref[j, pl.ds(HALO - 1, tm, stride=1), :] for j in range(width // LANES)], axis=1)
        a_next = jnp.concatenate([a_ref[j, pl.ds(HALO + 1, tm, stride=1), :] for j in range(width // LANES)], axis=1)
        cw = cw_ref[:, cols]
        conv = a_prev * cw[0:1] + a_mid * cw[1:2] + a_next * cw[2:3] + cb_ref[:, cols]
        act = (_gelu_tanh(conv) * gate).astype(_BF16)
        part = jnp.dot(act, wdn_ref[cols, :], preferred_element_type=_F32)
        acc = part if acc is None else acc + part
    h = h_ref[...] + acc
    if final:
        out_refs[0][...] = _rms(h, g_ref[...])
    else:
        out_refs[0][...] = h
        out_refs[1][...] = _rms(h, g_ref[...]).astype(out_refs[1].dtype)


def _ffn_call(xn2d, h2d, w_up, conv_w, conv_b, w_down, g_next, final, to_cast=(), cast_layer=0):
    rows = xn2d.shape[0]
    tm = ROW_TILE
    per = tm // HALO
    last = rows // HALO - 1
    row_spec = pl.BlockSpec((tm, D_MODEL), lambda i: (i, 0))
    if final:
        out_specs = [row_spec]
        out_shape = [jax.ShapeDtypeStruct((rows, D_MODEL), _F32)]
    else:
        out_specs = [row_spec, row_spec]
        out_shape = [jax.ShapeDtypeStruct((rows, D_MODEL), _F32),
                     jax.ShapeDtypeStruct((rows, D_MODEL), _BF16)]
    riders = [_cast_rider(w, cast_layer, rows // tm, lambda i: i) for w in to_cast]
    outs = pl.pallas_call(
        functools.partial(_ffn_kernel, final=final, n_cast=len(riders)),
        grid=(rows // tm,),
        in_specs=[pl.BlockSpec((HALO, D_MODEL), lambda i: (jnp.maximum(i * per - 1, 0), 0)),
                  row_spec,
                  pl.BlockSpec((HALO, D_MODEL), lambda i: (jnp.minimum((i + 1) * per, last), 0)),
                  row_spec,
                  _const_spec((D_MODEL, 2 * D_FF)),
                  _const_spec((3, D_FF)),
                  _const_spec((1, D_FF)),
                  _const_spec((D_FF, D_MODEL)),
                  _const_spec((1, D_MODEL))] + [r[0] for r in riders],
        out_specs=out_specs + [r[1] for r in riders],
        out_shape=out_shape + [r[2] for r in riders],
        scratch_shapes=[pltpu.VMEM((width // LANES, tm + 2 * HALO, LANES), _F32) for width in FF_CHUNKS],
        compiler_params=pltpu.CompilerParams(dimension_semantics=("parallel",),
                                             vmem_limit_bytes=VMEM_LIMIT),
        name="ffn",
    )(xn2d, xn2d, xn2d, h2d, w_up, conv_w, conv_b.reshape(1, D_FF), w_down, g_next.reshape(1, D_MODEL), *to_cast)
    return outs[:len(out_shape)], outs[len(out_shape):]


def kernel(x, w_in, w_pool, pool_scale, w_a, w_b, w_o, norm1, norm2, w_up, conv_w, conv_b, w_down, rel_bias, norm_f):
    batch, seq, d = x.shape
    assert (seq, d) == (SEQ, D_MODEL)
    depth = w_in.shape[0]
    rows = batch * seq
    bias_rows = _bias_rows(rel_bias)
    h = x.reshape(rows, d)
    xn = None
    w_pool_bf = w_pool.astype(_BF16)
    w_in_bf = w_in[0].astype(_BF16)
    for layer in range(depth):
        if xn is None:
            qkv0, xn3 = _qkv_call(x, w_in_bf, (0,), norm_gain=norm1[layer])
            xn = xn3.reshape(rows, d)
            qkv12 = _qkv_call(xn3, w_in_bf, (1, 2))
            qkv = [(qkv0, 0), (qkv12, 0), (qkv12, 1)]
        else:
            xn3 = xn.reshape(batch, seq, d)
            qkv_all = _qkv_call(xn3, w_in_bf, (0, 1, 2))
            qkv = [(qkv_all, gi) for gi in range(3)]
        pool = _pool_call(xn3, w_in_bf, w_pool_bf, pool_scale[layer], layer)
        attn, (w_a_bf, w_b_bf, w_o_bf, w_up_bf, w_down_bf) = _attn_call(qkv, bias_rows, (w_a, w_b, w_o, w_up, w_down), layer)
        h, xn = _mix_call(xn, pool.reshape(rows, POOL_WIDTH), attn.reshape(rows, ATTN_OUT), h,
                          w_in_bf, w_a_bf, w_b_bf, w_o_bf, norm2[layer])
        final = layer == depth - 1
        g_next = norm_f if final else norm1[layer + 1]
        outs, cast = _ffn_call(xn, h, w_up_bf, conv_w[layer], conv_b[layer], w_down_bf, g_next, final,
                               to_cast=() if final else (w_in,), cast_layer=layer + 1)
        if final:
            return outs[0].reshape(batch, seq, d)
        h, xn = outs
        w_in_bf, = cast
```

```python
import functools

import numpy as np
import jax
import jax.numpy as jnp
from jax import lax
from jax.experimental import pallas as pl
from jax.experimental.pallas import tpu as pltpu

D_MODEL = 1024
SEQ = 2048
POOL_WINDOWS = (2, 4, 8, 16)
POOL_GROUP_DIM = 256
POOL_WIDTH = 1024
ATTN_DILATIONS = (1, 4, 16)
HALF_WINDOW = 64
HEADS_PER_GROUP = 4
N_HEADS = 12
HEAD_DIM = 128
ATTN_WIDTH = N_HEADS * HEAD_DIM
ATTN_OUT = HEADS_PER_GROUP * HEAD_DIM
NEG_INF = -1e30
N_BUCKETS = 32
MAX_DISTANCE = 1024
D_FF = 2816
EPS = 1e-6
PROJ_WIDTH = POOL_WIDTH + 3 * ATTN_WIDTH
GATE_WIDTH = 2 * D_MODEL
IN_WIDTH = PROJ_WIDTH + GATE_WIDTH
GATE_BLOCK = 2560

Q_TILE = 128
UNITS_PER_STEP = 16
LOG2_E = float(np.log2(np.e))
RESIDUE_PITCH = Q_TILE + 4
NORM_ROWS = 256
PROJ_TN = 512
ROW_TILE = 512
BF16_TILE_ROWS = 16
HALO = BF16_TILE_ROWS
FF_CHUNKS = (1536, 1280)
VMEM_LIMIT = 56 * 1024 * 1024
LANES = 128
REGROUP_TILE = 256

_F32 = jnp.float32
_BF16 = jnp.bfloat16


def _rms(x, g):
    return x * lax.rsqrt(jnp.mean(x * x, axis=-1, keepdims=True) + EPS) * g


def _const_spec(shape, index=None):
    index = tuple(index) if index is not None else (0,) * len(shape)
    return pl.BlockSpec(tuple(shape), lambda *_: index, pipeline_mode=pl.Buffered(1))


def _layer_spec(layer, shape, index=None):
    index = tuple(index) if index is not None else (0,) * len(shape)
    return pl.BlockSpec((None,) + tuple(shape), lambda *_: (layer,) + index, pipeline_mode=pl.Buffered(1))


def _cast_rider(param, layer, n_steps, step_of):
    _, rows, cols = param.shape
    n_blocks = max(n for n in range(1, n_steps + 1) if rows % n == 0 and (rows // n) % BF16_TILE_ROWS == 0)
    rb = rows // n_blocks

    def block(*g):
        return step_of(*g) * n_blocks // n_steps

    return (pl.BlockSpec((None, rb, cols), lambda *g: (layer, block(*g), 0)),
            pl.BlockSpec((rb, cols), lambda *g: (block(*g), 0)),
            jax.ShapeDtypeStruct((rows, cols), _BF16))


def _cast_blocks(src_refs, dst_refs):
    for src, dst in zip(src_refs, dst_refs, strict=True):
        dst[...] = src[...].astype(dst.dtype)


POOL_PAD = 16
POOL_EDGE = 8
assert POOL_EDGE >= max(POOL_WINDOWS) // 2 and POOL_PAD - POOL_EDGE >= max(POOL_WINDOWS) // 4


def _pool_kernel(xn_ref, w_ref, wpool_ref, pscale_ref, o_ref, *scratch):
    *u_refs, t_ref = scratch
    n = SEQ + 2 * POOL_PAD
    slabs = POOL_GROUP_DIM // LANES

    def shifted(view, start, rows, k_back, k_fwd):
        return view[pl.ds(start - k_back, rows, stride=1), :] + view[pl.ds(start + k_fwd, rows, stride=1), :]

    def steps(w):
        return (1, 0) if w == 1 else (w // 2, w // 2)

    zeros = jnp.zeros((POOL_PAD, LANES), _F32)
    for t in range(2):
        for s in range(slabs):
            t_ref[t, s, 0:POOL_EDGE, :] = zeros[:POOL_EDGE]
            t_ref[t, s, n - POOL_EDGE:n, :] = zeros[:POOL_EDGE]
    xn = xn_ref[...]

    def project(gi):
        u = jnp.dot(xn, w_ref[:, gi * POOL_GROUP_DIM:(gi + 1) * POOL_GROUP_DIM], preferred_element_type=_F32)
        for s in range(slabs):
            u_refs[gi][s, 0:POOL_PAD, :] = zeros
            u_refs[gi][s, n - POOL_PAD:n, :] = zeros
            u_refs[gi][s, POOL_PAD:POOL_PAD + SEQ, :] = u[:, s * LANES:(s + 1) * LANES]

    edge_row = lax.broadcasted_iota(jnp.int32, (POOL_PAD, POOL_GROUP_DIM), 0)

    def pool(gi):
        window = POOL_WINDOWS[gi]
        u_ref = u_refs[gi]
        cols = slice(gi * POOL_GROUP_DIM, (gi + 1) * POOL_GROUP_DIM)
        totals = []
        for s in range(slabs):
            src, w, slot = u_ref.at[s], 1, 0
            while 2 * w < window:
                t_ref[slot, s, pl.ds(POOL_EDGE, n - 2 * POOL_EDGE), :] = shifted(src, POOL_EDGE, n - 2 * POOL_EDGE,
                                                                              *steps(w))
                src, w, slot = t_ref.at[slot, s], 2 * w, 1 - slot
            totals.append(shifted(src, POOL_PAD, SEQ, *steps(w)))
        total = jnp.concatenate(totals, axis=1)
        u = jnp.concatenate([u_ref[s, POOL_PAD:POOL_PAD + SEQ, :] for s in range(slabs)], axis=1)
        wpool = wpool_ref[gi]
        scale = pscale_ref[:, cols]
        pooled = total * (1.0 / window) - u
        z = jnp.dot(pooled.astype(_BF16), wpool, preferred_element_type=_F32)
        o_ref[:, cols] = (z * scale).astype(o_ref.dtype)
        for r0 in (0, SEQ - POOL_PAD):
            pos = edge_row + r0
            size = (jnp.minimum(pos + window // 2, SEQ) - jnp.maximum(pos - window // 2, 0)).astype(_F32)
            pooled = total[r0:r0 + POOL_PAD] / size - u[r0:r0 + POOL_PAD]
            z = jnp.dot(pooled.astype(_BF16), wpool, preferred_element_type=_F32)
            o_ref[r0:r0 + POOL_PAD, cols] = (z * scale).astype(o_ref.dtype)

    n_groups = len(POOL_WINDOWS)
    project(0)
    for gi in range(n_groups):
        if gi + 1 < n_groups:
            project(gi + 1)
        pool(gi)


def _pool_call(xn3, w_in, w_pool, pool_scale, layer):
    batch = xn3.shape[0]
    return pl.pallas_call(
        _pool_kernel,
        grid=(batch,),
        in_specs=[pl.BlockSpec((None, SEQ, D_MODEL), lambda b: (b, 0, 0)),
                  _const_spec((D_MODEL, POOL_WIDTH)),
                  _layer_spec(layer, (len(POOL_WINDOWS), POOL_GROUP_DIM, POOL_GROUP_DIM)),
                  _const_spec((1, POOL_WIDTH))],
        out_specs=pl.BlockSpec((None, SEQ, POOL_WIDTH), lambda b: (b, 0, 0)),
        out_shape=jax.ShapeDtypeStruct((batch, SEQ, POOL_WIDTH), _BF16),
        scratch_shapes=[pltpu.VMEM((POOL_GROUP_DIM // LANES, SEQ + 2 * POOL_PAD, LANES), _F32) for _ in POOL_WINDOWS]
        + [pltpu.VMEM((2, POOL_GROUP_DIM // LANES, SEQ + 2 * POOL_PAD, LANES), _F32)],
        compiler_params=pltpu.CompilerParams(dimension_semantics=("parallel",),
                                             vmem_limit_bytes=VMEM_LIMIT),
        name="pool",
    )(xn3, w_in, w_pool, pool_scale.reshape(1, POOL_WIDTH))


def _qkv_kernel(x_ref, *refs, dils, norm_input):
    if norm_input:
        g_ref, wq_ref, wk_ref, wv_ref, o_ref, xn_ref, *scratch = refs
        raw_ref = x_ref

        @pl.when(pl.program_id(1) == 0)
        def _():
            for r0 in range(0, SEQ, NORM_ROWS):
                xn_ref[r0:r0 + NORM_ROWS, :] = _rms(raw_ref[r0:r0 + NORM_ROWS, :], g_ref[...]).astype(xn_ref.dtype)

        x_ref = xn_ref
    else:
        wq_ref, wk_ref, wv_ref, o_ref, *scratch = refs
    if dils == (1,):
        x = x_ref[...]
    else:
        xp_ref, = scratch
        for step, dil in enumerate(dils):
            @pl.when(pl.program_id(1) == step)
            def _(dil=dil):
                if dil == 1:
                    xp_ref[...] = x_ref[...]
                    return
                sub = SEQ // dil
                sub_t = REGROUP_TILE // dil
                i = lax.broadcasted_iota(jnp.int32, (REGROUP_TILE, REGROUP_TILE), 0)
                j = lax.broadcasted_iota(jnp.int32, (REGROUP_TILE, REGROUP_TILE), 1)
                pick = (j == (i % sub_t) * dil + i // sub_t).astype(_BF16)
                for t in range(SEQ // REGROUP_TILE):
                    tile = x_ref[t * REGROUP_TILE:(t + 1) * REGROUP_TILE, :]
                    srt = jnp.dot(pick, tile, preferred_element_type=_F32).astype(_BF16)
                    for r in range(dil):
                        xp_ref[r * sub + t * sub_t:r * sub + (t + 1) * sub_t, :] = srt[r * sub_t:(r + 1) * sub_t]
        x = xp_ref[...]
    for which, w_ref in enumerate((wq_ref, wk_ref, wv_ref)):
        y = jnp.dot(x, w_ref[...], preferred_element_type=_F32)
        if which == 0:
            y = y * HEAD_DIM ** -0.5
        for hd in range(HEADS_PER_GROUP):
            o_ref[which * HEADS_PER_GROUP + hd] = y[:, hd * HEAD_DIM:(hd + 1) * HEAD_DIM].astype(o_ref.dtype)


def _qkv_call(x3, w_in, gis, norm_gain=None):
    batch = x3.shape[0]
    dils = tuple(ATTN_DILATIONS[gi] for gi in gis)
    assert tuple(gis) == tuple(range(gis[0], gis[0] + len(gis)))
    first = POOL_WIDTH // PROJ_TN + gis[0]
    norm_input = norm_gain is not None
    x_spec = pl.BlockSpec((None, SEQ, D_MODEL), lambda b, g: (b, 0, 0))
    in_specs = [x_spec] + ([_const_spec((1, D_MODEL))] if norm_input else [])
    if len(gis) == 1:
        in_specs += [_const_spec((D_MODEL, PROJ_TN), (0, first + 3 * which)) for which in range(3)]
    else:
        in_specs += [pl.BlockSpec((D_MODEL, PROJ_TN), lambda b, g, which=which: (0, first + g + 3 * which))
                     for which in range(3)]
    out_specs = [pl.BlockSpec((None, None, 3 * HEADS_PER_GROUP, SEQ, HEAD_DIM), lambda b, g: (b, g, 0, 0, 0))]
    out_shape = [jax.ShapeDtypeStruct((batch, len(gis), 3 * HEADS_PER_GROUP, SEQ, HEAD_DIM), _BF16)]
    if norm_input:
        out_specs.append(x_spec)
        out_shape.append(jax.ShapeDtypeStruct((batch, SEQ, D_MODEL), _BF16))
    args = (x3,) + ((norm_gain.reshape(1, D_MODEL),) if norm_input else ()) + (w_in, w_in, w_in)
    outs = pl.pallas_call(
        functools.partial(_qkv_kernel, dils=dils, norm_input=norm_input),
        grid=(batch, len(gis)),
        in_specs=in_specs,
        out_specs=out_specs,
        out_shape=out_shape,
        scratch_shapes=[] if dils == (1,) else [pltpu.VMEM((SEQ, D_MODEL), _BF16)],
        compiler_params=pltpu.CompilerParams(dimension_semantics=("parallel", "arbitrary"),
                                             vmem_limit_bytes=VMEM_LIMIT),
        name="qkv" + "".join(str(gi) for gi in gis),
    )(*args)
    return outs if norm_input else outs[0]


def _t5_buckets_np(rel):
    n = -rel
    half = N_BUCKETS // 2
    ret = (n < 0).astype(np.int32) * half
    n = np.abs(n)
    max_exact = half // 2
    large = max_exact + (np.log(np.maximum(n, 1) / max_exact)
                         / np.log(MAX_DISTANCE / max_exact) * (half - max_exact)).astype(np.int32)
    large = np.minimum(large, half - 1)
    return (ret + np.where(n < max_exact, n, large)).astype(np.int32)


def _key_span(dil):
    return min(Q_TILE + 2 * HALF_WINDOW, SEQ // dil)


BIAS_LANES = 512


def _bias_rows(rel_bias):
    rows = []
    for gi, dil in enumerate(ATTN_DILATIONS):
        buckets = _t5_buckets_np(dil * np.arange(-HALF_WINDOW, HALF_WINDOW + 1))
        bias = rel_bias[buckets][:, gi * HEADS_PER_GROUP:(gi + 1) * HEADS_PER_GROUP].T.astype(_F32)
        rows.append(jnp.pad(bias, ((0, 0), (0, BIAS_LANES - bias.shape[1])), constant_values=NEG_INF))
    return jnp.stack(rows, axis=1)


def _tile_deltas(dil):
    return (0, HALF_WINDOW, 2 * HALF_WINDOW) if SEQ // dil > Q_TILE else (0,)


def _attn_kernel(q0_ref, k0_ref, v0_ref, q1_ref, k1_ref, v1_ref, q2_ref, k2_ref, v2_ref, e_ref, *rest, n_cast):
    cast_src, (o_ref, *rest) = rest[:n_cast], rest[n_cast:]
    cast_dst, (t0_ref, t1_ref, t2_ref, *stats) = rest[:n_cast], rest[n_cast:]
    near_o_ref, near_m_ref, near_l_ref, far_o_ref, far_m_ref, far_l_ref = stats
    _cast_blocks(cast_src, cast_dst)
    groups = ((q0_ref, k0_ref, v0_ref, t0_ref), (q1_ref, k1_ref, v1_ref, t1_ref), (q2_ref, k2_ref, v2_ref, t2_ref))

    @pl.when(pl.program_id(1) == 0)
    def _():
        for gi, (_, _, _, t_ref) in enumerate(groups):
            dil = ATTN_DILATIONS[gi]
            row = jnp.broadcast_to(e_ref[gi:gi + 1, :], (Q_TILE, BIAS_LANES))
            for vi, delta in enumerate(_tile_deltas(dil)):
                skew = pltpu.roll(row, (delta - HALF_WINDOW) % BIAS_LANES, axis=1, stride=1, stride_axis=0)
                t_ref[vi] = skew[:, :_key_span(dil)]

    def scores(u, gi):
        q_ref, k_ref, _, t_ref = groups[gi]
        dil = ATTN_DILATIONS[gi]
        sub = SEQ // dil
        span = _key_span(dil)
        blocks = sub // Q_TILE
        r = u // blocks
        m0 = (u % blocks) * Q_TILE
        start = jnp.clip(m0 - HALF_WINDOW, 0, sub - span)
        variant = (m0 - start) // HALF_WINDOW
        q = q_ref[pl.ds(pl.multiple_of(u * Q_TILE, Q_TILE), Q_TILE), :]
        k0 = pl.multiple_of(r * sub + start, HALF_WINDOW)
        k = k_ref[pl.ds(k0, span), :]
        s = lax.dot_general(q, k, (((1,), (1,)), ((), ())), preferred_element_type=_F32)
        if dil == 1:
            dest = (gi, pl.ds(pl.multiple_of(m0, Q_TILE), Q_TILE))
        elif blocks > 1:
            dest = (gi, pl.ds(m0 * dil + r, Q_TILE, stride=dil))
        else:
            dest = (None, pl.ds(r * RESIDUE_PITCH, Q_TILE, stride=1))
        return s + t_ref[variant], k0, dest

    def weights(s):
        m = jnp.max(s, axis=-1, keepdims=True)
        return m, jnp.exp((s - m).astype(_BF16))

    ones = jnp.ones((Q_TILE + 2 * HALF_WINDOW, HEAD_DIM), _BF16)

    def values(gi, k0, dest, m, p):
        span = p.shape[1]
        v = groups[gi][2][pl.ds(k0, span), :]
        ov = jnp.dot(p, jnp.concatenate([v, ones[:span]], axis=1), preferred_element_type=_F32)
        m2 = jnp.broadcast_to(m * LOG2_E, (Q_TILE, HEAD_DIM))
        slot, rows = dest
        if slot is None:
            far_o_ref[rows, :] = ov[:, :HEAD_DIM]
            far_m_ref[rows, :] = m2
            far_l_ref[rows, :] = ov[:, HEAD_DIM:]
        else:
            near_o_ref[slot, rows, :] = ov[:, :HEAD_DIM]
            near_m_ref[slot, rows, :] = m2
            near_l_ref[slot, rows, :] = ov[:, HEAD_DIM:]

    def units(it, carry):
        todo = [(it * UNITS_PER_STEP + j, gi) for j in range(UNITS_PER_STEP) for gi in range(3)]
        scored = [scores(u, gi) for u, gi in todo]
        soft = [weights(s) for s, _, _ in scored]
        for (_, gi), (_, k0, dest), (m, p) in zip(todo, scored, soft):
            values(gi, k0, dest, m, p)
        return carry

    lax.fori_loop(0, SEQ // Q_TILE // UNITS_PER_STEP, units, 0)

    far_dil = ATTN_DILATIONS[2]

    def merge(t, carry):
        rows = pl.ds(pl.multiple_of(t * Q_TILE, Q_TILE), Q_TILE)

        def far(ref):
            return jnp.concatenate([ref[pl.ds(t * (Q_TILE // far_dil) + j, far_dil, stride=RESIDUE_PITCH), :]
                                    for j in range(Q_TILE // far_dil)], axis=0)

        ms = [near_m_ref[0, rows, :], near_m_ref[1, rows, :], far(far_m_ref)]
        outs = [near_o_ref[0, rows, :], near_o_ref[1, rows, :], far(far_o_ref)]
        ls = [near_l_ref[0, rows, :], near_l_ref[1, rows, :], far(far_l_ref)]
        top = jnp.maximum(jnp.maximum(ms[0], ms[1]), ms[2])
        ws = [jnp.exp2(x - top) for x in ms]
        num = ws[0] * outs[0] + ws[1] * outs[1] + ws[2] * outs[2]
        den = ws[0] * ls[0] + ws[1] * ls[1] + ws[2] * ls[2]
        o_ref[rows, :] = (num / den).astype(o_ref.dtype)
        return carry

    lax.fori_loop(0, SEQ // Q_TILE, merge, 0, unroll=4)


def _attn_call(qkv, bias_rows, to_cast, layer):
    batch = qkv[0][0].shape[0]

    def head_spec(slot, which):
        return pl.BlockSpec((None, None, None, SEQ, HEAD_DIM),
                            lambda h, b: (b, slot, which * HEADS_PER_GROUP + h, 0, 0))

    in_specs = [head_spec(slot, which) for _, slot in qkv for which in range(3)]
    in_specs.append(pl.BlockSpec((None, 3, BIAS_LANES), lambda h, b: (h, 0, 0)))
    riders = [_cast_rider(w, layer, HEADS_PER_GROUP * batch, lambda h, b: h * batch + b) for w in to_cast]
    in_specs += [r[0] for r in riders]
    scratch = [pltpu.VMEM((len(_tile_deltas(dil)), Q_TILE, _key_span(dil)), _F32) for dil in ATTN_DILATIONS]
    assert SEQ // ATTN_DILATIONS[2] == Q_TILE and all(SEQ // d > Q_TILE for d in ATTN_DILATIONS[:2])
    scratch += [pltpu.VMEM((2, SEQ, HEAD_DIM), _F32) for _ in range(3)]
    scratch += [pltpu.VMEM((ATTN_DILATIONS[2] * RESIDUE_PITCH, HEAD_DIM), _F32) for _ in range(3)]
    attn, *cast = pl.pallas_call(
        functools.partial(_attn_kernel, n_cast=len(riders)),
        grid=(HEADS_PER_GROUP, batch),
        in_specs=in_specs,
        out_specs=[pl.BlockSpec((None, SEQ, HEAD_DIM), lambda h, b: (b, 0, h))] + [r[1] for r in riders],
        out_shape=[jax.ShapeDtypeStruct((batch, SEQ, ATTN_OUT), _BF16)] + [r[2] for r in riders],
        scratch_shapes=scratch,
        compiler_params=pltpu.CompilerParams(dimension_semantics=("arbitrary", "arbitrary"),
                                             vmem_limit_bytes=VMEM_LIMIT),
        name="attn",
    )(*[arr for arr, _ in qkv for _ in range(3)], bias_rows, *to_cast)
    return attn, cast


def _mix_kernel(xn_ref, pool_ref, attn_ref, h_ref, wg_ref, wa_ref, wb_ref, wo_ref, g_ref, h_out_ref, xn_out_ref):
    xn = xn_ref[...]
    gates = jnp.dot(xn, wg_ref[:, GATE_BLOCK - GATE_WIDTH:], preferred_element_type=_F32)
    a = jnp.dot(pool_ref[...], wa_ref[...], preferred_element_type=_F32)
    b = jnp.dot(attn_ref[...], wb_ref[...], preferred_element_type=_F32)
    merged = jax.nn.sigmoid(gates[:, :D_MODEL]) * a + jax.nn.sigmoid(gates[:, D_MODEL:]) * b
    h = h_ref[...] + jnp.dot(merged.astype(_BF16), wo_ref[...], preferred_element_type=_F32)
    h_out_ref[...] = h
    xn_out_ref[...] = _rms(h, g_ref[...]).astype(xn_out_ref.dtype)


def _mix_call(xn2d, pool2d, attn2d, h2d, w_in, w_a, w_b, w_o, g_next):
    rows = xn2d.shape[0]
    tm = ROW_TILE
    assert IN_WIDTH % GATE_BLOCK == 0 and GATE_BLOCK >= GATE_WIDTH
    return pl.pallas_call(
        _mix_kernel,
        grid=(rows // tm,),
        in_specs=[pl.BlockSpec((tm, D_MODEL), lambda i: (i, 0)),
                  pl.BlockSpec((tm, POOL_WIDTH), lambda i: (i, 0)),
                  pl.BlockSpec((tm, ATTN_OUT), lambda i: (i, 0)),
                  pl.BlockSpec((tm, D_MODEL), lambda i: (i, 0)),
                  _const_spec((D_MODEL, GATE_BLOCK), (0, IN_WIDTH // GATE_BLOCK - 1)),
                  _const_spec((POOL_WIDTH, D_MODEL)),
                  _const_spec((ATTN_OUT, D_MODEL)),
                  _const_spec((D_MODEL, D_MODEL)),
                  _const_spec((1, D_MODEL))],
        out_specs=[pl.BlockSpec((tm, D_MODEL), lambda i: (i, 0)),
                   pl.BlockSpec((tm, D_MODEL), lambda i: (i, 0))],
        out_shape=[jax.ShapeDtypeStruct((rows, D_MODEL), _F32),
                   jax.ShapeDtypeStruct((rows, D_MODEL), _BF16)],
        compiler_params=pltpu.CompilerParams(dimension_semantics=("parallel",),
                                             vmem_limit_bytes=VMEM_LIMIT),
        name="mix",
    )(xn2d, pool2d, attn2d, h2d, w_in, w_a, w_b, w_o, g_next.reshape(1, D_MODEL))


def _gelu_tanh(x):
    return 0.5 * x * (1.0 + jnp.tanh(np.sqrt(2.0 / np.pi).astype(np.float32) * (x + 0.044715 * (x * x * x))))


def _ffn_kernel(xp_ref, x_ref, xnx_ref, h_ref, wup_ref, cw_ref, cb_ref, wdn_ref, g_ref, *rest, final, n_cast):
    n_out = 1 if final else 2
    cast_src, out_refs, rest = rest[:n_cast], rest[n_cast:n_cast + n_out], rest[n_cast + n_out:]
    cast_dst, a_refs = rest[:n_cast], rest[n_cast:]
    _cast_blocks(cast_src, cast_dst)
    tm = x_ref.shape[0]
    tiles_per_seq = SEQ // tm
    t = pl.program_id(0) % tiles_per_seq
    x = x_ref[...]
    x_prev = jnp.where(t > 0, xp_ref[...], jnp.zeros_like(xp_ref))
    x_next = jnp.where(t < tiles_per_seq - 1, xnx_ref[...], jnp.zeros_like(xnx_ref))
    xe = jnp.concatenate([x_prev, x, x_next], axis=0)
    acc = None
    assert sum(FF_CHUNKS) == D_FF
    for c, width in enumerate(FF_CHUNKS):
        c0 = sum(FF_CHUNKS[:c])
        cols = slice(c0, c0 + width)
        gcols = slice(D_FF + c0, D_FF + c0 + width)
        a_ext = jnp.dot(xe, wup_ref[:, cols], preferred_element_type=_F32)
        gate = jnp.dot(x, wup_ref[:, gcols], preferred_element_type=_F32)
        a_mid = a_ext[HALO:HALO + tm]
        a_ref = a_refs[c]
        for j in range(width // LANES):
            a_ref[j] = a_ext[:, j * LANES:(j + 1) * LANES]
        a_prev = jnp.concatenate([a_ref[j, pl.ds(HALO - 1, tm, stride=1), :] for j in range(width // LANES)], axis=1)
        a_next = jnp.concatenate([a_ref[j, pl.ds(HALO + 1, tm, stride=1), :] for j in range(width // LANES)], axis=1)
        cw = cw_ref[:, cols]
        conv = a_prev * cw[0:1] + a_mid * cw[1:2] + a_next * cw[2:3] + cb_ref[:, cols]
        act = (_gelu_tanh(conv) * gate).astype(_BF16)
        part = jnp.dot(act, wdn_ref[cols, :], preferred_element_type=_F32)
        acc = part if acc is None else acc + part
    h = h_ref[...] + acc
    if final:
        out_refs[0][...] = _rms(h, g_ref[...])
    else:
        out_refs[0][...] = h
        out_refs[1][...] = _rms(h, g_ref[...]).astype(out_refs[1].dtype)


def _ffn_call(xn2d, h2d, w_up, conv_w, conv_b, w_down, g_next, final, to_cast=(), cast_layer=0):
    rows = xn2d.shape[0]
    tm = ROW_TILE
    per = tm // HALO
    last = rows // HALO - 1
    row_spec = pl.BlockSpec((tm, D_MODEL), lambda i: (i, 0))
    if final:
        out_specs = [row_spec]
        out_shape = [jax.ShapeDtypeStruct((rows, D_MODEL), _F32)]
    else:
        out_specs = [row_spec, row_spec]
        out_shape = [jax.ShapeDtypeStruct((rows, D_MODEL), _F32),
                     jax.ShapeDtypeStruct((rows, D_MODEL), _BF16)]
    riders = [_cast_rider(w, cast_layer, rows // tm, lambda i: i) for w in to_cast]
    outs = pl.pallas_call(
        functools.partial(_ffn_kernel, final=final, n_cast=len(riders)),
        grid=(rows // tm,),
        in_specs=[pl.BlockSpec((HALO, D_MODEL), lambda i: (jnp.maximum(i * per - 1, 0), 0)),
                  row_spec,
                  pl.BlockSpec((HALO, D_MODEL), lambda i: (jnp.minimum((i + 1) * per, last), 0)),
                  row_spec,
                  _const_spec((D_MODEL, 2 * D_FF)),
                  _const_spec((3, D_FF)),
                  _const_spec((1, D_FF)),
                  _const_spec((D_FF, D_MODEL)),
                  _const_spec((1, D_MODEL))] + [r[0] for r in riders],
        out_specs=out_specs + [r[1] for r in riders],
        out_shape=out_shape + [r[2] for r in riders],
        scratch_shapes=[pltpu.VMEM((width // LANES, tm + 2 * HALO, LANES), _F32) for width in FF_CHUNKS],
        compiler_params=pltpu.CompilerParams(dimension_semantics=("parallel",),
                                             vmem_limit_bytes=VMEM_LIMIT),
        name="ffn",
    )(xn2d, xn2d, xn2d, h2d, w_up, conv_w, conv_b.reshape(1, D_FF), w_down, g_next.reshape(1, D_MODEL), *to_cast)
    return outs[:len(out_shape)], outs[len(out_shape):]


def kernel(x, w_in, w_pool, pool_scale, w_a, w_b, w_o, norm1, norm2, w_up, conv_w, conv_b, w_down, rel_bias, norm_f):
    batch, seq, d = x.shape
    assert (seq, d) == (SEQ, D_MODEL)
    depth = w_in.shape[0]
    rows = batch * seq
    bias_rows = _bias_rows(rel_bias)
    h = x.reshape(rows, d)
    xn = None
    w_pool_bf = w_pool.astype(_BF16)
    w_in_bf = w_in[0].astype(_BF16)
    for layer in range(depth):
        if xn is None:
            qkv_all, xn3 = _qkv_call(x, w_in_bf, (0, 1, 2), norm_gain=norm1[layer])
            xn = xn3.reshape(rows, d)
            qkv = [(qkv_all, gi) for gi in range(3)]
        else:
            xn3 = xn.reshape(batch, seq, d)
            qkv_all = _qkv_call(xn3, w_in_bf, (0, 1, 2))
            qkv = [(qkv_all, gi) for gi in range(3)]
        pool = _pool_call(xn3, w_in_bf, w_pool_bf, pool_scale[layer], layer)
        attn, (w_a_bf, w_b_bf, w_o_bf, w_up_bf, w_down_bf) = _attn_call(qkv, bias_rows, (w_a, w_b, w_o, w_up, w_down), layer)
        h, xn = _mix_call(xn, pool.reshape(rows, POOL_WIDTH), attn.reshape(rows, ATTN_OUT), h,
                          w_in_bf, w_a_bf, w_b_bf, w_o_bf, norm2[layer])
        final = layer == depth - 1
        g_next = norm_f if final else norm1[layer + 1]
        outs, cast = _ffn_call(xn, h, w_up_bf, conv_w[layer], conv_b[layer], w_down_bf, g_next, final,
                               to_cast=() if final else (w_in,), cast_layer=layer + 1)
        if final:
            return outs[0].reshape(batch, seq, d)
        h, xn = outs
        w_in_bf, = cast
```

```python
import functools

import numpy as np
import jax
import jax.numpy as jnp
from jax import lax
from jax.experimental import pallas as pl
from jax.experimental.pallas import tpu as pltpu

D_MODEL = 1024
SEQ = 2048
POOL_WINDOWS = (2, 4, 8, 16)
POOL_GROUP_DIM = 256
POOL_WIDTH = 1024
ATTN_DILATIONS = (1, 4, 16)
HALF_WINDOW = 64
HEADS_PER_GROUP = 4
N_HEADS = 12
HEAD_DIM = 128
ATTN_WIDTH = N_HEADS * HEAD_DIM
ATTN_OUT = HEADS_PER_GROUP * HEAD_DIM
NEG_INF = -1e30
N_BUCKETS = 32
MAX_DISTANCE = 1024
D_FF = 2816
EPS = 1e-6
PROJ_WIDTH = POOL_WIDTH + 3 * ATTN_WIDTH
GATE_WIDTH = 2 * D_MODEL
IN_WIDTH = PROJ_WIDTH + GATE_WIDTH
GATE_BLOCK = 2560

Q_TILE = 128
UNITS_PER_STEP = 16
LOG2_E = float(np.log2(np.e))
RESIDUE_PITCH = Q_TILE + 4
NORM_ROWS = 256
PROJ_TN = 512
ROW_TILE = 512
BF16_TILE_ROWS = 16
HALO = BF16_TILE_ROWS
FF_CHUNKS = (1536, 1280)
VMEM_LIMIT = 56 * 1024 * 1024
LANES = 128
REGROUP_TILE = 256

_F32 = jnp.float32
_BF16 = jnp.bfloat16


def _rms(x, g):
    return x * lax.rsqrt(jnp.mean(x * x, axis=-1, keepdims=True) + EPS) * g


def _const_spec(shape, index=None):
    index = tuple(index) if index is not None else (0,) * len(shape)
    return pl.BlockSpec(tuple(shape), lambda *_: index, pipeline_mode=pl.Buffered(1))


def _layer_spec(layer, shape, index=None):
    index = tuple(index) if index is not None else (0,) * len(shape)
    return pl.BlockSpec((None,) + tuple(shape), lambda *_: (layer,) + index, pipeline_mode=pl.Buffered(1))


def _cast_rider(param, layer, n_steps, step_of):
    _, rows, cols = param.shape
    n_blocks = max(n for n in range(1, n_steps + 1) if rows % n == 0 and (rows // n) % BF16_TILE_ROWS == 0)
    rb = rows // n_blocks

    def block(*g):
        return step_of(*g) * n_blocks // n_steps

    return (pl.BlockSpec((None, rb, cols), lambda *g: (layer, block(*g), 0)),
            pl.BlockSpec((rb, cols), lambda *g: (block(*g), 0)),
            jax.ShapeDtypeStruct((rows, cols), _BF16))


def _cast_blocks(src_refs, dst_refs):
    for src, dst in zip(src_refs, dst_refs, strict=True):
        dst[...] = src[...].astype(dst.dtype)


POOL_PAD = 16
POOL_EDGE = 8
assert POOL_EDGE >= max(POOL_WINDOWS) // 2 and POOL_PAD - POOL_EDGE >= max(POOL_WINDOWS) // 4


def _pool_kernel(xn_ref, w_ref, wpool_ref, pscale_ref, o_ref, *scratch):
    *u_refs, t_ref = scratch
    n = SEQ + 2 * POOL_PAD
    slabs = POOL_GROUP_DIM // LANES

    def shifted(view, start, rows, k_back, k_fwd):
        return view[pl.ds(start - k_back, rows, stride=1), :] + view[pl.ds(start + k_fwd, rows, stride=1), :]

    def steps(w):
        return (1, 0) if w == 1 else (w // 2, w // 2)

    zeros = jnp.zeros((POOL_PAD, LANES), _F32)
    for t in range(2):
        for s in range(slabs):
            t_ref[t, s, 0:POOL_EDGE, :] = zeros[:POOL_EDGE]
            t_ref[t, s, n - POOL_EDGE:n, :] = zeros[:POOL_EDGE]
    xn = xn_ref[...]

    def project(gi):
        u = jnp.dot(xn, w_ref[:, gi * POOL_GROUP_DIM:(gi + 1) * POOL_GROUP_DIM], preferred_element_type=_F32)
        for s in range(slabs):
            u_refs[gi][s, 0:POOL_PAD, :] = zeros
            u_refs[gi][s, n - POOL_PAD:n, :] = zeros
            u_refs[gi][s, POOL_PAD:POOL_PAD + SEQ, :] = u[:, s * LANES:(s + 1) * LANES]

    edge_row = lax.broadcasted_iota(jnp.int32, (POOL_PAD, POOL_GROUP_DIM), 0)

    def pool(gi):
        window = POOL_WINDOWS[gi]
        u_ref = u_refs[gi]
        cols = slice(gi * POOL_GROUP_DIM, (gi + 1) * POOL_GROUP_DIM)
        totals = []
        for s in range(slabs):
            src, w, slot = u_ref.at[s], 1, 0
            while 2 * w < window:
                t_ref[slot, s, pl.ds(POOL_EDGE, n - 2 * POOL_EDGE), :] = shifted(src, POOL_EDGE, n - 2 * POOL_EDGE,
                                                                              *steps(w))
                src, w, slot = t_ref.at[slot, s], 2 * w, 1 - slot
            totals.append(shifted(src, POOL_PAD, SEQ, *steps(w)))
        total = jnp.concatenate(totals, axis=1)
        u = jnp.concatenate([u_ref[s, POOL_PAD:POOL_PAD + SEQ, :] for s in range(slabs)], axis=1)
        wpool = wpool_ref[gi]
        scale = pscale_ref[:, cols]
        pooled = total * (1.0 / window) - u
        z = jnp.dot(pooled.astype(_BF16), wpool, preferred_element_type=_F32)
        o_ref[:, cols] = (z * scale).astype(o_ref.dtype)
        for r0 in (0, SEQ - POOL_PAD):
            pos = edge_row + r0
            size = (jnp.minimum(pos + window // 2, SEQ) - jnp.maximum(pos - window // 2, 0)).astype(_F32)
            pooled = total[r0:r0 + POOL_PAD] / size - u[r0:r0 + POOL_PAD]
            z = jnp.dot(pooled.astype(_BF16), wpool, preferred_element_type=_F32)
            o_ref[r0:r0 + POOL_PAD, cols] = (z * scale).astype(o_ref.dtype)

    n_groups = len(POOL_WINDOWS)
    project(0)
    for gi in range(n_groups):
        if gi + 1 < n_groups:
            project(gi + 1)
        pool(gi)


def _pool_call(xn3, w_in, w_pool, pool_scale, layer):
    batch = xn3.shape[0]
    return pl.pallas_call(
        _pool_kernel,
        grid=(batch,),
        in_specs=[pl.BlockSpec((None, SEQ, D_MODEL), lambda b: (b, 0, 0)),
                  _const_spec((D_MODEL, POOL_WIDTH)),
                  _layer_spec(layer, (len(POOL_WINDOWS), POOL_GROUP_DIM, POOL_GROUP_DIM)),
                  _const_spec((1, POOL_WIDTH))],
        out_specs=pl.BlockSpec((None, SEQ, POOL_WIDTH), lambda b: (b, 0, 0)),
        out_shape=jax.ShapeDtypeStruct((batch, SEQ, POOL_WIDTH), _BF16),
        scratch_shapes=[pltpu.VMEM((POOL_GROUP_DIM // LANES, SEQ + 2 * POOL_PAD, LANES), _F32) for _ in POOL_WINDOWS]
        + [pltpu.VMEM((2, POOL_GROUP_DIM // LANES, SEQ + 2 * POOL_PAD, LANES), _F32)],
        compiler_params=pltpu.CompilerParams(dimension_semantics=("parallel",),
                                             vmem_limit_bytes=VMEM_LIMIT),
        name="pool",
    )(xn3, w_in, w_pool, pool_scale.reshape(1, POOL_WIDTH))


def _qkv_kernel(x_ref, *refs, dils, norm_input):
    if norm_input:
        g_ref, wq_ref, wk_ref, wv_ref, o_ref, xn_ref, *scratch = refs
        for r0 in range(0, SEQ, NORM_ROWS):
            xn_ref[r0:r0 + NORM_ROWS, :] = _rms(x_ref[r0:r0 + NORM_ROWS, :], g_ref[...]).astype(xn_ref.dtype)
        x_ref = xn_ref
    else:
        wq_ref, wk_ref, wv_ref, o_ref, *scratch = refs
    def project(x):
        for which, w_ref in enumerate((wq_ref, wk_ref, wv_ref)):
            y = jnp.dot(x, w_ref[...], preferred_element_type=_F32)
            if which == 0:
                y = y * HEAD_DIM ** -0.5
            for hd in range(HEADS_PER_GROUP):
                o_ref[which * HEADS_PER_GROUP + hd] = y[:, hd * HEAD_DIM:(hd + 1) * HEAD_DIM].astype(o_ref.dtype)

    if dils == (1,):
        project(x_ref[...])
    else:
        xp_ref, = scratch
        for step, dil in enumerate(dils):
            @pl.when(pl.program_id(1) == step)
            def _(dil=dil):
                if dil == 1:
                    project(x_ref[...])
                    return
                sub = SEQ // dil
                sub_t = REGROUP_TILE // dil
                i = lax.broadcasted_iota(jnp.int32, (REGROUP_TILE, REGROUP_TILE), 0)
                j = lax.broadcasted_iota(jnp.int32, (REGROUP_TILE, REGROUP_TILE), 1)
                pick = (j == (i % sub_t) * dil + i // sub_t).astype(_BF16)
                for t in range(SEQ // REGROUP_TILE):
                    tile = x_ref[t * REGROUP_TILE:(t + 1) * REGROUP_TILE, :]
                    srt = jnp.dot(pick, tile, preferred_element_type=_F32).astype(_BF16)
                    for r in range(dil):
                        xp_ref[r * sub + t * sub_t:r * sub + (t + 1) * sub_t, :] = srt[r * sub_t:(r + 1) * sub_t]
                project(xp_ref[...])


def _qkv_call(x3, w_in, gis, norm_gain=None):
    batch = x3.shape[0]
    dils = tuple(ATTN_DILATIONS[gi] for gi in gis)
    assert tuple(gis) == tuple(range(gis[0], gis[0] + len(gis)))
    first = POOL_WIDTH // PROJ_TN + gis[0]
    norm_input = norm_gain is not None
    x_spec = pl.BlockSpec((None, SEQ, D_MODEL), lambda b, g: (b, 0, 0))
    in_specs = [x_spec] + ([_const_spec((1, D_MODEL))] if norm_input else [])
    if len(gis) == 1:
        in_specs += [_const_spec((D_MODEL, PROJ_TN), (0, first + 3 * which)) for which in range(3)]
    else:
        in_specs += [pl.BlockSpec((D_MODEL, PROJ_TN), lambda b, g, which=which: (0, first + g + 3 * which))
                     for which in range(3)]
    out_specs = [pl.BlockSpec((None, None, 3 * HEADS_PER_GROUP, SEQ, HEAD_DIM), lambda b, g: (b, g, 0, 0, 0))]
    out_shape = [jax.ShapeDtypeStruct((batch, len(gis), 3 * HEADS_PER_GROUP, SEQ, HEAD_DIM), _BF16)]
    if norm_input:
        out_specs.append(x_spec)
        out_shape.append(jax.ShapeDtypeStruct((batch, SEQ, D_MODEL), _BF16))
    args = (x3,) + ((norm_gain.reshape(1, D_MODEL),) if norm_input else ()) + (w_in, w_in, w_in)
    outs = pl.pallas_call(
        functools.partial(_qkv_kernel, dils=dils, norm_input=norm_input),
        grid=(batch, len(gis)),
        in_specs=in_specs,
        out_specs=out_specs,
        out_shape=out_shape,
        scratch_shapes=[] if dils == (1,) else [pltpu.VMEM((SEQ, D_MODEL), _BF16)],
        compiler_params=pltpu.CompilerParams(dimension_semantics=("parallel", "arbitrary"),
                                             vmem_limit_bytes=VMEM_LIMIT),
        name="qkv" + "".join(str(gi) for gi in gis),
    )(*args)
    return outs if norm_input else outs[0]


def _t5_buckets_np(rel):
    n = -rel
    half = N_BUCKETS // 2
    ret = (n < 0).astype(np.int32) * half
    n = np.abs(n)
    max_exact = half // 2
    large = max_exact + (np.log(np.maximum(n, 1) / max_exact)
                         / np.log(MAX_DISTANCE / max_exact) * (half - max_exact)).astype(np.int32)
    large = np.minimum(large, half - 1)
    return (ret + np.where(n < max_exact, n, large)).astype(np.int32)


def _key_span(dil):
    return min(Q_TILE + 2 * HALF_WINDOW, SEQ // dil)


BIAS_LANES = 512


def _bias_rows(rel_bias):
    rows = []
    for gi, dil in enumerate(ATTN_DILATIONS):
        buckets = _t5_buckets_np(dil * np.arange(-HALF_WINDOW, HALF_WINDOW + 1))
        bias = rel_bias[buckets][:, gi * HEADS_PER_GROUP:(gi + 1) * HEADS_PER_GROUP].T.astype(_F32)
        rows.append(jnp.pad(bias, ((0, 0), (0, BIAS_LANES - bias.shape[1])), constant_values=NEG_INF))
    return jnp.stack(rows, axis=1)


def _tile_deltas(dil):
    return (0, HALF_WINDOW, 2 * HALF_WINDOW) if SEQ // dil > Q_TILE else (0,)


def _attn_kernel(q0_ref, k0_ref, v0_ref, q1_ref, k1_ref, v1_ref, q2_ref, k2_ref, v2_ref, e_ref, *rest, n_cast):
    cast_src, (o_ref, *rest) = rest[:n_cast], rest[n_cast:]
    cast_dst, (t0_ref, t1_ref, t2_ref, *stats) = rest[:n_cast], rest[n_cast:]
    near_o_ref, near_m_ref, near_l_ref, far_o_ref, far_m_ref, far_l_ref = stats
    _cast_blocks(cast_src, cast_dst)
    groups = ((q0_ref, k0_ref, v0_ref, t0_ref), (q1_ref, k1_ref, v1_ref, t1_ref), (q2_ref, k2_ref, v2_ref, t2_ref))

    @pl.when(pl.program_id(1) == 0)
    def _():
        for gi, (_, _, _, t_ref) in enumerate(groups):
            dil = ATTN_DILATIONS[gi]
            row = jnp.broadcast_to(e_ref[gi:gi + 1, :], (Q_TILE, BIAS_LANES))
            for vi, delta in enumerate(_tile_deltas(dil)):
                skew = pltpu.roll(row, (delta - HALF_WINDOW) % BIAS_LANES, axis=1, stride=1, stride_axis=0)
                t_ref[vi] = skew[:, :_key_span(dil)]

    def scores(u, gi):
        q_ref, k_ref, _, t_ref = groups[gi]
        dil = ATTN_DILATIONS[gi]
        sub = SEQ // dil
        span = _key_span(dil)
        blocks = sub // Q_TILE
        r = u // blocks
        m0 = (u % blocks) * Q_TILE
        start = jnp.clip(m0 - HALF_WINDOW, 0, sub - span)
        variant = (m0 - start) // HALF_WINDOW
        q = q_ref[pl.ds(pl.multiple_of(u * Q_TILE, Q_TILE), Q_TILE), :]
        k0 = pl.multiple_of(r * sub + start, HALF_WINDOW)
        k = k_ref[pl.ds(k0, span), :]
        s = lax.dot_general(q, k, (((1,), (1,)), ((), ())), preferred_element_type=_F32)
        if dil == 1:
            dest = (gi, pl.ds(pl.multiple_of(m0, Q_TILE), Q_TILE))
        elif blocks > 1:
            dest = (gi, pl.ds(m0 * dil + r, Q_TILE, stride=dil))
        else:
            dest = (None, pl.ds(r * RESIDUE_PITCH, Q_TILE, stride=1))
        return s + t_ref[variant], k0, dest

    def weights(s):
        m = jnp.max(s, axis=-1, keepdims=True)
        return m, jnp.exp((s - m).astype(_BF16))

    ones = jnp.ones((Q_TILE + 2 * HALF_WINDOW, HEAD_DIM), _BF16)

    def values(gi, k0, dest, m, p):
        span = p.shape[1]
        v = groups[gi][2][pl.ds(k0, span), :]
        ov = jnp.dot(p, jnp.concatenate([v, ones[:span]], axis=1), preferred_element_type=_F32)
        m2 = jnp.broadcast_to(m * LOG2_E, (Q_TILE, HEAD_DIM))
        slot, rows = dest
        if slot is None:
            far_o_ref[rows, :] = ov[:, :HEAD_DIM]
            far_m_ref[rows, :] = m2
            far_l_ref[rows, :] = ov[:, HEAD_DIM:]
        else:
            near_o_ref[slot, rows, :] = ov[:, :HEAD_DIM]
            near_m_ref[slot, rows, :] = m2
            near_l_ref[slot, rows, :] = ov[:, HEAD_DIM:]

    def units(it, carry):
        todo = [(it * UNITS_PER_STEP + j, gi) for j in range(UNITS_PER_STEP) for gi in range(3)]
        scored = [scores(u, gi) for u, gi in todo]
        soft = [weights(s) for s, _, _ in scored]
        for (_, gi), (_, k0, dest), (m, p) in zip(todo, scored, soft):
            values(gi, k0, dest, m, p)
        return carry

    lax.fori_loop(0, SEQ // Q_TILE // UNITS_PER_STEP, units, 0)

    far_dil = ATTN_DILATIONS[2]

    def merge(t, carry):
        rows = pl.ds(pl.multiple_of(t * Q_TILE, Q_TILE), Q_TILE)

        def far(ref):
            return jnp.concatenate([ref[pl.ds(t * (Q_TILE // far_dil) + j, far_dil, stride=RESIDUE_PITCH), :]
                                    for j in range(Q_TILE // far_dil)], axis=0)

        ms = [near_m_ref[0, rows, :], near_m_ref[1, rows, :], far(far_m_ref)]
        outs = [near_o_ref[0, rows, :], near_o_ref[1, rows, :], far(far_o_ref)]
        ls = [near_l_ref[0, rows, :], near_l_ref[1, rows, :], far(far_l_ref)]
        top = jnp.maximum(jnp.maximum(ms[0], ms[1]), ms[2])
        ws = [jnp.exp2(x - top) for x in ms]
        num = ws[0] * outs[0] + ws[1] * outs[1] + ws[2] * outs[2]
        den = ws[0] * ls[0] + ws[1] * ls[1] + ws[2] * ls[2]
        o_ref[rows, :] = (num / den).astype(o_ref.dtype)
        return carry

    lax.fori_loop(0, SEQ // Q_TILE, merge, 0, unroll=4)


def _attn_call(qkv, bias_rows, to_cast, layer):
    batch = qkv[0][0].shape[0]

    def head_spec(slot, which):
        return pl.BlockSpec((None, None, None, SEQ, HEAD_DIM),
                            lambda h, b: (b, slot, which * HEADS_PER_GROUP + h, 0, 0))

    in_specs = [head_spec(slot, which) for _, slot in qkv for which in range(3)]
    in_specs.append(pl.BlockSpec((None, 3, BIAS_LANES), lambda h, b: (h, 0, 0)))
    riders = [_cast_rider(w, layer, HEADS_PER_GROUP * batch, lambda h, b: h * batch + b) for w in to_cast]
    in_specs += [r[0] for r in riders]
    scratch = [pltpu.VMEM((len(_tile_deltas(dil)), Q_TILE, _key_span(dil)), _F32) for dil in ATTN_DILATIONS]
    assert SEQ // ATTN_DILATIONS[2] == Q_TILE and all(SEQ // d > Q_TILE for d in ATTN_DILATIONS[:2])
    scratch += [pltpu.VMEM((2, SEQ, HEAD_DIM), _F32) for _ in range(3)]
    scratch += [pltpu.VMEM((ATTN_DILATIONS[2] * RESIDUE_PITCH, HEAD_DIM), _F32) for _ in range(3)]
    attn, *cast = pl.pallas_call(
        functools.partial(_attn_kernel, n_cast=len(riders)),
        grid=(HEADS_PER_GROUP, batch),
        in_specs=in_specs,
        out_specs=[pl.BlockSpec((None, SEQ, HEAD_DIM), lambda h, b: (b, 0, h))] + [r[1] for r in riders],
        out_shape=[jax.ShapeDtypeStruct((batch, SEQ, ATTN_OUT), _BF16)] + [r[2] for r in riders],
        scratch_shapes=scratch,
        compiler_params=pltpu.CompilerParams(dimension_semantics=("arbitrary", "arbitrary"),
                                             vmem_limit_bytes=VMEM_LIMIT),
        name="attn",
    )(*[arr for arr, _ in qkv for _ in range(3)], bias_rows, *to_cast)
    return attn, cast


def _mix_kernel(xn_ref, pool_ref, attn_ref, h_ref, wg_ref, wa_ref, wb_ref, wo_ref, g_ref, h_out_ref, xn_out_ref):
    xn = xn_ref[...]
    gates = jnp.dot(xn, wg_ref[:, GATE_BLOCK - GATE_WIDTH:], preferred_element_type=_F32)
    a = jnp.dot(pool_ref[...], wa_ref[...], preferred_element_type=_F32)
    b = jnp.dot(attn_ref[...], wb_ref[...], preferred_element_type=_F32)
    merged = jax.nn.sigmoid(gates[:, :D_MODEL]) * a + jax.nn.sigmoid(gates[:, D_MODEL:]) * b
    h = h_ref[...] + jnp.dot(merged.astype(_BF16), wo_ref[...], preferred_element_type=_F32)
    h_out_ref[...] = h
    xn_out_ref[...] = _rms(h, g_ref[...]).astype(xn_out_ref.dtype)


def _mix_call(xn2d, pool2d, attn2d, h2d, w_in, w_a, w_b, w_o, g_next):
    rows = xn2d.shape[0]
    tm = ROW_TILE
    assert IN_WIDTH % GATE_BLOCK == 0 and GATE_BLOCK >= GATE_WIDTH
    return pl.pallas_call(
        _mix_kernel,
        grid=(rows // tm,),
        in_specs=[pl.BlockSpec((tm, D_MODEL), lambda i: (i, 0)),
                  pl.BlockSpec((tm, POOL_WIDTH), lambda i: (i, 0)),
                  pl.BlockSpec((tm, ATTN_OUT), lambda i: (i, 0)),
                  pl.BlockSpec((tm, D_MODEL), lambda i: (i, 0)),
                  _const_spec((D_MODEL, GATE_BLOCK), (0, IN_WIDTH // GATE_BLOCK - 1)),
                  _const_spec((POOL_WIDTH, D_MODEL)),
                  _const_spec((ATTN_OUT, D_MODEL)),
                  _const_spec((D_MODEL, D_MODEL)),
                  _const_spec((1, D_MODEL))],
        out_specs=[pl.BlockSpec((tm, D_MODEL), lambda i: (i, 0)),
                   pl.BlockSpec((tm, D_MODEL), lambda i: (i, 0))],
        out_shape=[jax.ShapeDtypeStruct((rows, D_MODEL), _F32),
                   jax.ShapeDtypeStruct((rows, D_MODEL), _BF16)],
        compiler_params=pltpu.CompilerParams(dimension_semantics=("parallel",),
                                             vmem_limit_bytes=VMEM_LIMIT),
        name="mix",
    )(xn2d, pool2d, attn2d, h2d, w_in, w_a, w_b, w_o, g_next.reshape(1, D_MODEL))


def _gelu_tanh(x):
    return 0.5 * x * (1.0 + jnp.tanh(np.sqrt(2.0 / np.pi).astype(np.float32) * (x + 0.044715 * (x * x * x))))


def _ffn_kernel(xp_ref, x_ref, xnx_ref, h_ref, wup_ref, cw_ref, cb_ref, wdn_ref, g_ref, *rest, final, n_cast):
    n_out = 1 if final else 2
    cast_src, out_refs, rest = rest[:n_cast], rest[n_cast:n_cast + n_out], rest[n_cast + n_out:]
    cast_dst, a_refs = rest[:n_cast], rest[n_cast:]
    _cast_blocks(cast_src, cast_dst)
    tm = x_ref.shape[0]
    tiles_per_seq = SEQ // tm
    t = pl.program_id(0) % tiles_per_seq
    x = x_ref[...]
    x_prev = jnp.where(t > 0, xp_ref[...], jnp.zeros_like(xp_ref))
    x_next = jnp.where(t < tiles_per_seq - 1, xnx_ref[...], jnp.zeros_like(xnx_ref))
    xe = jnp.concatenate([x_prev, x, x_next], axis=0)
    acc = None
    assert sum(FF_CHUNKS) == D_FF
    for c, width in enumerate(FF_CHUNKS):
        c0 = sum(FF_CHUNKS[:c])
        cols = slice(c0, c0 + width)
        gcols = slice(D_FF + c0, D_FF + c0 + width)
        a_ext = jnp.dot(xe, wup_ref[:, cols], preferred_element_type=_F32)
        gate = jnp.dot(x, wup_ref[:, gcols], preferred_element_type=_F32)
        a_mid = a_ext[HALO:HALO + tm]
        a_ref = a_refs[c]
        for j in range(width // LANES):
            a_ref[j] = a_ext[:, j * LANES:(j + 1) * LANES]
        a_prev = jnp.concatenate([a_ref[j, pl.ds(HALO - 1, tm, stride=1), :] for j in range(width // LANES)], axis=1)
        a_next = jnp.concatenate([a_ref[j, pl.ds(HALO + 1, tm, stride=1), :] for j in range(width // LANES)], axis=1)
        cw = cw_ref[:, cols]
        conv = a_prev * cw[0:1] + a_mid * cw[1:2] + a_next * cw[2:3] + cb_ref[:, cols]
        act = (_gelu_tanh(conv) * gate).astype(_BF16)
        part = jnp.dot(act, wdn_ref[cols, :], preferred_element_type=_F32)
        acc = part if acc is None else acc + part
    h = h_ref[...] + acc
    if final:
        out_refs[0][...] = _rms(h, g_ref[...])
    else:
        out_refs[0][...] = h
        out_refs[1][...] = _rms(h, g_ref[...]).astype(out_refs[1].dtype)


def _ffn_call(xn2d, h2d, w_up, conv_w, conv_b, w_down, g_next, final, to_cast=(), cast_layer=0):
    rows = xn2d.shape[0]
    tm = ROW_TILE
    per = tm // HALO
    last = rows // HALO - 1
    row_spec = pl.BlockSpec((tm, D_MODEL), lambda i: (i, 0))
    if final:
        out_specs = [row_spec]
        out_shape = [jax.ShapeDtypeStruct((rows, D_MODEL), _F32)]
    else:
        out_specs = [row_spec, row_spec]
        out_shape = [jax.ShapeDtypeStruct((rows, D_MODEL), _F32),
                     jax.ShapeDtypeStruct((rows, D_MODEL), _BF16)]
    riders = [_cast_rider(w, cast_layer, rows // tm, lambda i: i) for w in to_cast]
    outs = pl.pallas_call(
        functools.partial(_ffn_kernel, final=final, n_cast=len(riders)),
        grid=(rows // tm,),
        in_specs=[pl.BlockSpec((HALO, D_MODEL), lambda i: (jnp.maximum(i * per - 1, 0), 0)),
                  row_spec,
                  pl.BlockSpec((HALO, D_MODEL), lambda i: (jnp.minimum((i + 1) * per, last), 0)),
                  row_spec,
                  _const_spec((D_MODEL, 2 * D_FF)),
                  _const_spec((3, D_FF)),
                  _const_spec((1, D_FF)),
                  _const_spec((D_FF, D_MODEL)),
                  _const_spec((1, D_MODEL))] + [r[0] for r in riders],
        out_specs=out_specs + [r[1] for r in riders],
        out_shape=out_shape + [r[2] for r in riders],
        scratch_shapes=[pltpu.VMEM((width // LANES, tm + 2 * HALO, LANES), _F32) for width in FF_CHUNKS],
        compiler_params=pltpu.CompilerParams(dimension_semantics=("parallel",),
                                             vmem_limit_bytes=VMEM_LIMIT),
        name="ffn",
    )(xn2d, xn2d, xn2d, h2d, w_up, conv_w, conv_b.reshape(1, D_FF), w_down, g_next.reshape(1, D_MODEL), *to_cast)
    return outs[:len(out_shape)], outs[len(out_shape):]


def kernel(x, w_in, w_pool, pool_scale, w_a, w_b, w_o, norm1, norm2, w_up, conv_w, conv_b, w_down, rel_bias, norm_f):
    batch, seq, d = x.shape
    assert (seq, d) == (SEQ, D_MODEL)
    depth = w_in.shape[0]
    rows = batch * seq
    bias_rows = _bias_rows(rel_bias)
    h = x.reshape(rows, d)
    xn = None
    w_pool_bf = w_pool.astype(_BF16)
    w_in_bf = w_in[0].astype(_BF16)
    for layer in range(depth):
        if xn is None:
            qkv0, xn3 = _qkv_call(x, w_in_bf, (0,), norm_gain=norm1[layer])
            xn = xn3.reshape(rows, d)
            qkv12 = _qkv_call(xn3, w_in_bf, (1, 2))
            qkv = [(qkv0, 0), (qkv12, 0), (qkv12, 1)]
        else:
            xn3 = xn.reshape(batch, seq, d)
            qkv_all = _qkv_call(xn3, w_in_bf, (0, 1, 2))
            qkv = [(qkv_all, gi) for gi in range(3)]
        pool = _pool_call(xn3, w_in_bf, w_pool_bf, pool_scale[layer], layer)
        attn, (w_a_bf, w_b_bf, w_o_bf, w_up_bf, w_down_bf) = _attn_call(qkv, bias_rows, (w_a, w_b, w_o, w_up, w_down), layer)
        h, xn = _mix_call(xn, pool.reshape(rows, POOL_WIDTH), attn.reshape(rows, ATTN_OUT), h,
                          w_in_bf, w_a_bf, w_b_bf, w_o_bf, norm2[layer])
        final = layer == depth - 1
        g_next = norm_f if final else norm1[layer + 1]
        outs, cast = _ffn_call(xn, h, w_up_bf, conv_w[layer], conv_b[layer], w_down_bf, g_next, final,
                               to_cast=() if final else (w_in,), cast_layer=layer + 1)
        if final:
            return outs[0].reshape(batch, seq, d)
        h, xn = outs
        w_in_bf, = cast
```
